```python
import math
import jax, jax.numpy as jnp
from jax import lax
import numpy as np

D_MODEL = 1024
BATCH = 32
SEQ = 2048
DEPTH = 1

HEAD_DIM = 64
MIX_WIDTH = D_MODEL
ATTN_WIDTH = MIX_WIDTH // 2
FOURIER_WIDTH = MIX_WIDTH - ATTN_WIDTH
N_FOURIER_GROUPS = FOURIER_WIDTH // HEAD_DIM
N_Q_HEADS = ATTN_WIDTH // HEAD_DIM
Q_PER_KV = 4
N_KV_HEADS = N_Q_HEADS // Q_PER_KV
KV_WIDTH = N_KV_HEADS * HEAD_DIM
IN_PROJ_WIDTH = FOURIER_WIDTH + ATTN_WIDTH + 2 * KV_WIDTH
WINDOW = 128
BLOCK = 128
N_BUCKETS = 32
MAX_DISTANCE = 128
N_EXPERTS = 16
CAPACITY_FACTOR = 2
D_EXPERT = 1024
N_ADA = 6
EPS = 1e-6

kernel_name = "hybrid_fourier_window_gqa_ec_moe_block"


def rmsnorm(x, g):
    xf = x.astype(jnp.float32)
    y = xf * lax.rsqrt(jnp.mean(xf * xf, axis=-1, keepdims=True) + EPS)
    return (y * g.astype(jnp.float32)).astype(x.dtype)


def t5_bucket(rel):
    half = N_BUCKETS // 2
    max_exact = half // 2
    ret = jnp.where(rel > 0, half, 0)
    n = jnp.abs(rel)
    nf = jnp.maximum(n, 1).astype(jnp.float32)
    large = max_exact + (jnp.log(nf / max_exact) / math.log(MAX_DISTANCE / max_exact)
                         * (half - max_exact)).astype(jnp.int32)
    large = jnp.minimum(large, half - 1)
    return ret + jnp.where(n < max_exact, n, large)


def fourier_mix(u, w_f, b_f):
    B, S, _ = u.shape
    ug = u.reshape(B, S, N_FOURIER_GROUPS, HEAD_DIM).astype(jnp.float32)
    mixed = jnp.fft.fft2(ug, axes=(1, 3), norm="ortho").real.astype(u.dtype)
    y = jnp.einsum('bsgc,gcd->bsgd', mixed, w_f) + b_f
    return y.reshape(B, S, FOURIER_WIDTH)


def windowed_gqa(q, k, v, rel_bias, sink):
    B, S = q.shape[:2]
    nb = S // BLOCK
    span = BLOCK + 2 * WINDOW
    pad = ((0, 0), (WINDOW, WINDOW), (0, 0), (0, 0))
    kp = jnp.pad(k, pad)
    vp = jnp.pad(v, pad)
    rel = jnp.arange(span)[None, :] - WINDOW - jnp.arange(BLOCK)[:, None]
    bias = rel_bias[t5_bucket(rel)].astype(jnp.float32)
    bias = bias.transpose(2, 0, 1).reshape(N_KV_HEADS, Q_PER_KV, BLOCK, span)
    band = jnp.abs(rel) <= WINDOW
    sink_l = sink.astype(jnp.float32).reshape(N_KV_HEADS, Q_PER_KV, 1, 1)
    scale = HEAD_DIM ** -0.5

    def one_block(i):
        start = i * BLOCK
        qb = lax.dynamic_slice_in_dim(q, start, BLOCK, axis=1)
        kb = lax.dynamic_slice_in_dim(kp, start, span, axis=1)
        vb = lax.dynamic_slice_in_dim(vp, start, span, axis=1)
        kpos = start - WINDOW + jnp.arange(span)
        valid = band & ((kpos >= 0) & (kpos < S))[None, :]
        logits = jnp.einsum('bqkrd,bjkd->bkrqj', qb, kb).astype(jnp.float32) * scale + bias
        logits = jnp.where(valid, logits, -jnp.inf)
        m = jnp.maximum(jnp.max(logits, axis=-1, keepdims=True), sink_l)
        p = jnp.exp(logits - m)
        denom = jnp.sum(p, axis=-1, keepdims=True) + jnp.exp(sink_l - m)
        return jnp.einsum('bkrqj,bjkd->bqkrd', (p / denom).astype(vb.dtype), vb)

    outs = lax.map(one_block, jnp.arange(nb))
    return outs.transpose(1, 0, 2, 3, 4, 5).reshape(B, S, ATTN_WIDTH)


def expert_choice_moe(h, w_router, w_gate, w_up, w_down):
    B, S, D = h.shape
    cap = CAPACITY_FACTOR * S // N_EXPERTS
    aff = jax.nn.softmax(jnp.einsum('bsd,de->bse', h, w_router).astype(jnp.float32), axis=-1)
    g, idx = lax.top_k(aff.transpose(0, 2, 1), cap)
    xin = jax.vmap(lambda hb, ib: hb[ib])(h, idx)
    a = jnp.einsum('becd,edf->becf', xin, w_gate)
    u = jnp.einsum('becd,edf->becf', xin, w_up)
    y = jnp.einsum('becf,efd->becd', jax.nn.silu(a) * u, w_down)
    y = y * g[..., None].astype(y.dtype)
    return jax.vmap(lambda yb, ib: jnp.zeros((S, D), yb.dtype)
                    .at[ib.reshape(-1)].add(yb.reshape(-1, D)))(y, idx)


def setup_inputs(seed: int = 0) -> dict:
    key = jax.random.key(seed)
    ks = jax.random.split(key, 18)
    f32 = jnp.float32
    nrm = lambda k, shape, s: jax.random.normal(k, shape, f32) * s
    L, D = DEPTH, D_MODEL
    return {
        "x": nrm(ks[0], (BATCH, SEQ, D), 1.0),
        "c": nrm(ks[1], (BATCH, D), 1.0),
        "rel_bias": nrm(ks[2], (N_BUCKETS, N_Q_HEADS), 0.5),
        "w_ada": nrm(ks[3], (L, D, N_ADA * D), 0.5 * D ** -0.5),
        "b_ada": nrm(ks[4], (L, N_ADA * D), 0.02),
        "norm_mix_g": 1.0 + nrm(ks[5], (L, D), 0.02),
        "norm_ffn_g": 1.0 + nrm(ks[6], (L, D), 0.02),
        "w_in": nrm(ks[7], (L, D, IN_PROJ_WIDTH), D ** -0.5),
        "w_fourier": nrm(ks[8], (L, N_FOURIER_GROUPS, HEAD_DIM, HEAD_DIM), HEAD_DIM ** -0.5),
        "b_fourier": nrm(ks[9], (L, N_FOURIER_GROUPS, HEAD_DIM), 0.02),
        "q_norm_g": 1.0 + nrm(ks[10], (L, HEAD_DIM), 0.02),
        "k_norm_g": 1.0 + nrm(ks[11], (L, HEAD_DIM), 0.02),
        "sink": nrm(ks[12], (L, N_Q_HEADS), 1.0),
        "w_out": nrm(ks[13], (L, MIX_WIDTH, D), MIX_WIDTH ** -0.5),
        "w_router": nrm(ks[14], (L, D, N_EXPERTS), D ** -0.5),
        "w_gate": nrm(ks[15], (L, N_EXPERTS, D, D_EXPERT), D ** -0.5),
        "w_up": nrm(ks[16], (L, N_EXPERTS, D, D_EXPERT), D ** -0.5),
        "w_down": nrm(ks[17], (L, N_EXPERTS, D_EXPERT, D), D_EXPERT ** -0.5),
    }


def reference(x, c, rel_bias, w_ada, b_ada, norm_mix_g, norm_ffn_g, w_in, w_fourier,
              b_fourier, q_norm_g, k_norm_g, sink, w_out, w_router, w_gate, w_up, w_down):
    B, S, _ = x.shape
    split_cols = [FOURIER_WIDTH, FOURIER_WIDTH + ATTN_WIDTH, FOURIER_WIDTH + ATTN_WIDTH + KV_WIDTH]
    c_act = jax.nn.silu(c)
    for l in range(DEPTH):
        mod = jnp.einsum('bd,de->be', c_act, w_ada[l]) + b_ada[l]
        sh1, sc1, g1, sh2, sc2, g2 = [m[:, None, :] for m in jnp.split(mod, N_ADA, axis=-1)]

        h = rmsnorm(x, norm_mix_g[l]) * (1.0 + sc1) + sh1
        proj = jnp.einsum('bsd,de->bse', h, w_in[l])
        u_f, q, k, v = jnp.split(proj, split_cols, axis=-1)
        q = rmsnorm(q.reshape(B, S, N_KV_HEADS, Q_PER_KV, HEAD_DIM), q_norm_g[l])
        k = rmsnorm(k.reshape(B, S, N_KV_HEADS, HEAD_DIM), k_norm_g[l])
        v = v.reshape(B, S, N_KV_HEADS, HEAD_DIM)
        y_f = fourier_mix(u_f, w_fourier[l], b_fourier[l])
        y_a = windowed_gqa(q, k, v, rel_bias, sink[l])
        mixed = jnp.einsum('bse,ed->bsd', jnp.concatenate([y_f, y_a], axis=-1), w_out[l])
        x = x + g1 * mixed

        h2 = rmsnorm(x, norm_ffn_g[l]) * (1.0 + sc2) + sh2
        x = x + g2 * expert_choice_moe(h2, w_router[l], w_gate[l], w_up[l], w_down[l])
    return x
```

```python
import functools
import math

import numpy as np
import jax
import jax.numpy as jnp
from jax import lax
from jax.experimental import pallas as pl
from jax.experimental.pallas import tpu as pltpu

D_MODEL = 1024
SEQ = 2048
HEAD_DIM = 64
FOURIER_WIDTH = 512
ATTN_WIDTH = 512
N_GROUPS = 8
N_Q_HEADS = 8
Q_PER_KV = 4
N_KV_HEADS = 2
KV_WIDTH = 128
IN_PROJ_WIDTH = 1280
WINDOW = 128
BLOCK = 128
SPAN = BLOCK + 2 * WINDOW
N_BLOCKS = SEQ // BLOCK
N_BUCKETS = 32
MAX_DISTANCE = 128
N_EXPERTS = 16
CAPACITY = 2 * SEQ // N_EXPERTS
D_EXPERT = 1024
N_ADA = 6
EPS = 1e-6

LANES = 128
SUBLANES = 8
ROW_SLAB = D_MODEL // LANES
VMEM_LIMIT = 56 * 1024 * 1024

F32 = jnp.float32
BF16 = jnp.bfloat16
NEG_INF = float("-inf")


def _params(sem, vmem=None):
    return pltpu.CompilerParams(dimension_semantics=sem, vmem_limit_bytes=vmem)


@functools.lru_cache(maxsize=None)
def _seq_dft():
    s = np.arange(SEQ, dtype=np.int64)
    ph = (s[:, None] * s[None, :]) % SEQ
    ang = 2.0 * np.pi * ph.astype(np.float64) / SEQ
    sc = 1.0 / math.sqrt(SEQ)
    return np.concatenate([np.cos(ang) * sc, -np.sin(ang) * sc], axis=1).astype(np.float32)


@functools.lru_cache(maxsize=None)
def _chan_dft():
    c = np.arange(HEAD_DIM, dtype=np.int64)
    ph = (c[:, None] * c[None, :]) % HEAD_DIM
    ang = 2.0 * np.pi * ph.astype(np.float64) / HEAD_DIM
    sc = 1.0 / math.sqrt(HEAD_DIM)
    eye = np.eye(N_GROUPS)
    cbd = np.kron(eye, np.cos(ang) * sc)
    sbd = np.kron(eye, np.sin(ang) * sc)
    return cbd.astype(np.float32), sbd.astype(np.float32)


@functools.lru_cache(maxsize=None)
def _bucket_table():
    rel = np.arange(SPAN)[None, :] - WINDOW - np.arange(BLOCK)[:, None]
    half = N_BUCKETS // 2
    max_exact = half // 2
    n = np.abs(rel)
    nf = np.maximum(n, 1).astype(np.float64)
    large = max_exact + (np.log(nf / max_exact) / math.log(MAX_DISTANCE / max_exact)
                         * (half - max_exact)).astype(np.int64)
    sq = np.maximum(n.astype(np.int64) ** 2 // (max_exact * max_exact), 1)
    large_int = max_exact + np.floor(np.log2(sq.astype(np.float64)) + 1e-9).astype(np.int64)
    assert np.array_equal(np.where(n >= max_exact, large, 0), np.where(n >= max_exact, large_int, 0))
    large = np.minimum(large, half - 1)
    bucket = np.where(rel > 0, half, 0) + np.where(n < max_exact, n, large)
    return bucket.astype(np.int32)


@functools.lru_cache(maxsize=None)
def _group_ones(width):
    return np.kron(np.eye(width // HEAD_DIM), np.ones((HEAD_DIM, HEAD_DIM))).astype(np.float32)


@functools.lru_cache(maxsize=None)
def _tri_incl():
    i = np.arange(LANES)
    return (i[:, None] <= i[None, :]).astype(np.float32)


def _ada_kernel(c_ref, w_ref, b_ref, o_ref):
    c = c_ref[...]
    ca = c * (1.0 / (1.0 + jnp.exp(-c)))
    o_ref[...] = jnp.dot(ca, w_ref[...], precision=lax.Precision.HIGHEST,
                         preferred_element_type=F32) + b_ref[...]


def _ada(c, w_ada, b_ada):
    B = c.shape[0]
    n = N_ADA * D_MODEL
    tn = D_MODEL
    return pl.pallas_call(
        _ada_kernel,
        grid=(n // tn,),
        in_specs=[pl.BlockSpec((B, D_MODEL), lambda j: (0, 0)),
                  pl.BlockSpec((D_MODEL, tn), lambda j: (0, j)),
                  pl.BlockSpec((1, tn), lambda j: (0, j))],
        out_specs=pl.BlockSpec((B, tn), lambda j: (0, j)),
        out_shape=jax.ShapeDtypeStruct((B, n), F32),
        compiler_params=_params(("arbitrary",)),
        name="ada",
    )(c, w_ada, b_ada.reshape(1, n))


def _fold_kernel(cbd_ref, sbd_ref, w_ref, o_ref):
    w = w_ref[...]
    o_ref[:, :FOURIER_WIDTH] = jnp.dot(cbd_ref[...], w, precision=lax.Precision.HIGHEST,
                                       preferred_element_type=F32).astype(BF16)
    o_ref[:, FOURIER_WIDTH:] = jnp.dot(sbd_ref[...], w, precision=lax.Precision.HIGHEST,
                                       preferred_element_type=F32).astype(BF16)


def _fold(w_fourier):
    wbd = (jnp.eye(N_GROUPS, dtype=F32)[:, None, :, None] * w_fourier[:, :, None, :]
           ).reshape(FOURIER_WIDTH, FOURIER_WIDTH)
    cbd, sbd = _chan_dft()
    return pl.pallas_call(
        _fold_kernel,
        out_shape=jax.ShapeDtypeStruct((FOURIER_WIDTH, 2 * FOURIER_WIDTH), BF16),
        name="fold",
    )(jnp.asarray(cbd), jnp.asarray(sbd), wbd)


def _bias_kernel(rb_ref, bucket_ref, o_ref):
    h = pl.program_id(0)
    bk = bucket_ref[...]
    acc = jnp.zeros((BLOCK, SPAN), F32)
    for b in range(N_BUCKETS):
        acc = jnp.where(bk == b, rb_ref[b, h], acc)
    j = lax.broadcasted_iota(jnp.int32, (BLOCK, SPAN), 1)
    q = lax.broadcasted_iota(jnp.int32, (BLOCK, SPAN), 0)
    band = jnp.abs(j - WINDOW - q) <= WINDOW
    base = jnp.where(band, acc, NEG_INF)
    o_ref[0] = jnp.where(j >= WINDOW, base, NEG_INF)
    o_ref[1] = base
    o_ref[2] = jnp.where(j < WINDOW + BLOCK, base, NEG_INF)


def _bias_table(rel_bias):
    return pl.pallas_call(
        _bias_kernel,
        grid=(N_Q_HEADS,),
        in_specs=[pl.BlockSpec(memory_space=pltpu.SMEM),
                  pl.BlockSpec((BLOCK, SPAN), lambda h: (0, 0))],
        out_specs=pl.BlockSpec((3, None, BLOCK, SPAN), lambda h: (0, h, 0, 0)),
        out_shape=jax.ShapeDtypeStruct((3, N_Q_HEADS, BLOCK, SPAN), F32),
        compiler_params=_params(("arbitrary",)),
        name="bias",
    )(rel_bias, jnp.asarray(_bucket_table()))


def _inproj_kernel(x_ref, mod_ref, g_ref, win_ref, pq_ref, bdq_ref, bdk_ref, gq_ref, gk_ref,
                   a_ref, b_ref, q_ref, k_ref, v_ref):
    x = x_ref[...]
    ms = jnp.mean(x * x, axis=-1, keepdims=True)
    y = x * lax.rsqrt(ms + EPS) * g_ref[...]
    h = y * (1.0 + mod_ref[1:2, :]) + mod_ref[0:1, :]
    proj = jnp.dot(h.astype(BF16), win_ref[...], preferred_element_type=F32)
    uf = proj[:, :FOURIER_WIDTH].astype(BF16)
    ab = jnp.dot(uf, pq_ref[...], preferred_element_type=F32)
    a_ref[...] = ab[:, :FOURIER_WIDTH].astype(BF16)
    b_ref[...] = ab[:, FOURIER_WIDTH:].astype(BF16)
    q0 = FOURIER_WIDTH
    k0 = q0 + ATTN_WIDTH
    v0 = k0 + KV_WIDTH
    q = proj[:, q0:k0]
    ssq = jnp.dot((q * q).astype(BF16), bdq_ref[...], preferred_element_type=F32)
    q_ref[...] = (q * lax.rsqrt(ssq * (1.0 / HEAD_DIM) + EPS) * gq_ref[...]).astype(BF16)
    k = proj[:, k0:v0]
    ssk = jnp.dot((k * k).astype(BF16), bdk_ref[...], preferred_element_type=F32)
    k_ref[...] = (k * lax.rsqrt(ssk * (1.0 / HEAD_DIM) + EPS) * gk_ref[...]).astype(BF16)
    v_ref[...] = proj[:, v0:].astype(BF16)


def _inproj(x, mod, g, win, pq, gq, gk, tm=512):
    B = x.shape[0]
    const = lambda shape: pl.BlockSpec(shape, lambda b, i: (0,) * len(shape))
    tok = lambda w: pl.BlockSpec((None, tm, w), lambda b, i: (b, i, 0))
    sds = lambda w: jax.ShapeDtypeStruct((B, SEQ, w), BF16)
    return pl.pallas_call(
        _inproj_kernel,
        grid=(B, SEQ // tm),
        in_specs=[tok(D_MODEL),
                  pl.BlockSpec((None, N_ADA, D_MODEL), lambda b, i: (b, 0, 0)),
                  const((1, D_MODEL)),
                  const((D_MODEL, IN_PROJ_WIDTH)),
                  const((FOURIER_WIDTH, 2 * FOURIER_WIDTH)),
                  const((ATTN_WIDTH, ATTN_WIDTH)),
                  const((KV_WIDTH, KV_WIDTH)),
                  const((1, ATTN_WIDTH)),
                  const((1, KV_WIDTH))],
        out_specs=[tok(FOURIER_WIDTH), tok(FOURIER_WIDTH), tok(ATTN_WIDTH), tok(KV_WIDTH),
                   tok(KV_WIDTH)],
        out_shape=[sds(FOURIER_WIDTH), sds(FOURIER_WIDTH), sds(ATTN_WIDTH), sds(KV_WIDTH),
                   sds(KV_WIDTH)],
        compiler_params=_params(("parallel", "parallel"), VMEM_LIMIT),
        name="inproj",
    )(x, mod, g, win, pq, jnp.asarray(_group_ones(ATTN_WIDTH)).astype(BF16),
      jnp.asarray(_group_ones(KV_WIDTH)).astype(BF16), gq, gk)


FOURIER_ROWS = 512


def _fourier_kernel(cs_ref, a_ref, b_ref, bf_ref, o_ref):
    a = a_ref[...]
    b = b_ref[...]
    for i in range(SEQ // FOURIER_ROWS):
        rows = pl.ds(i * FOURIER_ROWS, FOURIER_ROWS)
        acc = jnp.dot(cs_ref[rows, :SEQ], a, preferred_element_type=F32)
        acc = acc + jnp.dot(cs_ref[rows, SEQ:], b, preferred_element_type=F32)
        o_ref[rows, :] = (acc + bf_ref[...]).astype(BF16)


def _fourier(cs, a, b, bf):
    B = a.shape[0]
    tok = pl.BlockSpec((None, SEQ, FOURIER_WIDTH), lambda i: (i, 0, 0))
    return pl.pallas_call(
        _fourier_kernel,
        grid=(B,),
        in_specs=[pl.BlockSpec((SEQ, 2 * SEQ), lambda i: (0, 0), pipeline_mode=pl.Buffered(1)),
                  tok, tok,
                  pl.BlockSpec((1, FOURIER_WIDTH), lambda i: (0, 0))],
        out_specs=tok,
        out_shape=jax.ShapeDtypeStruct((B, SEQ, FOURIER_WIDTH), BF16),
        compiler_params=_params(("parallel",), VMEM_LIMIT),
        name="fourier",
    )(cs, a, b, bf)


def _attn_kernel(sink_ref, q_ref, kl_ref, km_ref, kr_ref, vl_ref, vm_ref, vr_ref, bias_ref,
                 o_ref):
    i = pl.program_id(1)
    variant = jnp.where(i == 0, 0, jnp.where(i == N_BLOCKS - 1, 2, 1))
    keys = jnp.concatenate([kl_ref[...], km_ref[...], kr_ref[...]], axis=0)
    vals = jnp.concatenate([vl_ref[...], vm_ref[...], vr_ref[...]], axis=0)
    lane = lax.broadcasted_iota(jnp.int32, (BLOCK, 2 * HEAD_DIM), 1)
    first = lane < HEAD_DIM
    for r in range(Q_PER_KV):
        qb = q_ref[:, r * LANES:(r + 1) * LANES]
        outs = []
        for kv in range(N_KV_HEADS):
            hidx = kv * Q_PER_KV + r
            qm = jnp.where(first if kv == 0 else jnp.logical_not(first), qb, jnp.zeros_like(qb))
            logits = lax.dot_general(qm, keys, (((1,), (1,)), ((), ())),
                                     preferred_element_type=F32)
            logits = logits + bias_ref[variant, hidx]
            sink = sink_ref[hidx]
            m = jnp.maximum(jnp.max(logits, axis=-1, keepdims=True), sink)
            p = jnp.exp(logits - m)
            denom = jnp.sum(p, axis=-1, keepdims=True) + jnp.exp(sink - m)
            pv = jnp.dot(p.astype(BF16), vals, preferred_element_type=F32)
            outs.append(pv / denom)
        o_ref[:, r * LANES:(r + 1) * LANES] = jnp.where(first, outs[0], outs[1]).astype(BF16)


def _attn(sink, q, k, v, bias):
    B = q.shape[0]
    kv_spec = lambda f: pl.BlockSpec((None, BLOCK, KV_WIDTH), lambda b, i: (b, f(i), 0))
    left = lambda i: jnp.maximum(i - 1, 0)
    mid = lambda i: i
    right = lambda i: jnp.minimum(i + 1, N_BLOCKS - 1)
    qspec = pl.BlockSpec((None, BLOCK, ATTN_WIDTH), lambda b, i: (b, i, 0))
    return pl.pallas_call(
        _attn_kernel,
        grid=(B, N_BLOCKS),
        in_specs=[pl.BlockSpec(memory_space=pltpu.SMEM), qspec,
                  kv_spec(left), kv_spec(mid), kv_spec(right),
                  kv_spec(left), kv_spec(mid), kv_spec(right),
                  pl.BlockSpec((3, N_Q_HEADS, BLOCK, SPAN), lambda b, i: (0, 0, 0, 0))],
        out_specs=qspec,
        out_shape=jax.ShapeDtypeStruct((B, SEQ, ATTN_WIDTH), BF16),
        compiler_params=_params(("parallel", "parallel"), VMEM_LIMIT),
        name="attn",
    )(sink, q, k, k, k, v, v, v, bias)


def _outproj_kernel(yf_ref, ya_ref, x_ref, mod_ref, g_ref, wf_ref, wa_ref, wrh_ref, wrl_ref,
                    x1_ref, h2_ref, aff_ref):
    tm = x_ref.shape[0]
    mixed = jnp.dot(yf_ref[...], wf_ref[...], preferred_element_type=F32)
    mixed = mixed + jnp.dot(ya_ref[...], wa_ref[...], preferred_element_type=F32)
    x1 = x_ref[...] + mod_ref[2:3, :] * mixed
    x1_ref[...] = x1
    ms = jnp.mean(x1 * x1, axis=-1, keepdims=True)
    y = x1 * lax.rsqrt(ms + EPS) * g_ref[...]
    h2 = y * (1.0 + mod_ref[4:5, :]) + mod_ref[3:4, :]
    for j in range(ROW_SLAB):
        h2_ref[pl.ds(j, tm, stride=ROW_SLAB), :] = h2[:, j * LANES:(j + 1) * LANES]
    hi = h2.astype(BF16)
    lo = (h2 - hi.astype(F32)).astype(BF16)
    logits = jnp.dot(hi, wrh_ref[...], preferred_element_type=F32)
    logits = logits + jnp.dot(lo, wrh_ref[...], preferred_element_type=F32)
    logits = logits + jnp.dot(hi, wrl_ref[...], preferred_element_type=F32)
    lane = lax.broadcasted_iota(jnp.int32, (tm, LANES), 1)
    logits = jnp.where(lane < N_EXPERTS, logits, NEG_INF)
    m = jnp.max(logits, axis=-1, keepdims=True)
    e = jnp.exp(logits - m)
    aff_ref[...] = e / jnp.sum(e, axis=-1, keepdims=True)


def _outproj(yf, ya, x, mod, g, wf, wa, wrh, wrl, tm=512):
    B = x.shape[0]
    const = lambda shape: pl.BlockSpec(shape, lambda b, i: (0,) * len(shape))
    tok = lambda w: pl.BlockSpec((None, tm, w), lambda b, i: (b, i, 0))
    return pl.pallas_call(
        _outproj_kernel,
        grid=(B, SEQ // tm),
        in_specs=[tok(FOURIER_WIDTH), tok(ATTN_WIDTH), tok(D_MODEL),
                  pl.BlockSpec((None, N_ADA, D_MODEL), lambda b, i: (b, 0, 0)),
                  const((1, D_MODEL)),
                  const((FOURIER_WIDTH, D_MODEL)), const((ATTN_WIDTH, D_MODEL)),
                  const((D_MODEL, LANES)), const((D_MODEL, LANES))],
        out_specs=[tok(D_MODEL),
                   pl.BlockSpec((None, tm * ROW_SLAB, LANES), lambda b, i: (b, i, 0)),
                   tok(LANES)],
        out_shape=[jax.ShapeDtypeStruct((B, SEQ, D_MODEL), F32),
                   jax.ShapeDtypeStruct((B, SEQ * ROW_SLAB, LANES), F32),
                   jax.ShapeDtypeStruct((B, SEQ, LANES), F32)],
        compiler_params=_params(("parallel", "parallel"), VMEM_LIMIT),
        name="outproj",
    )(yf, ya, x, mod, g, wf, wa, wrh, wrl)


def _route_kernel(aff_ref, tri_ref, idx_ref, gate_ref):
    aff = aff_ref[...]
    aff_t = jnp.transpose(aff)[:N_EXPERTS]
    bits = pltpu.bitcast(aff_t, jnp.int32)
    cap = float(CAPACITY)

    t = jnp.zeros((N_EXPERTS, 1), jnp.int32)
    for bit in range(30, -1, -1):
        cand = t | (1 << bit)
        cnt = jnp.sum(jnp.where(bits >= cand, 1.0, 0.0), axis=1, keepdims=True)
        t = jnp.where(cnt >= cap, cand, t)
    gt = bits > t
    eq = bits == t
    need = cap - jnp.sum(jnp.where(gt, 1.0, 0.0), axis=1, keepdims=True)

    tri = tri_ref[...]
    n_chunks = SEQ // LANES

    def prefix(flags_f32):
        outs = []
        carry = jnp.zeros((N_EXPERTS, 1), F32)
        for c in range(n_chunks):
            f = flags_f32[:, c * LANES:(c + 1) * LANES]
            incl = jnp.dot(f.astype(BF16), tri, preferred_element_type=F32)
            outs.append(incl - f + carry)
            carry = carry + jnp.sum(f, axis=1, keepdims=True)
        return jnp.concatenate(outs, axis=1)

    eq_f = jnp.where(eq, 1.0, 0.0)
    eq_rank = prefix(eq_f)
    sel_f = jnp.where(gt, 1.0, jnp.where(eq_rank < need, eq_f, 0.0))
    pos = prefix(sel_f)
    posm = jnp.where(sel_f > 0.0, pos, -1.0)

    hi = aff.astype(BF16).astype(F32)
    r1 = aff - hi
    mid = r1.astype(BF16).astype(F32)
    lo = r1 - mid
    lane = lax.broadcasted_iota(jnp.int32, (SEQ, LANES), 1)
    tok = lax.broadcasted_iota(jnp.int32, (SEQ, LANES), 0)
    vals = hi + pltpu.roll(mid, N_EXPERTS, axis=1) + pltpu.roll(lo, 2 * N_EXPERTS, axis=1)
    vals = vals + jnp.where(lane == 3 * N_EXPERTS, (tok >> 6).astype(F32), 0.0)
    vals = vals + jnp.where(lane == 3 * N_EXPERTS + 1, (tok & 63).astype(F32), 0.0)
    vals = vals.astype(BF16)

    slot = lax.broadcasted_iota(jnp.int32, (CAPACITY, SEQ), 0).astype(F32)
    lane_c = lax.broadcasted_iota(jnp.int32, (CAPACITY, LANES), 1)
    w_idx = jnp.where(lane_c == 3 * N_EXPERTS, 64.0,
                      jnp.where(lane_c == 3 * N_EXPERTS + 1, 1.0, 0.0))
    for e in range(N_EXPERTS):
        onehot = jnp.where(posm[e:e + 1, :] == slot, 1.0, 0.0).astype(BF16)
        res = jnp.dot(onehot, vals, preferred_element_type=F32)
        w_g = jnp.where((lane_c == e) | (lane_c == N_EXPERTS + e) | (lane_c == 2 * N_EXPERTS + e),
                        1.0, 0.0)
        rows = pl.ds(e * CAPACITY, CAPACITY)
        idx_ref[rows, :] = jnp.sum(res * w_idx, axis=1, keepdims=True).astype(jnp.int32)
        gate_ref[rows, :] = jnp.sum(res * w_g, axis=1, keepdims=True)


def _route(aff):
    B = aff.shape[0]
    n = N_EXPERTS * CAPACITY
    return pl.pallas_call(
        _route_kernel,
        grid=(B,),
        in_specs=[pl.BlockSpec((None, SEQ, LANES), lambda b: (b, 0, 0)),
                  pl.BlockSpec((LANES, LANES), lambda b: (0, 0))],
        out_specs=[pl.BlockSpec((None, n, 1), lambda b: (b, 0, 0)),
                   pl.BlockSpec((None, n, 1), lambda b: (b, 0, 0))],
        out_shape=[jax.ShapeDtypeStruct((B, n, 1), jnp.int32),
                   jax.ShapeDtypeStruct((B, n, 1), F32)],
        compiler_params=_params(("parallel",), VMEM_LIMIT),
        name="route",
    )(aff, jnp.asarray(_tri_incl()).astype(BF16))


GATHER_UNROLL = 8
SCATTER_UNROLL = 4


def _moe_kernel(idx_ref, gate_ref, h2_ref, wg_ref, wu_ref, wd_ref, acc_ref, xin_ref, y_ref):
    e = pl.program_id(1)
    base = e * CAPACITY

    @pl.when(e == 0)
    def _():
        acc_ref[...] = jnp.zeros_like(acc_ref)

    def gather(g, carry):
        for u in range(GATHER_UNROLL):
            p = g * GATHER_UNROLL + u
            src = pl.multiple_of(idx_ref[0, base + p] * ROW_SLAB, ROW_SLAB)
            dst = pl.multiple_of(p * ROW_SLAB, ROW_SLAB)
            xin_ref[pl.ds(dst, ROW_SLAB), :] = h2_ref[pl.ds(src, ROW_SLAB), :]
        return carry

    lax.fori_loop(0, CAPACITY // GATHER_UNROLL, gather, 0)

    xin = jnp.concatenate(
        [xin_ref[pl.ds(j, CAPACITY, stride=ROW_SLAB), :].astype(BF16) for j in range(ROW_SLAB)],
        axis=1)
    a = jnp.dot(xin, wg_ref[...], preferred_element_type=F32)
    u = jnp.dot(xin, wu_ref[...], preferred_element_type=F32)
    hmid = (a * (1.0 / (1.0 + jnp.exp(-a))) * u).astype(BF16)
    y = jnp.dot(hmid, wd_ref[...], preferred_element_type=F32)
    for j in range(ROW_SLAB):
        y_ref[pl.ds(j, CAPACITY, stride=ROW_SLAB), :] = y[:, j * LANES:(j + 1) * LANES]

    def scatter(g, carry):
        new = []
        for u in range(SCATTER_UNROLL):
            p = g * SCATTER_UNROLL + u
            dst = pl.multiple_of(idx_ref[0, base + p] * ROW_SLAB, ROW_SLAB)
            src = pl.multiple_of(p * ROW_SLAB, ROW_SLAB)
            new.append((dst, acc_ref[pl.ds(dst, ROW_SLAB), :]
                        + y_ref[pl.ds(src, ROW_SLAB), :] * gate_ref[0, base + p]))
        for dst, val in new:
            acc_ref[pl.ds(dst, ROW_SLAB), :] = val
        return carry

    lax.fori_loop(0, CAPACITY // SCATTER_UNROLL, scatter, 0)


def _moe(idx, gate, h2, wg, wu, wd):
    B = h2.shape[0]
    n = N_EXPERTS * CAPACITY
    rows = SEQ * ROW_SLAB
    smem = pl.BlockSpec((None, 1, n), lambda b, e: (b, 0, 0), memory_space=pltpu.SMEM)
    wspec = lambda r, c: pl.BlockSpec((None, r, c), lambda b, e: (e, 0, 0))
    slab = pl.BlockSpec((None, rows, LANES), lambda b, e: (b, 0, 0))
    return pl.pallas_call(
        _moe_kernel,
        grid=(B, N_EXPERTS),
        in_specs=[smem, smem, slab,
                  wspec(D_MODEL, D_EXPERT), wspec(D_MODEL, D_EXPERT), wspec(D_EXPERT, D_MODEL)],
        out_specs=slab,
        out_shape=jax.ShapeDtypeStruct((B, rows, LANES), F32),
        scratch_shapes=[pltpu.VMEM((CAPACITY * ROW_SLAB, LANES), F32),
                        pltpu.VMEM((CAPACITY * ROW_SLAB, LANES), F32)],
        compiler_params=_params(("parallel", "arbitrary"), VMEM_LIMIT),
        name="moe",
    )(idx, gate, h2, wg, wu, wd)


def _combine_kernel(x1_ref, moe_ref, mod_ref, o_ref):
    tm = x1_ref.shape[0]
    for j in range(ROW_SLAB):
        cols = slice(j * LANES, (j + 1) * LANES)
        chunk = moe_ref[pl.ds(j, tm, stride=ROW_SLAB), :]
        o_ref[:, cols] = x1_ref[:, cols] + mod_ref[5:6, cols] * chunk


def _combine(x1, moe, mod, tm=512):
    B = x1.shape[0]
    tok = pl.BlockSpec((None, tm, D_MODEL), lambda b, i: (b, i, 0))
    return pl.pallas_call(
        _combine_kernel,
        grid=(B, SEQ // tm),
        in_specs=[tok,
                  pl.BlockSpec((None, tm * ROW_SLAB, LANES), lambda b, i: (b, i, 0)),
                  pl.BlockSpec((None, N_ADA, D_MODEL), lambda b, i: (b, 0, 0))],
        out_specs=tok,
        out_shape=jax.ShapeDtypeStruct((B, SEQ, D_MODEL), F32),
        compiler_params=_params(("parallel", "parallel"), VMEM_LIMIT),
        name="combine",
    )(x1, moe, mod)


def _head_perm():
    perm = []
    for r in range(Q_PER_KV):
        for kv in range(N_KV_HEADS):
            h = kv * Q_PER_KV + r
            perm.extend(range(h * HEAD_DIM, (h + 1) * HEAD_DIM))
    return np.asarray(perm, dtype=np.int32)


def kernel(x, c, rel_bias, w_ada, b_ada, norm_mix_g, norm_ffn_g, w_in, w_fourier, b_fourier,
           q_norm_g, k_norm_g, sink, w_out, w_router, w_gate, w_up, w_down):
    B = x.shape[0]
    perm = _head_perm()
    l = 0
    mod = _ada(c, w_ada[l], b_ada[l]).reshape(B, N_ADA, D_MODEL)
    pq = _fold(w_fourier[l])
    bias = _bias_table(rel_bias)

    wi = w_in[l]
    q_cols = wi[:, FOURIER_WIDTH:FOURIER_WIDTH + ATTN_WIDTH][:, perm]
    win = jnp.concatenate([wi[:, :FOURIER_WIDTH], q_cols, wi[:, FOURIER_WIDTH + ATTN_WIDTH:]],
                          axis=1).astype(BF16)
    gq = (jnp.tile(q_norm_g[l], N_Q_HEADS) * (HEAD_DIM ** -0.5)).reshape(1, ATTN_WIDTH)
    gk = jnp.tile(k_norm_g[l], N_KV_HEADS).reshape(1, KV_WIDTH)
    a, b, q, k, v = _inproj(x, mod, norm_mix_g[l].reshape(1, D_MODEL), win, pq, gq, gk)

    cs = jnp.asarray(_seq_dft()).astype(BF16)
    yf = _fourier(cs, a, b, b_fourier[l].reshape(1, FOURIER_WIDTH))
    ya = _attn(sink[l], q, k, v, bias)

    wo = w_out[l]
    wf = wo[:FOURIER_WIDTH].astype(BF16)
    wa = wo[FOURIER_WIDTH:][perm].astype(BF16)
    wr = jnp.pad(w_router[l], ((0, 0), (0, LANES - N_EXPERTS)))
    wrh = wr.astype(BF16)
    wrl = (wr - wrh.astype(F32)).astype(BF16)
    x1, h2, aff = _outproj(yf, ya, x, mod, norm_ffn_g[l].reshape(1, D_MODEL), wf, wa, wrh, wrl)

    idx, gate = _route(aff)
    n = N_EXPERTS * CAPACITY
    moe = _moe(idx.reshape(B, 1, n), gate.reshape(B, 1, n), h2,
               w_gate[l].astype(BF16), w_up[l].astype(BF16), w_down[l].astype(BF16))
    return _combine(x1, moe, mod)
```

```python
import functools
import math

import numpy as np
import jax
import jax.numpy as jnp
from jax import lax
from jax.experimental import pallas as pl
from jax.experimental.pallas import tpu as pltpu

D_MODEL = 1024
SEQ = 2048
HEAD_DIM = 64
FOURIER_WIDTH = 512
ATTN_WIDTH = 512
N_GROUPS = 8
N_Q_HEADS = 8
Q_PER_KV = 4
N_KV_HEADS = 2
KV_WIDTH = 128
IN_PROJ_WIDTH = 1280
WINDOW = 128
BLOCK = 128
SPAN = BLOCK + 2 * WINDOW
N_BLOCKS = SEQ // BLOCK
N_BUCKETS = 32
MAX_DISTANCE = 128
N_EXPERTS = 16
CAPACITY = 2 * SEQ // N_EXPERTS
D_EXPERT = 1024
N_ADA = 6
EPS = 1e-6

LANES = 128
SUBLANES = 8
ROW_SLAB = D_MODEL // LANES
VMEM_LIMIT = 56 * 1024 * 1024

F32 = jnp.float32
BF16 = jnp.bfloat16
NEG_INF = float("-inf")


def _params(sem, vmem=None):
    return pltpu.CompilerParams(dimension_semantics=sem, vmem_limit_bytes=vmem)


@functools.lru_cache(maxsize=None)
def _seq_dft():
    s = np.arange(SEQ, dtype=np.int64)
    ph = (s[:, None] * s[None, :]) % SEQ
    ang = 2.0 * np.pi * ph.astype(np.float64) / SEQ
    sc = 1.0 / math.sqrt(SEQ)
    return np.concatenate([np.cos(ang) * sc, -np.sin(ang) * sc], axis=1).astype(np.float32)


@functools.lru_cache(maxsize=None)
def _chan_dft():
    c = np.arange(HEAD_DIM, dtype=np.int64)
    ph = (c[:, None] * c[None, :]) % HEAD_DIM
    ang = 2.0 * np.pi * ph.astype(np.float64) / HEAD_DIM
    sc = 1.0 / math.sqrt(HEAD_DIM)
    eye = np.eye(N_GROUPS)
    cbd = np.kron(eye, np.cos(ang) * sc)
    sbd = np.kron(eye, np.sin(ang) * sc)
    return cbd.astype(np.float32), sbd.astype(np.float32)


@functools.lru_cache(maxsize=None)
def _bucket_table():
    rel = np.arange(SPAN)[None, :] - WINDOW - np.arange(BLOCK)[:, None]
    half = N_BUCKETS // 2
    max_exact = half // 2
    n = np.abs(rel)
    nf = np.maximum(n, 1).astype(np.float64)
    large = max_exact + (np.log(nf / max_exact) / math.log(MAX_DISTANCE / max_exact)
                         * (half - max_exact)).astype(np.int64)
    sq = np.maximum(n.astype(np.int64) ** 2 // (max_exact * max_exact), 1)
    large_int = max_exact + np.floor(np.log2(sq.astype(np.float64)) + 1e-9).astype(np.int64)
    assert np.array_equal(np.where(n >= max_exact, large, 0), np.where(n >= max_exact, large_int, 0))
    large = np.minimum(large, half - 1)
    bucket = np.where(rel > 0, half, 0) + np.where(n < max_exact, n, large)
    return bucket.astype(np.int32)


@functools.lru_cache(maxsize=None)
def _group_ones(width):
    return np.kron(np.eye(width // HEAD_DIM), np.ones((HEAD_DIM, HEAD_DIM))).astype(np.float32)


@functools.lru_cache(maxsize=None)
def _tri_incl():
    i = np.arange(LANES)
    return (i[:, None] <= i[None, :]).astype(np.float32)


def _ada_kernel(c_ref, w_ref, b_ref, o_ref):
    c = c_ref[...]
    ca = c * (1.0 / (1.0 + jnp.exp(-c)))
    o_ref[...] = jnp.dot(ca, w_ref[...], precision=lax.Precision.HIGHEST,
                         preferred_element_type=F32) + b_ref[...]


def _ada(c, w_ada, b_ada):
    B = c.shape[0]
    n = N_ADA * D_MODEL
    tn = D_MODEL
    return pl.pallas_call(
        _ada_kernel,
        grid=(n // tn,),
        in_specs=[pl.BlockSpec((B, D_MODEL), lambda j: (0, 0)),
                  pl.BlockSpec((D_MODEL, tn), lambda j: (0, j)),
                  pl.BlockSpec((1, tn), lambda j: (0, j))],
        out_specs=pl.BlockSpec((B, tn), lambda j: (0, j)),
        out_shape=jax.ShapeDtypeStruct((B, n), F32),
        compiler_params=_params(("arbitrary",)),
        name="ada",
    )(c, w_ada, b_ada.reshape(1, n))


def _fold_kernel(cbd_ref, sbd_ref, w_ref, o_ref):
    w = w_ref[...]
    o_ref[:, :FOURIER_WIDTH] = jnp.dot(cbd_ref[...], w, precision=lax.Precision.HIGHEST,
                                       preferred_element_type=F32).astype(BF16)
    o_ref[:, FOURIER_WIDTH:] = jnp.dot(sbd_ref[...], w, precision=lax.Precision.HIGHEST,
                                       preferred_element_type=F32).astype(BF16)


def _fold(w_fourier):
    wbd = (jnp.eye(N_GROUPS, dtype=F32)[:, None, :, None] * w_fourier[:, :, None, :]
           ).reshape(FOURIER_WIDTH, FOURIER_WIDTH)
    cbd, sbd = _chan_dft()
    return pl.pallas_call(
        _fold_kernel,
        out_shape=jax.ShapeDtypeStruct((FOURIER_WIDTH, 2 * FOURIER_WIDTH), BF16),
        name="fold",
    )(jnp.asarray(cbd), jnp.asarray(sbd), wbd)


def _bias_kernel(rb_ref, bucket_ref, o_ref):
    h = pl.program_id(0)
    bk = bucket_ref[...]
    acc = jnp.zeros((BLOCK, SPAN), F32)
    for b in range(N_BUCKETS):
        acc = jnp.where(bk == b, rb_ref[b, h], acc)
    j = lax.broadcasted_iota(jnp.int32, (BLOCK, SPAN), 1)
    q = lax.broadcasted_iota(jnp.int32, (BLOCK, SPAN), 0)
    band = jnp.abs(j - WINDOW - q) <= WINDOW
    base = jnp.where(band, acc, NEG_INF)
    o_ref[0] = jnp.where(j >= WINDOW, base, NEG_INF)
    o_ref[1] = base
    o_ref[2] = jnp.where(j < WINDOW + BLOCK, base, NEG_INF)


def _bias_table(rel_bias):
    return pl.pallas_call(
        _bias_kernel,
        grid=(N_Q_HEADS,),
        in_specs=[pl.BlockSpec(memory_space=pltpu.SMEM),
                  pl.BlockSpec((BLOCK, SPAN), lambda h: (0, 0))],
        out_specs=pl.BlockSpec((3, None, BLOCK, SPAN), lambda h: (0, h, 0, 0)),
        out_shape=jax.ShapeDtypeStruct((3, N_Q_HEADS, BLOCK, SPAN), F32),
        compiler_params=_params(("arbitrary",)),
        name="bias",
    )(rel_bias, jnp.asarray(_bucket_table()))


def _inproj_kernel(x_ref, mod_ref, g_ref, win_ref, pq_ref, bdq_ref, bdk_ref, gq_ref, gk_ref,
                   a_ref, b_ref, q_ref, k_ref, v_ref):
    x = x_ref[...]
    ms = jnp.mean(x * x, axis=-1, keepdims=True)
    y = x * lax.rsqrt(ms + EPS) * g_ref[...]
    h = y * (1.0 + mod_ref[1:2, :]) + mod_ref[0:1, :]
    proj = jnp.dot(h.astype(BF16), win_ref[...], preferred_element_type=F32)
    uf = proj[:, :FOURIER_WIDTH].astype(BF16)
    ab = jnp.dot(uf, pq_ref[...], preferred_element_type=F32)
    a_ref[...] = ab[:, :FOURIER_WIDTH].astype(BF16)
    b_ref[...] = ab[:, FOURIER_WIDTH:].astype(BF16)
    q0 = FOURIER_WIDTH
    k0 = q0 + ATTN_WIDTH
    v0 = k0 + KV_WIDTH
    q = proj[:, q0:k0]
    ssq = jnp.dot((q * q).astype(BF16), bdq_ref[...], preferred_element_type=F32)
    q_ref[...] = (q * lax.rsqrt(ssq * (1.0 / HEAD_DIM) + EPS) * gq_ref[...]).astype(BF16)
    k = proj[:, k0:v0]
    ssk = jnp.dot((k * k).astype(BF16), bdk_ref[...], preferred_element_type=F32)
    k_ref[...] = (k * lax.rsqrt(ssk * (1.0 / HEAD_DIM) + EPS) * gk_ref[...]).astype(BF16)
    v_ref[...] = proj[:, v0:].astype(BF16)


def _inproj(x, mod, g, win, pq, gq, gk, tm=512):
    B = x.shape[0]
    const = lambda shape: pl.BlockSpec(shape, lambda b, i: (0,) * len(shape))
    tok = lambda w: pl.BlockSpec((None, tm, w), lambda b, i: (b, i, 0))
    sds = lambda w: jax.ShapeDtypeStruct((B, SEQ, w), BF16)
    return pl.pallas_call(
        _inproj_kernel,
        grid=(B, SEQ // tm),
        in_specs=[tok(D_MODEL),
                  pl.BlockSpec((None, N_ADA, D_MODEL), lambda b, i: (b, 0, 0)),
                  const((1, D_MODEL)),
                  const((D_MODEL, IN_PROJ_WIDTH)),
                  const((FOURIER_WIDTH, 2 * FOURIER_WIDTH)),
                  const((ATTN_WIDTH, ATTN_WIDTH)),
                  const((KV_WIDTH, KV_WIDTH)),
                  const((1, ATTN_WIDTH)),
                  const((1, KV_WIDTH))],
        out_specs=[tok(FOURIER_WIDTH), tok(FOURIER_WIDTH), tok(ATTN_WIDTH), tok(KV_WIDTH),
                   tok(KV_WIDTH)],
        out_shape=[sds(FOURIER_WIDTH), sds(FOURIER_WIDTH), sds(ATTN_WIDTH), sds(KV_WIDTH),
                   sds(KV_WIDTH)],
        compiler_params=_params(("parallel", "parallel"), VMEM_LIMIT),
        name="inproj",
    )(x, mod, g, win, pq, jnp.asarray(_group_ones(ATTN_WIDTH)).astype(BF16),
      jnp.asarray(_group_ones(KV_WIDTH)).astype(BF16), gq, gk)


FOURIER_ROWS = 512


def _fourier_kernel(cs_ref, a_ref, b_ref, bf_ref, o_ref):
    a = a_ref[...]
    b = b_ref[...]
    for i in range(SEQ // FOURIER_ROWS):
        rows = pl.ds(i * FOURIER_ROWS, FOURIER_ROWS)
        acc = jnp.dot(cs_ref[rows, :SEQ], a, preferred_element_type=F32)
        acc = acc + jnp.dot(cs_ref[rows, SEQ:], b, preferred_element_type=F32)
        o_ref[rows, :] = (acc + bf_ref[...]).astype(BF16)


def _fourier(cs, a, b, bf):
    B = a.shape[0]
    tok = pl.BlockSpec((None, SEQ, FOURIER_WIDTH), lambda i: (i, 0, 0))
    return pl.pallas_call(
        _fourier_kernel,
        grid=(B,),
        in_specs=[pl.BlockSpec((SEQ, 2 * SEQ), lambda i: (0, 0), pipeline_mode=pl.Buffered(1)),
                  tok, tok,
                  pl.BlockSpec((1, FOURIER_WIDTH), lambda i: (0, 0))],
        out_specs=tok,
        out_shape=jax.ShapeDtypeStruct((B, SEQ, FOURIER_WIDTH), BF16),
        compiler_params=_params(("parallel",), VMEM_LIMIT),
        name="fourier",
    )(cs, a, b, bf)


def _attn_kernel(sink_ref, q_ref, kl_ref, km_ref, kr_ref, vl_ref, vm_ref, vr_ref, bias_ref,
                 o_ref):
    i = pl.program_id(1)
    variant = jnp.where(i == 0, 0, jnp.where(i == N_BLOCKS - 1, 2, 1))
    keys = jnp.concatenate([kl_ref[...], km_ref[...], kr_ref[...]], axis=0)
    vals = jnp.concatenate([vl_ref[...], vm_ref[...], vr_ref[...]], axis=0)
    first_k = lax.broadcasted_iota(jnp.int32, (SPAN, 2 * HEAD_DIM), 1) < HEAD_DIM
    first_q = lax.broadcasted_iota(jnp.int32, (BLOCK, 2 * HEAD_DIM), 1) < HEAD_DIM
    qs = jnp.concatenate([q_ref[:, r * LANES:(r + 1) * LANES] for r in range(Q_PER_KV)], axis=0)
    row_head = lax.broadcasted_iota(jnp.int32, (Q_PER_KV * BLOCK, 1), 0) // BLOCK
    pvs = []
    for kv in range(N_KV_HEADS):
        half = first_k if kv == 0 else jnp.logical_not(first_k)
        keys_kv = jnp.where(half, keys, jnp.zeros_like(keys))
        logits = lax.dot_general(qs, keys_kv, (((1,), (1,)), ((), ())),
                                 preferred_element_type=F32)
        bias = bias_ref[variant, pl.ds(kv * Q_PER_KV, Q_PER_KV)]
        logits = logits + bias.reshape(Q_PER_KV * BLOCK, SPAN)
        sink = jnp.zeros((Q_PER_KV * BLOCK, 1), F32)
        for r in range(Q_PER_KV):
            sink = jnp.where(row_head == r, sink_ref[kv * Q_PER_KV + r], sink)
        m = jnp.maximum(jnp.max(logits, axis=-1, keepdims=True), sink)
        p = jnp.exp(logits - m)
        denom = jnp.sum(p, axis=-1, keepdims=True) + jnp.exp(sink - m)
        pv = jnp.dot(p.astype(BF16), vals, preferred_element_type=F32)
        pvs.append(pv / denom)
    for r in range(Q_PER_KV):
        rows = slice(r * BLOCK, (r + 1) * BLOCK)
        o_ref[:, r * LANES:(r + 1) * LANES] = jnp.where(first_q, pvs[0][rows], pvs[1][rows]
                                                        ).astype(BF16)


def _attn(sink, q, k, v, bias):
    B = q.shape[0]
    kv_spec = lambda f: pl.BlockSpec((None, BLOCK, KV_WIDTH), lambda b, i: (b, f(i), 0))
    left = lambda i: jnp.maximum(i - 1, 0)
    mid = lambda i: i
    right = lambda i: jnp.minimum(i + 1, N_BLOCKS - 1)
    qspec = pl.BlockSpec((None, BLOCK, ATTN_WIDTH), lambda b, i: (b, i, 0))
    return pl.pallas_call(
        _attn_kernel,
        grid=(B, N_BLOCKS),
        in_specs=[pl.BlockSpec(memory_space=pltpu.SMEM), qspec,
                  kv_spec(left), kv_spec(mid), kv_spec(right),
                  kv_spec(left), kv_spec(mid), kv_spec(right),
                  pl.BlockSpec((3, N_Q_HEADS, BLOCK, SPAN), lambda b, i: (0, 0, 0, 0))],
        out_specs=qspec,
        out_shape=jax.ShapeDtypeStruct((B, SEQ, ATTN_WIDTH), BF16),
        compiler_params=_params(("parallel", "parallel"), VMEM_LIMIT),
        name="attn",
    )(sink, q, k, k, k, v, v, v, bias)


def _outproj_kernel(yf_ref, ya_ref, x_ref, mod_ref, g_ref, wf_ref, wa_ref, wrh_ref, wrl_ref,
                    x1_ref, h2_ref, aff_ref):
    tm = x_ref.shape[0]
    mixed = jnp.dot(yf_ref[...], wf_ref[...], preferred_element_type=F32)
    mixed = mixed + jnp.dot(ya_ref[...], wa_ref[...], preferred_element_type=F32)
    x1 = x_ref[...] + mod_ref[2:3, :] * mixed
    x1_ref[...] = x1
    ms = jnp.mean(x1 * x1, axis=-1, keepdims=True)
    y = x1 * lax.rsqrt(ms + EPS) * g_ref[...]
    h2 = y * (1.0 + mod_ref[4:5, :]) + mod_ref[3:4, :]
    for j in range(ROW_SLAB):
        h2_ref[pl.ds(j, tm, stride=ROW_SLAB), :] = h2[:, j * LANES:(j + 1) * LANES]
    hi = h2.astype(BF16)
    lo = (h2 - hi.astype(F32)).astype(BF16)
    logits = jnp.dot(hi, wrh_ref[...], preferred_element_type=F32)
    logits = logits + jnp.dot(lo, wrh_ref[...], preferred_element_type=F32)
    logits = logits + jnp.dot(hi, wrl_ref[...], preferred_element_type=F32)
    lane = lax.broadcasted_iota(jnp.int32, (tm, LANES), 1)
    logits = jnp.where(lane < N_EXPERTS, logits, NEG_INF)
    m = jnp.max(logits, axis=-1, keepdims=True)
    e = jnp.exp(logits - m)
    aff_ref[...] = e / jnp.sum(e, axis=-1, keepdims=True)


def _outproj(yf, ya, x, mod, g, wf, wa, wrh, wrl, tm=512):
    B = x.shape[0]
    const = lambda shape: pl.BlockSpec(shape, lambda b, i: (0,) * len(shape))
    tok = lambda w: pl.BlockSpec((None, tm, w), lambda b, i: (b, i, 0))
    return pl.pallas_call(
        _outproj_kernel,
        grid=(B, SEQ // tm),
        in_specs=[tok(FOURIER_WIDTH), tok(ATTN_WIDTH), tok(D_MODEL),
                  pl.BlockSpec((None, N_ADA, D_MODEL), lambda b, i: (b, 0, 0)),
                  const((1, D_MODEL)),
                  const((FOURIER_WIDTH, D_MODEL)), const((ATTN_WIDTH, D_MODEL)),
                  const((D_MODEL, LANES)), const((D_MODEL, LANES))],
        out_specs=[tok(D_MODEL),
                   pl.BlockSpec((None, tm * ROW_SLAB, LANES), lambda b, i: (b, i, 0)),
                   tok(LANES)],
        out_shape=[jax.ShapeDtypeStruct((B, SEQ, D_MODEL), F32),
                   jax.ShapeDtypeStruct((B, SEQ * ROW_SLAB, LANES), F32),
                   jax.ShapeDtypeStruct((B, SEQ, LANES), F32)],
        compiler_params=_params(("parallel", "parallel"), VMEM_LIMIT),
        name="outproj",
    )(yf, ya, x, mod, g, wf, wa, wrh, wrl)


def _route_kernel(aff_ref, tri_ref, idx_ref, gate_ref):
    aff = aff_ref[...]
    aff_t = jnp.transpose(aff)[:N_EXPERTS]
    bits = pltpu.bitcast(aff_t, jnp.int32)
    cap = float(CAPACITY)

    t = jnp.zeros((N_EXPERTS, 1), jnp.int32)
    for bit in range(30, -1, -1):
        cand = t | (1 << bit)
        cnt = jnp.sum(jnp.where(bits >= cand, 1.0, 0.0), axis=1, keepdims=True)
        t = jnp.where(cnt >= cap, cand, t)
    gt = bits > t
    eq = bits == t
    need = cap - jnp.sum(jnp.where(gt, 1.0, 0.0), axis=1, keepdims=True)

    tri = tri_ref[...]
    n_chunks = SEQ // LANES

    def prefix(flags_f32):
        outs = []
        carry = jnp.zeros((N_EXPERTS, 1), F32)
        for c in range(n_chunks):
            f = flags_f32[:, c * LANES:(c + 1) * LANES]
            incl = jnp.dot(f.astype(BF16), tri, preferred_element_type=F32)
            outs.append(incl - f + carry)
            carry = carry + jnp.sum(f, axis=1, keepdims=True)
        return jnp.concatenate(outs, axis=1)

    eq_f = jnp.where(eq, 1.0, 0.0)
    eq_rank = prefix(eq_f)
    sel_f = jnp.where(gt, 1.0, jnp.where(eq_rank < need, eq_f, 0.0))
    pos = prefix(sel_f)
    posm = jnp.where(sel_f > 0.0, pos, -1.0)

    hi = aff.astype(BF16).astype(F32)
    r1 = aff - hi
    mid = r1.astype(BF16).astype(F32)
    lo = r1 - mid
    lane = lax.broadcasted_iota(jnp.int32, (SEQ, LANES), 1)
    tok = lax.broadcasted_iota(jnp.int32, (SEQ, LANES), 0)
    vals = hi + pltpu.roll(mid, N_EXPERTS, axis=1) + pltpu.roll(lo, 2 * N_EXPERTS, axis=1)
    vals = vals + jnp.where(lane == 3 * N_EXPERTS, (tok >> 6).astype(F32), 0.0)
    vals = vals + jnp.where(lane == 3 * N_EXPERTS + 1, (tok & 63).astype(F32), 0.0)
    vals = vals.astype(BF16)

    slot = lax.broadcasted_iota(jnp.int32, (CAPACITY, SEQ), 0).astype(F32)
    lane_c = lax.broadcasted_iota(jnp.int32, (CAPACITY, LANES), 1)
    w_idx = jnp.where(lane_c == 3 * N_EXPERTS, 64.0,
                      jnp.where(lane_c == 3 * N_EXPERTS + 1, 1.0, 0.0))
    for e in range(N_EXPERTS):
        onehot = jnp.where(posm[e:e + 1, :] == slot, 1.0, 0.0).astype(BF16)
        res = jnp.dot(onehot, vals, preferred_element_type=F32)
        w_g = jnp.where((lane_c == e) | (lane_c == N_EXPERTS + e) | (lane_c == 2 * N_EXPERTS + e),
                        1.0, 0.0)
        rows = pl.ds(e * CAPACITY, CAPACITY)
        idx_ref[rows, :] = jnp.sum(res * w_idx, axis=1, keepdims=True).astype(jnp.int32) * ROW_SLAB
        gate_ref[rows, :] = jnp.sum(res * w_g, axis=1, keepdims=True)


def _route(aff):
    B = aff.shape[0]
    n = N_EXPERTS * CAPACITY
    return pl.pallas_call(
        _route_kernel,
        grid=(B,),
        in_specs=[pl.BlockSpec((None, SEQ, LANES), lambda b: (b, 0, 0)),
                  pl.BlockSpec((LANES, LANES), lambda b: (0, 0))],
        out_specs=[pl.BlockSpec((None, n, 1), lambda b: (b, 0, 0)),
                   pl.BlockSpec((None, n, 1), lambda b: (b, 0, 0))],
        out_shape=[jax.ShapeDtypeStruct((B, n, 1), jnp.int32),
                   jax.ShapeDtypeStruct((B, n, 1), F32)],
        compiler_params=_params(("parallel",), VMEM_LIMIT),
        name="route",
    )(aff, jnp.asarray(_tri_incl()).astype(BF16))


SCATTER_UNROLL = 8


def _moe_kernel(idx_ref, gate_ref, h2_ref, wg_ref, wu_ref, wd_ref, acc_ref,
                xin0_ref, xin1_ref, y0_ref, y1_ref):
    e = pl.program_id(1)
    last = N_EXPERTS - 1

    def gather_rows(ex, dst_ref):
        base = ex * CAPACITY
        for p in range(CAPACITY):
            src = pl.multiple_of(idx_ref[0, base + p], ROW_SLAB)
            dst_ref[pl.ds(p * ROW_SLAB, ROW_SLAB), :] = h2_ref[pl.ds(src, ROW_SLAB), :]

    def scatter_rows(ex, src_ref):
        base = ex * CAPACITY
        for g in range(CAPACITY // SCATTER_UNROLL):
            new = []
            for u in range(SCATTER_UNROLL):
                p = g * SCATTER_UNROLL + u
                dst = pl.multiple_of(idx_ref[0, base + p], ROW_SLAB)
                new.append((dst, acc_ref[pl.ds(dst, ROW_SLAB), :]
                            + src_ref[pl.ds(p * ROW_SLAB, ROW_SLAB), :]))
            for dst, val in new:
                acc_ref[pl.ds(dst, ROW_SLAB), :] = val

    def expert(xin_ref, y_ref):
        xin = jnp.concatenate(
            [xin_ref[pl.ds(j, CAPACITY, stride=ROW_SLAB), :].astype(BF16)
             for j in range(ROW_SLAB)], axis=1)
        a = jnp.dot(xin, wg_ref[...], preferred_element_type=F32)
        u = jnp.dot(xin, wu_ref[...], preferred_element_type=F32)
        hmid = (a * (1.0 / (1.0 + jnp.exp(-a))) * u).astype(BF16)
        y = jnp.dot(hmid, wd_ref[...], preferred_element_type=F32) * gate_ref[...]
        for j in range(ROW_SLAB):
            y_ref[pl.ds(j, CAPACITY, stride=ROW_SLAB), :] = y[:, j * LANES:(j + 1) * LANES]

    @pl.when(e == 0)
    def _():
        acc_ref[...] = jnp.zeros_like(acc_ref)
        y1_ref[...] = jnp.zeros_like(y1_ref)
        gather_rows(0, xin0_ref)

    def step(xin_cur, xin_nxt, y_cur, y_prv):
        scatter_rows(jnp.maximum(e - 1, 0), y_prv)
        gather_rows(jnp.minimum(e + 1, last), xin_nxt)
        expert(xin_cur, y_cur)

    @pl.when(e % 2 == 0)
    def _():
        step(xin0_ref, xin1_ref, y0_ref, y1_ref)

    @pl.when(e % 2 == 1)
    def _():
        step(xin1_ref, xin0_ref, y1_ref, y0_ref)

    @pl.when(e == last)
    def _():
        scatter_rows(last, y1_ref if last % 2 == 1 else y0_ref)


def _moe(idx, gate, h2, wg, wu, wd):
    B = h2.shape[0]
    n = N_EXPERTS * CAPACITY
    rows = SEQ * ROW_SLAB
    smem = pl.BlockSpec((None, 1, n), lambda b, e: (b, 0, 0), memory_space=pltpu.SMEM)
    wspec = lambda r, c: pl.BlockSpec((None, r, c), lambda b, e: (e, 0, 0))
    slab = pl.BlockSpec((None, rows, LANES), lambda b, e: (b, 0, 0))
    stage = pltpu.VMEM((CAPACITY * ROW_SLAB, LANES), F32)
    return pl.pallas_call(
        _moe_kernel,
        grid=(B, N_EXPERTS),
        in_specs=[smem, pl.BlockSpec((None, CAPACITY, 1), lambda b, e: (b, e, 0)), slab,
                  wspec(D_MODEL, D_EXPERT), wspec(D_MODEL, D_EXPERT), wspec(D_EXPERT, D_MODEL)],
        out_specs=slab,
        out_shape=jax.ShapeDtypeStruct((B, rows, LANES), F32),
        scratch_shapes=[stage, stage, stage, stage],
        compiler_params=_params(("parallel", "arbitrary"), VMEM_LIMIT),
        name="moe",
    )(idx, gate, h2, wg, wu, wd)


def _combine_kernel(x1_ref, moe_ref, mod_ref, o_ref):
    tm = x1_ref.shape[0]
    for j in range(ROW_SLAB):
        cols = slice(j * LANES, (j + 1) * LANES)
        chunk = moe_ref[pl.ds(j, tm, stride=ROW_SLAB), :]
        o_ref[:, cols] = x1_ref[:, cols] + mod_ref[5:6, cols] * chunk


def _combine(x1, moe, mod, tm=512):
    B = x1.shape[0]
    tok = pl.BlockSpec((None, tm, D_MODEL), lambda b, i: (b, i, 0))
    return pl.pallas_call(
        _combine_kernel,
        grid=(B, SEQ // tm),
        in_specs=[tok,
                  pl.BlockSpec((None, tm * ROW_SLAB, LANES), lambda b, i: (b, i, 0)),
                  pl.BlockSpec((None, N_ADA, D_MODEL), lambda b, i: (b, 0, 0))],
        out_specs=tok,
        out_shape=jax.ShapeDtypeStruct((B, SEQ, D_MODEL), F32),
        compiler_params=_params(("parallel", "parallel"), VMEM_LIMIT),
        name="combine",
    )(x1, moe, mod)


def _head_perm():
    perm = []
    for r in range(Q_PER_KV):
        for kv in range(N_KV_HEADS):
            h = kv * Q_PER_KV + r
            perm.extend(range(h * HEAD_DIM, (h + 1) * HEAD_DIM))
    return np.asarray(perm, dtype=np.int32)


def kernel(x, c, rel_bias, w_ada, b_ada, norm_mix_g, norm_ffn_g, w_in, w_fourier, b_fourier,
           q_norm_g, k_norm_g, sink, w_out, w_router, w_gate, w_up, w_down):
    B = x.shape[0]
    perm = _head_perm()
    l = 0
    mod = _ada(c, w_ada[l], b_ada[l]).reshape(B, N_ADA, D_MODEL)
    pq = _fold(w_fourier[l])
    bias = _bias_table(rel_bias)

    wi = w_in[l]
    q_cols = wi[:, FOURIER_WIDTH:FOURIER_WIDTH + ATTN_WIDTH][:, perm]
    win = jnp.concatenate([wi[:, :FOURIER_WIDTH], q_cols, wi[:, FOURIER_WIDTH + ATTN_WIDTH:]],
                          axis=1).astype(BF16)
    gq = (jnp.tile(q_norm_g[l], N_Q_HEADS) * (HEAD_DIM ** -0.5)).reshape(1, ATTN_WIDTH)
    gk = jnp.tile(k_norm_g[l], N_KV_HEADS).reshape(1, KV_WIDTH)
    a, b, q, k, v = _inproj(x, mod, norm_mix_g[l].reshape(1, D_MODEL), win, pq, gq, gk)

    cs = jnp.asarray(_seq_dft()).astype(BF16)
    yf = _fourier(cs, a, b, b_fourier[l].reshape(1, FOURIER_WIDTH))
    ya = _attn(sink[l], q, k, v, bias)

    wo = w_out[l]
    wf = wo[:FOURIER_WIDTH].astype(BF16)
    wa = wo[FOURIER_WIDTH:][perm].astype(BF16)
    wr = jnp.pad(w_router[l], ((0, 0), (0, LANES - N_EXPERTS)))
    wrh = wr.astype(BF16)
    wrl = (wr - wrh.astype(F32)).astype(BF16)
    x1, h2, aff = _outproj(yf, ya, x, mod, norm_ffn_g[l].reshape(1, D_MODEL), wf, wa, wrh, wrl)

    idx, gate = _route(aff)
    n = N_EXPERTS * CAPACITY
    moe = _moe(idx.reshape(B, 1, n), gate, h2,
               w_gate[l].astype(BF16), w_up[l].astype(BF16), w_down[l].astype(BF16))
    return _combine(x1, moe, mod)
```

```python
import functools
import math

import numpy as np
import jax
import jax.numpy as jnp
from jax import lax
from jax.experimental import pallas as pl
from jax.experimental.pallas import tpu as pltpu

D_MODEL = 1024
SEQ = 2048
HEAD_DIM = 64
FOURIER_WIDTH = 512
ATTN_WIDTH = 512
N_GROUPS = 8
N_Q_HEADS = 8
Q_PER_KV = 4
N_KV_HEADS = 2
KV_WIDTH = 128
IN_PROJ_WIDTH = 1280
WINDOW = 128
BLOCK = 128
SPAN = BLOCK + 2 * WINDOW
N_BLOCKS = SEQ // BLOCK
N_BUCKETS = 32
MAX_DISTANCE = 128
N_EXPERTS = 16
CAPACITY = 2 * SEQ // N_EXPERTS
D_EXPERT = 1024
N_ADA = 6
EPS = 1e-6

LANES = 128
SUBLANES = 8
ROW_SLAB = D_MODEL // LANES
VMEM_LIMIT = 56 * 1024 * 1024

F32 = jnp.float32
BF16 = jnp.bfloat16
NEG_INF = float("-inf")


def _params(sem, vmem=None):
    return pltpu.CompilerParams(dimension_semantics=sem, vmem_limit_bytes=vmem)


@functools.lru_cache(maxsize=None)
def _seq_dft():
    s = np.arange(SEQ, dtype=np.int64)
    ph = (s[:, None] * s[None, :]) % SEQ
    ang = 2.0 * np.pi * ph.astype(np.float64) / SEQ
    sc = 1.0 / math.sqrt(SEQ)
    return np.concatenate([np.cos(ang) * sc, -np.sin(ang) * sc], axis=1).astype(np.float32)


@functools.lru_cache(maxsize=None)
def _chan_dft():
    c = np.arange(HEAD_DIM, dtype=np.int64)
    ph = (c[:, None] * c[None, :]) % HEAD_DIM
    ang = 2.0 * np.pi * ph.astype(np.float64) / HEAD_DIM
    sc = 1.0 / math.sqrt(HEAD_DIM)
    eye = np.eye(N_GROUPS)
    cbd = np.kron(eye, np.cos(ang) * sc)
    sbd = np.kron(eye, np.sin(ang) * sc)
    return cbd.astype(np.float32), sbd.astype(np.float32)


@functools.lru_cache(maxsize=None)
def _bucket_table():
    rel = np.arange(SPAN)[None, :] - WINDOW - np.arange(BLOCK)[:, None]
    half = N_BUCKETS // 2
    max_exact = half // 2
    n = np.abs(rel)
    nf = np.maximum(n, 1).astype(np.float64)
    large = max_exact + (np.log(nf / max_exact) / math.log(MAX_DISTANCE / max_exact)
                         * (half - max_exact)).astype(np.int64)
    sq = np.maximum(n.astype(np.int64) ** 2 // (max_exact * max_exact), 1)
    large_int = max_exact + np.floor(np.log2(sq.astype(np.float64)) + 1e-9).astype(np.int64)
    assert np.array_equal(np.where(n >= max_exact, large, 0), np.where(n >= max_exact, large_int, 0))
    large = np.minimum(large, half - 1)
    bucket = np.where(rel > 0, half, 0) + np.where(n < max_exact, n, large)
    return bucket.astype(np.int32)


@functools.lru_cache(maxsize=None)
def _group_ones(width):
    return np.kron(np.eye(width // HEAD_DIM), np.ones((HEAD_DIM, HEAD_DIM))).astype(np.float32)


@functools.lru_cache(maxsize=None)
def _tri_incl():
    i = np.arange(LANES)
    return (i[:, None] <= i[None, :]).astype(np.float32)


def _ada_kernel(c_ref, w_ref, b_ref, o_ref):
    c = c_ref[...]
    ca = c * (1.0 / (1.0 + jnp.exp(-c)))
    o_ref[...] = jnp.dot(ca, w_ref[...], precision=lax.Precision.HIGHEST,
                         preferred_element_type=F32) + b_ref[...]


def _ada(c, w_ada, b_ada):
    B = c.shape[0]
    n = N_ADA * D_MODEL
    tn = D_MODEL
    return pl.pallas_call(
        _ada_kernel,
        grid=(n // tn,),
        in_specs=[pl.BlockSpec((B, D_MODEL), lambda j: (0, 0)),
                  pl.BlockSpec((D_MODEL, tn), lambda j: (0, j)),
                  pl.BlockSpec((1, tn), lambda j: (0, j))],
        out_specs=pl.BlockSpec((B, tn), lambda j: (0, j)),
        out_shape=jax.ShapeDtypeStruct((B, n), F32),
        compiler_params=_params(("arbitrary",)),
        name="ada",
    )(c, w_ada, b_ada.reshape(1, n))


def _fold_kernel(cbd_ref, sbd_ref, w_ref, o_ref):
    w = w_ref[...]
    o_ref[:, :FOURIER_WIDTH] = jnp.dot(cbd_ref[...], w, precision=lax.Precision.HIGHEST,
                                       preferred_element_type=F32).astype(BF16)
    o_ref[:, FOURIER_WIDTH:] = jnp.dot(sbd_ref[...], w, precision=lax.Precision.HIGHEST,
                                       preferred_element_type=F32).astype(BF16)


def _fold(w_fourier):
    wbd = (jnp.eye(N_GROUPS, dtype=F32)[:, None, :, None] * w_fourier[:, :, None, :]
           ).reshape(FOURIER_WIDTH, FOURIER_WIDTH)
    cbd, sbd = _chan_dft()
    return pl.pallas_call(
        _fold_kernel,
        out_shape=jax.ShapeDtypeStruct((FOURIER_WIDTH, 2 * FOURIER_WIDTH), BF16),
        name="fold",
    )(jnp.asarray(cbd), jnp.asarray(sbd), wbd)


def _bias_kernel(rb_ref, bucket_ref, o_ref):
    h = pl.program_id(0)
    bk = bucket_ref[...]
    acc = jnp.zeros((BLOCK, SPAN), F32)
    for b in range(N_BUCKETS):
        acc = jnp.where(bk == b, rb_ref[b, h], acc)
    j = lax.broadcasted_iota(jnp.int32, (BLOCK, SPAN), 1)
    q = lax.broadcasted_iota(jnp.int32, (BLOCK, SPAN), 0)
    band = jnp.abs(j - WINDOW - q) <= WINDOW
    base = jnp.where(band, acc, NEG_INF)
    o_ref[0] = jnp.where(j >= WINDOW, base, NEG_INF)
    o_ref[1] = base
    o_ref[2] = jnp.where(j < WINDOW + BLOCK, base, NEG_INF)


def _bias_table(rel_bias):
    return pl.pallas_call(
        _bias_kernel,
        grid=(N_Q_HEADS,),
        in_specs=[pl.BlockSpec(memory_space=pltpu.SMEM),
                  pl.BlockSpec((BLOCK, SPAN), lambda h: (0, 0))],
        out_specs=pl.BlockSpec((3, None, BLOCK, SPAN), lambda h: (0, h, 0, 0)),
        out_shape=jax.ShapeDtypeStruct((3, N_Q_HEADS, BLOCK, SPAN), F32),
        compiler_params=_params(("arbitrary",)),
        name="bias",
    )(rel_bias, jnp.asarray(_bucket_table()))


def _inproj_kernel(x_ref, mod_ref, g_ref, win_ref, pq_ref, bdq_ref, bdk_ref, gq_ref, gk_ref,
                   a_ref, b_ref, q_ref, k_ref, v_ref):
    x = x_ref[...]
    ms = jnp.mean(x * x, axis=-1, keepdims=True)
    y = x * lax.rsqrt(ms + EPS) * g_ref[...]
    h = y * (1.0 + mod_ref[1:2, :]) + mod_ref[0:1, :]
    proj = jnp.dot(h.astype(BF16), win_ref[...], preferred_element_type=F32)
    uf = proj[:, :FOURIER_WIDTH].astype(BF16)
    ab = jnp.dot(uf, pq_ref[...], preferred_element_type=F32)
    a_ref[...] = ab[:, :FOURIER_WIDTH].astype(BF16)
    b_ref[...] = ab[:, FOURIER_WIDTH:].astype(BF16)
    q0 = FOURIER_WIDTH
    k0 = q0 + ATTN_WIDTH
    v0 = k0 + KV_WIDTH
    q = proj[:, q0:k0]
    ssq = jnp.dot((q * q).astype(BF16), bdq_ref[...], preferred_element_type=F32)
    q_ref[...] = (q * lax.rsqrt(ssq * (1.0 / HEAD_DIM) + EPS) * gq_ref[...]).astype(BF16)
    k = proj[:, k0:v0]
    ssk = jnp.dot((k * k).astype(BF16), bdk_ref[...], preferred_element_type=F32)
    k_ref[...] = (k * lax.rsqrt(ssk * (1.0 / HEAD_DIM) + EPS) * gk_ref[...]).astype(BF16)
    v_ref[...] = proj[:, v0:].astype(BF16)


def _inproj(x, mod, g, win, pq, gq, gk, tm=512):
    B = x.shape[0]
    const = lambda shape: pl.BlockSpec(shape, lambda b, i: (0,) * len(shape))
    tok = lambda w: pl.BlockSpec((None, tm, w), lambda b, i: (b, i, 0))
    sds = lambda w: jax.ShapeDtypeStruct((B, SEQ, w), BF16)
    return pl.pallas_call(
        _inproj_kernel,
        grid=(B, SEQ // tm),
        in_specs=[tok(D_MODEL),
                  pl.BlockSpec((None, N_ADA, D_MODEL), lambda b, i: (b, 0, 0)),
                  const((1, D_MODEL)),
                  const((D_MODEL, IN_PROJ_WIDTH)),
                  const((FOURIER_WIDTH, 2 * FOURIER_WIDTH)),
                  const((ATTN_WIDTH, ATTN_WIDTH)),
                  const((KV_WIDTH, KV_WIDTH)),
                  const((1, ATTN_WIDTH)),
                  const((1, KV_WIDTH))],
        out_specs=[tok(FOURIER_WIDTH), tok(FOURIER_WIDTH), tok(ATTN_WIDTH), tok(KV_WIDTH),
                   tok(KV_WIDTH)],
        out_shape=[sds(FOURIER_WIDTH), sds(FOURIER_WIDTH), sds(ATTN_WIDTH), sds(KV_WIDTH),
                   sds(KV_WIDTH)],
        compiler_params=_params(("parallel", "parallel"), VMEM_LIMIT),
        name="inproj",
    )(x, mod, g, win, pq, jnp.asarray(_group_ones(ATTN_WIDTH)).astype(BF16),
      jnp.asarray(_group_ones(KV_WIDTH)).astype(BF16), gq, gk)


FOURIER_ROWS = 512


def _fourier_kernel(cs_ref, a_ref, b_ref, bf_ref, o_ref):
    a = a_ref[...]
    b = b_ref[...]
    for i in range(SEQ // FOURIER_ROWS):
        rows = pl.ds(i * FOURIER_ROWS, FOURIER_ROWS)
        acc = jnp.dot(cs_ref[rows, :SEQ], a, preferred_element_type=F32)
        acc = acc + jnp.dot(cs_ref[rows, SEQ:], b, preferred_element_type=F32)
        o_ref[rows, :] = (acc + bf_ref[...]).astype(BF16)


def _fourier(cs, a, b, bf):
    B = a.shape[0]
    tok = pl.BlockSpec((None, SEQ, FOURIER_WIDTH), lambda i: (i, 0, 0))
    return pl.pallas_call(
        _fourier_kernel,
        grid=(B,),
        in_specs=[pl.BlockSpec((SEQ, 2 * SEQ), lambda i: (0, 0), pipeline_mode=pl.Buffered(1)),
                  tok, tok,
                  pl.BlockSpec((1, FOURIER_WIDTH), lambda i: (0, 0))],
        out_specs=tok,
        out_shape=jax.ShapeDtypeStruct((B, SEQ, FOURIER_WIDTH), BF16),
        compiler_params=_params(("parallel",), VMEM_LIMIT),
        name="fourier",
    )(cs, a, b, bf)


def _attn_kernel(sink_ref, q_ref, kl_ref, km_ref, kr_ref, vl_ref, vm_ref, vr_ref, bias_ref,
                 o_ref):
    i = pl.program_id(1)
    variant = jnp.where(i == 0, 0, jnp.where(i == N_BLOCKS - 1, 2, 1))
    keys = jnp.concatenate([kl_ref[...], km_ref[...], kr_ref[...]], axis=0)
    vals = jnp.concatenate([vl_ref[...], vm_ref[...], vr_ref[...]], axis=0)
    first_k = lax.broadcasted_iota(jnp.int32, (SPAN, 2 * HEAD_DIM), 1) < HEAD_DIM
    first_q = lax.broadcasted_iota(jnp.int32, (BLOCK, 2 * HEAD_DIM), 1) < HEAD_DIM
    qs = jnp.concatenate([q_ref[:, r * LANES:(r + 1) * LANES] for r in range(Q_PER_KV)], axis=0)
    row_head = lax.broadcasted_iota(jnp.int32, (Q_PER_KV * BLOCK, 1), 0) // BLOCK
    pvs = []
    for kv in range(N_KV_HEADS):
        half = first_k if kv == 0 else jnp.logical_not(first_k)
        keys_kv = jnp.where(half, keys, jnp.zeros_like(keys))
        logits = lax.dot_general(qs, keys_kv, (((1,), (1,)), ((), ())),
                                 preferred_element_type=F32)
        bias = bias_ref[variant, pl.ds(kv * Q_PER_KV, Q_PER_KV)]
        logits = logits + bias.reshape(Q_PER_KV * BLOCK, SPAN)
        sink = jnp.zeros((Q_PER_KV * BLOCK, 1), F32)
        for r in range(Q_PER_KV):
            sink = jnp.where(row_head == r, sink_ref[kv * Q_PER_KV + r], sink)
        m = jnp.maximum(jnp.max(logits, axis=-1, keepdims=True), sink)
        p = jnp.exp(logits - m)
        denom = jnp.sum(p, axis=-1, keepdims=True) + jnp.exp(sink - m)
        pv = jnp.dot(p.astype(BF16), vals, preferred_element_type=F32)
        pvs.append(pv / denom)
    for r in range(Q_PER_KV):
        rows = slice(r * BLOCK, (r + 1) * BLOCK)
        o_ref[:, r * LANES:(r + 1) * LANES] = jnp.where(first_q, pvs[0][rows], pvs[1][rows]
                                                        ).astype(BF16)


def _attn(sink, q, k, v, bias):
    B = q.shape[0]
    kv_spec = lambda f: pl.BlockSpec((None, BLOCK, KV_WIDTH), lambda b, i: (b, f(i), 0))
    left = lambda i: jnp.maximum(i - 1, 0)
    mid = lambda i: i
    right = lambda i: jnp.minimum(i + 1, N_BLOCKS - 1)
    qspec = pl.BlockSpec((None, BLOCK, ATTN_WIDTH), lambda b, i: (b, i, 0))
    return pl.pallas_call(
        _attn_kernel,
        grid=(B, N_BLOCKS),
        in_specs=[pl.BlockSpec(memory_space=pltpu.SMEM), qspec,
                  kv_spec(left), kv_spec(mid), kv_spec(right),
                  kv_spec(left), kv_spec(mid), kv_spec(right),
                  pl.BlockSpec((3, N_Q_HEADS, BLOCK, SPAN), lambda b, i: (0, 0, 0, 0))],
        out_specs=qspec,
        out_shape=jax.ShapeDtypeStruct((B, SEQ, ATTN_WIDTH), BF16),
        compiler_params=_params(("parallel", "parallel"), VMEM_LIMIT),
        name="attn",
    )(sink, q, k, k, k, v, v, v, bias)


OUT_ROWS = 256


def _outproj_kernel(yf_ref, ya_ref, x_ref, mod_ref, g_ref, wf_ref, wa_ref, wrh_ref, wrl_ref,
                    x1_ref, h2_ref, aff_ref):
    tm = x_ref.shape[0]
    gain = g_ref[...] * (1.0 + mod_ref[4:5, :])
    shift = mod_ref[3:4, :]
    gate1 = mod_ref[2:3, :]
    lane = lax.broadcasted_iota(jnp.int32, (OUT_ROWS, LANES), 1)
    for c in range(tm // OUT_ROWS):
        rows = pl.ds(c * OUT_ROWS, OUT_ROWS)
        mixed = jnp.dot(yf_ref[rows, :], wf_ref[...], preferred_element_type=F32)
        mixed = mixed + jnp.dot(ya_ref[rows, :], wa_ref[...], preferred_element_type=F32)
        x1 = x_ref[rows, :] + gate1 * mixed
        x1_ref[rows, :] = x1
        ms = jnp.mean(x1 * x1, axis=-1, keepdims=True)
        h2 = x1 * lax.rsqrt(ms + EPS) * gain + shift
        for j in range(ROW_SLAB):
            h2_ref[pl.ds(c * OUT_ROWS * ROW_SLAB + j, OUT_ROWS, stride=ROW_SLAB), :] = (
                h2[:, j * LANES:(j + 1) * LANES])
        hi = h2.astype(BF16)
        lo = (h2 - hi.astype(F32)).astype(BF16)
        part = jnp.dot(hi, wrh_ref[...], preferred_element_type=F32)
        part = part + jnp.dot(lo, wrl_ref[...], preferred_element_type=F32)
        logits = part + pltpu.roll(part, LANES - N_EXPERTS, axis=1)
        logits = jnp.where(lane < N_EXPERTS, logits, NEG_INF)
        m = jnp.max(logits, axis=-1, keepdims=True)
        e = jnp.exp(logits - m)
        aff_ref[rows, :] = e / jnp.sum(e, axis=-1, keepdims=True)


def _outproj(yf, ya, x, mod, g, wf, wa, wrh, wrl, tm=512):
    B = x.shape[0]
    const = lambda shape: pl.BlockSpec(shape, lambda b, i: (0,) * len(shape))
    tok = lambda w: pl.BlockSpec((None, tm, w), lambda b, i: (b, i, 0))
    return pl.pallas_call(
        _outproj_kernel,
        grid=(B, SEQ // tm),
        in_specs=[tok(FOURIER_WIDTH), tok(ATTN_WIDTH), tok(D_MODEL),
                  pl.BlockSpec((None, N_ADA, D_MODEL), lambda b, i: (b, 0, 0)),
                  const((1, D_MODEL)),
                  const((FOURIER_WIDTH, D_MODEL)), const((ATTN_WIDTH, D_MODEL)),
                  const((D_MODEL, LANES)), const((D_MODEL, LANES))],
        out_specs=[tok(D_MODEL),
                   pl.BlockSpec((None, tm * ROW_SLAB, LANES), lambda b, i: (b, i, 0)),
                   tok(LANES)],
        out_shape=[jax.ShapeDtypeStruct((B, SEQ, D_MODEL), F32),
                   jax.ShapeDtypeStruct((B, SEQ * ROW_SLAB, LANES), F32),
                   jax.ShapeDtypeStruct((B, SEQ, LANES), F32)],
        compiler_params=_params(("parallel", "parallel"), VMEM_LIMIT),
        name="outproj",
    )(yf, ya, x, mod, g, wf, wa, wrh, wrl)


def _route_kernel(aff_ref, tri_ref, idx_ref, gate_ref):
    aff = aff_ref[...]
    aff_t = jnp.transpose(aff)[:N_EXPERTS]
    bits = pltpu.bitcast(aff_t, jnp.int32)
    cap = float(CAPACITY)

    t = jnp.zeros((N_EXPERTS, 1), jnp.int32)
    for bit in range(30, -1, -1):
        cand = t | (1 << bit)
        cnt = jnp.sum(jnp.where(bits >= cand, 1.0, 0.0), axis=1, keepdims=True)
        t = jnp.where(cnt >= cap, cand, t)
    gt = bits > t
    eq = bits == t
    need = cap - jnp.sum(jnp.where(gt, 1.0, 0.0), axis=1, keepdims=True)

    tri = tri_ref[...]
    n_chunks = SEQ // LANES

    def prefix(flags_f32):
        outs = []
        carry = jnp.zeros((N_EXPERTS, 1), F32)
        for c in range(n_chunks):
            f = flags_f32[:, c * LANES:(c + 1) * LANES]
            incl = jnp.dot(f.astype(BF16), tri, preferred_element_type=F32)
            outs.append(incl - f + carry)
            carry = carry + jnp.sum(f, axis=1, keepdims=True)
        return jnp.concatenate(outs, axis=1)

    eq_f = jnp.where(eq, 1.0, 0.0)
    eq_rank = prefix(eq_f)
    sel_f = jnp.where(gt, 1.0, jnp.where(eq_rank < need, eq_f, 0.0))
    pos = prefix(sel_f)
    posm = jnp.where(sel_f > 0.0, pos, -1.0)

    hi = aff.astype(BF16).astype(F32)
    r1 = aff - hi
    mid = r1.astype(BF16).astype(F32)
    lo = r1 - mid
    lane = lax.broadcasted_iota(jnp.int32, (SEQ, LANES), 1)
    tok = lax.broadcasted_iota(jnp.int32, (SEQ, LANES), 0)
    vals = hi + pltpu.roll(mid, N_EXPERTS, axis=1) + pltpu.roll(lo, 2 * N_EXPERTS, axis=1)
    vals = vals + jnp.where(lane == 3 * N_EXPERTS, (tok >> 6).astype(F32), 0.0)
    vals = vals + jnp.where(lane == 3 * N_EXPERTS + 1, (tok & 63).astype(F32), 0.0)
    vals = vals.astype(BF16)

    slot = lax.broadcasted_iota(jnp.int32, (CAPACITY, SEQ), 0).astype(F32).astype(BF16)
    posm_b = posm.astype(BF16)
    one_b = jnp.ones((CAPACITY, SEQ), BF16)
    zero_b = jnp.zeros((CAPACITY, SEQ), BF16)
    lane_c = lax.broadcasted_iota(jnp.int32, (CAPACITY, LANES), 1)
    w_idx = jnp.where(lane_c == 3 * N_EXPERTS, 64.0,
                      jnp.where(lane_c == 3 * N_EXPERTS + 1, 1.0, 0.0))
    for e in range(N_EXPERTS):
        onehot = jnp.where(posm_b[e:e + 1, :] == slot, one_b, zero_b)
        res = jnp.dot(onehot, vals, preferred_element_type=F32)
        w_g = jnp.where((lane_c == e) | (lane_c == N_EXPERTS + e) | (lane_c == 2 * N_EXPERTS + e),
                        1.0, 0.0)
        rows = pl.ds(e * CAPACITY, CAPACITY)
        idx_ref[rows, :] = jnp.sum(res * w_idx, axis=1, keepdims=True).astype(jnp.int32) * ROW_SLAB
        gate_ref[rows, :] = jnp.sum(res * w_g, axis=1, keepdims=True)


def _route(aff):
    B = aff.shape[0]
    n = N_EXPERTS * CAPACITY
    return pl.pallas_call(
        _route_kernel,
        grid=(B,),
        in_specs=[pl.BlockSpec((None, SEQ, LANES), lambda b: (b, 0, 0)),
                  pl.BlockSpec((LANES, LANES), lambda b: (0, 0))],
        out_specs=[pl.BlockSpec((None, n, 1), lambda b: (b, 0, 0)),
                   pl.BlockSpec((None, n, 1), lambda b: (b, 0, 0))],
        out_shape=[jax.ShapeDtypeStruct((B, n, 1), jnp.int32),
                   jax.ShapeDtypeStruct((B, n, 1), F32)],
        compiler_params=_params(("parallel",), VMEM_LIMIT),
        name="route",
    )(aff, jnp.asarray(_tri_incl()).astype(BF16))


SCATTER_UNROLL = 8


def _moe_kernel(idx_ref, gate_ref, h2_ref, wg_ref, wu_ref, wd_ref, acc_ref,
                xin0_ref, xin1_ref, y0_ref, y1_ref):
    e = pl.program_id(1)
    last = N_EXPERTS - 1

    def gather_rows(ex, dst_ref):
        base = ex * CAPACITY
        for p in range(CAPACITY):
            src = pl.multiple_of(idx_ref[0, base + p], ROW_SLAB)
            dst_ref[pl.ds(p * ROW_SLAB, ROW_SLAB), :] = h2_ref[pl.ds(src, ROW_SLAB), :]

    def scatter_rows(ex, src_ref):
        base = ex * CAPACITY
        for g in range(CAPACITY // SCATTER_UNROLL):
            new = []
            for u in range(SCATTER_UNROLL):
                p = g * SCATTER_UNROLL + u
                dst = pl.multiple_of(idx_ref[0, base + p], ROW_SLAB)
                new.append((dst, acc_ref[pl.ds(dst, ROW_SLAB), :]
                            + src_ref[pl.ds(p * ROW_SLAB, ROW_SLAB), :]))
            for dst, val in new:
                acc_ref[pl.ds(dst, ROW_SLAB), :] = val

    def expert(xin_ref, y_ref):
        xin = jnp.concatenate(
            [xin_ref[pl.ds(j, CAPACITY, stride=ROW_SLAB), :].astype(BF16)
             for j in range(ROW_SLAB)], axis=1)
        a = jnp.dot(xin, wg_ref[...], preferred_element_type=F32)
        u = jnp.dot(xin, wu_ref[...], preferred_element_type=F32)
        hmid = (a * (1.0 / (1.0 + jnp.exp(-a))) * u).astype(BF16)
        y = jnp.dot(hmid, wd_ref[...], preferred_element_type=F32) * gate_ref[...]
        for j in range(ROW_SLAB):
            y_ref[pl.ds(j, CAPACITY, stride=ROW_SLAB), :] = y[:, j * LANES:(j + 1) * LANES]

    @pl.when(e == 0)
    def _():
        acc_ref[...] = jnp.zeros_like(acc_ref)
        y1_ref[...] = jnp.zeros_like(y1_ref)
        gather_rows(0, xin0_ref)

    def step(xin_cur, xin_nxt, y_cur, y_prv):
        scatter_rows(jnp.maximum(e - 1, 0), y_prv)
        gather_rows(jnp.minimum(e + 1, last), xin_nxt)
        expert(xin_cur, y_cur)

    @pl.when(e % 2 == 0)
    def _():
        step(xin0_ref, xin1_ref, y0_ref, y1_ref)

    @pl.when(e % 2 == 1)
    def _():
        step(xin1_ref, xin0_ref, y1_ref, y0_ref)

    @pl.when(e == last)
    def _():
        scatter_rows(last, y1_ref if last % 2 == 1 else y0_ref)


def _moe(idx, gate, h2, wg, wu, wd):
    B = h2.shape[0]
    n = N_EXPERTS * CAPACITY
    rows = SEQ * ROW_SLAB
    smem = pl.BlockSpec((None, 1, n), lambda b, e: (b, 0, 0), memory_space=pltpu.SMEM)
    wspec = lambda r, c: pl.BlockSpec((None, r, c), lambda b, e: (e, 0, 0))
    slab = pl.BlockSpec((None, rows, LANES), lambda b, e: (b, 0, 0))
    stage = pltpu.VMEM((CAPACITY * ROW_SLAB, LANES), F32)
    return pl.pallas_call(
        _moe_kernel,
        grid=(B, N_EXPERTS),
        in_specs=[smem, pl.BlockSpec((None, CAPACITY, 1), lambda b, e: (b, e, 0)), slab,
                  wspec(D_MODEL, D_EXPERT), wspec(D_MODEL, D_EXPERT), wspec(D_EXPERT, D_MODEL)],
        out_specs=slab,
        out_shape=jax.ShapeDtypeStruct((B, rows, LANES), F32),
        scratch_shapes=[stage, stage, stage, stage],
        compiler_params=_params(("parallel", "arbitrary"), VMEM_LIMIT),
        name="moe",
    )(idx, gate, h2, wg, wu, wd)


def _combine_kernel(x1_ref, moe_ref, mod_ref, o_ref):
    tm = x1_ref.shape[0]
    for j in range(ROW_SLAB):
        cols = slice(j * LANES, (j + 1) * LANES)
        chunk = moe_ref[pl.ds(j, tm, stride=ROW_SLAB), :]
        o_ref[:, cols] = x1_ref[:, cols] + mod_ref[5:6, cols] * chunk


def _combine(x1, moe, mod, tm=512):
    B = x1.shape[0]
    tok = pl.BlockSpec((None, tm, D_MODEL), lambda b, i: (b, i, 0))
    return pl.pallas_call(
        _combine_kernel,
        grid=(B, SEQ // tm),
        in_specs=[tok,
                  pl.BlockSpec((None, tm * ROW_SLAB, LANES), lambda b, i: (b, i, 0)),
                  pl.BlockSpec((None, N_ADA, D_MODEL), lambda b, i: (b, 0, 0))],
        out_specs=tok,
        out_shape=jax.ShapeDtypeStruct((B, SEQ, D_MODEL), F32),
        compiler_params=_params(("parallel", "parallel"), VMEM_LIMIT),
        name="combine",
    )(x1, moe, mod)


def _head_perm():
    perm = []
    for r in range(Q_PER_KV):
        for kv in range(N_KV_HEADS):
            h = kv * Q_PER_KV + r
            perm.extend(range(h * HEAD_DIM, (h + 1) * HEAD_DIM))
    return np.asarray(perm, dtype=np.int32)


def kernel(x, c, rel_bias, w_ada, b_ada, norm_mix_g, norm_ffn_g, w_in, w_fourier, b_fourier,
           q_norm_g, k_norm_g, sink, w_out, w_router, w_gate, w_up, w_down):
    B = x.shape[0]
    perm = _head_perm()
    l = 0
    mod = _ada(c, w_ada[l], b_ada[l]).reshape(B, N_ADA, D_MODEL)
    pq = _fold(w_fourier[l])
    bias = _bias_table(rel_bias)

    wi = w_in[l]
    q_cols = wi[:, FOURIER_WIDTH:FOURIER_WIDTH + ATTN_WIDTH][:, perm]
    win = jnp.concatenate([wi[:, :FOURIER_WIDTH], q_cols, wi[:, FOURIER_WIDTH + ATTN_WIDTH:]],
                          axis=1).astype(BF16)
    gq = (jnp.tile(q_norm_g[l], N_Q_HEADS) * (HEAD_DIM ** -0.5)).reshape(1, ATTN_WIDTH)
    gk = jnp.tile(k_norm_g[l], N_KV_HEADS).reshape(1, KV_WIDTH)
    a, b, q, k, v = _inproj(x, mod, norm_mix_g[l].reshape(1, D_MODEL), win, pq, gq, gk)

    cs = jnp.asarray(_seq_dft()).astype(BF16)
    yf = _fourier(cs, a, b, b_fourier[l].reshape(1, FOURIER_WIDTH))
    ya = _attn(sink[l], q, k, v, bias)

    wo = w_out[l]
    wf = wo[:FOURIER_WIDTH].astype(BF16)
    wa = wo[FOURIER_WIDTH:][perm].astype(BF16)
    w_hi = w_router[l].astype(BF16)
    w_lo = (w_router[l] - w_hi.astype(F32)).astype(BF16)
    wrh = jnp.pad(jnp.concatenate([w_hi, w_lo], axis=1), ((0, 0), (0, LANES - 2 * N_EXPERTS)))
    wrl = jnp.pad(w_hi, ((0, 0), (0, LANES - N_EXPERTS)))
    x1, h2, aff = _outproj(yf, ya, x, mod, norm_ffn_g[l].reshape(1, D_MODEL), wf, wa, wrh, wrl)

    idx, gate = _route(aff)
    n = N_EXPERTS * CAPACITY
    moe = _moe(idx.reshape(B, 1, n), gate, h2,
               w_gate[l].astype(BF16), w_up[l].astype(BF16), w_down[l].astype(BF16))
    return _combine(x1, moe, mod)
```

```python
import functools
import math

import numpy as np
import jax
import jax.numpy as jnp
from jax import lax
from jax.experimental import pallas as pl
from jax.experimental.pallas import tpu as pltpu

D_MODEL = 1024
SEQ = 2048
HEAD_DIM = 64
FOURIER_WIDTH = 512
ATTN_WIDTH = 512
N_GROUPS = 8
N_Q_HEADS = 8
Q_PER_KV = 4
N_KV_HEADS = 2
KV_WIDTH = 128
IN_PROJ_WIDTH = 1280
WINDOW = 128
BLOCK = 128
SPAN = BLOCK + 2 * WINDOW
N_BLOCKS = SEQ // BLOCK
N_BUCKETS = 32
MAX_DISTANCE = 128
N_EXPERTS = 16
CAPACITY = 2 * SEQ // N_EXPERTS
D_EXPERT = 1024
N_ADA = 6
EPS = 1e-6

LANES = 128
SUBLANES = 8
ROW_SLAB = D_MODEL // LANES
VMEM_LIMIT = 56 * 1024 * 1024

F32 = jnp.float32
BF16 = jnp.bfloat16
NEG_INF = float("-inf")
LOG2E = math.log2(math.e)


def _params(sem, vmem=None):
    return pltpu.CompilerParams(dimension_semantics=sem, vmem_limit_bytes=vmem)


@functools.lru_cache(maxsize=None)
def _seq_dft():
    s = np.arange(SEQ, dtype=np.int64)
    ph = (s[:, None] * s[None, :]) % SEQ
    ang = 2.0 * np.pi * ph.astype(np.float64) / SEQ
    sc = 1.0 / math.sqrt(SEQ)
    return np.concatenate([np.cos(ang) * sc, -np.sin(ang) * sc], axis=1).astype(np.float32)


@functools.lru_cache(maxsize=None)
def _chan_dft():
    c = np.arange(HEAD_DIM, dtype=np.int64)
    ph = (c[:, None] * c[None, :]) % HEAD_DIM
    ang = 2.0 * np.pi * ph.astype(np.float64) / HEAD_DIM
    sc = 1.0 / math.sqrt(HEAD_DIM)
    eye = np.eye(N_GROUPS)
    cbd = np.kron(eye, np.cos(ang) * sc)
    sbd = np.kron(eye, np.sin(ang) * sc)
    return cbd.astype(np.float32), sbd.astype(np.float32)


@functools.lru_cache(maxsize=None)
def _bucket_table():
    rel = np.arange(SPAN)[None, :] - WINDOW - np.arange(BLOCK)[:, None]
    half = N_BUCKETS // 2
    max_exact = half // 2
    n = np.abs(rel)
    nf = np.maximum(n, 1).astype(np.float64)
    large = max_exact + (np.log(nf / max_exact) / math.log(MAX_DISTANCE / max_exact)
                         * (half - max_exact)).astype(np.int64)
    sq = np.maximum(n.astype(np.int64) ** 2 // (max_exact * max_exact), 1)
    large_int = max_exact + np.floor(np.log2(sq.astype(np.float64)) + 1e-9).astype(np.int64)
    assert np.array_equal(np.where(n >= max_exact, large, 0), np.where(n >= max_exact, large_int, 0))
    large = np.minimum(large, half - 1)
    bucket = np.where(rel > 0, half, 0) + np.where(n < max_exact, n, large)
    return bucket.astype(np.int32)


@functools.lru_cache(maxsize=None)
def _group_ones(width):
    return np.kron(np.eye(width // HEAD_DIM), np.ones((HEAD_DIM, HEAD_DIM))).astype(np.float32)


@functools.lru_cache(maxsize=None)
def _tri_incl():
    i = np.arange(LANES)
    return (i[:, None] <= i[None, :]).astype(np.float32)


def _ada_kernel(c_ref, w_ref, b_ref, o_ref):
    c = c_ref[...]
    ca = c * (1.0 / (1.0 + jnp.exp(-c)))
    o_ref[...] = jnp.dot(ca, w_ref[...], precision=lax.Precision.HIGHEST,
                         preferred_element_type=F32) + b_ref[...]


def _ada(c, w_ada, b_ada):
    B = c.shape[0]
    n = N_ADA * D_MODEL
    tn = D_MODEL
    return pl.pallas_call(
        _ada_kernel,
        grid=(n // tn,),
        in_specs=[pl.BlockSpec((B, D_MODEL), lambda j: (0, 0)),
                  pl.BlockSpec((D_MODEL, tn), lambda j: (0, j)),
                  pl.BlockSpec((1, tn), lambda j: (0, j))],
        out_specs=pl.BlockSpec((B, tn), lambda j: (0, j)),
        out_shape=jax.ShapeDtypeStruct((B, n), F32),
        compiler_params=_params(("arbitrary",)),
        name="ada",
    )(c, w_ada, b_ada.reshape(1, n))


def _fold_kernel(cbd_ref, sbd_ref, w_ref, o_ref):
    w = w_ref[...]
    o_ref[:, :FOURIER_WIDTH] = jnp.dot(cbd_ref[...], w, precision=lax.Precision.HIGHEST,
                                       preferred_element_type=F32).astype(BF16)
    o_ref[:, FOURIER_WIDTH:] = jnp.dot(sbd_ref[...], w, precision=lax.Precision.HIGHEST,
                                       preferred_element_type=F32).astype(BF16)


def _fold(w_fourier):
    wbd = (jnp.eye(N_GROUPS, dtype=F32)[:, None, :, None] * w_fourier[:, :, None, :]
           ).reshape(FOURIER_WIDTH, FOURIER_WIDTH)
    cbd, sbd = _chan_dft()
    return pl.pallas_call(
        _fold_kernel,
        out_shape=jax.ShapeDtypeStruct((FOURIER_WIDTH, 2 * FOURIER_WIDTH), BF16),
        name="fold",
    )(jnp.asarray(cbd), jnp.asarray(sbd), wbd)


def _bias_kernel(rb_ref, bucket_ref, o_ref):
    h = pl.program_id(0)
    bk = bucket_ref[...]
    acc = jnp.zeros((BLOCK, SPAN), F32)
    for b in range(N_BUCKETS):
        acc = jnp.where(bk == b, rb_ref[b, h], acc)
    j = lax.broadcasted_iota(jnp.int32, (BLOCK, SPAN), 1)
    q = lax.broadcasted_iota(jnp.int32, (BLOCK, SPAN), 0)
    band = jnp.abs(j - WINDOW - q) <= WINDOW
    base = jnp.where(band, acc * LOG2E, NEG_INF)
    o_ref[0] = jnp.where(j >= WINDOW, base, NEG_INF)
    o_ref[1] = base
    o_ref[2] = jnp.where(j < WINDOW + BLOCK, base, NEG_INF)


def _bias_table(rel_bias):
    return pl.pallas_call(
        _bias_kernel,
        grid=(N_Q_HEADS,),
        in_specs=[pl.BlockSpec(memory_space=pltpu.SMEM),
                  pl.BlockSpec((BLOCK, SPAN), lambda h: (0, 0))],
        out_specs=pl.BlockSpec((3, None, BLOCK, SPAN), lambda h: (0, h, 0, 0)),
        out_shape=jax.ShapeDtypeStruct((3, N_Q_HEADS, BLOCK, SPAN), F32),
        compiler_params=_params(("arbitrary",)),
        name="bias",
    )(rel_bias, jnp.asarray(_bucket_table()))


def _inproj_kernel(x_ref, mod_ref, g_ref, win_ref, pq_ref, bdq_ref, bdk_ref, gq_ref, gk_ref,
                   a_ref, b_ref, q_ref, k_ref, v_ref):
    x = x_ref[...]
    ms = jnp.mean(x * x, axis=-1, keepdims=True)
    y = x * lax.rsqrt(ms + EPS) * g_ref[...]
    h = y * (1.0 + mod_ref[1:2, :]) + mod_ref[0:1, :]
    proj = jnp.dot(h.astype(BF16), win_ref[...], preferred_element_type=F32)
    uf = proj[:, :FOURIER_WIDTH].astype(BF16)
    ab = jnp.dot(uf, pq_ref[...], preferred_element_type=F32)
    a_ref[...] = ab[:, :FOURIER_WIDTH].astype(BF16)
    b_ref[...] = ab[:, FOURIER_WIDTH:].astype(BF16)
    q0 = FOURIER_WIDTH
    k0 = q0 + ATTN_WIDTH
    v0 = k0 + KV_WIDTH
    q = proj[:, q0:k0]
    ssq = jnp.dot((q * q).astype(BF16), bdq_ref[...], preferred_element_type=F32)
    q_ref[...] = (q * lax.rsqrt(ssq * (1.0 / HEAD_DIM) + EPS) * gq_ref[...]).astype(BF16)
    k = proj[:, k0:v0]
    ssk = jnp.dot((k * k).astype(BF16), bdk_ref[...], preferred_element_type=F32)
    k_ref[...] = (k * lax.rsqrt(ssk * (1.0 / HEAD_DIM) + EPS) * gk_ref[...]).astype(BF16)
    v_ref[...] = proj[:, v0:].astype(BF16)


def _inproj(x, mod, g, win, pq, gq, gk, tm=512):
    B = x.shape[0]
    const = lambda shape: pl.BlockSpec(shape, lambda b, i: (0,) * len(shape))
    tok = lambda w: pl.BlockSpec((None, tm, w), lambda b, i: (b, i, 0))
    sds = lambda w: jax.ShapeDtypeStruct((B, SEQ, w), BF16)
    return pl.pallas_call(
        _inproj_kernel,
        grid=(B, SEQ // tm),
        in_specs=[tok(D_MODEL),
                  pl.BlockSpec((None, N_ADA, D_MODEL), lambda b, i: (b, 0, 0)),
                  const((1, D_MODEL)),
                  const((D_MODEL, IN_PROJ_WIDTH)),
                  const((FOURIER_WIDTH, 2 * FOURIER_WIDTH)),
                  const((ATTN_WIDTH, ATTN_WIDTH)),
                  const((KV_WIDTH, KV_WIDTH)),
                  const((1, ATTN_WIDTH)),
                  const((1, KV_WIDTH))],
        out_specs=[tok(FOURIER_WIDTH), tok(FOURIER_WIDTH), tok(ATTN_WIDTH), tok(KV_WIDTH),
                   tok(KV_WIDTH)],
        out_shape=[sds(FOURIER_WIDTH), sds(FOURIER_WIDTH), sds(ATTN_WIDTH), sds(KV_WIDTH),
                   sds(KV_WIDTH)],
        compiler_params=_params(("parallel", "parallel"), VMEM_LIMIT),
        name="inproj",
    )(x, mod, g, win, pq, jnp.asarray(_group_ones(ATTN_WIDTH)).astype(BF16),
      jnp.asarray(_group_ones(KV_WIDTH)).astype(BF16), gq, gk)


FOURIER_ROWS = 512


def _fourier_kernel(cs_ref, a_ref, b_ref, bf_ref, o_ref):
    a = a_ref[...]
    b = b_ref[...]
    for i in range(SEQ // FOURIER_ROWS):
        rows = pl.ds(i * FOURIER_ROWS, FOURIER_ROWS)
        acc = jnp.dot(cs_ref[rows, :SEQ], a, preferred_element_type=F32)
        acc = acc + jnp.dot(cs_ref[rows, SEQ:], b, preferred_element_type=F32)
        o_ref[rows, :] = (acc + bf_ref[...]).astype(BF16)


def _fourier(cs, a, b, bf):
    B = a.shape[0]
    tok = pl.BlockSpec((None, SEQ, FOURIER_WIDTH), lambda i: (i, 0, 0))
    return pl.pallas_call(
        _fourier_kernel,
        grid=(B,),
        in_specs=[pl.BlockSpec((SEQ, 2 * SEQ), lambda i: (0, 0), pipeline_mode=pl.Buffered(1)),
                  tok, tok,
                  pl.BlockSpec((1, FOURIER_WIDTH), lambda i: (0, 0))],
        out_specs=tok,
        out_shape=jax.ShapeDtypeStruct((B, SEQ, FOURIER_WIDTH), BF16),
        compiler_params=_params(("parallel",), VMEM_LIMIT),
        name="fourier",
    )(cs, a, b, bf)


ATT_SUB = 4
ATT_ROWS = ATT_SUB * BLOCK
ATT_STEPS = N_BLOCKS // ATT_SUB


def _attn_kernel(sink_ref, q_ref, kl_ref, km_ref, kr_ref, vl_ref, vm_ref, vr_ref, bias_ref,
                 o_ref):
    i = pl.program_id(1)
    keys = jnp.concatenate([kl_ref[...], km_ref[...], kr_ref[...]], axis=0)
    vals = jnp.concatenate([vl_ref[...], vm_ref[...], vr_ref[...]], axis=0)
    first_k = lax.broadcasted_iota(jnp.int32, keys.shape, 1) < HEAD_DIM
    first_q = lax.broadcasted_iota(jnp.int32, (BLOCK, 2 * HEAD_DIM), 1) < HEAD_DIM
    row_head = lax.broadcasted_iota(jnp.int32, (Q_PER_KV * BLOCK, 1), 0) // BLOCK
    keys_kv = [jnp.where(first_k, keys, jnp.zeros_like(keys)),
               jnp.where(first_k, jnp.zeros_like(keys), keys)]
    sinks = []
    for kv in range(N_KV_HEADS):
        sink = jnp.zeros((Q_PER_KV * BLOCK, 1), F32)
        for r in range(Q_PER_KV):
            sink = jnp.where(row_head == r, sink_ref[kv * Q_PER_KV + r] * LOG2E, sink)
        sinks.append(sink)
    for j in range(ATT_SUB):
        variant = 1
        if j == 0:
            variant = jnp.where(i == 0, 0, variant)
        if j == ATT_SUB - 1:
            variant = jnp.where(i == ATT_STEPS - 1, 2, variant)
        qrows = pl.ds(j * BLOCK, BLOCK)
        krows = slice(j * BLOCK, j * BLOCK + SPAN)
        qs = jnp.concatenate([q_ref[qrows, r * LANES:(r + 1) * LANES] for r in range(Q_PER_KV)],
                             axis=0)
        pvs = []
        for kv in range(N_KV_HEADS):
            logits = lax.dot_general(qs, keys_kv[kv][krows], (((1,), (1,)), ((), ())),
                                     preferred_element_type=F32)
            bias = bias_ref[variant, pl.ds(kv * Q_PER_KV, Q_PER_KV)]
            logits = logits + bias.reshape(Q_PER_KV * BLOCK, SPAN)
            sink = sinks[kv]
            m = jnp.maximum(jnp.max(logits, axis=-1, keepdims=True), sink)
            p = jnp.exp2(logits - m)
            denom = jnp.sum(p, axis=-1, keepdims=True) + jnp.exp2(sink - m)
            pv = jnp.dot(p.astype(BF16), vals[krows], preferred_element_type=F32)
            pvs.append(pv / denom)
        for r in range(Q_PER_KV):
            rows = slice(r * BLOCK, (r + 1) * BLOCK)
            o_ref[qrows, r * LANES:(r + 1) * LANES] = jnp.where(
                first_q, pvs[0][rows], pvs[1][rows]).astype(BF16)


def _attn(sink, q, k, v, bias):
    B = q.shape[0]
    edge = lambda f: pl.BlockSpec((None, BLOCK, KV_WIDTH), lambda b, i: (b, f(i), 0))
    left = lambda i: jnp.maximum(i * ATT_SUB - 1, 0)
    right = lambda i: jnp.minimum((i + 1) * ATT_SUB, N_BLOCKS - 1)
    mid = pl.BlockSpec((None, ATT_ROWS, KV_WIDTH), lambda b, i: (b, i, 0))
    qspec = pl.BlockSpec((None, ATT_ROWS, ATTN_WIDTH), lambda b, i: (b, i, 0))
    return pl.pallas_call(
        _attn_kernel,
        grid=(B, ATT_STEPS),
        in_specs=[pl.BlockSpec(memory_space=pltpu.SMEM), qspec,
                  edge(left), mid, edge(right),
                  edge(left), mid, edge(right),
                  pl.BlockSpec((3, N_Q_HEADS, BLOCK, SPAN), lambda b, i: (0, 0, 0, 0))],
        out_specs=qspec,
        out_shape=jax.ShapeDtypeStruct((B, SEQ, ATTN_WIDTH), BF16),
        compiler_params=_params(("parallel", "parallel"), VMEM_LIMIT),
        name="attn",
    )(sink, q, k, k, k, v, v, v, bias)


OUT_ROWS = 256


def _outproj_kernel(yf_ref, ya_ref, x_ref, mod_ref, g_ref, wf_ref, wa_ref, wrh_ref, wrl_ref,
                    x1_ref, h2_ref, aff_ref):
    tm = x_ref.shape[0]
    gain = g_ref[...] * (1.0 + mod_ref[4:5, :])
    shift = mod_ref[3:4, :]
    gate1 = mod_ref[2:3, :]
    lane = lax.broadcasted_iota(jnp.int32, (OUT_ROWS, LANES), 1)
    for c in range(tm // OUT_ROWS):
        rows = pl.ds(c * OUT_ROWS, OUT_ROWS)
        mixed = jnp.dot(yf_ref[rows, :], wf_ref[...], preferred_element_type=F32)
        mixed = mixed + jnp.dot(ya_ref[rows, :], wa_ref[...], preferred_element_type=F32)
        x1 = x_ref[rows, :] + gate1 * mixed
        x1_ref[rows, :] = x1
        ms = jnp.mean(x1 * x1, axis=-1, keepdims=True)
        h2 = x1 * lax.rsqrt(ms + EPS) * gain + shift
        for j in range(ROW_SLAB):
            h2_ref[pl.ds(c * OUT_ROWS * ROW_SLAB + j, OUT_ROWS, stride=ROW_SLAB), :] = (
                h2[:, j * LANES:(j + 1) * LANES])
        hi = h2.astype(BF16)
        lo = (h2 - hi.astype(F32)).astype(BF16)
        part = jnp.dot(hi, wrh_ref[...], preferred_element_type=F32)
        part = part + jnp.dot(lo, wrl_ref[...], preferred_element_type=F32)
        logits = part + pltpu.roll(part, LANES - N_EXPERTS, axis=1)
        logits = jnp.where(lane < N_EXPERTS, logits, NEG_INF)
        m = jnp.max(logits, axis=-1, keepdims=True)
        e = jnp.exp(logits - m)
        aff_ref[rows, :] = e / jnp.sum(e, axis=-1, keepdims=True)


def _outproj(yf, ya, x, mod, g, wf, wa, wrh, wrl, tm=512):
    B = x.shape[0]
    const = lambda shape: pl.BlockSpec(shape, lambda b, i: (0,) * len(shape))
    tok = lambda w: pl.BlockSpec((None, tm, w), lambda b, i: (b, i, 0))
    return pl.pallas_call(
        _outproj_kernel,
        grid=(B, SEQ // tm),
        in_specs=[tok(FOURIER_WIDTH), tok(ATTN_WIDTH), tok(D_MODEL),
                  pl.BlockSpec((None, N_ADA, D_MODEL), lambda b, i: (b, 0, 0)),
                  const((1, D_MODEL)),
                  const((FOURIER_WIDTH, D_MODEL)), const((ATTN_WIDTH, D_MODEL)),
                  const((D_MODEL, LANES)), const((D_MODEL, LANES))],
        out_specs=[tok(D_MODEL),
                   pl.BlockSpec((None, tm * ROW_SLAB, LANES), lambda b, i: (b, i, 0)),
                   tok(LANES)],
        out_shape=[jax.ShapeDtypeStruct((B, SEQ, D_MODEL), F32),
                   jax.ShapeDtypeStruct((B, SEQ * ROW_SLAB, LANES), F32),
                   jax.ShapeDtypeStruct((B, SEQ, LANES), F32)],
        compiler_params=_params(("parallel", "parallel"), VMEM_LIMIT),
        name="outproj",
    )(yf, ya, x, mod, g, wf, wa, wrh, wrl)


def _route_kernel(aff_ref, tri_ref, idx_ref, gate_ref):
    aff = aff_ref[...]
    aff_t = jnp.transpose(aff)[:N_EXPERTS]
    bits = pltpu.bitcast(aff_t, jnp.int32)
    cap = float(CAPACITY)

    t = jnp.zeros((N_EXPERTS, 1), jnp.int32)
    for bit in range(30, -1, -1):
        cand = t | (1 << bit)
        cnt = jnp.sum(jnp.where(bits >= cand, 1.0, 0.0), axis=1, keepdims=True)
        t = jnp.where(cnt >= cap, cand, t)
    gt = bits > t
    eq = bits == t
    need = cap - jnp.sum(jnp.where(gt, 1.0, 0.0), axis=1, keepdims=True)

    tri = tri_ref[...]
    n_chunks = SEQ // LANES

    def prefix(flags_f32):
        outs = []
        carry = jnp.zeros((N_EXPERTS, 1), F32)
        for c in range(n_chunks):
            f = flags_f32[:, c * LANES:(c + 1) * LANES]
            incl = jnp.dot(f.astype(BF16), tri, preferred_element_type=F32)
            outs.append(incl - f + carry)
            carry = carry + jnp.sum(f, axis=1, keepdims=True)
        return jnp.concatenate(outs, axis=1)

    eq_f = jnp.where(eq, 1.0, 0.0)
    eq_rank = prefix(eq_f)
    sel_f = jnp.where(gt, 1.0, jnp.where(eq_rank < need, eq_f, 0.0))
    pos = prefix(sel_f)
    posm = jnp.where(sel_f > 0.0, pos, -1.0)

    hi = aff.astype(BF16).astype(F32)
    r1 = aff - hi
    mid = r1.astype(BF16).astype(F32)
    lo = r1 - mid
    lane = lax.broadcasted_iota(jnp.int32, (SEQ, LANES), 1)
    tok = lax.broadcasted_iota(jnp.int32, (SEQ, LANES), 0)
    vals = hi + pltpu.roll(mid, N_EXPERTS, axis=1) + pltpu.roll(lo, 2 * N_EXPERTS, axis=1)
    vals = vals + jnp.where(lane == 3 * N_EXPERTS, (tok >> 6).astype(F32), 0.0)
    vals = vals + jnp.where(lane == 3 * N_EXPERTS + 1, (tok & 63).astype(F32), 0.0)
    vals = vals.astype(BF16)

    slot = lax.broadcasted_iota(jnp.int32, (CAPACITY, SEQ), 0).astype(F32).astype(BF16)
    posm_b = posm.astype(BF16)
    one_b = jnp.ones((CAPACITY, SEQ), BF16)
    zero_b = jnp.zeros((CAPACITY, SEQ), BF16)
    lane_c = lax.broadcasted_iota(jnp.int32, (CAPACITY, LANES), 1)
    w_idx = jnp.where(lane_c == 3 * N_EXPERTS, 64.0,
                      jnp.where(lane_c == 3 * N_EXPERTS + 1, 1.0, 0.0))
    for e in range(N_EXPERTS):
        onehot = jnp.where(posm_b[e:e + 1, :] == slot, one_b, zero_b)
        res = jnp.dot(onehot, vals, preferred_element_type=F32)
        w_g = jnp.where((lane_c == e) | (lane_c == N_EXPERTS + e) | (lane_c == 2 * N_EXPERTS + e),
                        1.0, 0.0)
        rows = pl.ds(e * CAPACITY, CAPACITY)
        idx_ref[rows, :] = jnp.sum(res * w_idx, axis=1, keepdims=True).astype(jnp.int32) * ROW_SLAB
        gate_ref[rows, :] = jnp.sum(res * w_g, axis=1, keepdims=True)


def _route(aff):
    B = aff.shape[0]
    n = N_EXPERTS * CAPACITY
    return pl.pallas_call(
        _route_kernel,
        grid=(B,),
        in_specs=[pl.BlockSpec((None, SEQ, LANES), lambda b: (b, 0, 0)),
                  pl.BlockSpec((LANES, LANES), lambda b: (0, 0))],
        out_specs=[pl.BlockSpec((None, n, 1), lambda b: (b, 0, 0)),
                   pl.BlockSpec((None, n, 1), lambda b: (b, 0, 0))],
        out_shape=[jax.ShapeDtypeStruct((B, n, 1), jnp.int32),
                   jax.ShapeDtypeStruct((B, n, 1), F32)],
        compiler_params=_params(("parallel",), VMEM_LIMIT),
        name="route",
    )(aff, jnp.asarray(_tri_incl()).astype(BF16))


PAIR = 2


def _moe_kernel(idx_ref, gate_ref, h2_ref, wg_ref, wu_ref, wd_ref, y_ref, xin0_ref, xin1_ref):
    e = pl.program_id(1)
    last = N_EXPERTS - 1
    n = N_EXPERTS * CAPACITY
    rows = PAIR * CAPACITY

    def gather_rows(ex, dst_ref):
        for bb in range(PAIR):
            base = bb * n + ex * CAPACITY
            for p in range(CAPACITY):
                src = pl.multiple_of(idx_ref[0, base + p], ROW_SLAB)
                dst_ref[pl.ds((bb * CAPACITY + p) * ROW_SLAB, ROW_SLAB), :] = (
                    h2_ref[bb, pl.ds(src, ROW_SLAB), :])

    def expert(xin_ref):
        xin = jnp.concatenate(
            [xin_ref[pl.ds(j, rows, stride=ROW_SLAB), :].astype(BF16)
             for j in range(ROW_SLAB)], axis=1)
        a = jnp.dot(xin, wg_ref[...], preferred_element_type=F32)
        u = jnp.dot(xin, wu_ref[...], preferred_element_type=F32)
        hmid = (a * (1.0 / (1.0 + jnp.exp(-a))) * u).astype(BF16)
        y = jnp.dot(hmid, wd_ref[...], preferred_element_type=F32)
        y = y * gate_ref[...].reshape(rows, 1)
        for bb in range(PAIR):
            for j in range(ROW_SLAB):
                y_ref[bb, pl.ds(j, CAPACITY, stride=ROW_SLAB), :] = (
                    y[bb * CAPACITY:(bb + 1) * CAPACITY, j * LANES:(j + 1) * LANES])

    @pl.when(e == 0)
    def _():
        gather_rows(0, xin0_ref)

    def step(xin_cur, xin_nxt):
        gather_rows(jnp.minimum(e + 1, last), xin_nxt)
        expert(xin_cur)

    @pl.when(e % 2 == 0)
    def _():
        step(xin0_ref, xin1_ref)

    @pl.when(e % 2 == 1)
    def _():
        step(xin1_ref, xin0_ref)


def _moe(idx, gate, h2, wg, wu, wd):
    B = h2.shape[0]
    n = N_EXPERTS * CAPACITY
    rows = SEQ * ROW_SLAB
    pairs = B // PAIR
    wspec = lambda r, c: pl.BlockSpec((None, r, c), lambda b, e: (e, 0, 0))
    stage = pltpu.VMEM((PAIR * CAPACITY * ROW_SLAB, LANES), F32)
    out = pl.pallas_call(
        _moe_kernel,
        grid=(pairs, N_EXPERTS),
        in_specs=[pl.BlockSpec((None, 1, PAIR * n), lambda b, e: (b, 0, 0),
                               memory_space=pltpu.SMEM),
                  pl.BlockSpec((None, PAIR, CAPACITY, 1), lambda b, e: (b, 0, e, 0)),
                  pl.BlockSpec((None, PAIR, rows, LANES), lambda b, e: (b, 0, 0, 0),
                               pipeline_mode=pl.Buffered(1)),
                  wspec(D_MODEL, D_EXPERT), wspec(D_MODEL, D_EXPERT), wspec(D_EXPERT, D_MODEL)],
        out_specs=pl.BlockSpec((None, PAIR, CAPACITY * ROW_SLAB, LANES),
                               lambda b, e: (b, 0, e, 0)),
        out_shape=jax.ShapeDtypeStruct((pairs, PAIR, n * ROW_SLAB, LANES), F32),
        scratch_shapes=[stage, stage],
        compiler_params=_params(("parallel", "arbitrary"), VMEM_LIMIT),
        name="moe",
    )(idx.reshape(pairs, 1, PAIR * n), gate.reshape(pairs, PAIR, n, 1),
      h2.reshape(pairs, PAIR, rows, LANES), wg, wu, wd)
    return out.reshape(B, n * ROW_SLAB, LANES)


COMBINE_EXPERTS = 4
SCATTER_UNROLL = 8
COMBINE_ROWS = 256


def _combine_kernel(idx_ref, y_ref, x1_ref, mod_ref, o_ref, acc_ref):
    j = pl.program_id(1)
    slots = COMBINE_EXPERTS * CAPACITY
    base = j * slots

    @pl.when(j == 0)
    def _():
        acc_ref[...] = jnp.zeros_like(acc_ref)

    for g in range(slots // SCATTER_UNROLL):
        new = []
        for u in range(SCATTER_UNROLL):
            r = g * SCATTER_UNROLL + u
            dst = pl.multiple_of(idx_ref[0, base + r], ROW_SLAB)
            new.append((dst, acc_ref[pl.ds(dst, ROW_SLAB), :]
                        + y_ref[pl.ds(r * ROW_SLAB, ROW_SLAB), :]))
        for dst, val in new:
            acc_ref[pl.ds(dst, ROW_SLAB), :] = val

    @pl.when(j == pl.num_programs(1) - 1)
    def _():
        for rb in range(SEQ // COMBINE_ROWS):
            rows = pl.ds(rb * COMBINE_ROWS, COMBINE_ROWS)
            for c in range(ROW_SLAB):
                cols = slice(c * LANES, (c + 1) * LANES)
                chunk = acc_ref[pl.ds(rb * COMBINE_ROWS * ROW_SLAB + c, COMBINE_ROWS,
                                      stride=ROW_SLAB), :]
                o_ref[rows, cols] = x1_ref[rows, cols] + mod_ref[5:6, cols] * chunk


def _combine(idx, y, x1, mod):
    B = x1.shape[0]
    n = N_EXPERTS * CAPACITY
    tok = pl.BlockSpec((None, SEQ, D_MODEL), lambda b, j: (b, 0, 0))
    return pl.pallas_call(
        _combine_kernel,
        grid=(B, N_EXPERTS // COMBINE_EXPERTS),
        in_specs=[pl.BlockSpec((None, 1, n), lambda b, j: (b, 0, 0), memory_space=pltpu.SMEM),
                  pl.BlockSpec((None, COMBINE_EXPERTS * CAPACITY * ROW_SLAB, LANES),
                               lambda b, j: (b, j, 0)),
                  tok,
                  pl.BlockSpec((None, N_ADA, D_MODEL), lambda b, j: (b, 0, 0))],
        out_specs=tok,
        out_shape=jax.ShapeDtypeStruct((B, SEQ, D_MODEL), F32),
        scratch_shapes=[pltpu.VMEM((SEQ * ROW_SLAB, LANES), F32)],
        compiler_params=_params(("parallel", "arbitrary"), VMEM_LIMIT),
        name="combine",
    )(idx, y, x1, mod)


def _head_perm():
    perm = []
    for r in range(Q_PER_KV):
        for kv in range(N_KV_HEADS):
            h = kv * Q_PER_KV + r
            perm.extend(range(h * HEAD_DIM, (h + 1) * HEAD_DIM))
    return np.asarray(perm, dtype=np.int32)


def kernel(x, c, rel_bias, w_ada, b_ada, norm_mix_g, norm_ffn_g, w_in, w_fourier, b_fourier,
           q_norm_g, k_norm_g, sink, w_out, w_router, w_gate, w_up, w_down):
    B = x.shape[0]
    perm = _head_perm()
    l = 0
    mod = _ada(c, w_ada[l], b_ada[l]).reshape(B, N_ADA, D_MODEL)
    pq = _fold(w_fourier[l])
    bias = _bias_table(rel_bias)

    wi = w_in[l]
    q_cols = wi[:, FOURIER_WIDTH:FOURIER_WIDTH + ATTN_WIDTH][:, perm]
    win = jnp.concatenate([wi[:, :FOURIER_WIDTH], q_cols, wi[:, FOURIER_WIDTH + ATTN_WIDTH:]],
                          axis=1).astype(BF16)
    gq = (jnp.tile(q_norm_g[l], N_Q_HEADS) * (HEAD_DIM ** -0.5 * LOG2E)).reshape(1, ATTN_WIDTH)
    gk = jnp.tile(k_norm_g[l], N_KV_HEADS).reshape(1, KV_WIDTH)
    a, b, q, k, v = _inproj(x, mod, norm_mix_g[l].reshape(1, D_MODEL), win, pq, gq, gk)

    cs = jnp.asarray(_seq_dft()).astype(BF16)
    yf = _fourier(cs, a, b, b_fourier[l].reshape(1, FOURIER_WIDTH))
    ya = _attn(sink[l], q, k, v, bias)

    wo = w_out[l]
    wf = wo[:FOURIER_WIDTH].astype(BF16)
    wa = wo[FOURIER_WIDTH:][perm].astype(BF16)
    w_hi = w_router[l].astype(BF16)
    w_lo = (w_router[l] - w_hi.astype(F32)).astype(BF16)
    wrh = jnp.pad(jnp.concatenate([w_hi, w_lo], axis=1), ((0, 0), (0, LANES - 2 * N_EXPERTS)))
    wrl = jnp.pad(w_hi, ((0, 0), (0, LANES - N_EXPERTS)))
    x1, h2, aff = _outproj(yf, ya, x, mod, norm_ffn_g[l].reshape(1, D_MODEL), wf, wa, wrh, wrl)

    idx, gate = _route(aff)
    n = N_EXPERTS * CAPACITY
    idx = idx.reshape(B, 1, n)
    y = _moe(idx, gate, h2,
             w_gate[l].astype(BF16), w_up[l].astype(BF16), w_down[l].astype(BF16))
    return _combine(idx, y, x1, mod)
```

```python
import functools
import math

import numpy as np
import jax
import jax.numpy as jnp
from jax import lax
from jax.experimental import pallas as pl
from jax.experimental.pallas import tpu as pltpu

D_MODEL = 1024
SEQ = 2048
HEAD_DIM = 64
FOURIER_WIDTH = 512
ATTN_WIDTH = 512
N_GROUPS = 8
N_Q_HEADS = 8
Q_PER_KV = 4
N_KV_HEADS = 2
KV_WIDTH = 128
IN_PROJ_WIDTH = 1280
WINDOW = 128
BLOCK = 128
SPAN = BLOCK + 2 * WINDOW
N_BLOCKS = SEQ // BLOCK
N_BUCKETS = 32
MAX_DISTANCE = 128
N_EXPERTS = 16
CAPACITY = 2 * SEQ // N_EXPERTS
D_EXPERT = 1024
N_ADA = 6
EPS = 1e-6

LANES = 128
SUBLANES = 8
ROW_SLAB = D_MODEL // LANES
VMEM_LIMIT = 56 * 1024 * 1024

F32 = jnp.float32
BF16 = jnp.bfloat16
NEG_INF = float("-inf")
LOG2E = math.log2(math.e)


def _params(sem, vmem=None):
    return pltpu.CompilerParams(dimension_semantics=sem, vmem_limit_bytes=vmem)


@functools.lru_cache(maxsize=None)
def _chan_dft():
    c = np.arange(HEAD_DIM, dtype=np.int64)
    ph = (c[:, None] * c[None, :]) % HEAD_DIM
    ang = 2.0 * np.pi * ph.astype(np.float64) / HEAD_DIM
    sc = 1.0 / math.sqrt(HEAD_DIM)
    eye = np.eye(N_GROUPS)
    cbd = np.kron(eye, np.cos(ang) * sc)
    sbd = np.kron(eye, np.sin(ang) * sc)
    return cbd.astype(np.float32), sbd.astype(np.float32)


@functools.lru_cache(maxsize=None)
def _bucket_table():
    rel = np.arange(SPAN)[None, :] - WINDOW - np.arange(BLOCK)[:, None]
    half = N_BUCKETS // 2
    max_exact = half // 2
    n = np.abs(rel)
    nf = np.maximum(n, 1).astype(np.float64)
    large = max_exact + (np.log(nf / max_exact) / math.log(MAX_DISTANCE / max_exact)
                         * (half - max_exact)).astype(np.int64)
    sq = np.maximum(n.astype(np.int64) ** 2 // (max_exact * max_exact), 1)
    large_int = max_exact + np.floor(np.log2(sq.astype(np.float64)) + 1e-9).astype(np.int64)
    assert np.array_equal(np.where(n >= max_exact, large, 0), np.where(n >= max_exact, large_int, 0))
    large = np.minimum(large, half - 1)
    bucket = np.where(rel > 0, half, 0) + np.where(n < max_exact, n, large)
    return bucket.astype(np.int32)


@functools.lru_cache(maxsize=None)
def _group_ones(width):
    return np.kron(np.eye(width // HEAD_DIM), np.ones((HEAD_DIM, HEAD_DIM))).astype(np.float32)


@functools.lru_cache(maxsize=None)
def _tri_incl():
    i = np.arange(LANES)
    return (i[:, None] <= i[None, :]).astype(np.float32)


def _ada_kernel(c_ref, w_ref, b_ref, o_ref):
    c = c_ref[...]
    ca = c * (1.0 / (1.0 + jnp.exp(-c)))
    o_ref[...] = jnp.dot(ca, w_ref[...], precision=lax.Precision.HIGHEST,
                         preferred_element_type=F32) + b_ref[...]


def _ada(c, w_ada, b_ada):
    B = c.shape[0]
    n = N_ADA * D_MODEL
    tn = D_MODEL
    return pl.pallas_call(
        _ada_kernel,
        grid=(n // tn,),
        in_specs=[pl.BlockSpec((B, D_MODEL), lambda j: (0, 0)),
                  pl.BlockSpec((D_MODEL, tn), lambda j: (0, j)),
                  pl.BlockSpec((1, tn), lambda j: (0, j))],
        out_specs=pl.BlockSpec((B, tn), lambda j: (0, j)),
        out_shape=jax.ShapeDtypeStruct((B, n), F32),
        compiler_params=_params(("arbitrary",)),
        name="ada",
    )(c, w_ada, b_ada.reshape(1, n))


def _fold_kernel(cbd_ref, sbd_ref, w_ref, o_ref):
    w = w_ref[...]
    o_ref[:, :FOURIER_WIDTH] = jnp.dot(cbd_ref[...], w, precision=lax.Precision.HIGHEST,
                                       preferred_element_type=F32).astype(BF16)
    o_ref[:, FOURIER_WIDTH:] = jnp.dot(sbd_ref[...], w, precision=lax.Precision.HIGHEST,
                                       preferred_element_type=F32).astype(BF16)


def _fold(w_fourier):
    wbd = (jnp.eye(N_GROUPS, dtype=F32)[:, None, :, None] * w_fourier[:, :, None, :]
           ).reshape(FOURIER_WIDTH, FOURIER_WIDTH)
    cbd, sbd = _chan_dft()
    return pl.pallas_call(
        _fold_kernel,
        out_shape=jax.ShapeDtypeStruct((FOURIER_WIDTH, 2 * FOURIER_WIDTH), BF16),
        name="fold",
    )(jnp.asarray(cbd), jnp.asarray(sbd), wbd)


def _bias_kernel(rb_ref, bucket_ref, o_ref):
    h = pl.program_id(0)
    bk = bucket_ref[...]
    acc = jnp.zeros((BLOCK, SPAN), F32)
    for b in range(N_BUCKETS):
        acc = jnp.where(bk == b, rb_ref[b, h], acc)
    j = lax.broadcasted_iota(jnp.int32, (BLOCK, SPAN), 1)
    q = lax.broadcasted_iota(jnp.int32, (BLOCK, SPAN), 0)
    band = jnp.abs(j - WINDOW - q) <= WINDOW
    base = jnp.where(band, acc * LOG2E, NEG_INF)
    o_ref[0] = jnp.where(j >= WINDOW, base, NEG_INF)
    o_ref[1] = base
    o_ref[2] = jnp.where(j < WINDOW + BLOCK, base, NEG_INF)


def _bias_table(rel_bias):
    return pl.pallas_call(
        _bias_kernel,
        grid=(N_Q_HEADS,),
        in_specs=[pl.BlockSpec(memory_space=pltpu.SMEM),
                  pl.BlockSpec((BLOCK, SPAN), lambda h: (0, 0))],
        out_specs=pl.BlockSpec((3, None, BLOCK, SPAN), lambda h: (0, h, 0, 0)),
        out_shape=jax.ShapeDtypeStruct((3, N_Q_HEADS, BLOCK, SPAN), F32),
        compiler_params=_params(("arbitrary",)),
        name="bias",
    )(rel_bias, jnp.asarray(_bucket_table()))


def _inproj_kernel(x_ref, mod_ref, g_ref, win_ref, pq_ref, bdq_ref, bdk_ref, gq_ref, gk_ref,
                   a_ref, b_ref, q_ref, k_ref, v_ref):
    x = x_ref[...]
    ms = jnp.mean(x * x, axis=-1, keepdims=True)
    y = x * lax.rsqrt(ms + EPS) * g_ref[...]
    h = y * (1.0 + mod_ref[1:2, :]) + mod_ref[0:1, :]
    proj = jnp.dot(h.astype(BF16), win_ref[...], preferred_element_type=F32)
    uf = proj[:, :FOURIER_WIDTH].astype(BF16)
    ab = jnp.dot(uf, pq_ref[...], preferred_element_type=F32)
    a_ref[...] = ab[:, :FOURIER_WIDTH].astype(BF16)
    b_ref[...] = ab[:, FOURIER_WIDTH:].astype(BF16)
    q0 = FOURIER_WIDTH
    k0 = q0 + ATTN_WIDTH
    v0 = k0 + KV_WIDTH
    q = proj[:, q0:k0]
    ssq = jnp.dot((q * q).astype(BF16), bdq_ref[...], preferred_element_type=F32)
    q_ref[...] = (q * lax.rsqrt(ssq * (1.0 / HEAD_DIM) + EPS) * gq_ref[...]).astype(BF16)
    k = proj[:, k0:v0]
    ssk = jnp.dot((k * k).astype(BF16), bdk_ref[...], preferred_element_type=F32)
    k_ref[...] = (k * lax.rsqrt(ssk * (1.0 / HEAD_DIM) + EPS) * gk_ref[...]).astype(BF16)
    v_ref[...] = proj[:, v0:].astype(BF16)


def _inproj(x, mod, g, win, pq, gq, gk, tm=512):
    B = x.shape[0]
    const = lambda shape: pl.BlockSpec(shape, lambda b, i: (0,) * len(shape))
    tok = lambda w: pl.BlockSpec((None, tm, w), lambda b, i: (b, i, 0))
    sds = lambda w: jax.ShapeDtypeStruct((B, SEQ, w), BF16)
    return pl.pallas_call(
        _inproj_kernel,
        grid=(B, SEQ // tm),
        in_specs=[tok(D_MODEL),
                  pl.BlockSpec((None, N_ADA, D_MODEL), lambda b, i: (b, 0, 0)),
                  const((1, D_MODEL)),
                  const((D_MODEL, IN_PROJ_WIDTH)),
                  const((FOURIER_WIDTH, 2 * FOURIER_WIDTH)),
                  const((ATTN_WIDTH, ATTN_WIDTH)),
                  const((KV_WIDTH, KV_WIDTH)),
                  const((1, ATTN_WIDTH)),
                  const((1, KV_WIDTH))],
        out_specs=[tok(FOURIER_WIDTH), tok(FOURIER_WIDTH), tok(ATTN_WIDTH), tok(KV_WIDTH),
                   tok(KV_WIDTH)],
        out_shape=[sds(FOURIER_WIDTH), sds(FOURIER_WIDTH), sds(ATTN_WIDTH), sds(KV_WIDTH),
                   sds(KV_WIDTH)],
        compiler_params=_params(("parallel", "parallel"), VMEM_LIMIT),
        name="inproj",
    )(x, mod, g, win, pq, jnp.asarray(_group_ones(ATTN_WIDTH)).astype(BF16),
      jnp.asarray(_group_ones(KV_WIDTH)).astype(BF16), gq, gk)


HALF = SEQ // 2
REV = 128
REV_BLOCKS = HALF // REV
FOURIER_ROWS = 512


def _fourier_kernel(cm_ref, sm_ref, psh_ref, alt_ref, altrow_ref, a_ref, b_ref, bf_ref, o_ref):
    psh = psh_ref[...]

    def reversed_block(win_lo, k, src):
        if k == 0:
            return jnp.dot(psh[:, :REV], src(win_lo, REV), preferred_element_type=F32)
        return jnp.dot(psh, src(win_lo, 2 * REV), preferred_element_type=F32)

    def folded(ref, sign):
        blocks = []
        for k in range(REV_BLOCKS):
            lo = SEQ - REV * (k + 1)
            rev = reversed_block(lo, k, lambda s, n: ref[pl.ds(s, n), :])
            blocks.append((ref[pl.ds(k * REV, REV), :].astype(F32) + sign * rev).astype(BF16))
        return jnp.concatenate(blocks, axis=0)

    a_even = folded(a_ref, 1.0)
    b_odd = folded(b_ref, -1.0)
    a_mid = a_ref[pl.ds(HALF, 1), :].astype(F32)
    bias = bf_ref[...]
    z_blocks = []
    for i in range(HALF // FOURIER_ROWS):
        rows = pl.ds(i * FOURIER_ROWS, FOURIER_ROWS)
        yc = jnp.dot(cm_ref[rows, :], a_even, preferred_element_type=F32)
        yc = yc + alt_ref[rows, :] * a_mid + bias
        ys = jnp.dot(sm_ref[rows, :], b_odd, preferred_element_type=F32)
        o_ref[rows, :] = (yc - ys).astype(BF16)
        z_blocks.append((yc + ys).astype(BF16))
    z = jnp.concatenate(z_blocks, axis=0)
    for k in range(REV_BLOCKS):
        lo = HALF - REV * (k + 1)
        top = reversed_block(lo, k, lambda s, n: z[s:s + n])
        o_ref[pl.ds(HALF + k * REV, REV), :] = top.astype(BF16)
    y_mid = jnp.dot(altrow_ref[...], a_ref[...], preferred_element_type=F32)[0:1, :] + bias
    o_ref[pl.ds(HALF, 1), :] = y_mid.astype(BF16)


@functools.lru_cache(maxsize=None)
def _fold_tables():
    s = np.arange(HALF, dtype=np.int64)
    ph = (s[:, None] * s[None, :]) % SEQ
    ang = 2.0 * np.pi * ph.astype(np.float64) / SEQ
    sc = 1.0 / math.sqrt(SEQ)
    cm = (np.cos(ang) * sc).astype(np.float32)
    sm = (np.sin(ang) * sc).astype(np.float32)
    psh = np.zeros((REV, 2 * REV), np.float32)
    psh[np.arange(REV), REV - np.arange(REV)] = 1.0
    alt = (np.where(s % 2 == 0, 1.0, -1.0) * sc).astype(np.float32).reshape(HALF, 1)
    t = np.arange(SEQ)
    altrow = np.zeros((SUBLANES, SEQ), np.float32)
    altrow[0] = np.where(t % 2 == 0, 1.0, -1.0) * sc
    return cm, sm, psh, alt, altrow


def _fourier(a, b, bf):
    B = a.shape[0]
    cm, sm, psh, alt, altrow = _fold_tables()
    tok = pl.BlockSpec((None, SEQ, FOURIER_WIDTH), lambda i: (i, 0, 0))
    const = lambda shape: pl.BlockSpec(shape, lambda i: (0,) * len(shape))
    return pl.pallas_call(
        _fourier_kernel,
        grid=(B,),
        in_specs=[const((HALF, HALF)), const((HALF, HALF)), const((REV, 2 * REV)),
                  const((HALF, 1)), const((SUBLANES, SEQ)),
                  tok, tok, const((1, FOURIER_WIDTH))],
        out_specs=tok,
        out_shape=jax.ShapeDtypeStruct((B, SEQ, FOURIER_WIDTH), BF16),
        compiler_params=_params(("parallel",), VMEM_LIMIT),
        name="fourier",
    )(jnp.asarray(cm).astype(BF16), jnp.asarray(sm).astype(BF16), jnp.asarray(psh).astype(BF16),
      jnp.asarray(alt), jnp.asarray(altrow).astype(BF16), a, b, bf)


ATT_SUB = 4
ATT_ROWS = ATT_SUB * BLOCK
ATT_STEPS = N_BLOCKS // ATT_SUB


def _attn_kernel(sink_ref, q_ref, kl_ref, km_ref, kr_ref, vl_ref, vm_ref, vr_ref, bias_ref,
                 o_ref):
    i = pl.program_id(1)
    keys = jnp.concatenate([kl_ref[...], km_ref[...], kr_ref[...]], axis=0)
    vals = jnp.concatenate([vl_ref[...], vm_ref[...], vr_ref[...]], axis=0)
    first_k = lax.broadcasted_iota(jnp.int32, keys.shape, 1) < HEAD_DIM
    first_q = lax.broadcasted_iota(jnp.int32, (BLOCK, 2 * HEAD_DIM), 1) < HEAD_DIM
    row_head = lax.broadcasted_iota(jnp.int32, (Q_PER_KV * BLOCK, 1), 0) // BLOCK
    keys_kv = [jnp.where(first_k, keys, jnp.zeros_like(keys)),
               jnp.where(first_k, jnp.zeros_like(keys), keys)]
    sinks = []
    for kv in range(N_KV_HEADS):
        sink = jnp.zeros((Q_PER_KV * BLOCK, 1), F32)
        for r in range(Q_PER_KV):
            sink = jnp.where(row_head == r, sink_ref[kv * Q_PER_KV + r] * LOG2E, sink)
        sinks.append(sink)
    for j in range(ATT_SUB):
        variant = 1
        if j == 0:
            variant = jnp.where(i == 0, 0, variant)
        if j == ATT_SUB - 1:
            variant = jnp.where(i == ATT_STEPS - 1, 2, variant)
        qrows = pl.ds(j * BLOCK, BLOCK)
        krows = slice(j * BLOCK, j * BLOCK + SPAN)
        qs = jnp.concatenate([q_ref[qrows, r * LANES:(r + 1) * LANES] for r in range(Q_PER_KV)],
                             axis=0)
        pvs = []
        for kv in range(N_KV_HEADS):
            logits = lax.dot_general(qs, keys_kv[kv][krows], (((1,), (1,)), ((), ())),
                                     preferred_element_type=F32)
            bias = bias_ref[variant, pl.ds(kv * Q_PER_KV, Q_PER_KV)]
            logits = logits + bias.reshape(Q_PER_KV * BLOCK, SPAN)
            sink = sinks[kv]
            m = jnp.maximum(jnp.max(logits, axis=-1, keepdims=True), sink)
            p = jnp.exp2(logits - m)
            denom = jnp.sum(p, axis=-1, keepdims=True) + jnp.exp2(sink - m)
            pv = jnp.dot(p.astype(BF16), vals[krows], preferred_element_type=F32)
            pvs.append(pv / denom)
        for r in range(Q_PER_KV):
            rows = slice(r * BLOCK, (r + 1) * BLOCK)
            o_ref[qrows, r * LANES:(r + 1) * LANES] = jnp.where(
                first_q, pvs[0][rows], pvs[1][rows]).astype(BF16)


def _attn(sink, q, k, v, bias):
    B = q.shape[0]
    edge = lambda f: pl.BlockSpec((None, BLOCK, KV_WIDTH), lambda b, i: (b, f(i), 0))
    left = lambda i: jnp.maximum(i * ATT_SUB - 1, 0)
    right = lambda i: jnp.minimum((i + 1) * ATT_SUB, N_BLOCKS - 1)
    mid = pl.BlockSpec((None, ATT_ROWS, KV_WIDTH), lambda b, i: (b, i, 0))
    qspec = pl.BlockSpec((None, ATT_ROWS, ATTN_WIDTH), lambda b, i: (b, i, 0))
    return pl.pallas_call(
        _attn_kernel,
        grid=(B, ATT_STEPS),
        in_specs=[pl.BlockSpec(memory_space=pltpu.SMEM), qspec,
                  edge(left), mid, edge(right),
                  edge(left), mid, edge(right),
                  pl.BlockSpec((3, N_Q_HEADS, BLOCK, SPAN), lambda b, i: (0, 0, 0, 0))],
        out_specs=qspec,
        out_shape=jax.ShapeDtypeStruct((B, SEQ, ATTN_WIDTH), BF16),
        compiler_params=_params(("parallel", "parallel"), VMEM_LIMIT),
        name="attn",
    )(sink, q, k, k, k, v, v, v, bias)


OUT_ROWS = 256


def _outproj_kernel(yf_ref, ya_ref, x_ref, mod_ref, g_ref, wf_ref, wa_ref, wrh_ref, wrl_ref,
                    x1_ref, h2_ref, aff_ref):
    tm = x_ref.shape[0]
    gain = g_ref[...] * (1.0 + mod_ref[4:5, :])
    shift = mod_ref[3:4, :]
    gate1 = mod_ref[2:3, :]
    lane = lax.broadcasted_iota(jnp.int32, (OUT_ROWS, LANES), 1)
    for c in range(tm // OUT_ROWS):
        rows = pl.ds(c * OUT_ROWS, OUT_ROWS)
        mixed = jnp.dot(yf_ref[rows, :], wf_ref[...], preferred_element_type=F32)
        mixed = mixed + jnp.dot(ya_ref[rows, :], wa_ref[...], preferred_element_type=F32)
        x1 = x_ref[rows, :] + gate1 * mixed
        x1_ref[rows, :] = x1
        ms = jnp.mean(x1 * x1, axis=-1, keepdims=True)
        h2 = x1 * lax.rsqrt(ms + EPS) * gain + shift
        for j in range(ROW_SLAB):
            h2_ref[pl.ds(c * OUT_ROWS * ROW_SLAB + j, OUT_ROWS, stride=ROW_SLAB), :] = (
                h2[:, j * LANES:(j + 1) * LANES])
        hi = h2.astype(BF16)
        lo = (h2 - hi.astype(F32)).astype(BF16)
        part = jnp.dot(hi, wrh_ref[...], preferred_element_type=F32)
        part = part + jnp.dot(lo, wrl_ref[...], preferred_element_type=F32)
        logits = part + pltpu.roll(part, LANES - N_EXPERTS, axis=1)
        logits = jnp.where(lane < N_EXPERTS, logits, NEG_INF)
        m = jnp.max(logits, axis=-1, keepdims=True)
        e = jnp.exp(logits - m)
        aff_ref[rows, :] = e / jnp.sum(e, axis=-1, keepdims=True)


def _outproj(yf, ya, x, mod, g, wf, wa, wrh, wrl, tm=512):
    B = x.shape[0]
    const = lambda shape: pl.BlockSpec(shape, lambda b, i: (0,) * len(shape))
    tok = lambda w: pl.BlockSpec((None, tm, w), lambda b, i: (b, i, 0))
    return pl.pallas_call(
        _outproj_kernel,
        grid=(B, SEQ // tm),
        in_specs=[tok(FOURIER_WIDTH), tok(ATTN_WIDTH), tok(D_MODEL),
                  pl.BlockSpec((None, N_ADA, D_MODEL), lambda b, i: (b, 0, 0)),
                  const((1, D_MODEL)),
                  const((FOURIER_WIDTH, D_MODEL)), const((ATTN_WIDTH, D_MODEL)),
                  const((D_MODEL, LANES)), const((D_MODEL, LANES))],
        out_specs=[tok(D_MODEL),
                   pl.BlockSpec((None, tm * ROW_SLAB, LANES), lambda b, i: (b, i, 0)),
                   tok(LANES)],
        out_shape=[jax.ShapeDtypeStruct((B, SEQ, D_MODEL), F32),
                   jax.ShapeDtypeStruct((B, SEQ * ROW_SLAB, LANES), F32),
                   jax.ShapeDtypeStruct((B, SEQ, LANES), F32)],
        compiler_params=_params(("parallel", "parallel"), VMEM_LIMIT),
        name="outproj",
    )(yf, ya, x, mod, g, wf, wa, wrh, wrl)


def _route_kernel(aff_ref, tri_ref, idx_ref, gate_ref):
    aff = aff_ref[...]
    aff_t = jnp.transpose(aff)[:N_EXPERTS]
    bits = pltpu.bitcast(aff_t, jnp.int32)
    cap = float(CAPACITY)

    t = jnp.zeros((N_EXPERTS, 1), jnp.int32)
    for bit in range(30, -1, -1):
        cand = t | (1 << bit)
        cnt = jnp.sum(jnp.where(bits >= cand, 1.0, 0.0), axis=1, keepdims=True)
        t = jnp.where(cnt >= cap, cand, t)
    gt = bits > t
    eq = bits == t
    need = cap - jnp.sum(jnp.where(gt, 1.0, 0.0), axis=1, keepdims=True)

    tri = tri_ref[...]
    n_chunks = SEQ // LANES

    def prefix(flags_f32):
        outs = []
        carry = jnp.zeros((N_EXPERTS, 1), F32)
        for c in range(n_chunks):
            f = flags_f32[:, c * LANES:(c + 1) * LANES]
            incl = jnp.dot(f.astype(BF16), tri, preferred_element_type=F32)
            outs.append(incl - f + carry)
            carry = carry + jnp.sum(f, axis=1, keepdims=True)
        return jnp.concatenate(outs, axis=1)

    eq_f = jnp.where(eq, 1.0, 0.0)
    eq_rank = prefix(eq_f)
    sel_f = jnp.where(gt, 1.0, jnp.where(eq_rank < need, eq_f, 0.0))
    pos = prefix(sel_f)
    posm = jnp.where(sel_f > 0.0, pos, -1.0)

    hi = aff.astype(BF16).astype(F32)
    r1 = aff - hi
    mid = r1.astype(BF16).astype(F32)
    lo = r1 - mid
    lane = lax.broadcasted_iota(jnp.int32, (SEQ, LANES), 1)
    tok = lax.broadcasted_iota(jnp.int32, (SEQ, LANES), 0)
    vals = hi + pltpu.roll(mid, N_EXPERTS, axis=1) + pltpu.roll(lo, 2 * N_EXPERTS, axis=1)
    vals = vals + jnp.where(lane == 3 * N_EXPERTS, (tok >> 6).astype(F32), 0.0)
    vals = vals + jnp.where(lane == 3 * N_EXPERTS + 1, (tok & 63).astype(F32), 0.0)
    vals = vals.astype(BF16)

    slot = lax.broadcasted_iota(jnp.int32, (CAPACITY, SEQ), 0).astype(F32).astype(BF16)
    posm_b = posm.astype(BF16)
    one_b = jnp.ones((CAPACITY, SEQ), BF16)
    zero_b = jnp.zeros((CAPACITY, SEQ), BF16)
    lane_c = lax.broadcasted_iota(jnp.int32, (CAPACITY, LANES), 1)
    w_idx = jnp.where(lane_c == 3 * N_EXPERTS, 64.0,
                      jnp.where(lane_c == 3 * N_EXPERTS + 1, 1.0, 0.0))
    for e in range(N_EXPERTS):
        onehot = jnp.where(posm_b[e:e + 1, :] == slot, one_b, zero_b)
        res = jnp.dot(onehot, vals, preferred_element_type=F32)
        w_g = jnp.where((lane_c == e) | (lane_c == N_EXPERTS + e) | (lane_c == 2 * N_EXPERTS + e),
                        1.0, 0.0)
        rows = pl.ds(e * CAPACITY, CAPACITY)
        idx_ref[rows, :] = jnp.sum(res * w_idx, axis=1, keepdims=True).astype(jnp.int32) * ROW_SLAB
        gate_ref[rows, :] = jnp.sum(res * w_g, axis=1, keepdims=True)


def _route(aff):
    B = aff.shape[0]
    n = N_EXPERTS * CAPACITY
    return pl.pallas_call(
        _route_kernel,
        grid=(B,),
        in_specs=[pl.BlockSpec((None, SEQ, LANES), lambda b: (b, 0, 0)),
                  pl.BlockSpec((LANES, LANES), lambda b: (0, 0))],
        out_specs=[pl.BlockSpec((None, n, 1), lambda b: (b, 0, 0)),
                   pl.BlockSpec((None, n, 1), lambda b: (b, 0, 0))],
        out_shape=[jax.ShapeDtypeStruct((B, n, 1), jnp.int32),
                   jax.ShapeDtypeStruct((B, n, 1), F32)],
        compiler_params=_params(("parallel",), VMEM_LIMIT),
        name="route",
    )(aff, jnp.asarray(_tri_incl()).astype(BF16))


PAIR = 2


def _moe_kernel(idx_ref, gate_ref, h2_ref, wg_ref, wu_ref, wd_ref, y_ref, xin0_ref, xin1_ref):
    e = pl.program_id(1)
    last = N_EXPERTS - 1
    n = N_EXPERTS * CAPACITY
    rows = PAIR * CAPACITY

    def gather_rows(ex, dst_ref):
        for bb in range(PAIR):
            base = bb * n + ex * CAPACITY
            for p in range(CAPACITY):
                src = pl.multiple_of(idx_ref[0, base + p], ROW_SLAB)
                dst_ref[pl.ds((bb * CAPACITY + p) * ROW_SLAB, ROW_SLAB), :] = (
                    h2_ref[bb, pl.ds(src, ROW_SLAB), :])

    def expert(xin_ref):
        xin = jnp.concatenate(
            [xin_ref[pl.ds(j, rows, stride=ROW_SLAB), :].astype(BF16)
             for j in range(ROW_SLAB)], axis=1)
        a = jnp.dot(xin, wg_ref[...], preferred_element_type=F32)
        u = jnp.dot(xin, wu_ref[...], preferred_element_type=F32)
        hmid = (a * (1.0 / (1.0 + jnp.exp(-a))) * u).astype(BF16)
        y = jnp.dot(hmid, wd_ref[...], preferred_element_type=F32)
        y = y * gate_ref[...].reshape(rows, 1)
        for bb in range(PAIR):
            for j in range(ROW_SLAB):
                y_ref[bb, pl.ds(j, CAPACITY, stride=ROW_SLAB), :] = (
                    y[bb * CAPACITY:(bb + 1) * CAPACITY, j * LANES:(j + 1) * LANES])

    @pl.when(e == 0)
    def _():
        gather_rows(0, xin0_ref)

    def step(xin_cur, xin_nxt):
        gather_rows(jnp.minimum(e + 1, last), xin_nxt)
        expert(xin_cur)

    @pl.when(e % 2 == 0)
    def _():
        step(xin0_ref, xin1_ref)

    @pl.when(e % 2 == 1)
    def _():
        step(xin1_ref, xin0_ref)


def _moe(idx, gate, h2, wg, wu, wd):
    B = h2.shape[0]
    n = N_EXPERTS * CAPACITY
    rows = SEQ * ROW_SLAB
    pairs = B // PAIR
    wspec = lambda r, c: pl.BlockSpec((None, r, c), lambda b, e: (e, 0, 0))
    stage = pltpu.VMEM((PAIR * CAPACITY * ROW_SLAB, LANES), F32)
    out = pl.pallas_call(
        _moe_kernel,
        grid=(pairs, N_EXPERTS),
        in_specs=[pl.BlockSpec((None, 1, PAIR * n), lambda b, e: (b, 0, 0),
                               memory_space=pltpu.SMEM),
                  pl.BlockSpec((None, PAIR, CAPACITY, 1), lambda b, e: (b, 0, e, 0)),
                  pl.BlockSpec((None, PAIR, rows, LANES), lambda b, e: (b, 0, 0, 0),
                               pipeline_mode=pl.Buffered(1)),
                  wspec(D_MODEL, D_EXPERT), wspec(D_MODEL, D_EXPERT), wspec(D_EXPERT, D_MODEL)],
        out_specs=pl.BlockSpec((None, PAIR, CAPACITY * ROW_SLAB, LANES),
                               lambda b, e: (b, 0, e, 0)),
        out_shape=jax.ShapeDtypeStruct((pairs, PAIR, n * ROW_SLAB, LANES), F32),
        scratch_shapes=[stage, stage],
        compiler_params=_params(("parallel", "arbitrary"), VMEM_LIMIT),
        name="moe",
    )(idx.reshape(pairs, 1, PAIR * n), gate.reshape(pairs, PAIR, n, 1),
      h2.reshape(pairs, PAIR, rows, LANES), wg, wu, wd)
    return out.reshape(B, n * ROW_SLAB, LANES)


COMBINE_EXPERTS = 4
SCATTER_UNROLL = 8
COMBINE_ROWS = 256


def _combine_kernel(idx_ref, y_ref, x1_ref, mod_ref, o_ref, acc_ref):
    j = pl.program_id(1)
    slots = COMBINE_EXPERTS * CAPACITY
    base = j * slots

    @pl.when(j == 0)
    def _():
        acc_ref[...] = jnp.zeros_like(acc_ref)

    for g in range(slots // SCATTER_UNROLL):
        new = []
        for u in range(SCATTER_UNROLL):
            r = g * SCATTER_UNROLL + u
            dst = pl.multiple_of(idx_ref[0, base + r], ROW_SLAB)
            new.append((dst, acc_ref[pl.ds(dst, ROW_SLAB), :]
                        + y_ref[pl.ds(r * ROW_SLAB, ROW_SLAB), :]))
        for dst, val in new:
            acc_ref[pl.ds(dst, ROW_SLAB), :] = val

    @pl.when(j == pl.num_programs(1) - 1)
    def _():
        for rb in range(SEQ // COMBINE_ROWS):
            rows = pl.ds(rb * COMBINE_ROWS, COMBINE_ROWS)
            for c in range(ROW_SLAB):
                cols = slice(c * LANES, (c + 1) * LANES)
                chunk = acc_ref[pl.ds(rb * COMBINE_ROWS * ROW_SLAB + c, COMBINE_ROWS,
                                      stride=ROW_SLAB), :]
                o_ref[rows, cols] = x1_ref[rows, cols] + mod_ref[5:6, cols] * chunk


def _combine(idx, y, x1, mod):
    B = x1.shape[0]
    n = N_EXPERTS * CAPACITY
    tok = pl.BlockSpec((None, SEQ, D_MODEL), lambda b, j: (b, 0, 0))
    return pl.pallas_call(
        _combine_kernel,
        grid=(B, N_EXPERTS // COMBINE_EXPERTS),
        in_specs=[pl.BlockSpec((None, 1, n), lambda b, j: (b, 0, 0), memory_space=pltpu.SMEM),
                  pl.BlockSpec((None, COMBINE_EXPERTS * CAPACITY * ROW_SLAB, LANES),
                               lambda b, j: (b, j, 0)),
                  tok,
                  pl.BlockSpec((None, N_ADA, D_MODEL), lambda b, j: (b, 0, 0))],
        out_specs=tok,
        out_shape=jax.ShapeDtypeStruct((B, SEQ, D_MODEL), F32),
        scratch_shapes=[pltpu.VMEM((SEQ * ROW_SLAB, LANES), F32)],
        compiler_params=_params(("parallel", "arbitrary"), VMEM_LIMIT),
        name="combine",
    )(idx, y, x1, mod)


def _head_perm():
    perm = []
    for r in range(Q_PER_KV):
        for kv in range(N_KV_HEADS):
            h = kv * Q_PER_KV + r
            perm.extend(range(h * HEAD_DIM, (h + 1) * HEAD_DIM))
    return np.asarray(perm, dtype=np.int32)


def kernel(x, c, rel_bias, w_ada, b_ada, norm_mix_g, norm_ffn_g, w_in, w_fourier, b_fourier,
           q_norm_g, k_norm_g, sink, w_out, w_router, w_gate, w_up, w_down):
    B = x.shape[0]
    perm = _head_perm()
    l = 0
    mod = _ada(c, w_ada[l], b_ada[l]).reshape(B, N_ADA, D_MODEL)
    pq = _fold(w_fourier[l])
    bias = _bias_table(rel_bias)

    wi = w_in[l]
    q_cols = wi[:, FOURIER_WIDTH:FOURIER_WIDTH + ATTN_WIDTH][:, perm]
    win = jnp.concatenate([wi[:, :FOURIER_WIDTH], q_cols, wi[:, FOURIER_WIDTH + ATTN_WIDTH:]],
                          axis=1).astype(BF16)
    gq = (jnp.tile(q_norm_g[l], N_Q_HEADS) * (HEAD_DIM ** -0.5 * LOG2E)).reshape(1, ATTN_WIDTH)
    gk = jnp.tile(k_norm_g[l], N_KV_HEADS).reshape(1, KV_WIDTH)
    a, b, q, k, v = _inproj(x, mod, norm_mix_g[l].reshape(1, D_MODEL), win, pq, gq, gk)

    yf = _fourier(a, b, b_fourier[l].reshape(1, FOURIER_WIDTH))
    ya = _attn(sink[l], q, k, v, bias)

    wo = w_out[l]
    wf = wo[:FOURIER_WIDTH].astype(BF16)
    wa = wo[FOURIER_WIDTH:][perm].astype(BF16)
    w_hi = w_router[l].astype(BF16)
    w_lo = (w_router[l] - w_hi.astype(F32)).astype(BF16)
    wrh = jnp.pad(jnp.concatenate([w_hi, w_lo], axis=1), ((0, 0), (0, LANES - 2 * N_EXPERTS)))
    wrl = jnp.pad(w_hi, ((0, 0), (0, LANES - N_EXPERTS)))
    x1, h2, aff = _outproj(yf, ya, x, mod, norm_ffn_g[l].reshape(1, D_MODEL), wf, wa, wrh, wrl)

    idx, gate = _route(aff)
    n = N_EXPERTS * CAPACITY
    idx = idx.reshape(B, 1, n)
    y = _moe(idx, gate, h2,
             w_gate[l].astype(BF16), w_up[l].astype(BF16), w_down[l].astype(BF16))
    return _combine(idx, y, x1, mod)
```

```python
import functools
import math

import numpy as np
import jax
import jax.numpy as jnp
from jax import lax
from jax.experimental import pallas as pl
from jax.experimental.pallas import tpu as pltpu

D_MODEL = 1024
SEQ = 2048
HEAD_DIM = 64
FOURIER_WIDTH = 512
ATTN_WIDTH = 512
N_GROUPS = 8
N_Q_HEADS = 8
Q_PER_KV = 4
N_KV_HEADS = 2
KV_WIDTH = 128
IN_PROJ_WIDTH = 1280
WINDOW = 128
BLOCK = 128
SPAN = BLOCK + 2 * WINDOW
N_BLOCKS = SEQ // BLOCK
N_BUCKETS = 32
MAX_DISTANCE = 128
N_EXPERTS = 16
CAPACITY = 2 * SEQ // N_EXPERTS
D_EXPERT = 1024
N_ADA = 6
EPS = 1e-6

LANES = 128
SUBLANES = 8
ROW_SLAB = D_MODEL // LANES
VMEM_LIMIT = 56 * 1024 * 1024

F32 = jnp.float32
BF16 = jnp.bfloat16
NEG_INF = float("-inf")
LOG2E = math.log2(math.e)


def _params(sem, vmem=None):
    return pltpu.CompilerParams(dimension_semantics=sem, vmem_limit_bytes=vmem)


@functools.lru_cache(maxsize=None)
def _chan_dft():
    c = np.arange(HEAD_DIM, dtype=np.int64)
    ph = (c[:, None] * c[None, :]) % HEAD_DIM
    ang = 2.0 * np.pi * ph.astype(np.float64) / HEAD_DIM
    sc = 1.0 / math.sqrt(HEAD_DIM)
    eye = np.eye(N_GROUPS)
    cbd = np.kron(eye, np.cos(ang) * sc)
    sbd = np.kron(eye, np.sin(ang) * sc)
    return cbd.astype(np.float32), sbd.astype(np.float32)


@functools.lru_cache(maxsize=None)
def _bucket_table():
    rel = np.arange(SPAN)[None, :] - WINDOW - np.arange(BLOCK)[:, None]
    half = N_BUCKETS // 2
    max_exact = half // 2
    n = np.abs(rel)
    nf = np.maximum(n, 1).astype(np.float64)
    large = max_exact + (np.log(nf / max_exact) / math.log(MAX_DISTANCE / max_exact)
                         * (half - max_exact)).astype(np.int64)
    sq = np.maximum(n.astype(np.int64) ** 2 // (max_exact * max_exact), 1)
    large_int = max_exact + np.floor(np.log2(sq.astype(np.float64)) + 1e-9).astype(np.int64)
    assert np.array_equal(np.where(n >= max_exact, large, 0), np.where(n >= max_exact, large_int, 0))
    large = np.minimum(large, half - 1)
    bucket = np.where(rel > 0, half, 0) + np.where(n < max_exact, n, large)
    return bucket.astype(np.int32)


@functools.lru_cache(maxsize=None)
def _group_ones(width):
    return np.kron(np.eye(width // HEAD_DIM), np.ones((HEAD_DIM, HEAD_DIM))).astype(np.float32)


@functools.lru_cache(maxsize=None)
def _tri_incl():
    i = np.arange(LANES)
    return (i[:, None] <= i[None, :]).astype(np.float32)


def _ada_kernel(c_ref, w_ref, b_ref, o_ref):
    c = c_ref[...]
    ca = c * (1.0 / (1.0 + jnp.exp(-c)))
    o_ref[...] = jnp.dot(ca, w_ref[...], precision=lax.Precision.HIGHEST,
                         preferred_element_type=F32) + b_ref[...]


def _ada(c, w_ada, b_ada):
    B = c.shape[0]
    n = N_ADA * D_MODEL
    tn = D_MODEL
    return pl.pallas_call(
        _ada_kernel,
        grid=(n // tn,),
        in_specs=[pl.BlockSpec((B, D_MODEL), lambda j: (0, 0)),
                  pl.BlockSpec((D_MODEL, tn), lambda j: (0, j)),
                  pl.BlockSpec((1, tn), lambda j: (0, j))],
        out_specs=pl.BlockSpec((B, tn), lambda j: (0, j)),
        out_shape=jax.ShapeDtypeStruct((B, n), F32),
        compiler_params=_params(("arbitrary",)),
        name="ada",
    )(c, w_ada, b_ada.reshape(1, n))


def _fold_kernel(cbd_ref, sbd_ref, w_ref, o_ref):
    w = w_ref[...]
    o_ref[:, :FOURIER_WIDTH] = jnp.dot(cbd_ref[...], w, precision=lax.Precision.HIGHEST,
                                       preferred_element_type=F32).astype(BF16)
    o_ref[:, FOURIER_WIDTH:] = jnp.dot(sbd_ref[...], w, precision=lax.Precision.HIGHEST,
                                       preferred_element_type=F32).astype(BF16)


def _fold(w_fourier):
    wbd = (jnp.eye(N_GROUPS, dtype=F32)[:, None, :, None] * w_fourier[:, :, None, :]
           ).reshape(FOURIER_WIDTH, FOURIER_WIDTH)
    cbd, sbd = _chan_dft()
    return pl.pallas_call(
        _fold_kernel,
        out_shape=jax.ShapeDtypeStruct((FOURIER_WIDTH, 2 * FOURIER_WIDTH), BF16),
        name="fold",
    )(jnp.asarray(cbd), jnp.asarray(sbd), wbd)


def _bias_kernel(rb_ref, bucket_ref, o_ref):
    h = pl.program_id(0)
    bk = bucket_ref[...]
    acc = jnp.zeros((BLOCK, SPAN), F32)
    for b in range(N_BUCKETS):
        acc = jnp.where(bk == b, rb_ref[b, h], acc)
    j = lax.broadcasted_iota(jnp.int32, (BLOCK, SPAN), 1)
    q = lax.broadcasted_iota(jnp.int32, (BLOCK, SPAN), 0)
    band = jnp.abs(j - WINDOW - q) <= WINDOW
    base = jnp.where(band, acc * LOG2E, NEG_INF)
    o_ref[0] = jnp.where(j >= WINDOW, base, NEG_INF)
    o_ref[1] = base
    o_ref[2] = jnp.where(j < WINDOW + BLOCK, base, NEG_INF)


def _bias_table(rel_bias):
    return pl.pallas_call(
        _bias_kernel,
        grid=(N_Q_HEADS,),
        in_specs=[pl.BlockSpec(memory_space=pltpu.SMEM),
                  pl.BlockSpec((BLOCK, SPAN), lambda h: (0, 0))],
        out_specs=pl.BlockSpec((3, None, BLOCK, SPAN), lambda h: (0, h, 0, 0)),
        out_shape=jax.ShapeDtypeStruct((3, N_Q_HEADS, BLOCK, SPAN), F32),
        compiler_params=_params(("arbitrary",)),
        name="bias",
    )(rel_bias, jnp.asarray(_bucket_table()))


def _inproj_kernel(x_ref, mod_ref, g_ref, win_ref, pq_ref, bdq_ref, bdk_ref, gq_ref, gk_ref,
                   a_ref, b_ref, q_ref, k_ref, v_ref):
    x = x_ref[...]
    ms = jnp.mean(x * x, axis=-1, keepdims=True)
    y = x * lax.rsqrt(ms + EPS) * g_ref[...]
    h = y * (1.0 + mod_ref[1:2, :]) + mod_ref[0:1, :]
    proj = jnp.dot(h.astype(BF16), win_ref[...], preferred_element_type=F32)
    uf = proj[:, :FOURIER_WIDTH].astype(BF16)
    ab = jnp.dot(uf, pq_ref[...], preferred_element_type=F32)
    a_ref[...] = ab[:, :FOURIER_WIDTH].astype(BF16)
    b_ref[...] = ab[:, FOURIER_WIDTH:].astype(BF16)
    q0 = FOURIER_WIDTH
    k0 = q0 + ATTN_WIDTH
    v0 = k0 + KV_WIDTH
    q = proj[:, q0:k0]
    ssq = jnp.dot((q * q).astype(BF16), bdq_ref[...], preferred_element_type=F32)
    q_ref[...] = (q * lax.rsqrt(ssq * (1.0 / HEAD_DIM) + EPS) * gq_ref[...]).astype(BF16)
    k = proj[:, k0:v0]
    ssk = jnp.dot((k * k).astype(BF16), bdk_ref[...], preferred_element_type=F32)
    k_ref[...] = (k * lax.rsqrt(ssk * (1.0 / HEAD_DIM) + EPS) * gk_ref[...]).astype(BF16)
    v_ref[...] = proj[:, v0:].astype(BF16)


def _inproj(x, mod, g, win, pq, gq, gk, tm=512):
    B = x.shape[0]
    const = lambda shape: pl.BlockSpec(shape, lambda b, i: (0,) * len(shape))
    tok = lambda w: pl.BlockSpec((None, tm, w), lambda b, i: (b, i, 0))
    sds = lambda w: jax.ShapeDtypeStruct((B, SEQ, w), BF16)
    return pl.pallas_call(
        _inproj_kernel,
        grid=(B, SEQ // tm),
        in_specs=[tok(D_MODEL),
                  pl.BlockSpec((None, N_ADA, D_MODEL), lambda b, i: (b, 0, 0)),
                  const((1, D_MODEL)),
                  const((D_MODEL, IN_PROJ_WIDTH)),
                  const((FOURIER_WIDTH, 2 * FOURIER_WIDTH)),
                  const((ATTN_WIDTH, ATTN_WIDTH)),
                  const((KV_WIDTH, KV_WIDTH)),
                  const((1, ATTN_WIDTH)),
                  const((1, KV_WIDTH))],
        out_specs=[tok(FOURIER_WIDTH), tok(FOURIER_WIDTH), tok(ATTN_WIDTH), tok(KV_WIDTH),
                   tok(KV_WIDTH)],
        out_shape=[sds(FOURIER_WIDTH), sds(FOURIER_WIDTH), sds(ATTN_WIDTH), sds(KV_WIDTH),
                   sds(KV_WIDTH)],
        compiler_params=_params(("parallel", "parallel"), VMEM_LIMIT),
        name="inproj",
    )(x, mod, g, win, pq, jnp.asarray(_group_ones(ATTN_WIDTH)).astype(BF16),
      jnp.asarray(_group_ones(KV_WIDTH)).astype(BF16), gq, gk)


HALF = SEQ // 2
REV = 128
REV_BLOCKS = HALF // REV
FOURIER_ROWS = 512


def _fourier_kernel(cm_ref, sm_ref, psh_ref, alt_ref, altrow_ref, a_ref, b_ref, bf_ref, o_ref):
    psh = psh_ref[...]

    def reversed_block(win_lo, k, src):
        if k == 0:
            return jnp.dot(psh[:, :REV], src(win_lo, REV), preferred_element_type=F32)
        return jnp.dot(psh, src(win_lo, 2 * REV), preferred_element_type=F32)

    def folded(ref, sign):
        blocks = []
        for k in range(REV_BLOCKS):
            lo = SEQ - REV * (k + 1)
            rev = reversed_block(lo, k, lambda s, n: ref[pl.ds(s, n), :])
            blocks.append((ref[pl.ds(k * REV, REV), :].astype(F32) + sign * rev).astype(BF16))
        return jnp.concatenate(blocks, axis=0)

    a_even = folded(a_ref, 1.0)
    b_odd = folded(b_ref, -1.0)
    a_mid = a_ref[pl.ds(HALF, 1), :].astype(F32)
    bias = bf_ref[...]
    z_blocks = []
    for i in range(HALF // FOURIER_ROWS):
        rows = pl.ds(i * FOURIER_ROWS, FOURIER_ROWS)
        yc = jnp.dot(cm_ref[rows, :], a_even, preferred_element_type=F32)
        yc = yc + alt_ref[rows, :] * a_mid + bias
        ys = jnp.dot(sm_ref[rows, :], b_odd, preferred_element_type=F32)
        o_ref[rows, :] = (yc - ys).astype(BF16)
        z_blocks.append((yc + ys).astype(BF16))
    z = jnp.concatenate(z_blocks, axis=0)
    for k in range(REV_BLOCKS):
        lo = HALF - REV * (k + 1)
        top = reversed_block(lo, k, lambda s, n: z[s:s + n])
        o_ref[pl.ds(HALF + k * REV, REV), :] = top.astype(BF16)
    y_mid = jnp.dot(altrow_ref[...], a_ref[...], preferred_element_type=F32)[0:1, :] + bias
    o_ref[pl.ds(HALF, 1), :] = y_mid.astype(BF16)


@functools.lru_cache(maxsize=None)
def _fold_tables():
    s = np.arange(HALF, dtype=np.int64)
    ph = (s[:, None] * s[None, :]) % SEQ
    ang = 2.0 * np.pi * ph.astype(np.float64) / SEQ
    sc = 1.0 / math.sqrt(SEQ)
    cm = (np.cos(ang) * sc).astype(np.float32)
    sm = (np.sin(ang) * sc).astype(np.float32)
    psh = np.zeros((REV, 2 * REV), np.float32)
    psh[np.arange(REV), REV - np.arange(REV)] = 1.0
    alt = (np.where(s % 2 == 0, 1.0, -1.0) * sc).astype(np.float32).reshape(HALF, 1)
    t = np.arange(SEQ)
    altrow = np.zeros((SUBLANES, SEQ), np.float32)
    altrow[0] = np.where(t % 2 == 0, 1.0, -1.0) * sc
    return cm, sm, psh, alt, altrow


def _fourier(a, b, bf):
    B = a.shape[0]
    cm, sm, psh, alt, altrow = _fold_tables()
    tok = pl.BlockSpec((None, SEQ, FOURIER_WIDTH), lambda i: (i, 0, 0))
    const = lambda shape: pl.BlockSpec(shape, lambda i: (0,) * len(shape))
    return pl.pallas_call(
        _fourier_kernel,
        grid=(B,),
        in_specs=[const((HALF, HALF)), const((HALF, HALF)), const((REV, 2 * REV)),
                  const((HALF, 1)), const((SUBLANES, SEQ)),
                  tok, tok, const((1, FOURIER_WIDTH))],
        out_specs=tok,
        out_shape=jax.ShapeDtypeStruct((B, SEQ, FOURIER_WIDTH), BF16),
        compiler_params=_params(("parallel",), VMEM_LIMIT),
        name="fourier",
    )(jnp.asarray(cm).astype(BF16), jnp.asarray(sm).astype(BF16), jnp.asarray(psh).astype(BF16),
      jnp.asarray(alt), jnp.asarray(altrow).astype(BF16), a, b, bf)


ATT_SUB = 4
ATT_ROWS = ATT_SUB * BLOCK
ATT_STEPS = N_BLOCKS // ATT_SUB


def _attn_kernel(sink_ref, q_ref, kl_ref, km_ref, kr_ref, vl_ref, vm_ref, vr_ref, bias_ref,
                 o_ref):
    i = pl.program_id(1)
    keys = jnp.concatenate([kl_ref[...], km_ref[...], kr_ref[...]], axis=0)
    vals = jnp.concatenate([vl_ref[...], vm_ref[...], vr_ref[...]], axis=0)
    first_k = lax.broadcasted_iota(jnp.int32, keys.shape, 1) < HEAD_DIM
    first_q = lax.broadcasted_iota(jnp.int32, (BLOCK, 2 * HEAD_DIM), 1) < HEAD_DIM
    row_head = lax.broadcasted_iota(jnp.int32, (Q_PER_KV * BLOCK, 1), 0) // BLOCK
    keys_kv = [jnp.where(first_k, keys, jnp.zeros_like(keys)),
               jnp.where(first_k, jnp.zeros_like(keys), keys)]
    sinks = []
    for kv in range(N_KV_HEADS):
        sink = jnp.zeros((Q_PER_KV * BLOCK, 1), F32)
        for r in range(Q_PER_KV):
            sink = jnp.where(row_head == r, sink_ref[kv * Q_PER_KV + r] * LOG2E, sink)
        sinks.append(sink)
    for j in range(ATT_SUB):
        variant = 1
        if j == 0:
            variant = jnp.where(i == 0, 0, variant)
        if j == ATT_SUB - 1:
            variant = jnp.where(i == ATT_STEPS - 1, 2, variant)
        qrows = pl.ds(j * BLOCK, BLOCK)
        krows = slice(j * BLOCK, j * BLOCK + SPAN)
        qs = jnp.concatenate([q_ref[qrows, r * LANES:(r + 1) * LANES] for r in range(Q_PER_KV)],
                             axis=0)
        pvs = []
        for kv in range(N_KV_HEADS):
            logits = lax.dot_general(qs, keys_kv[kv][krows], (((1,), (1,)), ((), ())),
                                     preferred_element_type=F32)
            bias = bias_ref[variant, pl.ds(kv * Q_PER_KV, Q_PER_KV)]
            logits = logits + bias.reshape(Q_PER_KV * BLOCK, SPAN)
            sink = sinks[kv]
            m = jnp.maximum(jnp.max(logits, axis=-1, keepdims=True), sink)
            p = jnp.exp2(logits - m)
            denom = jnp.sum(p, axis=-1, keepdims=True) + jnp.exp2(sink - m)
            pv = jnp.dot(p.astype(BF16), vals[krows], preferred_element_type=F32)
            pvs.append(pv / denom)
        for r in range(Q_PER_KV):
            rows = slice(r * BLOCK, (r + 1) * BLOCK)
            o_ref[qrows, r * LANES:(r + 1) * LANES] = jnp.where(
                first_q, pvs[0][rows], pvs[1][rows]).astype(BF16)


def _attn(sink, q, k, v, bias):
    B = q.shape[0]
    edge = lambda f: pl.BlockSpec((None, BLOCK, KV_WIDTH), lambda b, i: (b, f(i), 0))
    left = lambda i: jnp.maximum(i * ATT_SUB - 1, 0)
    right = lambda i: jnp.minimum((i + 1) * ATT_SUB, N_BLOCKS - 1)
    mid = pl.BlockSpec((None, ATT_ROWS, KV_WIDTH), lambda b, i: (b, i, 0))
    qspec = pl.BlockSpec((None, ATT_ROWS, ATTN_WIDTH), lambda b, i: (b, i, 0))
    return pl.pallas_call(
        _attn_kernel,
        grid=(B, ATT_STEPS),
        in_specs=[pl.BlockSpec(memory_space=pltpu.SMEM), qspec,
                  edge(left), mid, edge(right),
                  edge(left), mid, edge(right),
                  pl.BlockSpec((3, N_Q_HEADS, BLOCK, SPAN), lambda b, i: (0, 0, 0, 0))],
        out_specs=qspec,
        out_shape=jax.ShapeDtypeStruct((B, SEQ, ATTN_WIDTH), BF16),
        compiler_params=_params(("parallel", "parallel"), VMEM_LIMIT),
        name="attn",
    )(sink, q, k, k, k, v, v, v, bias)


OUT_ROWS = 256


def _outproj_kernel(yf_ref, ya_ref, x_ref, mod_ref, g_ref, wf_ref, wa_ref, wrh_ref, wrl_ref,
                    x1_ref, h2_ref, aff_ref):
    tm = x_ref.shape[0]
    gain = g_ref[...] * (1.0 + mod_ref[4:5, :])
    shift = mod_ref[3:4, :]
    gate1 = mod_ref[2:3, :]
    lane = lax.broadcasted_iota(jnp.int32, (OUT_ROWS, LANES), 1)
    for c in range(tm // OUT_ROWS):
        rows = pl.ds(c * OUT_ROWS, OUT_ROWS)
        mixed = jnp.dot(yf_ref[rows, :], wf_ref[...], preferred_element_type=F32)
        mixed = mixed + jnp.dot(ya_ref[rows, :], wa_ref[...], preferred_element_type=F32)
        x1 = x_ref[rows, :] + gate1 * mixed
        x1_ref[rows, :] = x1
        ms = jnp.mean(x1 * x1, axis=-1, keepdims=True)
        h2 = x1 * lax.rsqrt(ms + EPS) * gain + shift
        for j in range(ROW_SLAB):
            h2_ref[pl.ds(c * OUT_ROWS * ROW_SLAB + j, OUT_ROWS, stride=ROW_SLAB), :] = (
                h2[:, j * LANES:(j + 1) * LANES])
        hi = h2.astype(BF16)
        lo = (h2 - hi.astype(F32)).astype(BF16)
        part = jnp.dot(hi, wrh_ref[...], preferred_element_type=F32)
        part = part + jnp.dot(lo, wrl_ref[...], preferred_element_type=F32)
        logits = part + pltpu.roll(part, LANES - N_EXPERTS, axis=1)
        logits = jnp.where(lane < N_EXPERTS, logits, NEG_INF)
        m = jnp.max(logits, axis=-1, keepdims=True)
        e = jnp.exp(logits - m)
        aff_ref[rows, :] = e / jnp.sum(e, axis=-1, keepdims=True)


def _outproj(yf, ya, x, mod, g, wf, wa, wrh, wrl, tm=512):
    B = x.shape[0]
    const = lambda shape: pl.BlockSpec(shape, lambda b, i: (0,) * len(shape))
    tok = lambda w: pl.BlockSpec((None, tm, w), lambda b, i: (b, i, 0))
    return pl.pallas_call(
        _outproj_kernel,
        grid=(B, SEQ // tm),
        in_specs=[tok(FOURIER_WIDTH), tok(ATTN_WIDTH), tok(D_MODEL),
                  pl.BlockSpec((None, N_ADA, D_MODEL), lambda b, i: (b, 0, 0)),
                  const((1, D_MODEL)),
                  const((FOURIER_WIDTH, D_MODEL)), const((ATTN_WIDTH, D_MODEL)),
                  const((D_MODEL, LANES)), const((D_MODEL, LANES))],
        out_specs=[tok(D_MODEL),
                   pl.BlockSpec((None, tm * ROW_SLAB, LANES), lambda b, i: (b, i, 0)),
                   tok(LANES)],
        out_shape=[jax.ShapeDtypeStruct((B, SEQ, D_MODEL), F32),
                   jax.ShapeDtypeStruct((B, SEQ * ROW_SLAB, LANES), F32),
                   jax.ShapeDtypeStruct((B, SEQ, LANES), F32)],
        compiler_params=_params(("parallel", "parallel"), VMEM_LIMIT),
        name="outproj",
    )(yf, ya, x, mod, g, wf, wa, wrh, wrl)


ROUTE_BATCHES = 2
SEARCH_BITS = 3


def _route_kernel(aff_ref, tri_ref, idx_ref, gate_ref):
    for bb in range(ROUTE_BATCHES):
        _route_one(aff_ref[bb], tri_ref[...], idx_ref.at[bb], gate_ref.at[bb])


def _route_one(aff, tri, idx_ref, gate_ref):
    aff_t = jnp.transpose(aff)[:N_EXPERTS]
    bits = pltpu.bitcast(aff_t, jnp.int32)
    cap = float(CAPACITY)

    t = jnp.zeros((N_EXPERTS, 1), jnp.int32)
    for shift in range(30 - SEARCH_BITS, -1, -SEARCH_BITS):
        digit = jnp.zeros((N_EXPERTS, 1), jnp.int32)
        for k in range(1, 1 << SEARCH_BITS):
            cnt = jnp.sum(jnp.where(bits >= (t | (k << shift)), 1.0, 0.0), axis=1, keepdims=True)
            digit = digit + jnp.where(cnt >= cap, 1, 0)
        t = t | (digit << shift)
    gt = bits > t
    eq = bits == t
    need = cap - jnp.sum(jnp.where(gt, 1.0, 0.0), axis=1, keepdims=True)

    n_chunks = SEQ // LANES

    def prefix(flags_f32):
        outs = []
        carry = jnp.zeros((N_EXPERTS, 1), F32)
        for c in range(n_chunks):
            f = flags_f32[:, c * LANES:(c + 1) * LANES]
            incl = jnp.dot(f.astype(BF16), tri, preferred_element_type=F32)
            outs.append(incl - f + carry)
            carry = carry + jnp.sum(f, axis=1, keepdims=True)
        return jnp.concatenate(outs, axis=1)

    eq_f = jnp.where(eq, 1.0, 0.0)
    eq_rank = prefix(eq_f)
    sel_f = jnp.where(gt, 1.0, jnp.where(eq_rank < need, eq_f, 0.0))
    pos = prefix(sel_f)
    posm = jnp.where(sel_f > 0.0, pos, -1.0)

    hi = aff_t.astype(BF16).astype(F32)
    r1 = aff_t - hi
    mid = r1.astype(BF16).astype(F32)
    lo = r1 - mid
    tok = lax.broadcasted_iota(jnp.int32, (N_EXPERTS, SEQ), 1)
    row = lax.broadcasted_iota(jnp.int32, (N_EXPERTS, SEQ), 0)
    tok_rows = jnp.where(row == 0, (tok >> 6).astype(F32),
                         jnp.where(row == 1, (tok & 63).astype(F32), 0.0))
    vals_t = jnp.concatenate([hi, mid, lo, tok_rows], axis=0).astype(BF16)

    slot = lax.broadcasted_iota(jnp.int32, (CAPACITY, SEQ), 0).astype(F32).astype(BF16)
    posm_b = posm.astype(BF16)
    one_b = jnp.ones((CAPACITY, SEQ), BF16)
    zero_b = jnp.zeros((CAPACITY, SEQ), BF16)
    for e in range(N_EXPERTS):
        onehot = jnp.where(posm_b[e:e + 1, :] == slot, one_b, zero_b)
        res = lax.dot_general(vals_t, onehot, (((1,), (1,)), ((), ())),
                              preferred_element_type=F32)
        cols = pl.ds(e * CAPACITY, CAPACITY)
        tok_idx = res[3 * N_EXPERTS:3 * N_EXPERTS + 1] * 64.0 + res[3 * N_EXPERTS + 1:
                                                                    3 * N_EXPERTS + 2]
        idx_ref[:, cols] = tok_idx.astype(jnp.int32) * ROW_SLAB
        gate_ref[:, cols] = (res[e:e + 1] + res[N_EXPERTS + e:N_EXPERTS + e + 1]
                             + res[2 * N_EXPERTS + e:2 * N_EXPERTS + e + 1])


def _route(aff):
    B = aff.shape[0]
    n = N_EXPERTS * CAPACITY
    return pl.pallas_call(
        _route_kernel,
        grid=(B // ROUTE_BATCHES,),
        in_specs=[pl.BlockSpec((ROUTE_BATCHES, SEQ, LANES), lambda b: (b, 0, 0)),
                  pl.BlockSpec((LANES, LANES), lambda b: (0, 0))],
        out_specs=[pl.BlockSpec((ROUTE_BATCHES, 1, n), lambda b: (b, 0, 0)),
                   pl.BlockSpec((ROUTE_BATCHES, 1, n), lambda b: (b, 0, 0))],
        out_shape=[jax.ShapeDtypeStruct((B, 1, n), jnp.int32),
                   jax.ShapeDtypeStruct((B, 1, n), F32)],
        compiler_params=_params(("parallel",), VMEM_LIMIT),
        name="route",
    )(aff, jnp.asarray(_tri_incl()).astype(BF16))


PAIR = 2


def _moe_kernel(idx_ref, h2_ref, wg_ref, wu_ref, wd_ref, y_ref, xin0_ref, xin1_ref):
    e = pl.program_id(1)
    last = N_EXPERTS - 1
    n = N_EXPERTS * CAPACITY
    rows = PAIR * CAPACITY

    def gather_rows(ex, dst_ref):
        for bb in range(PAIR):
            base = bb * n + ex * CAPACITY
            for p in range(CAPACITY):
                src = pl.multiple_of(idx_ref[0, base + p], ROW_SLAB)
                dst_ref[pl.ds((bb * CAPACITY + p) * ROW_SLAB, ROW_SLAB), :] = (
                    h2_ref[bb, pl.ds(src, ROW_SLAB), :])

    def expert(xin_ref):
        xin = jnp.concatenate(
            [xin_ref[pl.ds(j, rows, stride=ROW_SLAB), :].astype(BF16)
             for j in range(ROW_SLAB)], axis=1)
        a = jnp.dot(xin, wg_ref[...], preferred_element_type=F32)
        u = jnp.dot(xin, wu_ref[...], preferred_element_type=F32)
        hmid = (a * (1.0 / (1.0 + jnp.exp(-a))) * u).astype(BF16)
        y = jnp.dot(hmid, wd_ref[...], preferred_element_type=F32)
        for bb in range(PAIR):
            for j in range(ROW_SLAB):
                y_ref[bb, pl.ds(j, CAPACITY, stride=ROW_SLAB), :] = (
                    y[bb * CAPACITY:(bb + 1) * CAPACITY, j * LANES:(j + 1) * LANES])

    @pl.when(e == 0)
    def _():
        gather_rows(0, xin0_ref)

    def step(xin_cur, xin_nxt):
        gather_rows(jnp.minimum(e + 1, last), xin_nxt)
        expert(xin_cur)

    @pl.when(e % 2 == 0)
    def _():
        step(xin0_ref, xin1_ref)

    @pl.when(e % 2 == 1)
    def _():
        step(xin1_ref, xin0_ref)


def _moe(idx, h2, wg, wu, wd):
    B = h2.shape[0]
    n = N_EXPERTS * CAPACITY
    rows = SEQ * ROW_SLAB
    pairs = B // PAIR
    wspec = lambda r, c: pl.BlockSpec((None, r, c), lambda b, e: (e, 0, 0))
    stage = pltpu.VMEM((PAIR * CAPACITY * ROW_SLAB, LANES), F32)
    out = pl.pallas_call(
        _moe_kernel,
        grid=(pairs, N_EXPERTS),
        in_specs=[pl.BlockSpec((None, 1, PAIR * n), lambda b, e: (b, 0, 0),
                               memory_space=pltpu.SMEM),
                  pl.BlockSpec((None, PAIR, rows, LANES), lambda b, e: (b, 0, 0, 0),
                               pipeline_mode=pl.Buffered(1)),
                  wspec(D_MODEL, D_EXPERT), wspec(D_MODEL, D_EXPERT), wspec(D_EXPERT, D_MODEL)],
        out_specs=pl.BlockSpec((None, PAIR, CAPACITY * ROW_SLAB, LANES),
                               lambda b, e: (b, 0, e, 0)),
        out_shape=jax.ShapeDtypeStruct((pairs, PAIR, n * ROW_SLAB, LANES), F32),
        scratch_shapes=[stage, stage],
        compiler_params=_params(("parallel", "arbitrary"), VMEM_LIMIT),
        name="moe",
    )(idx.reshape(pairs, 1, PAIR * n), h2.reshape(pairs, PAIR, rows, LANES), wg, wu, wd)
    return out.reshape(B, n * ROW_SLAB, LANES)


COMBINE_EXPERTS = 4
SCATTER_UNROLL = 8
COMBINE_ROWS = 256


def _combine_kernel(idx_ref, gate_ref, y_ref, x1_ref, mod_ref, o_ref, acc_ref):
    j = pl.program_id(1)
    slots = COMBINE_EXPERTS * CAPACITY
    base = j * slots

    @pl.when(j == 0)
    def _():
        acc_ref[...] = jnp.zeros_like(acc_ref)

    for g in range(slots // SCATTER_UNROLL):
        new = []
        for u in range(SCATTER_UNROLL):
            r = g * SCATTER_UNROLL + u
            dst = pl.multiple_of(idx_ref[0, base + r], ROW_SLAB)
            new.append((dst, acc_ref[pl.ds(dst, ROW_SLAB), :]
                        + y_ref[pl.ds(r * ROW_SLAB, ROW_SLAB), :] * gate_ref[0, base + r]))
        for dst, val in new:
            acc_ref[pl.ds(dst, ROW_SLAB), :] = val

    @pl.when(j == pl.num_programs(1) - 1)
    def _():
        for rb in range(SEQ // COMBINE_ROWS):
            rows = pl.ds(rb * COMBINE_ROWS, COMBINE_ROWS)
            for c in range(ROW_SLAB):
                cols = slice(c * LANES, (c + 1) * LANES)
                chunk = acc_ref[pl.ds(rb * COMBINE_ROWS * ROW_SLAB + c, COMBINE_ROWS,
                                      stride=ROW_SLAB), :]
                o_ref[rows, cols] = x1_ref[rows, cols] + mod_ref[5:6, cols] * chunk


def _combine(idx, gate, y, x1, mod):
    B = x1.shape[0]
    n = N_EXPERTS * CAPACITY
    tok = pl.BlockSpec((None, SEQ, D_MODEL), lambda b, j: (b, 0, 0))
    smem = pl.BlockSpec((None, 1, n), lambda b, j: (b, 0, 0), memory_space=pltpu.SMEM)
    return pl.pallas_call(
        _combine_kernel,
        grid=(B, N_EXPERTS // COMBINE_EXPERTS),
        in_specs=[smem, smem,
                  pl.BlockSpec((None, COMBINE_EXPERTS * CAPACITY * ROW_SLAB, LANES),
                               lambda b, j: (b, j, 0)),
                  tok,
                  pl.BlockSpec((None, N_ADA, D_MODEL), lambda b, j: (b, 0, 0))],
        out_specs=tok,
        out_shape=jax.ShapeDtypeStruct((B, SEQ, D_MODEL), F32),
        scratch_shapes=[pltpu.VMEM((SEQ * ROW_SLAB, LANES), F32)],
        compiler_params=_params(("parallel", "arbitrary"), VMEM_LIMIT),
        name="combine",
    )(idx, gate, y, x1, mod)


def _head_perm():
    perm = []
    for r in range(Q_PER_KV):
        for kv in range(N_KV_HEADS):
            h = kv * Q_PER_KV + r
            perm.extend(range(h * HEAD_DIM, (h + 1) * HEAD_DIM))
    return np.asarray(perm, dtype=np.int32)


def kernel(x, c, rel_bias, w_ada, b_ada, norm_mix_g, norm_ffn_g, w_in, w_fourier, b_fourier,
           q_norm_g, k_norm_g, sink, w_out, w_router, w_gate, w_up, w_down):
    B = x.shape[0]
    perm = _head_perm()
    l = 0
    mod = _ada(c, w_ada[l], b_ada[l]).reshape(B, N_ADA, D_MODEL)
    pq = _fold(w_fourier[l])
    bias = _bias_table(rel_bias)

    wi = w_in[l]
    q_cols = wi[:, FOURIER_WIDTH:FOURIER_WIDTH + ATTN_WIDTH][:, perm]
    win = jnp.concatenate([wi[:, :FOURIER_WIDTH], q_cols, wi[:, FOURIER_WIDTH + ATTN_WIDTH:]],
                          axis=1).astype(BF16)
    gq = (jnp.tile(q_norm_g[l], N_Q_HEADS) * (HEAD_DIM ** -0.5 * LOG2E)).reshape(1, ATTN_WIDTH)
    gk = jnp.tile(k_norm_g[l], N_KV_HEADS).reshape(1, KV_WIDTH)
    a, b, q, k, v = _inproj(x, mod, norm_mix_g[l].reshape(1, D_MODEL), win, pq, gq, gk)

    yf = _fourier(a, b, b_fourier[l].reshape(1, FOURIER_WIDTH))
    ya = _attn(sink[l], q, k, v, bias)

    wo = w_out[l]
    wf = wo[:FOURIER_WIDTH].astype(BF16)
    wa = wo[FOURIER_WIDTH:][perm].astype(BF16)
    w_hi = w_router[l].astype(BF16)
    w_lo = (w_router[l] - w_hi.astype(F32)).astype(BF16)
    wrh = jnp.pad(jnp.concatenate([w_hi, w_lo], axis=1), ((0, 0), (0, LANES - 2 * N_EXPERTS)))
    wrl = jnp.pad(w_hi, ((0, 0), (0, LANES - N_EXPERTS)))
    x1, h2, aff = _outproj(yf, ya, x, mod, norm_ffn_g[l].reshape(1, D_MODEL), wf, wa, wrh, wrl)

    idx, gate = _route(aff)
    n = N_EXPERTS * CAPACITY
    y = _moe(idx, h2,
             w_gate[l].astype(BF16), w_up[l].astype(BF16), w_down[l].astype(BF16))
    return _combine(idx, gate, y, x1, mod)
```

```python
import functools
import math

import numpy as np
import jax
import jax.numpy as jnp
from jax import lax
from jax.experimental import pallas as pl
from jax.experimental.pallas import tpu as pltpu

D_MODEL = 1024
SEQ = 2048
HEAD_DIM = 64
FOURIER_WIDTH = 512
ATTN_WIDTH = 512
N_GROUPS = 8
N_Q_HEADS = 8
Q_PER_KV = 4
N_KV_HEADS = 2
KV_WIDTH = 128
IN_PROJ_WIDTH = 1280
WINDOW = 128
BLOCK = 128
SPAN = BLOCK + 2 * WINDOW
N_BLOCKS = SEQ // BLOCK
N_BUCKETS = 32
MAX_DISTANCE = 128
N_EXPERTS = 16
CAPACITY = 2 * SEQ // N_EXPERTS
D_EXPERT = 1024
N_ADA = 6
EPS = 1e-6

LANES = 128
SUBLANES = 8
ROW_SLAB = D_MODEL // LANES
PACK_ROWS = ROW_SLAB // 2
VMEM_LIMIT = 56 * 1024 * 1024

F32 = jnp.float32
BF16 = jnp.bfloat16
NEG_INF = float("-inf")
LOG2E = math.log2(math.e)


def _params(sem, vmem=None):
    return pltpu.CompilerParams(dimension_semantics=sem, vmem_limit_bytes=vmem)


@functools.lru_cache(maxsize=None)
def _chan_dft():
    c = np.arange(HEAD_DIM, dtype=np.int64)
    ph = (c[:, None] * c[None, :]) % HEAD_DIM
    ang = 2.0 * np.pi * ph.astype(np.float64) / HEAD_DIM
    sc = 1.0 / math.sqrt(HEAD_DIM)
    eye = np.eye(N_GROUPS)
    cbd = np.kron(eye, np.cos(ang) * sc)
    sbd = np.kron(eye, np.sin(ang) * sc)
    return cbd.astype(np.float32), sbd.astype(np.float32)


@functools.lru_cache(maxsize=None)
def _bucket_table():
    rel = np.arange(SPAN)[None, :] - WINDOW - np.arange(BLOCK)[:, None]
    half = N_BUCKETS // 2
    max_exact = half // 2
    n = np.abs(rel)
    nf = np.maximum(n, 1).astype(np.float64)
    large = max_exact + (np.log(nf / max_exact) / math.log(MAX_DISTANCE / max_exact)
                         * (half - max_exact)).astype(np.int64)
    sq = np.maximum(n.astype(np.int64) ** 2 // (max_exact * max_exact), 1)
    large_int = max_exact + np.floor(np.log2(sq.astype(np.float64)) + 1e-9).astype(np.int64)
    assert np.array_equal(np.where(n >= max_exact, large, 0), np.where(n >= max_exact, large_int, 0))
    large = np.minimum(large, half - 1)
    bucket = np.where(rel > 0, half, 0) + np.where(n < max_exact, n, large)
    return bucket.astype(np.int32)


@functools.lru_cache(maxsize=None)
def _group_ones(width):
    return np.kron(np.eye(width // HEAD_DIM), np.ones((HEAD_DIM, HEAD_DIM))).astype(np.float32)


@functools.lru_cache(maxsize=None)
def _tri_incl():
    i = np.arange(LANES)
    return (i[:, None] <= i[None, :]).astype(np.float32)


def _ada_kernel(c_ref, w_ref, b_ref, o_ref):
    c = c_ref[...]
    ca = c * (1.0 / (1.0 + jnp.exp(-c)))
    o_ref[...] = jnp.dot(ca, w_ref[...], precision=lax.Precision.HIGHEST,
                         preferred_element_type=F32) + b_ref[...]


def _ada(c, w_ada, b_ada):
    B = c.shape[0]
    n = N_ADA * D_MODEL
    tn = D_MODEL
    return pl.pallas_call(
        _ada_kernel,
        grid=(n // tn,),
        in_specs=[pl.BlockSpec((B, D_MODEL), lambda j: (0, 0)),
                  pl.BlockSpec((D_MODEL, tn), lambda j: (0, j)),
                  pl.BlockSpec((1, tn), lambda j: (0, j))],
        out_specs=pl.BlockSpec((B, tn), lambda j: (0, j)),
        out_shape=jax.ShapeDtypeStruct((B, n), F32),
        compiler_params=_params(("arbitrary",)),
        name="ada",
    )(c, w_ada, b_ada.reshape(1, n))


def _fold_kernel(cbd_ref, sbd_ref, w_ref, o_ref):
    w = w_ref[...]
    o_ref[:, :FOURIER_WIDTH] = jnp.dot(cbd_ref[...], w, precision=lax.Precision.HIGHEST,
                                       preferred_element_type=F32).astype(BF16)
    o_ref[:, FOURIER_WIDTH:] = jnp.dot(sbd_ref[...], w, precision=lax.Precision.HIGHEST,
                                       preferred_element_type=F32).astype(BF16)


def _fold(w_fourier):
    wbd = (jnp.eye(N_GROUPS, dtype=F32)[:, None, :, None] * w_fourier[:, :, None, :]
           ).reshape(FOURIER_WIDTH, FOURIER_WIDTH)
    cbd, sbd = _chan_dft()
    return pl.pallas_call(
        _fold_kernel,
        out_shape=jax.ShapeDtypeStruct((FOURIER_WIDTH, 2 * FOURIER_WIDTH), BF16),
        name="fold",
    )(jnp.asarray(cbd), jnp.asarray(sbd), wbd)


def _bias_kernel(rb_ref, bucket_ref, o_ref):
    h = pl.program_id(0)
    bk = bucket_ref[...]
    acc = jnp.zeros((BLOCK, SPAN), F32)
    for b in range(N_BUCKETS):
        acc = jnp.where(bk == b, rb_ref[b, h], acc)
    j = lax.broadcasted_iota(jnp.int32, (BLOCK, SPAN), 1)
    q = lax.broadcasted_iota(jnp.int32, (BLOCK, SPAN), 0)
    band = jnp.abs(j - WINDOW - q) <= WINDOW
    base = jnp.where(band, acc * LOG2E, NEG_INF)
    o_ref[0] = jnp.where(j >= WINDOW, base, NEG_INF)
    o_ref[1] = base
    o_ref[2] = jnp.where(j < WINDOW + BLOCK, base, NEG_INF)


def _bias_table(rel_bias):
    return pl.pallas_call(
        _bias_kernel,
        grid=(N_Q_HEADS,),
        in_specs=[pl.BlockSpec(memory_space=pltpu.SMEM),
                  pl.BlockSpec((BLOCK, SPAN), lambda h: (0, 0))],
        out_specs=pl.BlockSpec((3, None, BLOCK, SPAN), lambda h: (0, h, 0, 0)),
        out_shape=jax.ShapeDtypeStruct((3, N_Q_HEADS, BLOCK, SPAN), F32),
        compiler_params=_params(("arbitrary",)),
        name="bias",
    )(rel_bias, jnp.asarray(_bucket_table()))


def _inproj_kernel(x_ref, mod_ref, g_ref, win_ref, pq_ref, bdq_ref, bdk_ref, gq_ref, gk_ref,
                   a_ref, b_ref, q_ref, k_ref, v_ref):
    x = x_ref[...]
    ms = jnp.mean(x * x, axis=-1, keepdims=True)
    y = x * lax.rsqrt(ms + EPS) * g_ref[...]
    h = y * (1.0 + mod_ref[1:2, :]) + mod_ref[0:1, :]
    proj = jnp.dot(h.astype(BF16), win_ref[...], preferred_element_type=F32)
    uf = proj[:, :FOURIER_WIDTH].astype(BF16)
    ab = jnp.dot(uf, pq_ref[...], preferred_element_type=F32)
    a_ref[...] = ab[:, :FOURIER_WIDTH].astype(BF16)
    b_ref[...] = ab[:, FOURIER_WIDTH:].astype(BF16)
    q0 = FOURIER_WIDTH
    k0 = q0 + ATTN_WIDTH
    v0 = k0 + KV_WIDTH
    q = proj[:, q0:k0]
    ssq = jnp.dot((q * q).astype(BF16), bdq_ref[...], preferred_element_type=F32)
    q_ref[...] = (q * lax.rsqrt(ssq * (1.0 / HEAD_DIM) + EPS) * gq_ref[...]).astype(BF16)
    k = proj[:, k0:v0]
    ssk = jnp.dot((k * k).astype(BF16), bdk_ref[...], preferred_element_type=F32)
    k_ref[...] = (k * lax.rsqrt(ssk * (1.0 / HEAD_DIM) + EPS) * gk_ref[...]).astype(BF16)
    v_ref[...] = proj[:, v0:].astype(BF16)


def _inproj(x, mod, g, win, pq, gq, gk, tm=512):
    B = x.shape[0]
    const = lambda shape: pl.BlockSpec(shape, lambda b, i: (0,) * len(shape))
    tok = lambda w: pl.BlockSpec((None, tm, w), lambda b, i: (b, i, 0))
    sds = lambda w: jax.ShapeDtypeStruct((B, SEQ, w), BF16)
    return pl.pallas_call(
        _inproj_kernel,
        grid=(B, SEQ // tm),
        in_specs=[tok(D_MODEL),
                  pl.BlockSpec((None, N_ADA, D_MODEL), lambda b, i: (b, 0, 0)),
                  const((1, D_MODEL)),
                  const((D_MODEL, IN_PROJ_WIDTH)),
                  const((FOURIER_WIDTH, 2 * FOURIER_WIDTH)),
                  const((ATTN_WIDTH, ATTN_WIDTH)),
                  const((KV_WIDTH, KV_WIDTH)),
                  const((1, ATTN_WIDTH)),
                  const((1, KV_WIDTH))],
        out_specs=[tok(FOURIER_WIDTH), tok(FOURIER_WIDTH), tok(ATTN_WIDTH), tok(KV_WIDTH),
                   tok(KV_WIDTH)],
        out_shape=[sds(FOURIER_WIDTH), sds(FOURIER_WIDTH), sds(ATTN_WIDTH), sds(KV_WIDTH),
                   sds(KV_WIDTH)],
        compiler_params=_params(("parallel", "parallel"), VMEM_LIMIT),
        name="inproj",
    )(x, mod, g, win, pq, jnp.asarray(_group_ones(ATTN_WIDTH)).astype(BF16),
      jnp.asarray(_group_ones(KV_WIDTH)).astype(BF16), gq, gk)


HALF = SEQ // 2
REV = 128
REV_BLOCKS = HALF // REV
FOURIER_ROWS = 512


def _fourier_kernel(cm_ref, sm_ref, psh_ref, alt_ref, altrow_ref, a_ref, b_ref, bf_ref, o_ref):
    psh = psh_ref[...]

    def reversed_block(win_lo, k, src):
        if k == 0:
            return jnp.dot(psh[:, :REV], src(win_lo, REV), preferred_element_type=F32)
        return jnp.dot(psh, src(win_lo, 2 * REV), preferred_element_type=F32)

    def folded(ref, sign):
        blocks = []
        for k in range(REV_BLOCKS):
            lo = SEQ - REV * (k + 1)
            rev = reversed_block(lo, k, lambda s, n: ref[pl.ds(s, n), :])
            blocks.append((ref[pl.ds(k * REV, REV), :].astype(F32) + sign * rev).astype(BF16))
        return jnp.concatenate(blocks, axis=0)

    a_even = folded(a_ref, 1.0)
    b_odd = folded(b_ref, -1.0)
    a_mid = a_ref[pl.ds(HALF, 1), :].astype(F32)
    bias = bf_ref[...]
    z_blocks = []
    for i in range(HALF // FOURIER_ROWS):
        rows = pl.ds(i * FOURIER_ROWS, FOURIER_ROWS)
        yc = jnp.dot(cm_ref[rows, :], a_even, preferred_element_type=F32)
        yc = yc + alt_ref[rows, :] * a_mid + bias
        ys = jnp.dot(sm_ref[rows, :], b_odd, preferred_element_type=F32)
        o_ref[rows, :] = (yc - ys).astype(BF16)
        z_blocks.append((yc + ys).astype(BF16))
    z = jnp.concatenate(z_blocks, axis=0)
    for k in range(REV_BLOCKS):
        lo = HALF - REV * (k + 1)
        top = reversed_block(lo, k, lambda s, n: z[s:s + n])
        o_ref[pl.ds(HALF + k * REV, REV), :] = top.astype(BF16)
    y_mid = jnp.dot(altrow_ref[...], a_ref[...], preferred_element_type=F32)[0:1, :] + bias
    o_ref[pl.ds(HALF, 1), :] = y_mid.astype(BF16)


@functools.lru_cache(maxsize=None)
def _fold_tables():
    s = np.arange(HALF, dtype=np.int64)
    ph = (s[:, None] * s[None, :]) % SEQ
    ang = 2.0 * np.pi * ph.astype(np.float64) / SEQ
    sc = 1.0 / math.sqrt(SEQ)
    cm = (np.cos(ang) * sc).astype(np.float32)
    sm = (np.sin(ang) * sc).astype(np.float32)
    psh = np.zeros((REV, 2 * REV), np.float32)
    psh[np.arange(REV), REV - np.arange(REV)] = 1.0
    alt = (np.where(s % 2 == 0, 1.0, -1.0) * sc).astype(np.float32).reshape(HALF, 1)
    t = np.arange(SEQ)
    altrow = np.zeros((SUBLANES, SEQ), np.float32)
    altrow[0] = np.where(t % 2 == 0, 1.0, -1.0) * sc
    return cm, sm, psh, alt, altrow


def _fourier(a, b, bf):
    B = a.shape[0]
    cm, sm, psh, alt, altrow = _fold_tables()
    tok = pl.BlockSpec((None, SEQ, FOURIER_WIDTH), lambda i: (i, 0, 0))
    const = lambda shape: pl.BlockSpec(shape, lambda i: (0,) * len(shape))
    return pl.pallas_call(
        _fourier_kernel,
        grid=(B,),
        in_specs=[const((HALF, HALF)), const((HALF, HALF)), const((REV, 2 * REV)),
                  const((HALF, 1)), const((SUBLANES, SEQ)),
                  tok, tok, const((1, FOURIER_WIDTH))],
        out_specs=tok,
        out_shape=jax.ShapeDtypeStruct((B, SEQ, FOURIER_WIDTH), BF16),
        compiler_params=_params(("parallel",), VMEM_LIMIT),
        name="fourier",
    )(jnp.asarray(cm).astype(BF16), jnp.asarray(sm).astype(BF16), jnp.asarray(psh).astype(BF16),
      jnp.asarray(alt), jnp.asarray(altrow).astype(BF16), a, b, bf)


ATT_SUB = 4
ATT_ROWS = ATT_SUB * BLOCK
ATT_STEPS = N_BLOCKS // ATT_SUB


def _attn_kernel(sink_ref, q_ref, kl_ref, km_ref, kr_ref, vl_ref, vm_ref, vr_ref, bias_ref,
                 o_ref):
    i = pl.program_id(1)
    keys = jnp.concatenate([kl_ref[...], km_ref[...], kr_ref[...]], axis=0)
    vals = jnp.concatenate([vl_ref[...], vm_ref[...], vr_ref[...]], axis=0)
    first_k = lax.broadcasted_iota(jnp.int32, keys.shape, 1) < HEAD_DIM
    first_q = lax.broadcasted_iota(jnp.int32, (BLOCK, 2 * HEAD_DIM), 1) < HEAD_DIM
    row_head = lax.broadcasted_iota(jnp.int32, (Q_PER_KV * BLOCK, 1), 0) // BLOCK
    keys_kv = [jnp.where(first_k, keys, jnp.zeros_like(keys)),
               jnp.where(first_k, jnp.zeros_like(keys), keys)]
    sinks = []
    for kv in range(N_KV_HEADS):
        sink = jnp.zeros((Q_PER_KV * BLOCK, 1), F32)
        for r in range(Q_PER_KV):
            sink = jnp.where(row_head == r, sink_ref[kv * Q_PER_KV + r] * LOG2E, sink)
        sinks.append(sink)
    for j in range(ATT_SUB):
        variant = 1
        if j == 0:
            variant = jnp.where(i == 0, 0, variant)
        if j == ATT_SUB - 1:
            variant = jnp.where(i == ATT_STEPS - 1, 2, variant)
        qrows = pl.ds(j * BLOCK, BLOCK)
        krows = slice(j * BLOCK, j * BLOCK + SPAN)
        qs = jnp.concatenate([q_ref[qrows, r * LANES:(r + 1) * LANES] for r in range(Q_PER_KV)],
                             axis=0)
        pvs = []
        for kv in range(N_KV_HEADS):
            logits = lax.dot_general(qs, keys_kv[kv][krows], (((1,), (1,)), ((), ())),
                                     preferred_element_type=F32)
            bias = bias_ref[variant, pl.ds(kv * Q_PER_KV, Q_PER_KV)]
            logits = logits + bias.reshape(Q_PER_KV * BLOCK, SPAN)
            sink = sinks[kv]
            m = jnp.maximum(jnp.max(logits, axis=-1, keepdims=True), sink)
            p = jnp.exp2(logits - m)
            denom = jnp.sum(p, axis=-1, keepdims=True) + jnp.exp2(sink - m)
            pv = jnp.dot(p.astype(BF16), vals[krows], preferred_element_type=F32)
            pvs.append(pv / denom)
        for r in range(Q_PER_KV):
            rows = slice(r * BLOCK, (r + 1) * BLOCK)
            o_ref[qrows, r * LANES:(r + 1) * LANES] = jnp.where(
                first_q, pvs[0][rows], pvs[1][rows]).astype(BF16)


def _attn(sink, q, k, v, bias):
    B = q.shape[0]
    edge = lambda f: pl.BlockSpec((None, BLOCK, KV_WIDTH), lambda b, i: (b, f(i), 0))
    left = lambda i: jnp.maximum(i * ATT_SUB - 1, 0)
    right = lambda i: jnp.minimum((i + 1) * ATT_SUB, N_BLOCKS - 1)
    mid = pl.BlockSpec((None, ATT_ROWS, KV_WIDTH), lambda b, i: (b, i, 0))
    qspec = pl.BlockSpec((None, ATT_ROWS, ATTN_WIDTH), lambda b, i: (b, i, 0))
    return pl.pallas_call(
        _attn_kernel,
        grid=(B, ATT_STEPS),
        in_specs=[pl.BlockSpec(memory_space=pltpu.SMEM), qspec,
                  edge(left), mid, edge(right),
                  edge(left), mid, edge(right),
                  pl.BlockSpec((3, N_Q_HEADS, BLOCK, SPAN), lambda b, i: (0, 0, 0, 0))],
        out_specs=qspec,
        out_shape=jax.ShapeDtypeStruct((B, SEQ, ATTN_WIDTH), BF16),
        compiler_params=_params(("parallel", "parallel"), VMEM_LIMIT),
        name="attn",
    )(sink, q, k, k, k, v, v, v, bias)


OUT_ROWS = 256


def _outproj_kernel(yf_ref, ya_ref, x_ref, mod_ref, g_ref, wf_ref, wa_ref, wrh_ref, wrl_ref,
                    x1_ref, h2_ref, aff_ref):
    tm = x_ref.shape[0]
    gain = g_ref[...] * (1.0 + mod_ref[4:5, :])
    shift = mod_ref[3:4, :]
    gate1 = mod_ref[2:3, :]
    lane = lax.broadcasted_iota(jnp.int32, (OUT_ROWS, LANES), 1)
    for c in range(tm // OUT_ROWS):
        rows = pl.ds(c * OUT_ROWS, OUT_ROWS)
        mixed = jnp.dot(yf_ref[rows, :], wf_ref[...], preferred_element_type=F32)
        mixed = mixed + jnp.dot(ya_ref[rows, :], wa_ref[...], preferred_element_type=F32)
        x1 = x_ref[rows, :] + gate1 * mixed
        x1_ref[rows, :] = x1
        ms = jnp.mean(x1 * x1, axis=-1, keepdims=True)
        h2 = x1 * lax.rsqrt(ms + EPS) * gain + shift
        hi = h2.astype(BF16)
        top = pltpu.bitcast(hi[:, :D_MODEL // 2].astype(F32), jnp.uint32)
        bot = pltpu.bitcast(hi[:, D_MODEL // 2:].astype(F32), jnp.uint32)
        words = top | (bot >> 16)
        for j in range(PACK_ROWS):
            h2_ref[pl.ds(c * OUT_ROWS * PACK_ROWS + j, OUT_ROWS, stride=PACK_ROWS), :] = (
                words[:, j * LANES:(j + 1) * LANES])
        lo = (h2 - hi.astype(F32)).astype(BF16)
        part = jnp.dot(hi, wrh_ref[...], preferred_element_type=F32)
        part = part + jnp.dot(lo, wrl_ref[...], preferred_element_type=F32)
        logits = part + pltpu.roll(part, LANES - N_EXPERTS, axis=1)
        logits = jnp.where(lane < N_EXPERTS, logits, NEG_INF)
        m = jnp.max(logits, axis=-1, keepdims=True)
        e = jnp.exp(logits - m)
        aff_ref[rows, :] = e / jnp.sum(e, axis=-1, keepdims=True)


def _outproj(yf, ya, x, mod, g, wf, wa, wrh, wrl, tm=512):
    B = x.shape[0]
    const = lambda shape: pl.BlockSpec(shape, lambda b, i: (0,) * len(shape))
    tok = lambda w: pl.BlockSpec((None, tm, w), lambda b, i: (b, i, 0))
    return pl.pallas_call(
        _outproj_kernel,
        grid=(B, SEQ // tm),
        in_specs=[tok(FOURIER_WIDTH), tok(ATTN_WIDTH), tok(D_MODEL),
                  pl.BlockSpec((None, N_ADA, D_MODEL), lambda b, i: (b, 0, 0)),
                  const((1, D_MODEL)),
                  const((FOURIER_WIDTH, D_MODEL)), const((ATTN_WIDTH, D_MODEL)),
                  const((D_MODEL, LANES)), const((D_MODEL, LANES))],
        out_specs=[tok(D_MODEL),
                   pl.BlockSpec((None, tm * PACK_ROWS, LANES), lambda b, i: (b, i, 0)),
                   tok(LANES)],
        out_shape=[jax.ShapeDtypeStruct((B, SEQ, D_MODEL), F32),
                   jax.ShapeDtypeStruct((B, SEQ * PACK_ROWS, LANES), jnp.uint32),
                   jax.ShapeDtypeStruct((B, SEQ, LANES), F32)],
        compiler_params=_params(("parallel", "parallel"), VMEM_LIMIT),
        name="outproj",
    )(yf, ya, x, mod, g, wf, wa, wrh, wrl)


ROUTE_BATCHES = 2
SEARCH_BITS = 3


def _route_kernel(aff_ref, tri_ref, idx_ref, gate_ref):
    for bb in range(ROUTE_BATCHES):
        _route_one(aff_ref[bb], tri_ref[...], idx_ref.at[bb], gate_ref.at[bb])


def _route_one(aff, tri, idx_ref, gate_ref):
    aff_t = jnp.transpose(aff)[:N_EXPERTS]
    bits = pltpu.bitcast(aff_t, jnp.int32)
    cap = float(CAPACITY)

    t = jnp.zeros((N_EXPERTS, 1), jnp.int32)
    for shift in range(30 - SEARCH_BITS, -1, -SEARCH_BITS):
        digit = jnp.zeros((N_EXPERTS, 1), jnp.int32)
        for k in range(1, 1 << SEARCH_BITS):
            cnt = jnp.sum(jnp.where(bits >= (t | (k << shift)), 1.0, 0.0), axis=1, keepdims=True)
            digit = digit + jnp.where(cnt >= cap, 1, 0)
        t = t | (digit << shift)
    gt = bits > t
    eq = bits == t
    need = cap - jnp.sum(jnp.where(gt, 1.0, 0.0), axis=1, keepdims=True)

    n_chunks = SEQ // LANES

    def prefix(flags_f32):
        outs = []
        carry = jnp.zeros((N_EXPERTS, 1), F32)
        for c in range(n_chunks):
            f = flags_f32[:, c * LANES:(c + 1) * LANES]
            incl = jnp.dot(f.astype(BF16), tri, preferred_element_type=F32)
            outs.append(incl - f + carry)
            carry = carry + jnp.sum(f, axis=1, keepdims=True)
        return jnp.concatenate(outs, axis=1)

    eq_f = jnp.where(eq, 1.0, 0.0)
    eq_rank = prefix(eq_f)
    sel_f = jnp.where(gt, 1.0, jnp.where(eq_rank < need, eq_f, 0.0))
    pos = prefix(sel_f)
    posm = jnp.where(sel_f > 0.0, pos, -1.0)

    hi = aff_t.astype(BF16).astype(F32)
    r1 = aff_t - hi
    mid = r1.astype(BF16).astype(F32)
    lo = r1 - mid
    tok = lax.broadcasted_iota(jnp.int32, (N_EXPERTS, SEQ), 1)
    row = lax.broadcasted_iota(jnp.int32, (N_EXPERTS, SEQ), 0)
    tok_rows = jnp.where(row == 0, (tok >> 6).astype(F32),
                         jnp.where(row == 1, (tok & 63).astype(F32), 0.0))
    vals_t = jnp.concatenate([hi, mid, lo, tok_rows], axis=0).astype(BF16)

    slot = lax.broadcasted_iota(jnp.int32, (CAPACITY, SEQ), 0).astype(F32).astype(BF16)
    posm_b = posm.astype(BF16)
    one_b = jnp.ones((CAPACITY, SEQ), BF16)
    zero_b = jnp.zeros((CAPACITY, SEQ), BF16)
    for e in range(N_EXPERTS):
        onehot = jnp.where(posm_b[e:e + 1, :] == slot, one_b, zero_b)
        res = lax.dot_general(vals_t, onehot, (((1,), (1,)), ((), ())),
                              preferred_element_type=F32)
        cols = pl.ds(e * CAPACITY, CAPACITY)
        tok_idx = res[3 * N_EXPERTS:3 * N_EXPERTS + 1] * 64.0 + res[3 * N_EXPERTS + 1:
                                                                    3 * N_EXPERTS + 2]
        idx_ref[:, cols] = tok_idx.astype(jnp.int32) * PACK_ROWS
        gate_ref[:, cols] = (res[e:e + 1] + res[N_EXPERTS + e:N_EXPERTS + e + 1]
                             + res[2 * N_EXPERTS + e:2 * N_EXPERTS + e + 1])


def _route(aff):
    B = aff.shape[0]
    n = N_EXPERTS * CAPACITY
    return pl.pallas_call(
        _route_kernel,
        grid=(B // ROUTE_BATCHES,),
        in_specs=[pl.BlockSpec((ROUTE_BATCHES, SEQ, LANES), lambda b: (b, 0, 0)),
                  pl.BlockSpec((LANES, LANES), lambda b: (0, 0))],
        out_specs=[pl.BlockSpec((ROUTE_BATCHES, 1, n), lambda b: (b, 0, 0)),
                   pl.BlockSpec((ROUTE_BATCHES, 1, n), lambda b: (b, 0, 0))],
        out_shape=[jax.ShapeDtypeStruct((B, 1, n), jnp.int32),
                   jax.ShapeDtypeStruct((B, 1, n), F32)],
        compiler_params=_params(("parallel",), VMEM_LIMIT),
        name="route",
    )(aff, jnp.asarray(_tri_incl()).astype(BF16))


PAIR = 2


def _moe_kernel(idx_ref, h2_ref, wg_ref, wu_ref, wd_ref, y_ref, xin0_ref, xin1_ref):
    e = pl.program_id(1)
    last = N_EXPERTS - 1
    n = N_EXPERTS * CAPACITY
    rows = PAIR * CAPACITY

    def gather_rows(ex, dst_ref):
        for bb in range(PAIR):
            base = bb * n + ex * CAPACITY
            for p in range(CAPACITY):
                off = idx_ref[0, base + p]
                tile = h2_ref[bb, pl.ds(pl.multiple_of((off >> 3) << 3, SUBLANES), SUBLANES), :]
                tile = pltpu.roll(tile, off & PACK_ROWS, axis=0)
                dst_ref[pl.ds((bb * CAPACITY + p) * PACK_ROWS, PACK_ROWS), :] = tile[:PACK_ROWS]

    def expert(xin_ref):
        words = [xin_ref[pl.ds(j, rows, stride=PACK_ROWS), :] for j in range(PACK_ROWS)]
        xin = jnp.concatenate(
            [pltpu.bitcast(w & jnp.uint32(0xFFFF0000), F32).astype(BF16) for w in words]
            + [pltpu.bitcast(w << 16, F32).astype(BF16) for w in words], axis=1)
        a = jnp.dot(xin, wg_ref[...], preferred_element_type=F32)
        u = jnp.dot(xin, wu_ref[...], preferred_element_type=F32)
        hmid = (a * (1.0 / (1.0 + jnp.exp(-a))) * u).astype(BF16)
        y = jnp.dot(hmid, wd_ref[...], preferred_element_type=F32)
        for bb in range(PAIR):
            for j in range(ROW_SLAB):
                y_ref[bb, pl.ds(j, CAPACITY, stride=ROW_SLAB), :] = (
                    y[bb * CAPACITY:(bb + 1) * CAPACITY, j * LANES:(j + 1) * LANES])

    @pl.when(e == 0)
    def _():
        gather_rows(0, xin0_ref)

    def step(xin_cur, xin_nxt):
        gather_rows(jnp.minimum(e + 1, last), xin_nxt)
        expert(xin_cur)

    @pl.when(e % 2 == 0)
    def _():
        step(xin0_ref, xin1_ref)

    @pl.when(e % 2 == 1)
    def _():
        step(xin1_ref, xin0_ref)


def _moe(idx, h2, wg, wu, wd):
    B = h2.shape[0]
    n = N_EXPERTS * CAPACITY
    rows = SEQ * PACK_ROWS
    pairs = B // PAIR
    wspec = lambda r, c: pl.BlockSpec((None, r, c), lambda b, e: (e, 0, 0))
    stage = pltpu.VMEM((PAIR * CAPACITY * PACK_ROWS, LANES), jnp.uint32)
    out = pl.pallas_call(
        _moe_kernel,
        grid=(pairs, N_EXPERTS),
        in_specs=[pl.BlockSpec((None, 1, PAIR * n), lambda b, e: (b, 0, 0),
                               memory_space=pltpu.SMEM),
                  pl.BlockSpec((None, PAIR, rows, LANES), lambda b, e: (b, 0, 0, 0)),
                  wspec(D_MODEL, D_EXPERT), wspec(D_MODEL, D_EXPERT), wspec(D_EXPERT, D_MODEL)],
        out_specs=pl.BlockSpec((None, PAIR, CAPACITY * ROW_SLAB, LANES),
                               lambda b, e: (b, 0, e, 0)),
        out_shape=jax.ShapeDtypeStruct((pairs, PAIR, n * ROW_SLAB, LANES), F32),
        scratch_shapes=[stage, stage],
        compiler_params=_params(("parallel", "arbitrary"), VMEM_LIMIT),
        name="moe",
    )(idx.reshape(pairs, 1, PAIR * n), h2.reshape(pairs, PAIR, rows, LANES), wg, wu, wd)
    return out.reshape(B, n * ROW_SLAB, LANES)


COMBINE_EXPERTS = 4
SCATTER_UNROLL = 8
COMBINE_ROWS = 256


def _combine_kernel(idx_ref, gate_ref, y_ref, x1_ref, mod_ref, o_ref, acc_ref):
    j = pl.program_id(1)
    slots = COMBINE_EXPERTS * CAPACITY
    base = j * slots

    @pl.when(j == 0)
    def _():
        acc_ref[...] = jnp.zeros_like(acc_ref)

    for g in range(slots // SCATTER_UNROLL):
        new = []
        for u in range(SCATTER_UNROLL):
            r = g * SCATTER_UNROLL + u
            dst = pl.multiple_of(idx_ref[0, base + r] * (ROW_SLAB // PACK_ROWS), ROW_SLAB)
            new.append((dst, acc_ref[pl.ds(dst, ROW_SLAB), :]
                        + y_ref[pl.ds(r * ROW_SLAB, ROW_SLAB), :] * gate_ref[0, base + r]))
        for dst, val in new:
            acc_ref[pl.ds(dst, ROW_SLAB), :] = val

    @pl.when(j == pl.num_programs(1) - 1)
    def _():
        for rb in range(SEQ // COMBINE_ROWS):
            rows = pl.ds(rb * COMBINE_ROWS, COMBINE_ROWS)
            for c in range(ROW_SLAB):
                cols = slice(c * LANES, (c + 1) * LANES)
                chunk = acc_ref[pl.ds(rb * COMBINE_ROWS * ROW_SLAB + c, COMBINE_ROWS,
                                      stride=ROW_SLAB), :]
                o_ref[rows, cols] = x1_ref[rows, cols] + mod_ref[5:6, cols] * chunk


def _combine(idx, gate, y, x1, mod):
    B = x1.shape[0]
    n = N_EXPERTS * CAPACITY
    tok = pl.BlockSpec((None, SEQ, D_MODEL), lambda b, j: (b, 0, 0))
    smem = pl.BlockSpec((None, 1, n), lambda b, j: (b, 0, 0), memory_space=pltpu.SMEM)
    return pl.pallas_call(
        _combine_kernel,
        grid=(B, N_EXPERTS // COMBINE_EXPERTS),
        in_specs=[smem, smem,
                  pl.BlockSpec((None, COMBINE_EXPERTS * CAPACITY * ROW_SLAB, LANES),
                               lambda b, j: (b, j, 0)),
                  tok,
                  pl.BlockSpec((None, N_ADA, D_MODEL), lambda b, j: (b, 0, 0))],
        out_specs=tok,
        out_shape=jax.ShapeDtypeStruct((B, SEQ, D_MODEL), F32),
        scratch_shapes=[pltpu.VMEM((SEQ * ROW_SLAB, LANES), F32)],
        compiler_params=_params(("parallel", "arbitrary"), VMEM_LIMIT),
        name="combine",
    )(idx, gate, y, x1, mod)


def _head_perm():
    perm = []
    for r in range(Q_PER_KV):
        for kv in range(N_KV_HEADS):
            h = kv * Q_PER_KV + r
            perm.extend(range(h * HEAD_DIM, (h + 1) * HEAD_DIM))
    return np.asarray(perm, dtype=np.int32)


def kernel(x, c, rel_bias, w_ada, b_ada, norm_mix_g, norm_ffn_g, w_in, w_fourier, b_fourier,
           q_norm_g, k_norm_g, sink, w_out, w_router, w_gate, w_up, w_down):
    B = x.shape[0]
    perm = _head_perm()
    l = 0
    mod = _ada(c, w_ada[l], b_ada[l]).reshape(B, N_ADA, D_MODEL)
    pq = _fold(w_fourier[l])
    bias = _bias_table(rel_bias)

    wi = w_in[l]
    q_cols = wi[:, FOURIER_WIDTH:FOURIER_WIDTH + ATTN_WIDTH][:, perm]
    win = jnp.concatenate([wi[:, :FOURIER_WIDTH], q_cols, wi[:, FOURIER_WIDTH + ATTN_WIDTH:]],
                          axis=1).astype(BF16)
    gq = (jnp.tile(q_norm_g[l], N_Q_HEADS) * (HEAD_DIM ** -0.5 * LOG2E)).reshape(1, ATTN_WIDTH)
    gk = jnp.tile(k_norm_g[l], N_KV_HEADS).reshape(1, KV_WIDTH)
    a, b, q, k, v = _inproj(x, mod, norm_mix_g[l].reshape(1, D_MODEL), win, pq, gq, gk)

    yf = _fourier(a, b, b_fourier[l].reshape(1, FOURIER_WIDTH))
    ya = _attn(sink[l], q, k, v, bias)

    wo = w_out[l]
    wf = wo[:FOURIER_WIDTH].astype(BF16)
    wa = wo[FOURIER_WIDTH:][perm].astype(BF16)
    w_hi = w_router[l].astype(BF16)
    w_lo = (w_router[l] - w_hi.astype(F32)).astype(BF16)
    wrh = jnp.pad(jnp.concatenate([w_hi, w_lo], axis=1), ((0, 0), (0, LANES - 2 * N_EXPERTS)))
    wrl = jnp.pad(w_hi, ((0, 0), (0, LANES - N_EXPERTS)))
    x1, h2, aff = _outproj(yf, ya, x, mod, norm_ffn_g[l].reshape(1, D_MODEL), wf, wa, wrh, wrl)

    idx, gate = _route(aff)
    n = N_EXPERTS * CAPACITY
    y = _moe(idx, h2,
             w_gate[l].astype(BF16), w_up[l].astype(BF16), w_down[l].astype(BF16))
    return _combine(idx, gate, y, x1, mod)
```

```python
import functools
import math

import numpy as np
import jax
import jax.numpy as jnp
from jax import lax
from jax.experimental import pallas as pl
from jax.experimental.pallas import tpu as pltpu

D_MODEL = 1024
SEQ = 2048
HEAD_DIM = 64
FOURIER_WIDTH = 512
ATTN_WIDTH = 512
N_GROUPS = 8
N_Q_HEADS = 8
Q_PER_KV = 4
N_KV_HEADS = 2
KV_WIDTH = 128
IN_PROJ_WIDTH = 1280
WINDOW = 128
BLOCK = 128
SPAN = BLOCK + 2 * WINDOW
N_BLOCKS = SEQ // BLOCK
N_BUCKETS = 32
MAX_DISTANCE = 128
N_EXPERTS = 16
CAPACITY = 2 * SEQ // N_EXPERTS
D_EXPERT = 1024
N_ADA = 6
EPS = 1e-6

LANES = 128
SUBLANES = 8
ROW_SLAB = D_MODEL // LANES
PACK_ROWS = ROW_SLAB // 2
VMEM_LIMIT = 56 * 1024 * 1024

F32 = jnp.float32
BF16 = jnp.bfloat16
NEG_INF = float("-inf")
LOG2E = math.log2(math.e)


def _params(sem, vmem=None):
    return pltpu.CompilerParams(dimension_semantics=sem, vmem_limit_bytes=vmem)


@functools.lru_cache(maxsize=None)
def _chan_dft():
    c = np.arange(HEAD_DIM, dtype=np.int64)
    ph = (c[:, None] * c[None, :]) % HEAD_DIM
    ang = 2.0 * np.pi * ph.astype(np.float64) / HEAD_DIM
    sc = 1.0 / math.sqrt(HEAD_DIM)
    eye = np.eye(N_GROUPS)
    cbd = np.kron(eye, np.cos(ang) * sc)
    sbd = np.kron(eye, np.sin(ang) * sc)
    return cbd.astype(np.float32), sbd.astype(np.float32)


@functools.lru_cache(maxsize=None)
def _bucket_table():
    rel = np.arange(SPAN)[None, :] - WINDOW - np.arange(BLOCK)[:, None]
    half = N_BUCKETS // 2
    max_exact = half // 2
    n = np.abs(rel)
    nf = np.maximum(n, 1).astype(np.float64)
    large = max_exact + (np.log(nf / max_exact) / math.log(MAX_DISTANCE / max_exact)
                         * (half - max_exact)).astype(np.int64)
    sq = np.maximum(n.astype(np.int64) ** 2 // (max_exact * max_exact), 1)
    large_int = max_exact + np.floor(np.log2(sq.astype(np.float64)) + 1e-9).astype(np.int64)
    assert np.array_equal(np.where(n >= max_exact, large, 0), np.where(n >= max_exact, large_int, 0))
    large = np.minimum(large, half - 1)
    bucket = np.where(rel > 0, half, 0) + np.where(n < max_exact, n, large)
    return bucket.astype(np.int32)


@functools.lru_cache(maxsize=None)
def _group_ones(width):
    return np.kron(np.eye(width // HEAD_DIM), np.ones((HEAD_DIM, HEAD_DIM))).astype(np.float32)


@functools.lru_cache(maxsize=None)
def _tri_incl():
    i = np.arange(LANES)
    return (i[:, None] <= i[None, :]).astype(np.float32)


def _ada_kernel(c_ref, w_ref, b_ref, o_ref):
    c = c_ref[...]
    ca = c * (1.0 / (1.0 + jnp.exp(-c)))
    o_ref[...] = jnp.dot(ca, w_ref[...], precision=lax.Precision.HIGHEST,
                         preferred_element_type=F32) + b_ref[...]


def _ada(c, w_ada, b_ada):
    B = c.shape[0]
    n = N_ADA * D_MODEL
    tn = D_MODEL
    return pl.pallas_call(
        _ada_kernel,
        grid=(n // tn,),
        in_specs=[pl.BlockSpec((B, D_MODEL), lambda j: (0, 0)),
                  pl.BlockSpec((D_MODEL, tn), lambda j: (0, j)),
                  pl.BlockSpec((1, tn), lambda j: (0, j))],
        out_specs=pl.BlockSpec((B, tn), lambda j: (0, j)),
        out_shape=jax.ShapeDtypeStruct((B, n), F32),
        compiler_params=_params(("arbitrary",)),
        name="ada",
    )(c, w_ada, b_ada.reshape(1, n))


def _fold_kernel(cbd_ref, sbd_ref, w_ref, o_ref):
    w = w_ref[...]
    o_ref[:, :FOURIER_WIDTH] = jnp.dot(cbd_ref[...], w, precision=lax.Precision.HIGHEST,
                                       preferred_element_type=F32).astype(BF16)
    o_ref[:, FOURIER_WIDTH:] = jnp.dot(sbd_ref[...], w, precision=lax.Precision.HIGHEST,
                                       preferred_element_type=F32).astype(BF16)


def _fold(w_fourier):
    wbd = (jnp.eye(N_GROUPS, dtype=F32)[:, None, :, None] * w_fourier[:, :, None, :]
           ).reshape(FOURIER_WIDTH, FOURIER_WIDTH)
    cbd, sbd = _chan_dft()
    return pl.pallas_call(
        _fold_kernel,
        out_shape=jax.ShapeDtypeStruct((FOURIER_WIDTH, 2 * FOURIER_WIDTH), BF16),
        name="fold",
    )(jnp.asarray(cbd), jnp.asarray(sbd), wbd)


def _bias_kernel(rb_ref, bucket_ref, o_ref):
    h = pl.program_id(0)
    bk = bucket_ref[...]
    acc = jnp.zeros((BLOCK, SPAN), F32)
    for b in range(N_BUCKETS):
        acc = jnp.where(bk == b, rb_ref[b, h], acc)
    j = lax.broadcasted_iota(jnp.int32, (BLOCK, SPAN), 1)
    q = lax.broadcasted_iota(jnp.int32, (BLOCK, SPAN), 0)
    band = jnp.abs(j - WINDOW - q) <= WINDOW
    base = jnp.where(band, acc * LOG2E, NEG_INF)
    o_ref[0] = jnp.where(j >= WINDOW, base, NEG_INF)
    o_ref[1] = base
    o_ref[2] = jnp.where(j < WINDOW + BLOCK, base, NEG_INF)


def _bias_table(rel_bias):
    return pl.pallas_call(
        _bias_kernel,
        grid=(N_Q_HEADS,),
        in_specs=[pl.BlockSpec(memory_space=pltpu.SMEM),
                  pl.BlockSpec((BLOCK, SPAN), lambda h: (0, 0))],
        out_specs=pl.BlockSpec((3, BLOCK, SPAN), lambda h: (0, h % Q_PER_KV, h // Q_PER_KV)),
        out_shape=jax.ShapeDtypeStruct((3, Q_PER_KV * BLOCK, N_KV_HEADS * SPAN), F32),
        compiler_params=_params(("arbitrary",)),
        name="bias",
    )(rel_bias, jnp.asarray(_bucket_table()))


def _inproj_kernel(x_ref, mod_ref, g_ref, win_ref, pq_ref, bdq_ref, bdk_ref, gq_ref, gk_ref,
                   a_ref, b_ref, q_ref, k_ref, v_ref):
    x = x_ref[...]
    ms = jnp.mean(x * x, axis=-1, keepdims=True)
    y = x * lax.rsqrt(ms + EPS) * g_ref[...]
    h = y * (1.0 + mod_ref[1:2, :]) + mod_ref[0:1, :]
    proj = jnp.dot(h.astype(BF16), win_ref[...], preferred_element_type=F32)
    uf = proj[:, :FOURIER_WIDTH].astype(BF16)
    ab = jnp.dot(uf, pq_ref[...], preferred_element_type=F32)
    a_ref[...] = ab[:, :FOURIER_WIDTH].astype(BF16)
    b_ref[...] = ab[:, FOURIER_WIDTH:].astype(BF16)
    q0 = FOURIER_WIDTH
    k0 = q0 + ATTN_WIDTH
    v0 = k0 + KV_WIDTH
    q = proj[:, q0:k0]
    ssq = jnp.dot((q * q).astype(BF16), bdq_ref[...], preferred_element_type=F32)
    q_ref[...] = (q * lax.rsqrt(ssq * (1.0 / HEAD_DIM) + EPS) * gq_ref[...]).astype(BF16)
    k = proj[:, k0:v0]
    ssk = jnp.dot((k * k).astype(BF16), bdk_ref[...], preferred_element_type=F32)
    k_ref[...] = (k * lax.rsqrt(ssk * (1.0 / HEAD_DIM) + EPS) * gk_ref[...]).astype(BF16)
    v_ref[...] = proj[:, v0:].astype(BF16)


def _inproj(x, mod, g, win, pq, gq, gk, tm=512):
    B = x.shape[0]
    const = lambda shape: pl.BlockSpec(shape, lambda b, i: (0,) * len(shape))
    tok = lambda w: pl.BlockSpec((None, tm, w), lambda b, i: (b, i, 0))
    sds = lambda w: jax.ShapeDtypeStruct((B, SEQ, w), BF16)
    return pl.pallas_call(
        _inproj_kernel,
        grid=(B, SEQ // tm),
        in_specs=[tok(D_MODEL),
                  pl.BlockSpec((None, N_ADA, D_MODEL), lambda b, i: (b, 0, 0)),
                  const((1, D_MODEL)),
                  const((D_MODEL, IN_PROJ_WIDTH)),
                  const((FOURIER_WIDTH, 2 * FOURIER_WIDTH)),
                  const((ATTN_WIDTH, ATTN_WIDTH)),
                  const((KV_WIDTH, KV_WIDTH)),
                  const((1, ATTN_WIDTH)),
                  const((1, KV_WIDTH))],
        out_specs=[tok(FOURIER_WIDTH), tok(FOURIER_WIDTH), tok(ATTN_WIDTH), tok(KV_WIDTH),
                   tok(KV_WIDTH)],
        out_shape=[sds(FOURIER_WIDTH), sds(FOURIER_WIDTH), sds(ATTN_WIDTH), sds(KV_WIDTH),
                   sds(KV_WIDTH)],
        compiler_params=_params(("parallel", "parallel"), VMEM_LIMIT),
        name="inproj",
    )(x, mod, g, win, pq, jnp.asarray(_group_ones(ATTN_WIDTH)).astype(BF16),
      jnp.asarray(_group_ones(KV_WIDTH)).astype(BF16), gq, gk)


HALF = SEQ // 2
REV = 128
REV_BLOCKS = HALF // REV
FOURIER_ROWS = 512


def _fourier_kernel(cm_ref, sm_ref, psh_ref, alt_ref, altrow_ref, a_ref, b_ref, bf_ref, o_ref):
    psh = psh_ref[...]

    def reversed_block(win_lo, k, src):
        if k == 0:
            return jnp.dot(psh[:, :REV], src(win_lo, REV), preferred_element_type=F32)
        return jnp.dot(psh, src(win_lo, 2 * REV), preferred_element_type=F32)

    def folded(ref, sign):
        blocks = []
        for k in range(REV_BLOCKS):
            lo = SEQ - REV * (k + 1)
            rev = reversed_block(lo, k, lambda s, n: ref[pl.ds(s, n), :])
            blocks.append((ref[pl.ds(k * REV, REV), :].astype(F32) + sign * rev).astype(BF16))
        return jnp.concatenate(blocks, axis=0)

    a_even = folded(a_ref, 1.0)
    b_odd = folded(b_ref, -1.0)
    a_mid = a_ref[pl.ds(HALF, 1), :].astype(F32)
    bias = bf_ref[...]
    z_blocks = []
    for i in range(HALF // FOURIER_ROWS):
        rows = pl.ds(i * FOURIER_ROWS, FOURIER_ROWS)
        yc = jnp.dot(cm_ref[rows, :], a_even, preferred_element_type=F32)
        yc = yc + alt_ref[rows, :] * a_mid + bias
        ys = jnp.dot(sm_ref[rows, :], b_odd, preferred_element_type=F32)
        o_ref[rows, :] = (yc - ys).astype(BF16)
        z_blocks.append((yc + ys).astype(BF16))
    z = jnp.concatenate(z_blocks, axis=0)
    for k in range(REV_BLOCKS):
        lo = HALF - REV * (k + 1)
        top = reversed_block(lo, k, lambda s, n: z[s:s + n])
        o_ref[pl.ds(HALF + k * REV, REV), :] = top.astype(BF16)
    y_mid = jnp.dot(altrow_ref[...], a_ref[...], preferred_element_type=F32)[0:1, :] + bias
    o_ref[pl.ds(HALF, 1), :] = y_mid.astype(BF16)


@functools.lru_cache(maxsize=None)
def _fold_tables():
    s = np.arange(HALF, dtype=np.int64)
    ph = (s[:, None] * s[None, :]) % SEQ
    ang = 2.0 * np.pi * ph.astype(np.float64) / SEQ
    sc = 1.0 / math.sqrt(SEQ)
    cm = (np.cos(ang) * sc).astype(np.float32)
    sm = (np.sin(ang) * sc).astype(np.float32)
    psh = np.zeros((REV, 2 * REV), np.float32)
    psh[np.arange(REV), REV - np.arange(REV)] = 1.0
    alt = (np.where(s % 2 == 0, 1.0, -1.0) * sc).astype(np.float32).reshape(HALF, 1)
    t = np.arange(SEQ)
    altrow = np.zeros((SUBLANES, SEQ), np.float32)
    altrow[0] = np.where(t % 2 == 0, 1.0, -1.0) * sc
    return cm, sm, psh, alt, altrow


def _fourier(a, b, bf):
    B = a.shape[0]
    cm, sm, psh, alt, altrow = _fold_tables()
    tok = pl.BlockSpec((None, SEQ, FOURIER_WIDTH), lambda i: (i, 0, 0))
    const = lambda shape: pl.BlockSpec(shape, lambda i: (0,) * len(shape))
    return pl.pallas_call(
        _fourier_kernel,
        grid=(B,),
        in_specs=[const((HALF, HALF)), const((HALF, HALF)), const((REV, 2 * REV)),
                  const((HALF, 1)), const((SUBLANES, SEQ)),
                  tok, tok, const((1, FOURIER_WIDTH))],
        out_specs=tok,
        out_shape=jax.ShapeDtypeStruct((B, SEQ, FOURIER_WIDTH), BF16),
        compiler_params=_params(("parallel",), VMEM_LIMIT),
        name="fourier",
    )(jnp.asarray(cm).astype(BF16), jnp.asarray(sm).astype(BF16), jnp.asarray(psh).astype(BF16),
      jnp.asarray(alt), jnp.asarray(altrow).astype(BF16), a, b, bf)


ATT_SUB = 4
ATT_ROWS = ATT_SUB * BLOCK
ATT_STEPS = N_BLOCKS // ATT_SUB


def _attn_kernel(sink_ref, q_ref, kl_ref, km_ref, kr_ref, vl_ref, vm_ref, vr_ref, bias_ref,
                 o_ref):
    i = pl.program_id(1)
    keys = jnp.concatenate([kl_ref[...], km_ref[...], kr_ref[...]], axis=0)
    vals = jnp.concatenate([vl_ref[...], vm_ref[...], vr_ref[...]], axis=0)
    first_k = lax.broadcasted_iota(jnp.int32, keys.shape, 1) < HEAD_DIM
    first_q = lax.broadcasted_iota(jnp.int32, (Q_PER_KV * BLOCK, LANES), 1) < HEAD_DIM
    row_head = lax.broadcasted_iota(jnp.int32, (Q_PER_KV * BLOCK, 1), 0) // BLOCK
    zero = jnp.zeros_like(keys)
    ones_lo = jnp.where(first_k, 1.0, 0.0).astype(BF16)
    ones_hi = jnp.where(first_k, 0.0, 1.0).astype(BF16)
    keys_kv = [jnp.where(first_k, keys, zero), jnp.where(first_k, zero, keys)]
    vals_kv = [jnp.concatenate([jnp.where(first_k, vals, zero), ones_lo], axis=1),
               jnp.concatenate([jnp.where(first_k, zero, vals), ones_hi], axis=1)]
    sinks = []
    for kv in range(N_KV_HEADS):
        sink = jnp.zeros((Q_PER_KV * BLOCK, 1), F32)
        for r in range(Q_PER_KV):
            sink = jnp.where(row_head == r, sink_ref[kv * Q_PER_KV + r] * LOG2E, sink)
        sinks.append(sink)
    for j in range(ATT_SUB):
        variant = 1
        if j == 0:
            variant = jnp.where(i == 0, 0, variant)
        if j == ATT_SUB - 1:
            variant = jnp.where(i == ATT_STEPS - 1, 2, variant)
        qrows = pl.ds(j * BLOCK, BLOCK)
        krows = slice(j * BLOCK, j * BLOCK + SPAN)
        qs = jnp.concatenate([q_ref[qrows, r * LANES:(r + 1) * LANES] for r in range(Q_PER_KV)],
                             axis=0)
        keys2 = jnp.concatenate([keys_kv[0][krows], keys_kv[1][krows]], axis=0)
        vals2 = jnp.concatenate([vals_kv[0][krows], vals_kv[1][krows]], axis=0)
        logits = lax.dot_general(qs, keys2, (((1,), (1,)), ((), ())),
                                 preferred_element_type=F32)
        logits = logits + bias_ref[variant]
        ms = [jnp.maximum(jnp.max(logits[:, kv * SPAN:(kv + 1) * SPAN], axis=-1, keepdims=True),
                          sinks[kv]) for kv in range(N_KV_HEADS)]
        p = jnp.exp2(jnp.concatenate([logits[:, kv * SPAN:(kv + 1) * SPAN] - ms[kv]
                                      for kv in range(N_KV_HEADS)], axis=1).astype(BF16))
        pv = jnp.dot(p, vals2, preferred_element_type=F32)
        denom = pv[:, LANES:] + jnp.exp2(jnp.where(first_q, sinks[0] - ms[0], sinks[1] - ms[1]))
        out = (pv[:, :LANES] / denom).astype(BF16)
        for r in range(Q_PER_KV):
            o_ref[qrows, r * LANES:(r + 1) * LANES] = out[r * BLOCK:(r + 1) * BLOCK]


def _attn(sink, q, k, v, bias):
    B = q.shape[0]
    edge = lambda f: pl.BlockSpec((None, BLOCK, KV_WIDTH), lambda b, i: (b, f(i), 0))
    left = lambda i: jnp.maximum(i * ATT_SUB - 1, 0)
    right = lambda i: jnp.minimum((i + 1) * ATT_SUB, N_BLOCKS - 1)
    mid = pl.BlockSpec((None, ATT_ROWS, KV_WIDTH), lambda b, i: (b, i, 0))
    qspec = pl.BlockSpec((None, ATT_ROWS, ATTN_WIDTH), lambda b, i: (b, i, 0))
    return pl.pallas_call(
        _attn_kernel,
        grid=(B, ATT_STEPS),
        in_specs=[pl.BlockSpec(memory_space=pltpu.SMEM), qspec,
                  edge(left), mid, edge(right),
                  edge(left), mid, edge(right),
                  pl.BlockSpec((3, Q_PER_KV * BLOCK, N_KV_HEADS * SPAN), lambda b, i: (0, 0, 0))],
        out_specs=qspec,
        out_shape=jax.ShapeDtypeStruct((B, SEQ, ATTN_WIDTH), BF16),
        compiler_params=_params(("parallel", "parallel"), VMEM_LIMIT),
        name="attn",
    )(sink, q, k, k, k, v, v, v, bias)


OUT_ROWS = 256


def _outproj_kernel(yf_ref, ya_ref, x_ref, mod_ref, g_ref, wf_ref, wa_ref, wrh_ref, wrl_ref,
                    x1_ref, h2_ref, aff_ref):
    tm = x_ref.shape[0]
    gain = g_ref[...] * (1.0 + mod_ref[4:5, :])
    shift = mod_ref[3:4, :]
    gate1 = mod_ref[2:3, :]
    lane = lax.broadcasted_iota(jnp.int32, (OUT_ROWS, LANES), 1)
    for c in range(tm // OUT_ROWS):
        rows = pl.ds(c * OUT_ROWS, OUT_ROWS)
        mixed = jnp.dot(yf_ref[rows, :], wf_ref[...], preferred_element_type=F32)
        mixed = mixed + jnp.dot(ya_ref[rows, :], wa_ref[...], preferred_element_type=F32)
        x1 = x_ref[rows, :] + gate1 * mixed
        x1_ref[rows, :] = x1
        ms = jnp.mean(x1 * x1, axis=-1, keepdims=True)
        h2 = x1 * lax.rsqrt(ms + EPS) * gain + shift
        hi = h2.astype(BF16)
        top = pltpu.bitcast(hi[:, :D_MODEL // 2].astype(F32), jnp.uint32)
        bot = pltpu.bitcast(hi[:, D_MODEL // 2:].astype(F32), jnp.uint32)
        words = top | (bot >> 16)
        for j in range(PACK_ROWS):
            h2_ref[pl.ds(c * OUT_ROWS * PACK_ROWS + j, OUT_ROWS, stride=PACK_ROWS), :] = (
                words[:, j * LANES:(j + 1) * LANES])
        lo = (h2 - hi.astype(F32)).astype(BF16)
        part = jnp.dot(hi, wrh_ref[...], preferred_element_type=F32)
        part = part + jnp.dot(lo, wrl_ref[...], preferred_element_type=F32)
        logits = part + pltpu.roll(part, LANES - N_EXPERTS, axis=1)
        logits = jnp.where(lane < N_EXPERTS, logits, NEG_INF)
        m = jnp.max(logits, axis=-1, keepdims=True)
        e = jnp.exp(logits - m)
        aff_ref[rows, :] = e / jnp.sum(e, axis=-1, keepdims=True)


def _outproj(yf, ya, x, mod, g, wf, wa, wrh, wrl, tm=512):
    B = x.shape[0]
    const = lambda shape: pl.BlockSpec(shape, lambda b, i: (0,) * len(shape))
    tok = lambda w: pl.BlockSpec((None, tm, w), lambda b, i: (b, i, 0))
    return pl.pallas_call(
        _outproj_kernel,
        grid=(B, SEQ // tm),
        in_specs=[tok(FOURIER_WIDTH), tok(ATTN_WIDTH), tok(D_MODEL),
                  pl.BlockSpec((None, N_ADA, D_MODEL), lambda b, i: (b, 0, 0)),
                  const((1, D_MODEL)),
                  const((FOURIER_WIDTH, D_MODEL)), const((ATTN_WIDTH, D_MODEL)),
                  const((D_MODEL, LANES)), const((D_MODEL, LANES))],
        out_specs=[tok(D_MODEL),
                   pl.BlockSpec((None, tm * PACK_ROWS, LANES), lambda b, i: (b, i, 0)),
                   tok(LANES)],
        out_shape=[jax.ShapeDtypeStruct((B, SEQ, D_MODEL), F32),
                   jax.ShapeDtypeStruct((B, SEQ * PACK_ROWS, LANES), jnp.uint32),
                   jax.ShapeDtypeStruct((B, SEQ, LANES), F32)],
        compiler_params=_params(("parallel", "parallel"), VMEM_LIMIT),
        name="outproj",
    )(yf, ya, x, mod, g, wf, wa, wrh, wrl)


ROUTE_BATCHES = 2
SEARCH_BITS = 3


def _route_kernel(aff_ref, tri_ref, idx_ref, gate_ref):
    for bb in range(ROUTE_BATCHES):
        _route_one(aff_ref[bb], tri_ref[...], idx_ref.at[bb], gate_ref.at[bb])


def _route_one(aff, tri, idx_ref, gate_ref):
    aff_t = jnp.transpose(aff)[:N_EXPERTS]
    bits = pltpu.bitcast(aff_t, jnp.int32)
    cap = float(CAPACITY)

    t = jnp.zeros((N_EXPERTS, 1), jnp.int32)
    for shift in range(30 - SEARCH_BITS, -1, -SEARCH_BITS):
        digit = jnp.zeros((N_EXPERTS, 1), jnp.int32)
        for k in range(1, 1 << SEARCH_BITS):
            cnt = jnp.sum(jnp.where(bits >= (t | (k << shift)), 1.0, 0.0), axis=1, keepdims=True)
            digit = digit + jnp.where(cnt >= cap, 1, 0)
        t = t | (digit << shift)
    gt = bits > t
    eq = bits == t
    need = cap - jnp.sum(jnp.where(gt, 1.0, 0.0), axis=1, keepdims=True)

    n_chunks = SEQ // LANES

    def prefix(flags_f32):
        outs = []
        carry = jnp.zeros((N_EXPERTS, 1), F32)
        for c in range(n_chunks):
            f = flags_f32[:, c * LANES:(c + 1) * LANES]
            incl = jnp.dot(f.astype(BF16), tri, preferred_element_type=F32)
            outs.append(incl - f + carry)
            carry = carry + jnp.sum(f, axis=1, keepdims=True)
        return jnp.concatenate(outs, axis=1)

    eq_f = jnp.where(eq, 1.0, 0.0)
    eq_rank = prefix(eq_f)
    sel_f = jnp.where(gt, 1.0, jnp.where(eq_rank < need, eq_f, 0.0))
    pos = prefix(sel_f)
    posm = jnp.where(sel_f > 0.0, pos, -1.0)

    hi = aff_t.astype(BF16).astype(F32)
    r1 = aff_t - hi
    mid = r1.astype(BF16).astype(F32)
    lo = r1 - mid
    tok = lax.broadcasted_iota(jnp.int32, (N_EXPERTS, SEQ), 1)
    row = lax.broadcasted_iota(jnp.int32, (N_EXPERTS, SEQ), 0)
    tok_rows = jnp.where(row == 0, (tok >> 6).astype(F32),
                         jnp.where(row == 1, (tok & 63).astype(F32), 0.0))
    vals_t = jnp.concatenate([hi, mid, lo, tok_rows], axis=0).astype(BF16)

    slot = lax.broadcasted_iota(jnp.int32, (CAPACITY, SEQ), 0).astype(F32).astype(BF16)
    posm_b = posm.astype(BF16)
    one_b = jnp.ones((CAPACITY, SEQ), BF16)
    zero_b = jnp.zeros((CAPACITY, SEQ), BF16)
    for e in range(N_EXPERTS):
        onehot = jnp.where(posm_b[e:e + 1, :] == slot, one_b, zero_b)
        res = lax.dot_general(vals_t, onehot, (((1,), (1,)), ((), ())),
                              preferred_element_type=F32)
        cols = pl.ds(e * CAPACITY, CAPACITY)
        tok_idx = res[3 * N_EXPERTS:3 * N_EXPERTS + 1] * 64.0 + res[3 * N_EXPERTS + 1:
                                                                    3 * N_EXPERTS + 2]
        idx_ref[:, cols] = tok_idx.astype(jnp.int32) * PACK_ROWS
        gate_ref[:, cols] = (res[e:e + 1] + res[N_EXPERTS + e:N_EXPERTS + e + 1]
                             + res[2 * N_EXPERTS + e:2 * N_EXPERTS + e + 1])


def _route(aff):
    B = aff.shape[0]
    n = N_EXPERTS * CAPACITY
    return pl.pallas_call(
        _route_kernel,
        grid=(B // ROUTE_BATCHES,),
        in_specs=[pl.BlockSpec((ROUTE_BATCHES, SEQ, LANES), lambda b: (b, 0, 0)),
                  pl.BlockSpec((LANES, LANES), lambda b: (0, 0))],
        out_specs=[pl.BlockSpec((ROUTE_BATCHES, 1, n), lambda b: (b, 0, 0)),
                   pl.BlockSpec((ROUTE_BATCHES, 1, n), lambda b: (b, 0, 0))],
        out_shape=[jax.ShapeDtypeStruct((B, 1, n), jnp.int32),
                   jax.ShapeDtypeStruct((B, 1, n), F32)],
        compiler_params=_params(("parallel",), VMEM_LIMIT),
        name="route",
    )(aff, jnp.asarray(_tri_incl()).astype(BF16))


PAIR = 2


def _moe_kernel(idx_ref, h2_ref, wg_ref, wu_ref, wd_ref, y_ref, xin0_ref, xin1_ref):
    e = pl.program_id(1)
    last = N_EXPERTS - 1
    n = N_EXPERTS * CAPACITY
    rows = PAIR * CAPACITY

    def gather_rows(ex, dst_ref):
        for bb in range(PAIR):
            base = bb * n + ex * CAPACITY
            for p in range(CAPACITY):
                off = idx_ref[0, base + p]
                tile = h2_ref[bb, pl.ds(pl.multiple_of((off >> 3) << 3, SUBLANES), SUBLANES), :]
                tile = pltpu.roll(tile, off & PACK_ROWS, axis=0)
                dst_ref[pl.ds((bb * CAPACITY + p) * PACK_ROWS, PACK_ROWS), :] = tile[:PACK_ROWS]

    def expert(xin_ref):
        words = [xin_ref[pl.ds(j, rows, stride=PACK_ROWS), :] for j in range(PACK_ROWS)]
        xin = jnp.concatenate(
            [pltpu.bitcast(w & jnp.uint32(0xFFFF0000), F32).astype(BF16) for w in words]
            + [pltpu.bitcast(w << 16, F32).astype(BF16) for w in words], axis=1)
        a = jnp.dot(xin, wg_ref[...], preferred_element_type=F32)
        u = jnp.dot(xin, wu_ref[...], preferred_element_type=F32)
        hmid = (a * (1.0 / (1.0 + jnp.exp(-a))) * u).astype(BF16)
        y = jnp.dot(hmid, wd_ref[...], preferred_element_type=F32)
        for bb in range(PAIR):
            for j in range(ROW_SLAB):
                y_ref[bb, pl.ds(j, CAPACITY, stride=ROW_SLAB), :] = (
                    y[bb * CAPACITY:(bb + 1) * CAPACITY, j * LANES:(j + 1) * LANES])

    @pl.when(e == 0)
    def _():
        gather_rows(0, xin0_ref)

    def step(xin_cur, xin_nxt):
        gather_rows(jnp.minimum(e + 1, last), xin_nxt)
        expert(xin_cur)

    @pl.when(e % 2 == 0)
    def _():
        step(xin0_ref, xin1_ref)

    @pl.when(e % 2 == 1)
    def _():
        step(xin1_ref, xin0_ref)


def _moe(idx, h2, wg, wu, wd):
    B = h2.shape[0]
    n = N_EXPERTS * CAPACITY
    rows = SEQ * PACK_ROWS
    pairs = B // PAIR
    wspec = lambda r, c: pl.BlockSpec((None, r, c), lambda b, e: (e, 0, 0))
    stage = pltpu.VMEM((PAIR * CAPACITY * PACK_ROWS, LANES), jnp.uint32)
    out = pl.pallas_call(
        _moe_kernel,
        grid=(pairs, N_EXPERTS),
        in_specs=[pl.BlockSpec((None, 1, PAIR * n), lambda b, e: (b, 0, 0),
                               memory_space=pltpu.SMEM),
                  pl.BlockSpec((None, PAIR, rows, LANES), lambda b, e: (b, 0, 0, 0)),
                  wspec(D_MODEL, D_EXPERT), wspec(D_MODEL, D_EXPERT), wspec(D_EXPERT, D_MODEL)],
        out_specs=pl.BlockSpec((None, PAIR, CAPACITY * ROW_SLAB, LANES),
                               lambda b, e: (b, 0, e, 0)),
        out_shape=jax.ShapeDtypeStruct((pairs, PAIR, n * ROW_SLAB, LANES), F32),
        scratch_shapes=[stage, stage],
        compiler_params=_params(("parallel", "arbitrary"), VMEM_LIMIT),
        name="moe",
    )(idx.reshape(pairs, 1, PAIR * n), h2.reshape(pairs, PAIR, rows, LANES), wg, wu, wd)
    return out.reshape(B, n * ROW_SLAB, LANES)


COMBINE_EXPERTS = 4
SCATTER_UNROLL = 8
COMBINE_ROWS = 256


def _combine_kernel(idx_ref, gate_ref, y_ref, x1_ref, mod_ref, o_ref, acc_ref):
    j = pl.program_id(1)
    slots = COMBINE_EXPERTS * CAPACITY
    base = j * slots

    @pl.when(j == 0)
    def _():
        acc_ref[...] = jnp.zeros_like(acc_ref)

    for g in range(slots // SCATTER_UNROLL):
        new = []
        for u in range(SCATTER_UNROLL):
            r = g * SCATTER_UNROLL + u
            dst = pl.multiple_of(idx_ref[0, base + r] * (ROW_SLAB // PACK_ROWS), ROW_SLAB)
            new.append((dst, acc_ref[pl.ds(dst, ROW_SLAB), :]
                        + y_ref[pl.ds(r * ROW_SLAB, ROW_SLAB), :] * gate_ref[0, base + r]))
        for dst, val in new:
            acc_ref[pl.ds(dst, ROW_SLAB), :] = val

    @pl.when(j == pl.num_programs(1) - 1)
    def _():
        for rb in range(SEQ // COMBINE_ROWS):
            rows = pl.ds(rb * COMBINE_ROWS, COMBINE_ROWS)
            for c in range(ROW_SLAB):
                cols = slice(c * LANES, (c + 1) * LANES)
                chunk = acc_ref[pl.ds(rb * COMBINE_ROWS * ROW_SLAB + c, COMBINE_ROWS,
                                      stride=ROW_SLAB), :]
                o_ref[rows, cols] = x1_ref[rows, cols] + mod_ref[5:6, cols] * chunk


def _combine(idx, gate, y, x1, mod):
    B = x1.shape[0]
    n = N_EXPERTS * CAPACITY
    tok = pl.BlockSpec((None, SEQ, D_MODEL), lambda b, j: (b, 0, 0))
    smem = pl.BlockSpec((None, 1, n), lambda b, j: (b, 0, 0), memory_space=pltpu.SMEM)
    return pl.pallas_call(
        _combine_kernel,
        grid=(B, N_EXPERTS // COMBINE_EXPERTS),
        in_specs=[smem, smem,
                  pl.BlockSpec((None, COMBINE_EXPERTS * CAPACITY * ROW_SLAB, LANES),
                               lambda b, j: (b, j, 0)),
                  tok,
                  pl.BlockSpec((None, N_ADA, D_MODEL), lambda b, j: (b, 0, 0))],
        out_specs=tok,
        out_shape=jax.ShapeDtypeStruct((B, SEQ, D_MODEL), F32),
        scratch_shapes=[pltpu.VMEM((SEQ * ROW_SLAB, LANES), F32)],
        compiler_params=_params(("parallel", "arbitrary"), VMEM_LIMIT),
        name="combine",
    )(idx, gate, y, x1, mod)


def _head_perm():
    perm = []
    for r in range(Q_PER_KV):
        for kv in range(N_KV_HEADS):
            h = kv * Q_PER_KV + r
            perm.extend(range(h * HEAD_DIM, (h + 1) * HEAD_DIM))
    return np.asarray(perm, dtype=np.int32)


def kernel(x, c, rel_bias, w_ada, b_ada, norm_mix_g, norm_ffn_g, w_in, w_fourier, b_fourier,
           q_norm_g, k_norm_g, sink, w_out, w_router, w_gate, w_up, w_down):
    B = x.shape[0]
    perm = _head_perm()
    l = 0
    mod = _ada(c, w_ada[l], b_ada[l]).reshape(B, N_ADA, D_MODEL)
    pq = _fold(w_fourier[l])
    bias = _bias_table(rel_bias)

    wi = w_in[l]
    q_cols = wi[:, FOURIER_WIDTH:FOURIER_WIDTH + ATTN_WIDTH][:, perm]
    win = jnp.concatenate([wi[:, :FOURIER_WIDTH], q_cols, wi[:, FOURIER_WIDTH + ATTN_WIDTH:]],
                          axis=1).astype(BF16)
    gq = (jnp.tile(q_norm_g[l], N_Q_HEADS) * (HEAD_DIM ** -0.5 * LOG2E)).reshape(1, ATTN_WIDTH)
    gk = jnp.tile(k_norm_g[l], N_KV_HEADS).reshape(1, KV_WIDTH)
    a, b, q, k, v = _inproj(x, mod, norm_mix_g[l].reshape(1, D_MODEL), win, pq, gq, gk)

    yf = _fourier(a, b, b_fourier[l].reshape(1, FOURIER_WIDTH))
    ya = _attn(sink[l], q, k, v, bias)

    wo = w_out[l]
    wf = wo[:FOURIER_WIDTH].astype(BF16)
    wa = wo[FOURIER_WIDTH:][perm].astype(BF16)
    w_hi = w_router[l].astype(BF16)
    w_lo = (w_router[l] - w_hi.astype(F32)).astype(BF16)
    wrh = jnp.pad(jnp.concatenate([w_hi, w_lo], axis=1), ((0, 0), (0, LANES - 2 * N_EXPERTS)))
    wrl = jnp.pad(w_hi, ((0, 0), (0, LANES - N_EXPERTS)))
    x1, h2, aff = _outproj(yf, ya, x, mod, norm_ffn_g[l].reshape(1, D_MODEL), wf, wa, wrh, wrl)

    idx, gate = _route(aff)
    n = N_EXPERTS * CAPACITY
    y = _moe(idx, h2,
             w_gate[l].astype(BF16), w_up[l].astype(BF16), w_down[l].astype(BF16))
    return _combine(idx, gate, y, x1, mod)
```

```python
import functools
import math

import numpy as np
import jax
import jax.numpy as jnp
from jax import lax
from jax.experimental import pallas as pl
from jax.experimental.pallas import tpu as pltpu

D_MODEL = 1024
SEQ = 2048
HEAD_DIM = 64
FOURIER_WIDTH = 512
ATTN_WIDTH = 512
N_GROUPS = 8
N_Q_HEADS = 8
Q_PER_KV = 4
N_KV_HEADS = 2
KV_WIDTH = 128
IN_PROJ_WIDTH = 1280
WINDOW = 128
BLOCK = 128
SPAN = BLOCK + 2 * WINDOW
N_BLOCKS = SEQ // BLOCK
N_BUCKETS = 32
MAX_DISTANCE = 128
N_EXPERTS = 16
CAPACITY = 2 * SEQ // N_EXPERTS
D_EXPERT = 1024
N_ADA = 6
EPS = 1e-6

LANES = 128
SUBLANES = 8
ROW_SLAB = D_MODEL // LANES
PACK_ROWS = ROW_SLAB // 2
VMEM_LIMIT = 56 * 1024 * 1024

F32 = jnp.float32
BF16 = jnp.bfloat16
NEG_INF = float("-inf")
LOG2E = math.log2(math.e)


def _params(sem, vmem=None):
    return pltpu.CompilerParams(dimension_semantics=sem, vmem_limit_bytes=vmem)


@functools.lru_cache(maxsize=None)
def _chan_dft():
    c = np.arange(HEAD_DIM, dtype=np.int64)
    ph = (c[:, None] * c[None, :]) % HEAD_DIM
    ang = 2.0 * np.pi * ph.astype(np.float64) / HEAD_DIM
    sc = 1.0 / math.sqrt(HEAD_DIM)
    eye = np.eye(N_GROUPS)
    cbd = np.kron(eye, np.cos(ang) * sc)
    sbd = np.kron(eye, np.sin(ang) * sc)
    return cbd.astype(np.float32), sbd.astype(np.float32)


@functools.lru_cache(maxsize=None)
def _bucket_table():
    rel = np.arange(SPAN)[None, :] - WINDOW - np.arange(BLOCK)[:, None]
    half = N_BUCKETS // 2
    max_exact = half // 2
    n = np.abs(rel)
    nf = np.maximum(n, 1).astype(np.float64)
    large = max_exact + (np.log(nf / max_exact) / math.log(MAX_DISTANCE / max_exact)
                         * (half - max_exact)).astype(np.int64)
    sq = np.maximum(n.astype(np.int64) ** 2 // (max_exact * max_exact), 1)
    large_int = max_exact + np.floor(np.log2(sq.astype(np.float64)) + 1e-9).astype(np.int64)
    assert np.array_equal(np.where(n >= max_exact, large, 0), np.where(n >= max_exact, large_int, 0))
    large = np.minimum(large, half - 1)
    bucket = np.where(rel > 0, half, 0) + np.where(n < max_exact, n, large)
    return bucket.astype(np.int32)


@functools.lru_cache(maxsize=None)
def _group_ones(width):
    return np.kron(np.eye(width // HEAD_DIM), np.ones((HEAD_DIM, HEAD_DIM))).astype(np.float32)


@functools.lru_cache(maxsize=None)
def _tri_incl():
    i = np.arange(LANES)
    return (i[:, None] <= i[None, :]).astype(np.float32)


def _ada_kernel(c_ref, w_ref, b_ref, o_ref):
    c = c_ref[...]
    ca = c * (1.0 / (1.0 + jnp.exp(-c)))
    o_ref[...] = jnp.dot(ca, w_ref[...], precision=lax.Precision.HIGHEST,
                         preferred_element_type=F32) + b_ref[...]


def _ada(c, w_ada, b_ada):
    B = c.shape[0]
    n = N_ADA * D_MODEL
    tn = D_MODEL
    return pl.pallas_call(
        _ada_kernel,
        grid=(n // tn,),
        in_specs=[pl.BlockSpec((B, D_MODEL), lambda j: (0, 0)),
                  pl.BlockSpec((D_MODEL, tn), lambda j: (0, j)),
                  pl.BlockSpec((1, tn), lambda j: (0, j))],
        out_specs=pl.BlockSpec((B, tn), lambda j: (0, j)),
        out_shape=jax.ShapeDtypeStruct((B, n), F32),
        compiler_params=_params(("arbitrary",)),
        name="ada",
    )(c, w_ada, b_ada.reshape(1, n))


def _fold_kernel(cbd_ref, sbd_ref, w_ref, o_ref):
    w = w_ref[...]
    o_ref[:, :FOURIER_WIDTH] = jnp.dot(cbd_ref[...], w, precision=lax.Precision.HIGHEST,
                                       preferred_element_type=F32).astype(BF16)
    o_ref[:, FOURIER_WIDTH:] = jnp.dot(sbd_ref[...], w, precision=lax.Precision.HIGHEST,
                                       preferred_element_type=F32).astype(BF16)


def _fold(w_fourier):
    wbd = (jnp.eye(N_GROUPS, dtype=F32)[:, None, :, None] * w_fourier[:, :, None, :]
           ).reshape(FOURIER_WIDTH, FOURIER_WIDTH)
    cbd, sbd = _chan_dft()
    return pl.pallas_call(
        _fold_kernel,
        out_shape=jax.ShapeDtypeStruct((FOURIER_WIDTH, 2 * FOURIER_WIDTH), BF16),
        name="fold",
    )(jnp.asarray(cbd), jnp.asarray(sbd), wbd)


def _bias_kernel(rb_ref, bucket_ref, o_ref):
    h = pl.program_id(0)
    bk = bucket_ref[...]
    acc = jnp.zeros((BLOCK, SPAN), F32)
    for b in range(N_BUCKETS):
        acc = jnp.where(bk == b, rb_ref[b, h], acc)
    j = lax.broadcasted_iota(jnp.int32, (BLOCK, SPAN), 1)
    q = lax.broadcasted_iota(jnp.int32, (BLOCK, SPAN), 0)
    band = jnp.abs(j - WINDOW - q) <= WINDOW
    base = jnp.where(band, acc * LOG2E, NEG_INF)
    o_ref[0] = jnp.where(j >= WINDOW, base, NEG_INF)
    o_ref[1] = base
    o_ref[2] = jnp.where(j < WINDOW + BLOCK, base, NEG_INF)


def _bias_table(rel_bias):
    return pl.pallas_call(
        _bias_kernel,
        grid=(N_Q_HEADS,),
        in_specs=[pl.BlockSpec(memory_space=pltpu.SMEM),
                  pl.BlockSpec((BLOCK, SPAN), lambda h: (0, 0))],
        out_specs=pl.BlockSpec((3, BLOCK, SPAN), lambda h: (0, h % Q_PER_KV, h // Q_PER_KV)),
        out_shape=jax.ShapeDtypeStruct((3, Q_PER_KV * BLOCK, N_KV_HEADS * SPAN), F32),
        compiler_params=_params(("arbitrary",)),
        name="bias",
    )(rel_bias, jnp.asarray(_bucket_table()))


IN_ROWS = 256


def _inproj_kernel(x_ref, mod_ref, g_ref, win_ref, pq_ref, bdq_ref, bdk_ref, gq_ref, gk_ref,
                   a_ref, b_ref, q_ref, k_ref, v_ref):
    gain = g_ref[...] * (1.0 + mod_ref[1:2, :])
    shift = mod_ref[0:1, :]
    q0 = FOURIER_WIDTH
    k0 = q0 + ATTN_WIDTH
    v0 = k0 + KV_WIDTH
    for c in range(x_ref.shape[0] // IN_ROWS):
        rows = pl.ds(c * IN_ROWS, IN_ROWS)
        x = x_ref[rows, :]
        ms = jnp.mean(x * x, axis=-1, keepdims=True)
        h = x * lax.rsqrt(ms + EPS) * gain + shift
        proj = jnp.dot(h.astype(BF16), win_ref[...], preferred_element_type=F32)
        uf = proj[:, :FOURIER_WIDTH].astype(BF16)
        ab = jnp.dot(uf, pq_ref[...], preferred_element_type=F32)
        a_ref[rows, :] = ab[:, :FOURIER_WIDTH].astype(BF16)
        b_ref[rows, :] = ab[:, FOURIER_WIDTH:].astype(BF16)
        q = proj[:, q0:k0]
        ssq = jnp.dot((q * q).astype(BF16), bdq_ref[...], preferred_element_type=F32)
        q_ref[rows, :] = (q * lax.rsqrt(ssq * (1.0 / HEAD_DIM) + EPS) * gq_ref[...]).astype(BF16)
        k = proj[:, k0:v0]
        ssk = jnp.dot((k * k).astype(BF16), bdk_ref[...], preferred_element_type=F32)
        k_ref[rows, :] = (k * lax.rsqrt(ssk * (1.0 / HEAD_DIM) + EPS) * gk_ref[...]).astype(BF16)
        v_ref[rows, :] = proj[:, v0:].astype(BF16)


def _inproj(x, mod, g, win, pq, gq, gk, tm=512):
    B = x.shape[0]
    const = lambda shape: pl.BlockSpec(shape, lambda b, i: (0,) * len(shape))
    tok = lambda w: pl.BlockSpec((None, tm, w), lambda b, i: (b, i, 0))
    sds = lambda w: jax.ShapeDtypeStruct((B, SEQ, w), BF16)
    return pl.pallas_call(
        _inproj_kernel,
        grid=(B, SEQ // tm),
        in_specs=[tok(D_MODEL),
                  pl.BlockSpec((None, N_ADA, D_MODEL), lambda b, i: (b, 0, 0)),
                  const((1, D_MODEL)),
                  const((D_MODEL, IN_PROJ_WIDTH)),
                  const((FOURIER_WIDTH, 2 * FOURIER_WIDTH)),
                  const((ATTN_WIDTH, ATTN_WIDTH)),
                  const((KV_WIDTH, KV_WIDTH)),
                  const((1, ATTN_WIDTH)),
                  const((1, KV_WIDTH))],
        out_specs=[tok(FOURIER_WIDTH), tok(FOURIER_WIDTH), tok(ATTN_WIDTH), tok(KV_WIDTH),
                   tok(KV_WIDTH)],
        out_shape=[sds(FOURIER_WIDTH), sds(FOURIER_WIDTH), sds(ATTN_WIDTH), sds(KV_WIDTH),
                   sds(KV_WIDTH)],
        compiler_params=_params(("parallel", "parallel"), VMEM_LIMIT),
        name="inproj",
    )(x, mod, g, win, pq, jnp.asarray(_group_ones(ATTN_WIDTH)).astype(BF16),
      jnp.asarray(_group_ones(KV_WIDTH)).astype(BF16), gq, gk)


HALF = SEQ // 2
REV = 128
REV_BLOCKS = HALF // REV
FOURIER_ROWS = 512


def _fourier_kernel(cm_ref, sm_ref, psh_ref, alt_ref, altrow_ref, a_ref, b_ref, bf_ref, o_ref):
    psh = psh_ref[...]

    def reversed_block(win_lo, k, src):
        if k == 0:
            return jnp.dot(psh[:, :REV], src(win_lo, REV), preferred_element_type=F32)
        return jnp.dot(psh, src(win_lo, 2 * REV), preferred_element_type=F32)

    def folded(ref, sign):
        blocks = []
        for k in range(REV_BLOCKS):
            lo = SEQ - REV * (k + 1)
            rev = reversed_block(lo, k, lambda s, n: ref[pl.ds(s, n), :])
            blocks.append((ref[pl.ds(k * REV, REV), :].astype(F32) + sign * rev).astype(BF16))
        return jnp.concatenate(blocks, axis=0)

    a_even = folded(a_ref, 1.0)
    b_odd = folded(b_ref, -1.0)
    a_mid = a_ref[pl.ds(HALF, 1), :].astype(F32)
    bias = bf_ref[...]
    z_blocks = []
    for i in range(HALF // FOURIER_ROWS):
        rows = pl.ds(i * FOURIER_ROWS, FOURIER_ROWS)
        yc = jnp.dot(cm_ref[rows, :], a_even, preferred_element_type=F32)
        yc = yc + alt_ref[rows, :] * a_mid + bias
        ys = jnp.dot(sm_ref[rows, :], b_odd, preferred_element_type=F32)
        o_ref[rows, :] = (yc - ys).astype(BF16)
        z_blocks.append((yc + ys).astype(BF16))
    z = jnp.concatenate(z_blocks, axis=0)
    for k in range(REV_BLOCKS):
        lo = HALF - REV * (k + 1)
        top = reversed_block(lo, k, lambda s, n: z[s:s + n])
        o_ref[pl.ds(HALF + k * REV, REV), :] = top.astype(BF16)
    y_mid = jnp.dot(altrow_ref[...], a_ref[...], preferred_element_type=F32)[0:1, :] + bias
    o_ref[pl.ds(HALF, 1), :] = y_mid.astype(BF16)


@functools.lru_cache(maxsize=None)
def _fold_tables():
    s = np.arange(HALF, dtype=np.int64)
    ph = (s[:, None] * s[None, :]) % SEQ
    ang = 2.0 * np.pi * ph.astype(np.float64) / SEQ
    sc = 1.0 / math.sqrt(SEQ)
    cm = (np.cos(ang) * sc).astype(np.float32)
    sm = (np.sin(ang) * sc).astype(np.float32)
    psh = np.zeros((REV, 2 * REV), np.float32)
    psh[np.arange(REV), REV - np.arange(REV)] = 1.0
    alt = (np.where(s % 2 == 0, 1.0, -1.0) * sc).astype(np.float32).reshape(HALF, 1)
    t = np.arange(SEQ)
    altrow = np.zeros((SUBLANES, SEQ), np.float32)
    altrow[0] = np.where(t % 2 == 0, 1.0, -1.0) * sc
    return cm, sm, psh, alt, altrow


def _fourier(a, b, bf):
    B = a.shape[0]
    cm, sm, psh, alt, altrow = _fold_tables()
    tok = pl.BlockSpec((None, SEQ, FOURIER_WIDTH), lambda i: (i, 0, 0))
    const = lambda shape: pl.BlockSpec(shape, lambda i: (0,) * len(shape))
    return pl.pallas_call(
        _fourier_kernel,
        grid=(B,),
        in_specs=[const((HALF, HALF)), const((HALF, HALF)), const((REV, 2 * REV)),
                  const((HALF, 1)), const((SUBLANES, SEQ)),
                  tok, tok, const((1, FOURIER_WIDTH))],
        out_specs=tok,
        out_shape=jax.ShapeDtypeStruct((B, SEQ, FOURIER_WIDTH), BF16),
        compiler_params=_params(("parallel",), VMEM_LIMIT),
        name="fourier",
    )(jnp.asarray(cm).astype(BF16), jnp.asarray(sm).astype(BF16), jnp.asarray(psh).astype(BF16),
      jnp.asarray(alt), jnp.asarray(altrow).astype(BF16), a, b, bf)


ATT_SUB = 4
ATT_ROWS = ATT_SUB * BLOCK
ATT_STEPS = N_BLOCKS // ATT_SUB


def _attn_kernel(sink_ref, q_ref, kl_ref, km_ref, kr_ref, vl_ref, vm_ref, vr_ref, bias_ref,
                 o_ref):
    i = pl.program_id(1)
    keys = jnp.concatenate([kl_ref[...], km_ref[...], kr_ref[...]], axis=0)
    vals = jnp.concatenate([vl_ref[...], vm_ref[...], vr_ref[...]], axis=0)
    first_k = lax.broadcasted_iota(jnp.int32, keys.shape, 1) < HEAD_DIM
    first_q = lax.broadcasted_iota(jnp.int32, (Q_PER_KV * BLOCK, LANES), 1) < HEAD_DIM
    row_head = lax.broadcasted_iota(jnp.int32, (Q_PER_KV * BLOCK, 1), 0) // BLOCK
    zero = jnp.zeros_like(keys)
    ones_lo = jnp.where(first_k, 1.0, 0.0).astype(BF16)
    ones_hi = jnp.where(first_k, 0.0, 1.0).astype(BF16)
    keys_kv = [jnp.where(first_k, keys, zero), jnp.where(first_k, zero, keys)]
    vals_kv = [jnp.concatenate([jnp.where(first_k, vals, zero), ones_lo], axis=1),
               jnp.concatenate([jnp.where(first_k, zero, vals), ones_hi], axis=1)]
    sinks = []
    for kv in range(N_KV_HEADS):
        sink = jnp.zeros((Q_PER_KV * BLOCK, 1), F32)
        for r in range(Q_PER_KV):
            sink = jnp.where(row_head == r, sink_ref[kv * Q_PER_KV + r] * LOG2E, sink)
        sinks.append(sink)
    for j in range(ATT_SUB):
        variant = 1
        if j == 0:
            variant = jnp.where(i == 0, 0, variant)
        if j == ATT_SUB - 1:
            variant = jnp.where(i == ATT_STEPS - 1, 2, variant)
        qrows = pl.ds(j * BLOCK, BLOCK)
        krows = slice(j * BLOCK, j * BLOCK + SPAN)
        qs = jnp.concatenate([q_ref[qrows, r * LANES:(r + 1) * LANES] for r in range(Q_PER_KV)],
                             axis=0)
        keys2 = jnp.concatenate([keys_kv[0][krows], keys_kv[1][krows]], axis=0)
        vals2 = jnp.concatenate([vals_kv[0][krows], vals_kv[1][krows]], axis=0)
        logits = lax.dot_general(qs, keys2, (((1,), (1,)), ((), ())),
                                 preferred_element_type=F32)
        logits = logits + bias_ref[variant]
        ms = [jnp.maximum(jnp.max(logits[:, kv * SPAN:(kv + 1) * SPAN], axis=-1, keepdims=True),
                          sinks[kv]) for kv in range(N_KV_HEADS)]
        p = jnp.exp2(jnp.concatenate([logits[:, kv * SPAN:(kv + 1) * SPAN] - ms[kv]
                                      for kv in range(N_KV_HEADS)], axis=1).astype(BF16))
        pv = jnp.dot(p, vals2, preferred_element_type=F32)
        denom = pv[:, LANES:] + jnp.exp2(jnp.where(first_q, sinks[0] - ms[0], sinks[1] - ms[1]))
        out = (pv[:, :LANES] / denom).astype(BF16)
        for r in range(Q_PER_KV):
            o_ref[qrows, r * LANES:(r + 1) * LANES] = out[r * BLOCK:(r + 1) * BLOCK]


def _attn(sink, q, k, v, bias):
    B = q.shape[0]
    edge = lambda f: pl.BlockSpec((None, BLOCK, KV_WIDTH), lambda b, i: (b, f(i), 0))
    left = lambda i: jnp.maximum(i * ATT_SUB - 1, 0)
    right = lambda i: jnp.minimum((i + 1) * ATT_SUB, N_BLOCKS - 1)
    mid = pl.BlockSpec((None, ATT_ROWS, KV_WIDTH), lambda b, i: (b, i, 0))
    qspec = pl.BlockSpec((None, ATT_ROWS, ATTN_WIDTH), lambda b, i: (b, i, 0))
    return pl.pallas_call(
        _attn_kernel,
        grid=(B, ATT_STEPS),
        in_specs=[pl.BlockSpec(memory_space=pltpu.SMEM), qspec,
                  edge(left), mid, edge(right),
                  edge(left), mid, edge(right),
                  pl.BlockSpec((3, Q_PER_KV * BLOCK, N_KV_HEADS * SPAN), lambda b, i: (0, 0, 0))],
        out_specs=qspec,
        out_shape=jax.ShapeDtypeStruct((B, SEQ, ATTN_WIDTH), BF16),
        compiler_params=_params(("parallel", "parallel"), VMEM_LIMIT),
        name="attn",
    )(sink, q, k, k, k, v, v, v, bias)


OUT_ROWS = 256


def _outproj_kernel(yf_ref, ya_ref, x_ref, mod_ref, g_ref, wf_ref, wa_ref, wrh_ref,
                    x1_ref, h2_ref, aff_ref):
    tm = x_ref.shape[0]
    gain = g_ref[...] * (1.0 + mod_ref[4:5, :])
    shift = mod_ref[3:4, :]
    gate1 = mod_ref[2:3, :]
    lane = lax.broadcasted_iota(jnp.int32, (OUT_ROWS, LANES), 1)
    for c in range(tm // OUT_ROWS):
        rows = pl.ds(c * OUT_ROWS, OUT_ROWS)
        mixed = jnp.dot(yf_ref[rows, :], wf_ref[...], preferred_element_type=F32)
        mixed = mixed + jnp.dot(ya_ref[rows, :], wa_ref[...], preferred_element_type=F32)
        x1 = x_ref[rows, :] + gate1 * mixed
        x1_ref[rows, :] = x1
        ms = jnp.mean(x1 * x1, axis=-1, keepdims=True)
        h2 = x1 * lax.rsqrt(ms + EPS) * gain + shift
        hi = h2.astype(BF16)
        top = pltpu.bitcast(hi[:, :D_MODEL // 2].astype(F32), jnp.uint32)
        bot = pltpu.bitcast(hi[:, D_MODEL // 2:].astype(F32), jnp.uint32)
        words = top | (bot >> 16)
        for j in range(PACK_ROWS):
            h2_ref[pl.ds(c * OUT_ROWS * PACK_ROWS + j, OUT_ROWS, stride=PACK_ROWS), :] = (
                words[:, j * LANES:(j + 1) * LANES])
        part = jnp.dot(hi, wrh_ref[...], preferred_element_type=F32)
        logits = part + pltpu.roll(part, LANES - N_EXPERTS, axis=1)
        logits = jnp.where(lane < N_EXPERTS, logits, NEG_INF)
        m = jnp.max(logits, axis=-1, keepdims=True)
        e = jnp.exp(logits - m)
        aff_ref[rows, :] = e / jnp.sum(e, axis=-1, keepdims=True)


def _outproj(yf, ya, x, mod, g, wf, wa, wrh, tm=512):
    B = x.shape[0]
    const = lambda shape: pl.BlockSpec(shape, lambda b, i: (0,) * len(shape))
    tok = lambda w: pl.BlockSpec((None, tm, w), lambda b, i: (b, i, 0))
    return pl.pallas_call(
        _outproj_kernel,
        grid=(B, SEQ // tm),
        in_specs=[tok(FOURIER_WIDTH), tok(ATTN_WIDTH), tok(D_MODEL),
                  pl.BlockSpec((None, N_ADA, D_MODEL), lambda b, i: (b, 0, 0)),
                  const((1, D_MODEL)),
                  const((FOURIER_WIDTH, D_MODEL)), const((ATTN_WIDTH, D_MODEL)),
                  const((D_MODEL, LANES))],
        out_specs=[tok(D_MODEL),
                   pl.BlockSpec((None, tm * PACK_ROWS, LANES), lambda b, i: (b, i, 0)),
                   tok(LANES)],
        out_shape=[jax.ShapeDtypeStruct((B, SEQ, D_MODEL), F32),
                   jax.ShapeDtypeStruct((B, SEQ * PACK_ROWS, LANES), jnp.uint32),
                   jax.ShapeDtypeStruct((B, SEQ, LANES), F32)],
        compiler_params=_params(("parallel", "parallel"), VMEM_LIMIT),
        name="outproj",
    )(yf, ya, x, mod, g, wf, wa, wrh)


ROUTE_BATCHES = 2
SEARCH_BITS = 3


def _route_kernel(aff_ref, tri_ref, idx_ref, gate_ref):
    for bb in range(ROUTE_BATCHES):
        _route_one(aff_ref[bb], tri_ref[...], idx_ref.at[bb], gate_ref.at[bb])


def _route_one(aff, tri, idx_ref, gate_ref):
    aff_t = jnp.transpose(aff)[:N_EXPERTS]
    bits = pltpu.bitcast(aff_t, jnp.int32)
    cap = float(CAPACITY)

    t = jnp.zeros((N_EXPERTS, 1), jnp.int32)
    for shift in range(30 - SEARCH_BITS, -1, -SEARCH_BITS):
        digit = jnp.zeros((N_EXPERTS, 1), jnp.int32)
        for k in range(1, 1 << SEARCH_BITS):
            cnt = jnp.sum(jnp.where(bits >= (t | (k << shift)), 1.0, 0.0), axis=1, keepdims=True)
            digit = digit + jnp.where(cnt >= cap, 1, 0)
        t = t | (digit << shift)
    gt = bits > t
    eq = bits == t
    need = cap - jnp.sum(jnp.where(gt, 1.0, 0.0), axis=1, keepdims=True)

    n_chunks = SEQ // LANES

    def prefix(flags_f32):
        outs = []
        carry = jnp.zeros((N_EXPERTS, 1), F32)
        for c in range(n_chunks):
            f = flags_f32[:, c * LANES:(c + 1) * LANES]
            incl = jnp.dot(f.astype(BF16), tri, preferred_element_type=F32)
            outs.append(incl - f + carry)
            carry = carry + jnp.sum(f, axis=1, keepdims=True)
        return jnp.concatenate(outs, axis=1)

    eq_f = jnp.where(eq, 1.0, 0.0)
    eq_rank = prefix(eq_f)
    sel_f = jnp.where(gt, 1.0, jnp.where(eq_rank < need, eq_f, 0.0))
    pos = prefix(sel_f)
    posm = jnp.where(sel_f > 0.0, pos, -1.0)

    hi = aff_t.astype(BF16).astype(F32)
    r1 = aff_t - hi
    mid = r1.astype(BF16).astype(F32)
    lo = r1 - mid
    tok = lax.broadcasted_iota(jnp.int32, (N_EXPERTS, SEQ), 1)
    row = lax.broadcasted_iota(jnp.int32, (N_EXPERTS, SEQ), 0)
    tok_rows = jnp.where(row == 0, (tok >> 6).astype(F32),
                         jnp.where(row == 1, (tok & 63).astype(F32), 0.0))
    vals_t = jnp.concatenate([hi, mid, lo, tok_rows], axis=0).astype(BF16)

    slot = lax.broadcasted_iota(jnp.int32, (CAPACITY, SEQ), 0).astype(F32).astype(BF16)
    posm_b = posm.astype(BF16)
    one_b = jnp.ones((CAPACITY, SEQ), BF16)
    zero_b = jnp.zeros((CAPACITY, SEQ), BF16)
    for e in range(N_EXPERTS):
        onehot = jnp.where(posm_b[e:e + 1, :] == slot, one_b, zero_b)
        res = lax.dot_general(vals_t, onehot, (((1,), (1,)), ((), ())),
                              preferred_element_type=F32)
        cols = pl.ds(e * CAPACITY, CAPACITY)
        tok_idx = res[3 * N_EXPERTS:3 * N_EXPERTS + 1] * 64.0 + res[3 * N_EXPERTS + 1:
                                                                    3 * N_EXPERTS + 2]
        idx_ref[:, cols] = tok_idx.astype(jnp.int32) * PACK_ROWS
        gate_ref[:, cols] = (res[e:e + 1] + res[N_EXPERTS + e:N_EXPERTS + e + 1]
                             + res[2 * N_EXPERTS + e:2 * N_EXPERTS + e + 1])


def _route(aff):
    B = aff.shape[0]
    n = N_EXPERTS * CAPACITY
    return pl.pallas_call(
        _route_kernel,
        grid=(B // ROUTE_BATCHES,),
        in_specs=[pl.BlockSpec((ROUTE_BATCHES, SEQ, LANES), lambda b: (b, 0, 0)),
                  pl.BlockSpec((LANES, LANES), lambda b: (0, 0))],
        out_specs=[pl.BlockSpec((ROUTE_BATCHES, 1, n), lambda b: (b, 0, 0)),
                   pl.BlockSpec((ROUTE_BATCHES, 1, n), lambda b: (b, 0, 0))],
        out_shape=[jax.ShapeDtypeStruct((B, 1, n), jnp.int32),
                   jax.ShapeDtypeStruct((B, 1, n), F32)],
        compiler_params=_params(("parallel",), VMEM_LIMIT),
        name="route",
    )(aff, jnp.asarray(_tri_incl()).astype(BF16))


PAIR = 2


def _moe_kernel(idx_ref, h2_ref, wg_ref, wu_ref, wd_ref, y_ref, xin0_ref, xin1_ref):
    e = pl.program_id(1)
    last = N_EXPERTS - 1
    n = N_EXPERTS * CAPACITY
    rows = PAIR * CAPACITY

    def gather_rows(ex, dst_ref):
        for bb in range(PAIR):
            base = bb * n + ex * CAPACITY
            for p in range(CAPACITY):
                off = idx_ref[0, base + p]
                tile = h2_ref[bb, pl.ds(pl.multiple_of((off >> 3) << 3, SUBLANES), SUBLANES), :]
                tile = pltpu.roll(tile, off & PACK_ROWS, axis=0)
                dst_ref[pl.ds((bb * CAPACITY + p) * PACK_ROWS, PACK_ROWS), :] = tile[:PACK_ROWS]

    def expert(xin_ref):
        words = [xin_ref[pl.ds(j, rows, stride=PACK_ROWS), :] for j in range(PACK_ROWS)]
        xin = jnp.concatenate(
            [pltpu.bitcast(w & jnp.uint32(0xFFFF0000), F32).astype(BF16) for w in words]
            + [pltpu.bitcast(w << 16, F32).astype(BF16) for w in words], axis=1)
        a = jnp.dot(xin, wg_ref[...], preferred_element_type=F32)
        u = jnp.dot(xin, wu_ref[...], preferred_element_type=F32)
        hmid = (a * (1.0 / (1.0 + jnp.exp(-a))) * u).astype(BF16)
        y = jnp.dot(hmid, wd_ref[...], preferred_element_type=F32).astype(BF16)
        top = pltpu.bitcast(y[:, :D_MODEL // 2].astype(F32), jnp.uint32)
        bot = pltpu.bitcast(y[:, D_MODEL // 2:].astype(F32), jnp.uint32)
        words = top | (bot >> 16)
        for bb in range(PAIR):
            for j in range(PACK_ROWS):
                y_ref[bb, pl.ds(j, CAPACITY, stride=PACK_ROWS), :] = (
                    words[bb * CAPACITY:(bb + 1) * CAPACITY, j * LANES:(j + 1) * LANES])

    @pl.when(e == 0)
    def _():
        gather_rows(0, xin0_ref)

    def step(xin_cur, xin_nxt):
        gather_rows(jnp.minimum(e + 1, last), xin_nxt)
        expert(xin_cur)

    @pl.when(e % 2 == 0)
    def _():
        step(xin0_ref, xin1_ref)

    @pl.when(e % 2 == 1)
    def _():
        step(xin1_ref, xin0_ref)


def _moe(idx, h2, wg, wu, wd):
    B = h2.shape[0]
    n = N_EXPERTS * CAPACITY
    rows = SEQ * PACK_ROWS
    pairs = B // PAIR
    wspec = lambda r, c: pl.BlockSpec((None, r, c), lambda b, e: (e, 0, 0))
    stage = pltpu.VMEM((PAIR * CAPACITY * PACK_ROWS, LANES), jnp.uint32)
    out = pl.pallas_call(
        _moe_kernel,
        grid=(pairs, N_EXPERTS),
        in_specs=[pl.BlockSpec((None, 1, PAIR * n), lambda b, e: (b, 0, 0),
                               memory_space=pltpu.SMEM),
                  pl.BlockSpec((None, PAIR, rows, LANES), lambda b, e: (b, 0, 0, 0)),
                  wspec(D_MODEL, D_EXPERT), wspec(D_MODEL, D_EXPERT), wspec(D_EXPERT, D_MODEL)],
        out_specs=pl.BlockSpec((None, PAIR, CAPACITY * PACK_ROWS, LANES),
                               lambda b, e: (b, 0, e, 0)),
        out_shape=jax.ShapeDtypeStruct((pairs, PAIR, n * PACK_ROWS, LANES), jnp.uint32),
        scratch_shapes=[stage, stage],
        compiler_params=_params(("parallel", "arbitrary"), VMEM_LIMIT),
        name="moe",
    )(idx.reshape(pairs, 1, PAIR * n), h2.reshape(pairs, PAIR, rows, LANES), wg, wu, wd)
    return out.reshape(B, n * PACK_ROWS, LANES)


COMBINE_EXPERTS = 4
SCATTER_UNROLL = 8
COMBINE_ROWS = 256


def _combine_kernel(idx_ref, gate_ref, y_ref, x1_ref, mod_ref, o_ref, acc_ref):
    j = pl.program_id(1)
    slots = COMBINE_EXPERTS * CAPACITY
    base = j * slots

    @pl.when(j == 0)
    def _():
        acc_ref[...] = jnp.zeros_like(acc_ref)

    upper = lax.broadcasted_iota(jnp.int32, (SUBLANES, LANES), 0) < PACK_ROWS
    for g in range(slots // SCATTER_UNROLL):
        new = []
        for u in range(0, SCATTER_UNROLL, 2):
            r = g * SCATTER_UNROLL + u
            words = y_ref[pl.ds(r * PACK_ROWS, SUBLANES), :]
            hi = pltpu.bitcast(words & jnp.uint32(0xFFFF0000), F32)
            lo = pltpu.bitcast(words << 16, F32)
            slabs = (jnp.where(upper, hi, pltpu.roll(lo, PACK_ROWS, axis=0)),
                     jnp.where(upper, pltpu.roll(hi, PACK_ROWS, axis=0), lo))
            for k in range(2):
                dst = pl.multiple_of(idx_ref[0, base + r + k] * (ROW_SLAB // PACK_ROWS), ROW_SLAB)
                new.append((dst, acc_ref[pl.ds(dst, ROW_SLAB), :]
                            + slabs[k] * gate_ref[0, base + r + k]))
        for dst, val in new:
            acc_ref[pl.ds(dst, ROW_SLAB), :] = val

    @pl.when(j == pl.num_programs(1) - 1)
    def _():
        for rb in range(SEQ // COMBINE_ROWS):
            rows = pl.ds(rb * COMBINE_ROWS, COMBINE_ROWS)
            for c in range(ROW_SLAB):
                cols = slice(c * LANES, (c + 1) * LANES)
                chunk = acc_ref[pl.ds(rb * COMBINE_ROWS * ROW_SLAB + c, COMBINE_ROWS,
                                      stride=ROW_SLAB), :]
                o_ref[rows, cols] = x1_ref[rows, cols] + mod_ref[5:6, cols] * chunk


def _combine(idx, gate, y, x1, mod):
    B = x1.shape[0]
    n = N_EXPERTS * CAPACITY
    tok = pl.BlockSpec((None, SEQ, D_MODEL), lambda b, j: (b, 0, 0))
    smem = pl.BlockSpec((None, 1, n), lambda b, j: (b, 0, 0), memory_space=pltpu.SMEM)
    return pl.pallas_call(
        _combine_kernel,
        grid=(B, N_EXPERTS // COMBINE_EXPERTS),
        in_specs=[smem, smem,
                  pl.BlockSpec((None, COMBINE_EXPERTS * CAPACITY * PACK_ROWS, LANES),
                               lambda b, j: (b, j, 0)),
                  tok,
                  pl.BlockSpec((None, N_ADA, D_MODEL), lambda b, j: (b, 0, 0))],
        out_specs=tok,
        out_shape=jax.ShapeDtypeStruct((B, SEQ, D_MODEL), F32),
        scratch_shapes=[pltpu.VMEM((SEQ * ROW_SLAB, LANES), F32)],
        compiler_params=_params(("parallel", "arbitrary"), VMEM_LIMIT),
        name="combine",
    )(idx, gate, y, x1, mod)


def _head_perm():
    perm = []
    for r in range(Q_PER_KV):
        for kv in range(N_KV_HEADS):
            h = kv * Q_PER_KV + r
            perm.extend(range(h * HEAD_DIM, (h + 1) * HEAD_DIM))
    return np.asarray(perm, dtype=np.int32)


def kernel(x, c, rel_bias, w_ada, b_ada, norm_mix_g, norm_ffn_g, w_in, w_fourier, b_fourier,
           q_norm_g, k_norm_g, sink, w_out, w_router, w_gate, w_up, w_down):
    B = x.shape[0]
    perm = _head_perm()
    l = 0
    mod = _ada(c, w_ada[l], b_ada[l]).reshape(B, N_ADA, D_MODEL)
    pq = _fold(w_fourier[l])
    bias = _bias_table(rel_bias)

    wi = w_in[l]
    q_cols = wi[:, FOURIER_WIDTH:FOURIER_WIDTH + ATTN_WIDTH][:, perm]
    win = jnp.concatenate([wi[:, :FOURIER_WIDTH], q_cols, wi[:, FOURIER_WIDTH + ATTN_WIDTH:]],
                          axis=1).astype(BF16)
    gq = (jnp.tile(q_norm_g[l], N_Q_HEADS) * (HEAD_DIM ** -0.5 * LOG2E)).reshape(1, ATTN_WIDTH)
    gk = jnp.tile(k_norm_g[l], N_KV_HEADS).reshape(1, KV_WIDTH)
    a, b, q, k, v = _inproj(x, mod, norm_mix_g[l].reshape(1, D_MODEL), win, pq, gq, gk)

    yf = _fourier(a, b, b_fourier[l].reshape(1, FOURIER_WIDTH))
    ya = _attn(sink[l], q, k, v, bias)

    wo = w_out[l]
    wf = wo[:FOURIER_WIDTH].astype(BF16)
    wa = wo[FOURIER_WIDTH:][perm].astype(BF16)
    w_hi = w_router[l].astype(BF16)
    w_lo = (w_router[l] - w_hi.astype(F32)).astype(BF16)
    wrh = jnp.pad(jnp.concatenate([w_hi, w_lo], axis=1), ((0, 0), (0, LANES - 2 * N_EXPERTS)))
    x1, h2, aff = _outproj(yf, ya, x, mod, norm_ffn_g[l].reshape(1, D_MODEL), wf, wa, wrh)

    idx, gate = _route(aff)
    n = N_EXPERTS * CAPACITY
    y = _moe(idx, h2,
             w_gate[l].astype(BF16), w_up[l].astype(BF16), w_down[l].astype(BF16))
    return _combine(idx, gate, y, x1, mod)
```

```python
import functools
import math

import numpy as np
import jax
import jax.numpy as jnp
from jax import lax
from jax.experimental import pallas as pl
from jax.experimental.pallas import tpu as pltpu

D_MODEL = 1024
SEQ = 2048
HEAD_DIM = 64
FOURIER_WIDTH = 512
ATTN_WIDTH = 512
N_GROUPS = 8
N_Q_HEADS = 8
Q_PER_KV = 4
N_KV_HEADS = 2
KV_WIDTH = 128
IN_PROJ_WIDTH = 1280
WINDOW = 128
BLOCK = 128
SPAN = BLOCK + 2 * WINDOW
N_BLOCKS = SEQ // BLOCK
N_BUCKETS = 32
MAX_DISTANCE = 128
N_EXPERTS = 16
CAPACITY = 2 * SEQ // N_EXPERTS
D_EXPERT = 1024
N_ADA = 6
EPS = 1e-6

LANES = 128
SUBLANES = 8
ROW_SLAB = D_MODEL // LANES
PACK_ROWS = ROW_SLAB // 2
VMEM_LIMIT = 56 * 1024 * 1024

F32 = jnp.float32
BF16 = jnp.bfloat16
NEG_INF = float("-inf")
LOG2E = math.log2(math.e)


def _params(sem, vmem=None):
    return pltpu.CompilerParams(dimension_semantics=sem, vmem_limit_bytes=vmem)


@functools.lru_cache(maxsize=None)
def _chan_dft():
    c = np.arange(HEAD_DIM, dtype=np.int64)
    ph = (c[:, None] * c[None, :]) % HEAD_DIM
    ang = 2.0 * np.pi * ph.astype(np.float64) / HEAD_DIM
    sc = 1.0 / math.sqrt(HEAD_DIM)
    eye = np.eye(N_GROUPS)
    cbd = np.kron(eye, np.cos(ang) * sc)
    sbd = np.kron(eye, np.sin(ang) * sc)
    return cbd.astype(np.float32), sbd.astype(np.float32)


@functools.lru_cache(maxsize=None)
def _bucket_table():
    rel = np.arange(SPAN)[None, :] - WINDOW - np.arange(BLOCK)[:, None]
    half = N_BUCKETS // 2
    max_exact = half // 2
    n = np.abs(rel)
    nf = np.maximum(n, 1).astype(np.float64)
    large = max_exact + (np.log(nf / max_exact) / math.log(MAX_DISTANCE / max_exact)
                         * (half - max_exact)).astype(np.int64)
    sq = np.maximum(n.astype(np.int64) ** 2 // (max_exact * max_exact), 1)
    large_int = max_exact + np.floor(np.log2(sq.astype(np.float64)) + 1e-9).astype(np.int64)
    assert np.array_equal(np.where(n >= max_exact, large, 0), np.where(n >= max_exact, large_int, 0))
    large = np.minimum(large, half - 1)
    bucket = np.where(rel > 0, half, 0) + np.where(n < max_exact, n, large)
    return bucket.astype(np.int32)


@functools.lru_cache(maxsize=None)
def _group_ones(width):
    return np.kron(np.eye(width // HEAD_DIM), np.ones((HEAD_DIM, HEAD_DIM))).astype(np.float32)


@functools.lru_cache(maxsize=None)
def _tri_incl():
    i = np.arange(LANES)
    return (i[:, None] <= i[None, :]).astype(np.float32)


def _ada_kernel(c_ref, w_ref, b_ref, o_ref):
    c = c_ref[...]
    ca = c * (1.0 / (1.0 + jnp.exp(-c)))
    o_ref[...] = jnp.dot(ca, w_ref[...], precision=lax.Precision.HIGHEST,
                         preferred_element_type=F32) + b_ref[...]


def _ada(c, w_ada, b_ada):
    B = c.shape[0]
    n = N_ADA * D_MODEL
    tn = D_MODEL
    return pl.pallas_call(
        _ada_kernel,
        grid=(n // tn,),
        in_specs=[pl.BlockSpec((B, D_MODEL), lambda j: (0, 0)),
                  pl.BlockSpec((D_MODEL, tn), lambda j: (0, j)),
                  pl.BlockSpec((1, tn), lambda j: (0, j))],
        out_specs=pl.BlockSpec((B, tn), lambda j: (0, j)),
        out_shape=jax.ShapeDtypeStruct((B, n), F32),
        compiler_params=_params(("arbitrary",)),
        name="ada",
    )(c, w_ada, b_ada.reshape(1, n))


def _fold_kernel(cbd_ref, sbd_ref, w_ref, o_ref):
    w = w_ref[...]
    o_ref[:, :FOURIER_WIDTH] = jnp.dot(cbd_ref[...], w, precision=lax.Precision.HIGHEST,
                                       preferred_element_type=F32).astype(BF16)
    o_ref[:, FOURIER_WIDTH:] = jnp.dot(sbd_ref[...], w, precision=lax.Precision.HIGHEST,
                                       preferred_element_type=F32).astype(BF16)


def _fold(w_fourier):
    wbd = (jnp.eye(N_GROUPS, dtype=F32)[:, None, :, None] * w_fourier[:, :, None, :]
           ).reshape(FOURIER_WIDTH, FOURIER_WIDTH)
    cbd, sbd = _chan_dft()
    return pl.pallas_call(
        _fold_kernel,
        out_shape=jax.ShapeDtypeStruct((FOURIER_WIDTH, 2 * FOURIER_WIDTH), BF16),
        name="fold",
    )(jnp.asarray(cbd), jnp.asarray(sbd), wbd)


def _bias_kernel(rb_ref, bucket_ref, o_ref):
    h = pl.program_id(0)
    bk = bucket_ref[...]
    acc = jnp.zeros((BLOCK, SPAN), F32)
    for b in range(N_BUCKETS):
        acc = jnp.where(bk == b, rb_ref[b, h], acc)
    j = lax.broadcasted_iota(jnp.int32, (BLOCK, SPAN), 1)
    q = lax.broadcasted_iota(jnp.int32, (BLOCK, SPAN), 0)
    band = jnp.abs(j - WINDOW - q) <= WINDOW
    base = jnp.where(band, acc * LOG2E, NEG_INF)
    o_ref[0] = jnp.where(j >= WINDOW, base, NEG_INF)
    o_ref[1] = base
    o_ref[2] = jnp.where(j < WINDOW + BLOCK, base, NEG_INF)


def _bias_table(rel_bias):
    return pl.pallas_call(
        _bias_kernel,
        grid=(N_Q_HEADS,),
        in_specs=[pl.BlockSpec(memory_space=pltpu.SMEM),
                  pl.BlockSpec((BLOCK, SPAN), lambda h: (0, 0))],
        out_specs=pl.BlockSpec((3, BLOCK, SPAN), lambda h: (0, h % Q_PER_KV, h // Q_PER_KV)),
        out_shape=jax.ShapeDtypeStruct((3, Q_PER_KV * BLOCK, N_KV_HEADS * SPAN), F32),
        compiler_params=_params(("arbitrary",)),
        name="bias",
    )(rel_bias, jnp.asarray(_bucket_table()))


IN_ROWS = 256


def _inproj_kernel(x_ref, mod_ref, g_ref, win_ref, pq_ref, bdq_ref, bdk_ref, gq_ref, gk_ref,
                   wg_ref, wu_ref, wd_ref,
                   a_ref, b_ref, q_ref, k_ref, v_ref, wg_out, wu_out, wd_out):
    wg_out[...] = wg_ref[...].astype(BF16)
    wu_out[...] = wu_ref[...].astype(BF16)
    wd_out[...] = wd_ref[...].astype(BF16)
    gain = g_ref[...] * (1.0 + mod_ref[1:2, :])
    shift = mod_ref[0:1, :]
    q0 = FOURIER_WIDTH
    k0 = q0 + ATTN_WIDTH
    v0 = k0 + KV_WIDTH
    for c in range(x_ref.shape[0] // IN_ROWS):
        rows = pl.ds(c * IN_ROWS, IN_ROWS)
        x = x_ref[rows, :]
        ms = jnp.mean(x * x, axis=-1, keepdims=True)
        h = x * lax.rsqrt(ms + EPS) * gain + shift
        proj = jnp.dot(h.astype(BF16), win_ref[...], preferred_element_type=F32)
        uf = proj[:, :FOURIER_WIDTH].astype(BF16)
        ab = jnp.dot(uf, pq_ref[...], preferred_element_type=F32)
        a_ref[rows, :] = ab[:, :FOURIER_WIDTH].astype(BF16)
        b_ref[rows, :] = ab[:, FOURIER_WIDTH:].astype(BF16)
        q = proj[:, q0:k0]
        ssq = jnp.dot((q * q).astype(BF16), bdq_ref[...], preferred_element_type=F32)
        q_ref[rows, :] = (q * lax.rsqrt(ssq * (1.0 / HEAD_DIM) + EPS) * gq_ref[...]).astype(BF16)
        k = proj[:, k0:v0]
        ssk = jnp.dot((k * k).astype(BF16), bdk_ref[...], preferred_element_type=F32)
        k_ref[rows, :] = (k * lax.rsqrt(ssk * (1.0 / HEAD_DIM) + EPS) * gk_ref[...]).astype(BF16)
        v_ref[rows, :] = proj[:, v0:].astype(BF16)


def _inproj(x, mod, g, win, pq, gq, gk, w_gate, w_up, w_down, tm=512):
    B = x.shape[0]
    steps_per_batch = SEQ // tm
    w_rows = N_EXPERTS * D_MODEL
    w_blk = w_rows // (B * steps_per_batch)
    const = lambda shape: pl.BlockSpec(shape, lambda b, i: (0,) * len(shape))
    tok = lambda w: pl.BlockSpec((None, tm, w), lambda b, i: (b, i, 0))
    wsl = lambda c: pl.BlockSpec((w_blk, c), lambda b, i: (b * steps_per_batch + i, 0))
    sds = lambda w: jax.ShapeDtypeStruct((B, SEQ, w), BF16)
    wsds = lambda c: jax.ShapeDtypeStruct((w_rows, c), BF16)
    a, b, q, k, v, wg, wu, wd = pl.pallas_call(
        _inproj_kernel,
        grid=(B, steps_per_batch),
        in_specs=[tok(D_MODEL),
                  pl.BlockSpec((None, N_ADA, D_MODEL), lambda b, i: (b, 0, 0)),
                  const((1, D_MODEL)),
                  const((D_MODEL, IN_PROJ_WIDTH)),
                  const((FOURIER_WIDTH, 2 * FOURIER_WIDTH)),
                  const((ATTN_WIDTH, ATTN_WIDTH)),
                  const((KV_WIDTH, KV_WIDTH)),
                  const((1, ATTN_WIDTH)),
                  const((1, KV_WIDTH)),
                  wsl(D_EXPERT), wsl(D_EXPERT), wsl(D_MODEL)],
        out_specs=[tok(FOURIER_WIDTH), tok(FOURIER_WIDTH), tok(ATTN_WIDTH), tok(KV_WIDTH),
                   tok(KV_WIDTH), wsl(D_EXPERT), wsl(D_EXPERT), wsl(D_MODEL)],
        out_shape=[sds(FOURIER_WIDTH), sds(FOURIER_WIDTH), sds(ATTN_WIDTH), sds(KV_WIDTH),
                   sds(KV_WIDTH), wsds(D_EXPERT), wsds(D_EXPERT), wsds(D_MODEL)],
        compiler_params=_params(("parallel", "parallel"), VMEM_LIMIT),
        name="inproj",
    )(x, mod, g, win, pq, jnp.asarray(_group_ones(ATTN_WIDTH)).astype(BF16),
      jnp.asarray(_group_ones(KV_WIDTH)).astype(BF16), gq, gk,
      w_gate.reshape(w_rows, D_EXPERT), w_up.reshape(w_rows, D_EXPERT),
      w_down.reshape(N_EXPERTS * D_EXPERT, D_MODEL))
    experts = (wg.reshape(N_EXPERTS, D_MODEL, D_EXPERT), wu.reshape(N_EXPERTS, D_MODEL, D_EXPERT),
               wd.reshape(N_EXPERTS, D_EXPERT, D_MODEL))
    return a, b, q, k, v, experts


HALF = SEQ // 2
REV = 128
REV_BLOCKS = HALF // REV
FOURIER_ROWS = 512


def _fourier_kernel(cm_ref, sm_ref, psh_ref, alt_ref, altrow_ref, a_ref, b_ref, bf_ref, o_ref):
    psh = psh_ref[...]

    def reversed_block(win_lo, k, src):
        if k == 0:
            return jnp.dot(psh[:, :REV], src(win_lo, REV), preferred_element_type=F32)
        return jnp.dot(psh, src(win_lo, 2 * REV), preferred_element_type=F32)

    def folded(ref, sign):
        blocks = []
        for k in range(REV_BLOCKS):
            lo = SEQ - REV * (k + 1)
            rev = reversed_block(lo, k, lambda s, n: ref[pl.ds(s, n), :])
            blocks.append((ref[pl.ds(k * REV, REV), :].astype(F32) + sign * rev).astype(BF16))
        return jnp.concatenate(blocks, axis=0)

    a_even = folded(a_ref, 1.0)
    b_odd = folded(b_ref, -1.0)
    a_mid = a_ref[pl.ds(HALF, 1), :].astype(F32)
    bias = bf_ref[...]
    z_blocks = []
    for i in range(HALF // FOURIER_ROWS):
        rows = pl.ds(i * FOURIER_ROWS, FOURIER_ROWS)
        yc = jnp.dot(cm_ref[rows, :], a_even, preferred_element_type=F32)
        yc = yc + alt_ref[rows, :] * a_mid + bias
        ys = jnp.dot(sm_ref[rows, :], b_odd, preferred_element_type=F32)
        o_ref[rows, :] = (yc - ys).astype(BF16)
        z_blocks.append((yc + ys).astype(BF16))
    z = jnp.concatenate(z_blocks, axis=0)
    for k in range(REV_BLOCKS):
        lo = HALF - REV * (k + 1)
        top = reversed_block(lo, k, lambda s, n: z[s:s + n])
        o_ref[pl.ds(HALF + k * REV, REV), :] = top.astype(BF16)
    y_mid = jnp.dot(altrow_ref[...], a_ref[...], preferred_element_type=F32)[0:1, :] + bias
    o_ref[pl.ds(HALF, 1), :] = y_mid.astype(BF16)


@functools.lru_cache(maxsize=None)
def _fold_tables():
    s = np.arange(HALF, dtype=np.int64)
    ph = (s[:, None] * s[None, :]) % SEQ
    ang = 2.0 * np.pi * ph.astype(np.float64) / SEQ
    sc = 1.0 / math.sqrt(SEQ)
    cm = (np.cos(ang) * sc).astype(np.float32)
    sm = (np.sin(ang) * sc).astype(np.float32)
    psh = np.zeros((REV, 2 * REV), np.float32)
    psh[np.arange(REV), REV - np.arange(REV)] = 1.0
    alt = (np.where(s % 2 == 0, 1.0, -1.0) * sc).astype(np.float32).reshape(HALF, 1)
    t = np.arange(SEQ)
    altrow = np.zeros((SUBLANES, SEQ), np.float32)
    altrow[0] = np.where(t % 2 == 0, 1.0, -1.0) * sc
    return cm, sm, psh, alt, altrow


def _fourier(a, b, bf):
    B = a.shape[0]
    cm, sm, psh, alt, altrow = _fold_tables()
    tok = pl.BlockSpec((None, SEQ, FOURIER_WIDTH), lambda i: (i, 0, 0))
    const = lambda shape: pl.BlockSpec(shape, lambda i: (0,) * len(shape))
    return pl.pallas_call(
        _fourier_kernel,
        grid=(B,),
        in_specs=[const((HALF, HALF)), const((HALF, HALF)), const((REV, 2 * REV)),
                  const((HALF, 1)), const((SUBLANES, SEQ)),
                  tok, tok, const((1, FOURIER_WIDTH))],
        out_specs=tok,
        out_shape=jax.ShapeDtypeStruct((B, SEQ, FOURIER_WIDTH), BF16),
        compiler_params=_params(("parallel",), VMEM_LIMIT),
        name="fourier",
    )(jnp.asarray(cm).astype(BF16), jnp.asarray(sm).astype(BF16), jnp.asarray(psh).astype(BF16),
      jnp.asarray(alt), jnp.asarray(altrow).astype(BF16), a, b, bf)


ATT_SUB = 4
ATT_ROWS = ATT_SUB * BLOCK
ATT_STEPS = N_BLOCKS // ATT_SUB


def _attn_kernel(sink_ref, q_ref, kl_ref, km_ref, kr_ref, vl_ref, vm_ref, vr_ref, bias_ref,
                 o_ref):
    i = pl.program_id(1)
    keys = jnp.concatenate([kl_ref[...], km_ref[...], kr_ref[...]], axis=0)
    vals = jnp.concatenate([vl_ref[...], vm_ref[...], vr_ref[...]], axis=0)
    first_k = lax.broadcasted_iota(jnp.int32, keys.shape, 1) < HEAD_DIM
    first_q = lax.broadcasted_iota(jnp.int32, (Q_PER_KV * BLOCK, LANES), 1) < HEAD_DIM
    row_head = lax.broadcasted_iota(jnp.int32, (Q_PER_KV * BLOCK, 1), 0) // BLOCK
    zero = jnp.zeros_like(keys)
    ones_lo = jnp.where(first_k, 1.0, 0.0).astype(BF16)
    ones_hi = jnp.where(first_k, 0.0, 1.0).astype(BF16)
    keys_kv = [jnp.where(first_k, keys, zero), jnp.where(first_k, zero, keys)]
    vals_kv = [jnp.concatenate([jnp.where(first_k, vals, zero), ones_lo], axis=1),
               jnp.concatenate([jnp.where(first_k, zero, vals), ones_hi], axis=1)]
    sinks = []
    for kv in range(N_KV_HEADS):
        sink = jnp.zeros((Q_PER_KV * BLOCK, 1), F32)
        for r in range(Q_PER_KV):
            sink = jnp.where(row_head == r, sink_ref[kv * Q_PER_KV + r] * LOG2E, sink)
        sinks.append(sink)
    for j in range(ATT_SUB):
        variant = 1
        if j == 0:
            variant = jnp.where(i == 0, 0, variant)
        if j == ATT_SUB - 1:
            variant = jnp.where(i == ATT_STEPS - 1, 2, variant)
        qrows = pl.ds(j * BLOCK, BLOCK)
        krows = slice(j * BLOCK, j * BLOCK + SPAN)
        qs = jnp.concatenate([q_ref[qrows, r * LANES:(r + 1) * LANES] for r in range(Q_PER_KV)],
                             axis=0)
        keys2 = jnp.concatenate([keys_kv[0][krows], keys_kv[1][krows]], axis=0)
        vals2 = jnp.concatenate([vals_kv[0][krows], vals_kv[1][krows]], axis=0)
        logits = lax.dot_general(qs, keys2, (((1,), (1,)), ((), ())),
                                 preferred_element_type=F32)
        logits = logits + bias_ref[variant]
        ms = [jnp.maximum(jnp.max(logits[:, kv * SPAN:(kv + 1) * SPAN], axis=-1, keepdims=True),
                          sinks[kv]) for kv in range(N_KV_HEADS)]
        p = jnp.exp2(jnp.concatenate([logits[:, kv * SPAN:(kv + 1) * SPAN] - ms[kv]
                                      for kv in range(N_KV_HEADS)], axis=1).astype(BF16))
        pv = jnp.dot(p, vals2, preferred_element_type=F32)
        denom = pv[:, LANES:] + jnp.exp2(jnp.where(first_q, sinks[0] - ms[0], sinks[1] - ms[1]))
        out = (pv[:, :LANES] / denom).astype(BF16)
        for r in range(Q_PER_KV):
            o_ref[qrows, r * LANES:(r + 1) * LANES] = out[r * BLOCK:(r + 1) * BLOCK]


def _attn(sink, q, k, v, bias):
    B = q.shape[0]
    edge = lambda f: pl.BlockSpec((None, BLOCK, KV_WIDTH), lambda b, i: (b, f(i), 0))
    left = lambda i: jnp.maximum(i * ATT_SUB - 1, 0)
    right = lambda i: jnp.minimum((i + 1) * ATT_SUB, N_BLOCKS - 1)
    mid = pl.BlockSpec((None, ATT_ROWS, KV_WIDTH), lambda b, i: (b, i, 0))
    qspec = pl.BlockSpec((None, ATT_ROWS, ATTN_WIDTH), lambda b, i: (b, i, 0))
    return pl.pallas_call(
        _attn_kernel,
        grid=(B, ATT_STEPS),
        in_specs=[pl.BlockSpec(memory_space=pltpu.SMEM), qspec,
                  edge(left), mid, edge(right),
                  edge(left), mid, edge(right),
                  pl.BlockSpec((3, Q_PER_KV * BLOCK, N_KV_HEADS * SPAN), lambda b, i: (0, 0, 0))],
        out_specs=qspec,
        out_shape=jax.ShapeDtypeStruct((B, SEQ, ATTN_WIDTH), BF16),
        compiler_params=_params(("parallel", "parallel"), VMEM_LIMIT),
        name="attn",
    )(sink, q, k, k, k, v, v, v, bias)


OUT_ROWS = 256


def _outproj_kernel(yf_ref, ya_ref, x_ref, mod_ref, g_ref, wf_ref, wa_ref, wrh_ref,
                    x1_ref, h2_ref, aff_ref):
    tm = x_ref.shape[0]
    gain = g_ref[...] * (1.0 + mod_ref[4:5, :])
    shift = mod_ref[3:4, :]
    gate1 = mod_ref[2:3, :]
    lane = lax.broadcasted_iota(jnp.int32, (OUT_ROWS, LANES), 1)
    for c in range(tm // OUT_ROWS):
        rows = pl.ds(c * OUT_ROWS, OUT_ROWS)
        mixed = jnp.dot(yf_ref[rows, :], wf_ref[...], preferred_element_type=F32)
        mixed = mixed + jnp.dot(ya_ref[rows, :], wa_ref[...], preferred_element_type=F32)
        x1 = x_ref[rows, :] + gate1 * mixed
        x1_ref[rows, :] = x1
        ms = jnp.mean(x1 * x1, axis=-1, keepdims=True)
        h2 = x1 * lax.rsqrt(ms + EPS) * gain + shift
        hi = h2.astype(BF16)
        top = pltpu.bitcast(hi[:, :D_MODEL // 2].astype(F32), jnp.uint32)
        bot = pltpu.bitcast(hi[:, D_MODEL // 2:].astype(F32), jnp.uint32)
        words = top | (bot >> 16)
        for j in range(PACK_ROWS):
            h2_ref[pl.ds(c * OUT_ROWS * PACK_ROWS + j, OUT_ROWS, stride=PACK_ROWS), :] = (
                words[:, j * LANES:(j + 1) * LANES])
        part = jnp.dot(hi, wrh_ref[...], preferred_element_type=F32)
        logits = part + pltpu.roll(part, LANES - N_EXPERTS, axis=1)
        logits = jnp.where(lane < N_EXPERTS, logits, NEG_INF)
        m = jnp.max(logits, axis=-1, keepdims=True)
        e = jnp.exp(logits - m)
        aff_ref[rows, :] = e / jnp.sum(e, axis=-1, keepdims=True)


def _outproj(yf, ya, x, mod, g, wf, wa, wrh, tm=512):
    B = x.shape[0]
    const = lambda shape: pl.BlockSpec(shape, lambda b, i: (0,) * len(shape))
    tok = lambda w: pl.BlockSpec((None, tm, w), lambda b, i: (b, i, 0))
    return pl.pallas_call(
        _outproj_kernel,
        grid=(B, SEQ // tm),
        in_specs=[tok(FOURIER_WIDTH), tok(ATTN_WIDTH), tok(D_MODEL),
                  pl.BlockSpec((None, N_ADA, D_MODEL), lambda b, i: (b, 0, 0)),
                  const((1, D_MODEL)),
                  const((FOURIER_WIDTH, D_MODEL)), const((ATTN_WIDTH, D_MODEL)),
                  const((D_MODEL, LANES))],
        out_specs=[tok(D_MODEL),
                   pl.BlockSpec((None, tm * PACK_ROWS, LANES), lambda b, i: (b, i, 0)),
                   tok(LANES)],
        out_shape=[jax.ShapeDtypeStruct((B, SEQ, D_MODEL), F32),
                   jax.ShapeDtypeStruct((B, SEQ * PACK_ROWS, LANES), jnp.uint32),
                   jax.ShapeDtypeStruct((B, SEQ, LANES), F32)],
        compiler_params=_params(("parallel", "parallel"), VMEM_LIMIT),
        name="outproj",
    )(yf, ya, x, mod, g, wf, wa, wrh)


ROUTE_BATCHES = 2
SEARCH_BITS = 3


def _route_kernel(aff_ref, tri_ref, idx_ref, gate_ref):
    for bb in range(ROUTE_BATCHES):
        _route_one(aff_ref[bb], tri_ref[...], idx_ref.at[bb], gate_ref.at[bb])


def _route_one(aff, tri, idx_ref, gate_ref):
    aff_t = jnp.transpose(aff)[:N_EXPERTS]
    bits = pltpu.bitcast(aff_t, jnp.int32)
    cap = float(CAPACITY)

    t = jnp.zeros((N_EXPERTS, 1), jnp.int32)
    for shift in range(30 - SEARCH_BITS, -1, -SEARCH_BITS):
        digit = jnp.zeros((N_EXPERTS, 1), jnp.int32)
        for k in range(1, 1 << SEARCH_BITS):
            cnt = jnp.sum(jnp.where(bits >= (t | (k << shift)), 1.0, 0.0), axis=1, keepdims=True)
            digit = digit + jnp.where(cnt >= cap, 1, 0)
        t = t | (digit << shift)
    gt = bits > t
    eq = bits == t
    need = cap - jnp.sum(jnp.where(gt, 1.0, 0.0), axis=1, keepdims=True)

    n_chunks = SEQ // LANES

    def prefix(flags_f32):
        outs = []
        carry = jnp.zeros((N_EXPERTS, 1), F32)
        for c in range(n_chunks):
            f = flags_f32[:, c * LANES:(c + 1) * LANES]
            incl = jnp.dot(f.astype(BF16), tri, preferred_element_type=F32)
            outs.append(incl - f + carry)
            carry = carry + jnp.sum(f, axis=1, keepdims=True)
        return jnp.concatenate(outs, axis=1)

    eq_f = jnp.where(eq, 1.0, 0.0)
    eq_rank = prefix(eq_f)
    sel_f = jnp.where(gt, 1.0, jnp.where(eq_rank < need, eq_f, 0.0))
    pos = prefix(sel_f)
    posm = jnp.where(sel_f > 0.0, pos, -1.0)

    hi = aff_t.astype(BF16).astype(F32)
    r1 = aff_t - hi
    mid = r1.astype(BF16).astype(F32)
    lo = r1 - mid
    tok = lax.broadcasted_iota(jnp.int32, (N_EXPERTS, SEQ), 1)
    row = lax.broadcasted_iota(jnp.int32, (N_EXPERTS, SEQ), 0)
    tok_rows = jnp.where(row == 0, (tok >> 6).astype(F32),
                         jnp.where(row == 1, (tok & 63).astype(F32), 0.0))
    vals_t = jnp.concatenate([hi, mid, lo, tok_rows], axis=0).astype(BF16)

    slot = lax.broadcasted_iota(jnp.int32, (CAPACITY, SEQ), 0).astype(F32).astype(BF16)
    posm_b = posm.astype(BF16)
    one_b = jnp.ones((CAPACITY, SEQ), BF16)
    zero_b = jnp.zeros((CAPACITY, SEQ), BF16)
    for e in range(N_EXPERTS):
        onehot = jnp.where(posm_b[e:e + 1, :] == slot, one_b, zero_b)
        res = lax.dot_general(vals_t, onehot, (((1,), (1,)), ((), ())),
                              preferred_element_type=F32)
        cols = pl.ds(e * CAPACITY, CAPACITY)
        tok_idx = res[3 * N_EXPERTS:3 * N_EXPERTS + 1] * 64.0 + res[3 * N_EXPERTS + 1:
                                                                    3 * N_EXPERTS + 2]
        idx_ref[:, cols] = tok_idx.astype(jnp.int32) * PACK_ROWS
        gate_ref[:, cols] = (res[e:e + 1] + res[N_EXPERTS + e:N_EXPERTS + e + 1]
                             + res[2 * N_EXPERTS + e:2 * N_EXPERTS + e + 1])


def _route(aff):
    B = aff.shape[0]
    n = N_EXPERTS * CAPACITY
    return pl.pallas_call(
        _route_kernel,
        grid=(B // ROUTE_BATCHES,),
        in_specs=[pl.BlockSpec((ROUTE_BATCHES, SEQ, LANES), lambda b: (b, 0, 0)),
                  pl.BlockSpec((LANES, LANES), lambda b: (0, 0))],
        out_specs=[pl.BlockSpec((ROUTE_BATCHES, 1, n), lambda b: (b, 0, 0)),
                   pl.BlockSpec((ROUTE_BATCHES, 1, n), lambda b: (b, 0, 0))],
        out_shape=[jax.ShapeDtypeStruct((B, 1, n), jnp.int32),
                   jax.ShapeDtypeStruct((B, 1, n), F32)],
        compiler_params=_params(("parallel",), VMEM_LIMIT),
        name="route",
    )(aff, jnp.asarray(_tri_incl()).astype(BF16))


PAIR = 2


def _moe_kernel(idx_ref, h2_ref, wg_ref, wu_ref, wd_ref, y_ref, xin0_ref, xin1_ref):
    e = pl.program_id(1)
    last = N_EXPERTS - 1
    n = N_EXPERTS * CAPACITY
    rows = PAIR * CAPACITY

    def gather_rows(ex, dst_ref):
        for bb in range(PAIR):
            base = bb * n + ex * CAPACITY
            for p in range(CAPACITY):
                off = idx_ref[0, base + p]
                tile = h2_ref[bb, pl.ds(pl.multiple_of((off >> 3) << 3, SUBLANES), SUBLANES), :]
                tile = pltpu.roll(tile, off & PACK_ROWS, axis=0)
                dst_ref[pl.ds((bb * CAPACITY + p) * PACK_ROWS, PACK_ROWS), :] = tile[:PACK_ROWS]

    def expert(xin_ref):
        words = [xin_ref[pl.ds(j, rows, stride=PACK_ROWS), :] for j in range(PACK_ROWS)]
        xin = jnp.concatenate(
            [pltpu.bitcast(w & jnp.uint32(0xFFFF0000), F32).astype(BF16) for w in words]
            + [pltpu.bitcast(w << 16, F32).astype(BF16) for w in words], axis=1)
        a = jnp.dot(xin, wg_ref[...], preferred_element_type=F32)
        u = jnp.dot(xin, wu_ref[...], preferred_element_type=F32)
        hmid = (a * (1.0 / (1.0 + jnp.exp(-a))) * u).astype(BF16)
        y = jnp.dot(hmid, wd_ref[...], preferred_element_type=F32).astype(BF16)
        top = pltpu.bitcast(y[:, :D_MODEL // 2].astype(F32), jnp.uint32)
        bot = pltpu.bitcast(y[:, D_MODEL // 2:].astype(F32), jnp.uint32)
        words = top | (bot >> 16)
        for bb in range(PAIR):
            for j in range(PACK_ROWS):
                y_ref[bb, pl.ds(j, CAPACITY, stride=PACK_ROWS), :] = (
                    words[bb * CAPACITY:(bb + 1) * CAPACITY, j * LANES:(j + 1) * LANES])

    @pl.when(e == 0)
    def _():
        gather_rows(0, xin0_ref)

    def step(xin_cur, xin_nxt):
        gather_rows(jnp.minimum(e + 1, last), xin_nxt)
        expert(xin_cur)

    @pl.when(e % 2 == 0)
    def _():
        step(xin0_ref, xin1_ref)

    @pl.when(e % 2 == 1)
    def _():
        step(xin1_ref, xin0_ref)


def _moe(idx, h2, wg, wu, wd):
    B = h2.shape[0]
    n = N_EXPERTS * CAPACITY
    rows = SEQ * PACK_ROWS
    pairs = B // PAIR
    wspec = lambda r, c: pl.BlockSpec((None, r, c), lambda b, e: (e, 0, 0))
    stage = pltpu.VMEM((PAIR * CAPACITY * PACK_ROWS, LANES), jnp.uint32)
    out = pl.pallas_call(
        _moe_kernel,
        grid=(pairs, N_EXPERTS),
        in_specs=[pl.BlockSpec((None, 1, PAIR * n), lambda b, e: (b, 0, 0),
                               memory_space=pltpu.SMEM),
                  pl.BlockSpec((None, PAIR, rows, LANES), lambda b, e: (b, 0, 0, 0)),
                  wspec(D_MODEL, D_EXPERT), wspec(D_MODEL, D_EXPERT), wspec(D_EXPERT, D_MODEL)],
        out_specs=pl.BlockSpec((None, PAIR, CAPACITY * PACK_ROWS, LANES),
                               lambda b, e: (b, 0, e, 0)),
        out_shape=jax.ShapeDtypeStruct((pairs, PAIR, n * PACK_ROWS, LANES), jnp.uint32),
        scratch_shapes=[stage, stage],
        compiler_params=_params(("parallel", "arbitrary"), VMEM_LIMIT),
        name="moe",
    )(idx.reshape(pairs, 1, PAIR * n), h2.reshape(pairs, PAIR, rows, LANES), wg, wu, wd)
    return out.reshape(B, n * PACK_ROWS, LANES)


COMBINE_EXPERTS = 4
SCATTER_UNROLL = 8
COMBINE_ROWS = 256


def _combine_kernel(idx_ref, gate_ref, y_ref, x1_ref, mod_ref, o_ref, acc_ref):
    j = pl.program_id(1)
    slots = COMBINE_EXPERTS * CAPACITY
    base = j * slots

    @pl.when(j == 0)
    def _():
        acc_ref[...] = jnp.zeros_like(acc_ref)

    upper = lax.broadcasted_iota(jnp.int32, (SUBLANES, LANES), 0) < PACK_ROWS
    for g in range(slots // SCATTER_UNROLL):
        new = []
        for u in range(0, SCATTER_UNROLL, 2):
            r = g * SCATTER_UNROLL + u
            words = y_ref[pl.ds(r * PACK_ROWS, SUBLANES), :]
            hi = pltpu.bitcast(words & jnp.uint32(0xFFFF0000), F32)
            lo = pltpu.bitcast(words << 16, F32)
            slabs = (jnp.where(upper, hi, pltpu.roll(lo, PACK_ROWS, axis=0)),
                     jnp.where(upper, pltpu.roll(hi, PACK_ROWS, axis=0), lo))
            for k in range(2):
                dst = pl.multiple_of(idx_ref[0, base + r + k] * (ROW_SLAB // PACK_ROWS), ROW_SLAB)
                new.append((dst, acc_ref[pl.ds(dst, ROW_SLAB), :]
                            + slabs[k] * gate_ref[0, base + r + k]))
        for dst, val in new:
            acc_ref[pl.ds(dst, ROW_SLAB), :] = val

    @pl.when(j == pl.num_programs(1) - 1)
    def _():
        for rb in range(SEQ // COMBINE_ROWS):
            rows = pl.ds(rb * COMBINE_ROWS, COMBINE_ROWS)
            for c in range(ROW_SLAB):
                cols = slice(c * LANES, (c + 1) * LANES)
                chunk = acc_ref[pl.ds(rb * COMBINE_ROWS * ROW_SLAB + c, COMBINE_ROWS,
                                      stride=ROW_SLAB), :]
                o_ref[rows, cols] = x1_ref[rows, cols] + mod_ref[5:6, cols] * chunk


def _combine(idx, gate, y, x1, mod):
    B = x1.shape[0]
    n = N_EXPERTS * CAPACITY
    tok = pl.BlockSpec((None, SEQ, D_MODEL), lambda b, j: (b, 0, 0))
    smem = pl.BlockSpec((None, 1, n), lambda b, j: (b, 0, 0), memory_space=pltpu.SMEM)
    return pl.pallas_call(
        _combine_kernel,
        grid=(B, N_EXPERTS // COMBINE_EXPERTS),
        in_specs=[smem, smem,
                  pl.BlockSpec((None, COMBINE_EXPERTS * CAPACITY * PACK_ROWS, LANES),
                               lambda b, j: (b, j, 0)),
                  tok,
                  pl.BlockSpec((None, N_ADA, D_MODEL), lambda b, j: (b, 0, 0))],
        out_specs=tok,
        out_shape=jax.ShapeDtypeStruct((B, SEQ, D_MODEL), F32),
        scratch_shapes=[pltpu.VMEM((SEQ * ROW_SLAB, LANES), F32)],
        compiler_params=_params(("parallel", "arbitrary"), VMEM_LIMIT),
        name="combine",
    )(idx, gate, y, x1, mod)


def _head_perm():
    perm = []
    for r in range(Q_PER_KV):
        for kv in range(N_KV_HEADS):
            h = kv * Q_PER_KV + r
            perm.extend(range(h * HEAD_DIM, (h + 1) * HEAD_DIM))
    return np.asarray(perm, dtype=np.int32)


def kernel(x, c, rel_bias, w_ada, b_ada, norm_mix_g, norm_ffn_g, w_in, w_fourier, b_fourier,
           q_norm_g, k_norm_g, sink, w_out, w_router, w_gate, w_up, w_down):
    B = x.shape[0]
    perm = _head_perm()
    l = 0
    mod = _ada(c, w_ada[l], b_ada[l]).reshape(B, N_ADA, D_MODEL)
    pq = _fold(w_fourier[l])
    bias = _bias_table(rel_bias)

    wi = w_in[l]
    q_cols = wi[:, FOURIER_WIDTH:FOURIER_WIDTH + ATTN_WIDTH][:, perm]
    win = jnp.concatenate([wi[:, :FOURIER_WIDTH], q_cols, wi[:, FOURIER_WIDTH + ATTN_WIDTH:]],
                          axis=1).astype(BF16)
    gq = (jnp.tile(q_norm_g[l], N_Q_HEADS) * (HEAD_DIM ** -0.5 * LOG2E)).reshape(1, ATTN_WIDTH)
    gk = jnp.tile(k_norm_g[l], N_KV_HEADS).reshape(1, KV_WIDTH)
    a, b, q, k, v, experts = _inproj(x, mod, norm_mix_g[l].reshape(1, D_MODEL), win, pq, gq, gk,
                                     w_gate[l], w_up[l], w_down[l])

    yf = _fourier(a, b, b_fourier[l].reshape(1, FOURIER_WIDTH))
    ya = _attn(sink[l], q, k, v, bias)

    wo = w_out[l]
    wf = wo[:FOURIER_WIDTH].astype(BF16)
    wa = wo[FOURIER_WIDTH:][perm].astype(BF16)
    w_hi = w_router[l].astype(BF16)
    w_lo = (w_router[l] - w_hi.astype(F32)).astype(BF16)
    wrh = jnp.pad(jnp.concatenate([w_hi, w_lo], axis=1), ((0, 0), (0, LANES - 2 * N_EXPERTS)))
    x1, h2, aff = _outproj(yf, ya, x, mod, norm_ffn_g[l].reshape(1, D_MODEL), wf, wa, wrh)

    idx, gate = _route(aff)
    n = N_EXPERTS * CAPACITY
    y = _moe(idx, h2, *experts)
    return _combine(idx, gate, y, x1, mod)
```

```python
import functools
import math

import numpy as np
import jax
import jax.numpy as jnp
from jax import lax
from jax.experimental import pallas as pl
from jax.experimental.pallas import tpu as pltpu

D_MODEL = 1024
SEQ = 2048
HEAD_DIM = 64
FOURIER_WIDTH = 512
ATTN_WIDTH = 512
N_GROUPS = 8
N_Q_HEADS = 8
Q_PER_KV = 4
N_KV_HEADS = 2
KV_WIDTH = 128
IN_PROJ_WIDTH = 1280
WINDOW = 128
BLOCK = 128
SPAN = BLOCK + 2 * WINDOW
N_BLOCKS = SEQ // BLOCK
N_BUCKETS = 32
MAX_DISTANCE = 128
N_EXPERTS = 16
CAPACITY = 2 * SEQ // N_EXPERTS
D_EXPERT = 1024
N_ADA = 6
EPS = 1e-6

LANES = 128
SUBLANES = 8
ROW_SLAB = D_MODEL // LANES
PACK_ROWS = ROW_SLAB // 2
VMEM_LIMIT = 56 * 1024 * 1024

F32 = jnp.float32
BF16 = jnp.bfloat16
NEG_INF = float("-inf")
LOG2E = math.log2(math.e)


def _params(sem, vmem=None):
    return pltpu.CompilerParams(dimension_semantics=sem, vmem_limit_bytes=vmem)


@functools.lru_cache(maxsize=None)
def _chan_dft():
    c = np.arange(HEAD_DIM, dtype=np.int64)
    ph = (c[:, None] * c[None, :]) % HEAD_DIM
    ang = 2.0 * np.pi * ph.astype(np.float64) / HEAD_DIM
    sc = 1.0 / math.sqrt(HEAD_DIM)
    eye = np.eye(N_GROUPS)
    cbd = np.kron(eye, np.cos(ang) * sc)
    sbd = np.kron(eye, np.sin(ang) * sc)
    return cbd.astype(np.float32), sbd.astype(np.float32)


@functools.lru_cache(maxsize=None)
def _bucket_table():
    rel = np.arange(SPAN)[None, :] - WINDOW - np.arange(BLOCK)[:, None]
    half = N_BUCKETS // 2
    max_exact = half // 2
    n = np.abs(rel)
    nf = np.maximum(n, 1).astype(np.float64)
    large = max_exact + (np.log(nf / max_exact) / math.log(MAX_DISTANCE / max_exact)
                         * (half - max_exact)).astype(np.int64)
    sq = np.maximum(n.astype(np.int64) ** 2 // (max_exact * max_exact), 1)
    large_int = max_exact + np.floor(np.log2(sq.astype(np.float64)) + 1e-9).astype(np.int64)
    assert np.array_equal(np.where(n >= max_exact, large, 0), np.where(n >= max_exact, large_int, 0))
    large = np.minimum(large, half - 1)
    bucket = np.where(rel > 0, half, 0) + np.where(n < max_exact, n, large)
    return bucket.astype(np.int32)


@functools.lru_cache(maxsize=None)
def _group_ones(width):
    return np.kron(np.eye(width // HEAD_DIM), np.ones((HEAD_DIM, HEAD_DIM))).astype(np.float32)


@functools.lru_cache(maxsize=None)
def _tri_incl():
    i = np.arange(LANES)
    return (i[:, None] <= i[None, :]).astype(np.float32)


def _ada_kernel(c_ref, w_ref, b_ref, o_ref):
    c = c_ref[...]
    ca = c * (1.0 / (1.0 + jnp.exp(-c)))
    o_ref[...] = jnp.dot(ca, w_ref[...], precision=lax.Precision.HIGHEST,
                         preferred_element_type=F32) + b_ref[...]


def _ada(c, w_ada, b_ada):
    B = c.shape[0]
    n = N_ADA * D_MODEL
    tn = D_MODEL
    return pl.pallas_call(
        _ada_kernel,
        grid=(n // tn,),
        in_specs=[pl.BlockSpec((B, D_MODEL), lambda j: (0, 0)),
                  pl.BlockSpec((D_MODEL, tn), lambda j: (0, j)),
                  pl.BlockSpec((1, tn), lambda j: (0, j))],
        out_specs=pl.BlockSpec((B, tn), lambda j: (0, j)),
        out_shape=jax.ShapeDtypeStruct((B, n), F32),
        compiler_params=_params(("arbitrary",)),
        name="ada",
    )(c, w_ada, b_ada.reshape(1, n))


def _fold_kernel(cbd_ref, sbd_ref, w_ref, o_ref):
    w = w_ref[...]
    o_ref[:, :FOURIER_WIDTH] = jnp.dot(cbd_ref[...], w, precision=lax.Precision.HIGHEST,
                                       preferred_element_type=F32).astype(BF16)
    o_ref[:, FOURIER_WIDTH:] = jnp.dot(sbd_ref[...], w, precision=lax.Precision.HIGHEST,
                                       preferred_element_type=F32).astype(BF16)


def _fold(w_fourier):
    wbd = (jnp.eye(N_GROUPS, dtype=F32)[:, None, :, None] * w_fourier[:, :, None, :]
           ).reshape(FOURIER_WIDTH, FOURIER_WIDTH)
    cbd, sbd = _chan_dft()
    return pl.pallas_call(
        _fold_kernel,
        out_shape=jax.ShapeDtypeStruct((FOURIER_WIDTH, 2 * FOURIER_WIDTH), BF16),
        name="fold",
    )(jnp.asarray(cbd), jnp.asarray(sbd), wbd)


def _bias_kernel(rb_ref, bucket_ref, o_ref):
    h = pl.program_id(0)
    bk = bucket_ref[...]
    acc = jnp.zeros((BLOCK, SPAN), F32)
    for b in range(N_BUCKETS):
        acc = jnp.where(bk == b, rb_ref[b, h], acc)
    j = lax.broadcasted_iota(jnp.int32, (BLOCK, SPAN), 1)
    q = lax.broadcasted_iota(jnp.int32, (BLOCK, SPAN), 0)
    band = jnp.abs(j - WINDOW - q) <= WINDOW
    base = jnp.where(band, acc * LOG2E, NEG_INF)
    o_ref[0] = jnp.where(j >= WINDOW, base, NEG_INF)
    o_ref[1] = base
    o_ref[2] = jnp.where(j < WINDOW + BLOCK, base, NEG_INF)


def _bias_table(rel_bias):
    return pl.pallas_call(
        _bias_kernel,
        grid=(N_Q_HEADS,),
        in_specs=[pl.BlockSpec(memory_space=pltpu.SMEM),
                  pl.BlockSpec((BLOCK, SPAN), lambda h: (0, 0))],
        out_specs=pl.BlockSpec((3, BLOCK, SPAN), lambda h: (0, h % Q_PER_KV, h // Q_PER_KV)),
        out_shape=jax.ShapeDtypeStruct((3, Q_PER_KV * BLOCK, N_KV_HEADS * SPAN), F32),
        compiler_params=_params(("arbitrary",)),
        name="bias",
    )(rel_bias, jnp.asarray(_bucket_table()))


IN_ROWS = 256


def _inproj_kernel(x_ref, mod_ref, g_ref, win_ref, pq_ref, bdq_ref, bdk_ref, gq_ref, gk_ref,
                   wg_ref, wu_ref, wd_ref,
                   a_ref, b_ref, q_ref, k_ref, v_ref, wg_out, wu_out, wd_out):
    wg_out[...] = wg_ref[...].astype(BF16)
    wu_out[...] = wu_ref[...].astype(BF16)
    wd_out[...] = wd_ref[...].astype(BF16)
    gain = g_ref[...] * (1.0 + mod_ref[1:2, :])
    shift = mod_ref[0:1, :]
    q0 = FOURIER_WIDTH
    k0 = q0 + ATTN_WIDTH
    v0 = k0 + KV_WIDTH
    for c in range(x_ref.shape[0] // IN_ROWS):
        rows = pl.ds(c * IN_ROWS, IN_ROWS)
        x = x_ref[rows, :]
        ms = jnp.mean(x * x, axis=-1, keepdims=True)
        h = x * lax.rsqrt(ms + EPS) * gain + shift
        proj = jnp.dot(h.astype(BF16), win_ref[...], preferred_element_type=F32)
        uf = proj[:, :FOURIER_WIDTH].astype(BF16)
        ab = jnp.dot(uf, pq_ref[...], preferred_element_type=F32)
        a_ref[rows, :] = ab[:, :FOURIER_WIDTH].astype(BF16)
        b_ref[rows, :] = ab[:, FOURIER_WIDTH:].astype(BF16)
        q = proj[:, q0:k0]
        ssq = jnp.dot((q * q).astype(BF16), bdq_ref[...], preferred_element_type=F32)
        q_ref[rows, :] = (q * lax.rsqrt(ssq * (1.0 / HEAD_DIM) + EPS) * gq_ref[...]).astype(BF16)
        k = proj[:, k0:v0]
        ssk = jnp.dot((k * k).astype(BF16), bdk_ref[...], preferred_element_type=F32)
        k_ref[rows, :] = (k * lax.rsqrt(ssk * (1.0 / HEAD_DIM) + EPS) * gk_ref[...]).astype(BF16)
        v_ref[rows, :] = proj[:, v0:].astype(BF16)


def _inproj(x, mod, g, win, pq, gq, gk, w_gate, w_up, w_down, tm=1024):
    B = x.shape[0]
    steps_per_batch = SEQ // tm
    w_rows = N_EXPERTS * D_MODEL
    w_blk = w_rows // (B * steps_per_batch)
    const = lambda shape: pl.BlockSpec(shape, lambda b, i: (0,) * len(shape))
    tok = lambda w: pl.BlockSpec((None, tm, w), lambda b, i: (b, i, 0))
    wsl = lambda c: pl.BlockSpec((w_blk, c), lambda b, i: (b * steps_per_batch + i, 0))
    sds = lambda w: jax.ShapeDtypeStruct((B, SEQ, w), BF16)
    wsds = lambda c: jax.ShapeDtypeStruct((w_rows, c), BF16)
    a, b, q, k, v, wg, wu, wd = pl.pallas_call(
        _inproj_kernel,
        grid=(B, steps_per_batch),
        in_specs=[tok(D_MODEL),
                  pl.BlockSpec((None, N_ADA, D_MODEL), lambda b, i: (b, 0, 0)),
                  const((1, D_MODEL)),
                  const((D_MODEL, IN_PROJ_WIDTH)),
                  const((FOURIER_WIDTH, 2 * FOURIER_WIDTH)),
                  const((ATTN_WIDTH, ATTN_WIDTH)),
                  const((KV_WIDTH, KV_WIDTH)),
                  const((1, ATTN_WIDTH)),
                  const((1, KV_WIDTH)),
                  wsl(D_EXPERT), wsl(D_EXPERT), wsl(D_MODEL)],
        out_specs=[tok(FOURIER_WIDTH), tok(FOURIER_WIDTH), tok(ATTN_WIDTH), tok(KV_WIDTH),
                   tok(KV_WIDTH), wsl(D_EXPERT), wsl(D_EXPERT), wsl(D_MODEL)],
        out_shape=[sds(FOURIER_WIDTH), sds(FOURIER_WIDTH), sds(ATTN_WIDTH), sds(KV_WIDTH),
                   sds(KV_WIDTH), wsds(D_EXPERT), wsds(D_EXPERT), wsds(D_MODEL)],
        compiler_params=_params(("parallel", "parallel"), VMEM_LIMIT),
        name="inproj",
    )(x, mod, g, win, pq, jnp.asarray(_group_ones(ATTN_WIDTH)).astype(BF16),
      jnp.asarray(_group_ones(KV_WIDTH)).astype(BF16), gq, gk,
      w_gate.reshape(w_rows, D_EXPERT), w_up.reshape(w_rows, D_EXPERT),
      w_down.reshape(N_EXPERTS * D_EXPERT, D_MODEL))
    experts = (wg.reshape(N_EXPERTS, D_MODEL, D_EXPERT), wu.reshape(N_EXPERTS, D_MODEL, D_EXPERT),
               wd.reshape(N_EXPERTS, D_EXPERT, D_MODEL))
    return a, b, q, k, v, experts


HALF = SEQ // 2
REV = 128
REV_BLOCKS = HALF // REV
FOURIER_ROWS = 512


def _fourier_kernel(cm_ref, sm_ref, psh_ref, alt_ref, altrow_ref, a_ref, b_ref, bf_ref, o_ref):
    psh = psh_ref[...]

    def reversed_block(win_lo, k, src):
        if k == 0:
            return jnp.dot(psh[:, :REV], src(win_lo, REV), preferred_element_type=F32)
        return jnp.dot(psh, src(win_lo, 2 * REV), preferred_element_type=F32)

    def folded(ref, sign):
        blocks = []
        for k in range(REV_BLOCKS):
            lo = SEQ - REV * (k + 1)
            rev = reversed_block(lo, k, lambda s, n: ref[pl.ds(s, n), :])
            blocks.append((ref[pl.ds(k * REV, REV), :].astype(F32) + sign * rev).astype(BF16))
        return jnp.concatenate(blocks, axis=0)

    a_even = folded(a_ref, 1.0)
    b_odd = folded(b_ref, -1.0)
    a_mid = a_ref[pl.ds(HALF, 1), :].astype(F32)
    bias = bf_ref[...]
    z_blocks = []
    for i in range(HALF // FOURIER_ROWS):
        rows = pl.ds(i * FOURIER_ROWS, FOURIER_ROWS)
        yc = jnp.dot(cm_ref[rows, :], a_even, preferred_element_type=F32)
        yc = yc + alt_ref[rows, :] * a_mid + bias
        ys = jnp.dot(sm_ref[rows, :], b_odd, preferred_element_type=F32)
        o_ref[rows, :] = (yc - ys).astype(BF16)
        z_blocks.append((yc + ys).astype(BF16))
    z = jnp.concatenate(z_blocks, axis=0)
    for k in range(REV_BLOCKS):
        lo = HALF - REV * (k + 1)
        top = reversed_block(lo, k, lambda s, n: z[s:s + n])
        o_ref[pl.ds(HALF + k * REV, REV), :] = top.astype(BF16)
    y_mid = jnp.dot(altrow_ref[...], a_ref[...], preferred_element_type=F32)[0:1, :] + bias
    o_ref[pl.ds(HALF, 1), :] = y_mid.astype(BF16)


@functools.lru_cache(maxsize=None)
def _fold_tables():
    s = np.arange(HALF, dtype=np.int64)
    ph = (s[:, None] * s[None, :]) % SEQ
    ang = 2.0 * np.pi * ph.astype(np.float64) / SEQ
    sc = 1.0 / math.sqrt(SEQ)
    cm = (np.cos(ang) * sc).astype(np.float32)
    sm = (np.sin(ang) * sc).astype(np.float32)
    psh = np.zeros((REV, 2 * REV), np.float32)
    psh[np.arange(REV), REV - np.arange(REV)] = 1.0
    alt = (np.where(s % 2 == 0, 1.0, -1.0) * sc).astype(np.float32).reshape(HALF, 1)
    t = np.arange(SEQ)
    altrow = np.zeros((SUBLANES, SEQ), np.float32)
    altrow[0] = np.where(t % 2 == 0, 1.0, -1.0) * sc
    return cm, sm, psh, alt, altrow


def _fourier(a, b, bf):
    B = a.shape[0]
    cm, sm, psh, alt, altrow = _fold_tables()
    tok = pl.BlockSpec((None, SEQ, FOURIER_WIDTH), lambda i: (i, 0, 0))
    const = lambda shape: pl.BlockSpec(shape, lambda i: (0,) * len(shape))
    return pl.pallas_call(
        _fourier_kernel,
        grid=(B,),
        in_specs=[const((HALF, HALF)), const((HALF, HALF)), const((REV, 2 * REV)),
                  const((HALF, 1)), const((SUBLANES, SEQ)),
                  tok, tok, const((1, FOURIER_WIDTH))],
        out_specs=tok,
        out_shape=jax.ShapeDtypeStruct((B, SEQ, FOURIER_WIDTH), BF16),
        compiler_params=_params(("parallel",), VMEM_LIMIT),
        name="fourier",
    )(jnp.asarray(cm).astype(BF16), jnp.asarray(sm).astype(BF16), jnp.asarray(psh).astype(BF16),
      jnp.asarray(alt), jnp.asarray(altrow).astype(BF16), a, b, bf)


ATT_SUB = 4
ATT_ROWS = ATT_SUB * BLOCK
ATT_STEPS = N_BLOCKS // ATT_SUB


def _attn_kernel(sink_ref, q_ref, kl_ref, km_ref, kr_ref, vl_ref, vm_ref, vr_ref, bias_ref,
                 o_ref):
    i = pl.program_id(1)
    keys = jnp.concatenate([kl_ref[...], km_ref[...], kr_ref[...]], axis=0)
    vals = jnp.concatenate([vl_ref[...], vm_ref[...], vr_ref[...]], axis=0)
    first_k = lax.broadcasted_iota(jnp.int32, keys.shape, 1) < HEAD_DIM
    first_q = lax.broadcasted_iota(jnp.int32, (Q_PER_KV * BLOCK, LANES), 1) < HEAD_DIM
    row_head = lax.broadcasted_iota(jnp.int32, (Q_PER_KV * BLOCK, 1), 0) // BLOCK
    zero = jnp.zeros_like(keys)
    ones_lo = jnp.where(first_k, 1.0, 0.0).astype(BF16)
    ones_hi = jnp.where(first_k, 0.0, 1.0).astype(BF16)
    keys_kv = [jnp.where(first_k, keys, zero), jnp.where(first_k, zero, keys)]
    vals_kv = [jnp.concatenate([jnp.where(first_k, vals, zero), ones_lo], axis=1),
               jnp.concatenate([jnp.where(first_k, zero, vals), ones_hi], axis=1)]
    sinks = []
    for kv in range(N_KV_HEADS):
        sink = jnp.zeros((Q_PER_KV * BLOCK, 1), F32)
        for r in range(Q_PER_KV):
            sink = jnp.where(row_head == r, sink_ref[kv * Q_PER_KV + r] * LOG2E, sink)
        sinks.append(sink)
    for j in range(ATT_SUB):
        variant = 1
        if j == 0:
            variant = jnp.where(i == 0, 0, variant)
        if j == ATT_SUB - 1:
            variant = jnp.where(i == ATT_STEPS - 1, 2, variant)
        qrows = pl.ds(j * BLOCK, BLOCK)
        krows = slice(j * BLOCK, j * BLOCK + SPAN)
        qs = jnp.concatenate([q_ref[qrows, r * LANES:(r + 1) * LANES] for r in range(Q_PER_KV)],
                             axis=0)
        keys2 = jnp.concatenate([keys_kv[0][krows], keys_kv[1][krows]], axis=0)
        vals2 = jnp.concatenate([vals_kv[0][krows], vals_kv[1][krows]], axis=0)
        logits = lax.dot_general(qs, keys2, (((1,), (1,)), ((), ())),
                                 preferred_element_type=F32)
        logits = logits + bias_ref[variant]
        ms = [jnp.maximum(jnp.max(logits[:, kv * SPAN:(kv + 1) * SPAN], axis=-1, keepdims=True),
                          sinks[kv]) for kv in range(N_KV_HEADS)]
        p = jnp.exp2(jnp.concatenate([logits[:, kv * SPAN:(kv + 1) * SPAN] - ms[kv]
                                      for kv in range(N_KV_HEADS)], axis=1).astype(BF16))
        pv = jnp.dot(p, vals2, preferred_element_type=F32)
        denom = pv[:, LANES:] + jnp.exp2(jnp.where(first_q, sinks[0] - ms[0], sinks[1] - ms[1]))
        out = (pv[:, :LANES] / denom).astype(BF16)
        for r in range(Q_PER_KV):
            o_ref[qrows, r * LANES:(r + 1) * LANES] = out[r * BLOCK:(r + 1) * BLOCK]


def _attn(sink, q, k, v, bias):
    B = q.shape[0]
    edge = lambda f: pl.BlockSpec((None, BLOCK, KV_WIDTH), lambda b, i: (b, f(i), 0))
    left = lambda i: jnp.maximum(i * ATT_SUB - 1, 0)
    right = lambda i: jnp.minimum((i + 1) * ATT_SUB, N_BLOCKS - 1)
    mid = pl.BlockSpec((None, ATT_ROWS, KV_WIDTH), lambda b, i: (b, i, 0))
    qspec = pl.BlockSpec((None, ATT_ROWS, ATTN_WIDTH), lambda b, i: (b, i, 0))
    return pl.pallas_call(
        _attn_kernel,
        grid=(B, ATT_STEPS),
        in_specs=[pl.BlockSpec(memory_space=pltpu.SMEM), qspec,
                  edge(left), mid, edge(right),
                  edge(left), mid, edge(right),
                  pl.BlockSpec((3, Q_PER_KV * BLOCK, N_KV_HEADS * SPAN), lambda b, i: (0, 0, 0))],
        out_specs=qspec,
        out_shape=jax.ShapeDtypeStruct((B, SEQ, ATTN_WIDTH), BF16),
        compiler_params=_params(("parallel", "parallel"), VMEM_LIMIT),
        name="attn",
    )(sink, q, k, k, k, v, v, v, bias)


OUT_ROWS = 256


def _outproj_kernel(yf_ref, ya_ref, x_ref, mod_ref, g_ref, wf_ref, wa_ref, wrh_ref,
                    x1_ref, h2_ref, aff_ref):
    tm = x_ref.shape[0]
    gain = g_ref[...] * (1.0 + mod_ref[4:5, :])
    shift = mod_ref[3:4, :]
    gate1 = mod_ref[2:3, :]
    lane = lax.broadcasted_iota(jnp.int32, (OUT_ROWS, LANES), 1)
    for c in range(tm // OUT_ROWS):
        rows = pl.ds(c * OUT_ROWS, OUT_ROWS)
        mixed = jnp.dot(yf_ref[rows, :], wf_ref[...], preferred_element_type=F32)
        mixed = mixed + jnp.dot(ya_ref[rows, :], wa_ref[...], preferred_element_type=F32)
        x1 = x_ref[rows, :] + gate1 * mixed
        x1_ref[rows, :] = x1
        ms = jnp.mean(x1 * x1, axis=-1, keepdims=True)
        h2 = x1 * lax.rsqrt(ms + EPS) * gain + shift
        hi = h2.astype(BF16)
        top = pltpu.bitcast(hi[:, :D_MODEL // 2].astype(F32), jnp.uint32)
        bot = pltpu.bitcast(hi[:, D_MODEL // 2:].astype(F32), jnp.uint32)
        words = top | (bot >> 16)
        for j in range(PACK_ROWS):
            h2_ref[pl.ds(c * OUT_ROWS * PACK_ROWS + j, OUT_ROWS, stride=PACK_ROWS), :] = (
                words[:, j * LANES:(j + 1) * LANES])
        part = jnp.dot(hi, wrh_ref[...], preferred_element_type=F32)
        logits = part + pltpu.roll(part, LANES - N_EXPERTS, axis=1)
        logits = jnp.where(lane < N_EXPERTS, logits, NEG_INF)
        m = jnp.max(logits, axis=-1, keepdims=True)
        e = jnp.exp(logits - m)
        aff_ref[rows, :] = e / jnp.sum(e, axis=-1, keepdims=True)


def _outproj(yf, ya, x, mod, g, wf, wa, wrh, tm=1024):
    B = x.shape[0]
    const = lambda shape: pl.BlockSpec(shape, lambda b, i: (0,) * len(shape))
    tok = lambda w: pl.BlockSpec((None, tm, w), lambda b, i: (b, i, 0))
    return pl.pallas_call(
        _outproj_kernel,
        grid=(B, SEQ // tm),
        in_specs=[tok(FOURIER_WIDTH), tok(ATTN_WIDTH), tok(D_MODEL),
                  pl.BlockSpec((None, N_ADA, D_MODEL), lambda b, i: (b, 0, 0)),
                  const((1, D_MODEL)),
                  const((FOURIER_WIDTH, D_MODEL)), const((ATTN_WIDTH, D_MODEL)),
                  const((D_MODEL, LANES))],
        out_specs=[tok(D_MODEL),
                   pl.BlockSpec((None, tm * PACK_ROWS, LANES), lambda b, i: (b, i, 0)),
                   tok(LANES)],
        out_shape=[jax.ShapeDtypeStruct((B, SEQ, D_MODEL), F32),
                   jax.ShapeDtypeStruct((B, SEQ * PACK_ROWS, LANES), jnp.uint32),
                   jax.ShapeDtypeStruct((B, SEQ, LANES), F32)],
        compiler_params=_params(("parallel", "parallel"), VMEM_LIMIT),
        name="outproj",
    )(yf, ya, x, mod, g, wf, wa, wrh)


ROUTE_BATCHES = 2
SEARCH_BITS = 3


def _route_kernel(aff_ref, tri_ref, idx_ref, gate_ref):
    for bb in range(ROUTE_BATCHES):
        _route_one(aff_ref[bb], tri_ref[...], idx_ref.at[bb], gate_ref.at[bb])


def _route_one(aff, tri, idx_ref, gate_ref):
    aff_t = jnp.transpose(aff)[:N_EXPERTS]
    bits = pltpu.bitcast(aff_t, jnp.int32)
    cap = float(CAPACITY)

    t = jnp.zeros((N_EXPERTS, 1), jnp.int32)
    for shift in range(30 - SEARCH_BITS, -1, -SEARCH_BITS):
        digit = jnp.zeros((N_EXPERTS, 1), jnp.int32)
        for k in range(1, 1 << SEARCH_BITS):
            cnt = jnp.sum(jnp.where(bits >= (t | (k << shift)), 1.0, 0.0), axis=1, keepdims=True)
            digit = digit + jnp.where(cnt >= cap, 1, 0)
        t = t | (digit << shift)
    gt = bits > t
    eq = bits == t
    need = cap - jnp.sum(jnp.where(gt, 1.0, 0.0), axis=1, keepdims=True)

    n_chunks = SEQ // LANES

    def prefix(flags_f32):
        outs = []
        carry = jnp.zeros((N_EXPERTS, 1), F32)
        for c in range(n_chunks):
            f = flags_f32[:, c * LANES:(c + 1) * LANES]
            incl = jnp.dot(f.astype(BF16), tri, preferred_element_type=F32)
            outs.append(incl - f + carry)
            carry = carry + jnp.sum(f, axis=1, keepdims=True)
        return jnp.concatenate(outs, axis=1)

    eq_f = jnp.where(eq, 1.0, 0.0)
    eq_rank = prefix(eq_f)
    sel_f = jnp.where(gt, 1.0, jnp.where(eq_rank < need, eq_f, 0.0))
    pos = prefix(sel_f)
    posm = jnp.where(sel_f > 0.0, pos, -1.0)

    hi = aff_t.astype(BF16).astype(F32)
    r1 = aff_t - hi
    mid = r1.astype(BF16).astype(F32)
    lo = r1 - mid
    tok = lax.broadcasted_iota(jnp.int32, (N_EXPERTS, SEQ), 1)
    row = lax.broadcasted_iota(jnp.int32, (N_EXPERTS, SEQ), 0)
    tok_rows = jnp.where(row == 0, (tok >> 6).astype(F32),
                         jnp.where(row == 1, (tok & 63).astype(F32), 0.0))
    vals_t = jnp.concatenate([hi, mid, lo, tok_rows], axis=0).astype(BF16)

    slot = lax.broadcasted_iota(jnp.int32, (CAPACITY, SEQ), 0).astype(F32).astype(BF16)
    posm_b = posm.astype(BF16)
    one_b = jnp.ones((CAPACITY, SEQ), BF16)
    zero_b = jnp.zeros((CAPACITY, SEQ), BF16)
    for e in range(N_EXPERTS):
        onehot = jnp.where(posm_b[e:e + 1, :] == slot, one_b, zero_b)
        res = lax.dot_general(vals_t, onehot, (((1,), (1,)), ((), ())),
                              preferred_element_type=F32)
        cols = pl.ds(e * CAPACITY, CAPACITY)
        tok_idx = res[3 * N_EXPERTS:3 * N_EXPERTS + 1] * 64.0 + res[3 * N_EXPERTS + 1:
                                                                    3 * N_EXPERTS + 2]
        idx_ref[:, cols] = tok_idx.astype(jnp.int32) * PACK_ROWS
        gate_ref[:, cols] = (res[e:e + 1] + res[N_EXPERTS + e:N_EXPERTS + e + 1]
                             + res[2 * N_EXPERTS + e:2 * N_EXPERTS + e + 1])


def _route(aff):
    B = aff.shape[0]
    n = N_EXPERTS * CAPACITY
    return pl.pallas_call(
        _route_kernel,
        grid=(B // ROUTE_BATCHES,),
        in_specs=[pl.BlockSpec((ROUTE_BATCHES, SEQ, LANES), lambda b: (b, 0, 0)),
                  pl.BlockSpec((LANES, LANES), lambda b: (0, 0))],
        out_specs=[pl.BlockSpec((ROUTE_BATCHES, 1, n), lambda b: (b, 0, 0)),
                   pl.BlockSpec((ROUTE_BATCHES, 1, n), lambda b: (b, 0, 0))],
        out_shape=[jax.ShapeDtypeStruct((B, 1, n), jnp.int32),
                   jax.ShapeDtypeStruct((B, 1, n), F32)],
        compiler_params=_params(("parallel",), VMEM_LIMIT),
        name="route",
    )(aff, jnp.asarray(_tri_incl()).astype(BF16))


PAIR = 2


def _moe_kernel(idx_ref, h2_ref, wg_ref, wu_ref, wd_ref, y_ref, xin0_ref, xin1_ref):
    e = pl.program_id(1)
    last = N_EXPERTS - 1
    n = N_EXPERTS * CAPACITY
    rows = PAIR * CAPACITY

    def gather_rows(ex, dst_ref):
        for bb in range(PAIR):
            base = bb * n + ex * CAPACITY
            for p in range(CAPACITY):
                off = idx_ref[0, base + p]
                tile = h2_ref[bb, pl.ds(pl.multiple_of((off >> 3) << 3, SUBLANES), SUBLANES), :]
                tile = pltpu.roll(tile, off & PACK_ROWS, axis=0)
                dst_ref[pl.ds((bb * CAPACITY + p) * PACK_ROWS, PACK_ROWS), :] = tile[:PACK_ROWS]

    def expert(xin_ref):
        words = [xin_ref[pl.ds(j, rows, stride=PACK_ROWS), :] for j in range(PACK_ROWS)]
        xin = jnp.concatenate(
            [pltpu.bitcast(w & jnp.uint32(0xFFFF0000), F32).astype(BF16) for w in words]
            + [pltpu.bitcast(w << 16, F32).astype(BF16) for w in words], axis=1)
        for bb in range(PAIR):
            xb = xin[bb * CAPACITY:(bb + 1) * CAPACITY]
            a = jnp.dot(xb, wg_ref[...], preferred_element_type=F32)
            u = jnp.dot(xb, wu_ref[...], preferred_element_type=F32)
            hmid = (a * (1.0 / (1.0 + jnp.exp(-a))) * u).astype(BF16)
            y = jnp.dot(hmid, wd_ref[...], preferred_element_type=F32).astype(BF16)
            top = pltpu.bitcast(y[:, :D_MODEL // 2].astype(F32), jnp.uint32)
            bot = pltpu.bitcast(y[:, D_MODEL // 2:].astype(F32), jnp.uint32)
            words = top | (bot >> 16)
            for j in range(PACK_ROWS):
                y_ref[bb, pl.ds(j, CAPACITY, stride=PACK_ROWS), :] = (
                    words[:, j * LANES:(j + 1) * LANES])

    @pl.when(e == 0)
    def _():
        gather_rows(0, xin0_ref)

    def step(xin_cur, xin_nxt):
        gather_rows(jnp.minimum(e + 1, last), xin_nxt)
        expert(xin_cur)

    @pl.when(e % 2 == 0)
    def _():
        step(xin0_ref, xin1_ref)

    @pl.when(e % 2 == 1)
    def _():
        step(xin1_ref, xin0_ref)


def _moe(idx, h2, wg, wu, wd):
    B = h2.shape[0]
    n = N_EXPERTS * CAPACITY
    rows = SEQ * PACK_ROWS
    pairs = B // PAIR
    wspec = lambda r, c: pl.BlockSpec((None, r, c), lambda b, e: (e, 0, 0))
    stage = pltpu.VMEM((PAIR * CAPACITY * PACK_ROWS, LANES), jnp.uint32)
    out = pl.pallas_call(
        _moe_kernel,
        grid=(pairs, N_EXPERTS),
        in_specs=[pl.BlockSpec((None, 1, PAIR * n), lambda b, e: (b, 0, 0),
                               memory_space=pltpu.SMEM),
                  pl.BlockSpec((None, PAIR, rows, LANES), lambda b, e: (b, 0, 0, 0)),
                  wspec(D_MODEL, D_EXPERT), wspec(D_MODEL, D_EXPERT), wspec(D_EXPERT, D_MODEL)],
        out_specs=pl.BlockSpec((None, PAIR, CAPACITY * PACK_ROWS, LANES),
                               lambda b, e: (b, 0, e, 0)),
        out_shape=jax.ShapeDtypeStruct((pairs, PAIR, n * PACK_ROWS, LANES), jnp.uint32),
        scratch_shapes=[stage, stage],
        compiler_params=_params(("parallel", "arbitrary"), VMEM_LIMIT),
        name="moe",
    )(idx.reshape(pairs, 1, PAIR * n), h2.reshape(pairs, PAIR, rows, LANES), wg, wu, wd)
    return out.reshape(B, n * PACK_ROWS, LANES)


COMBINE_EXPERTS = 4
SCATTER_UNROLL = 8
COMBINE_ROWS = 256


def _combine_kernel(idx_ref, gate_ref, y_ref, x1_ref, mod_ref, o_ref, acc_ref):
    j = pl.program_id(1)
    slots = COMBINE_EXPERTS * CAPACITY
    base = j * slots

    @pl.when(j == 0)
    def _():
        acc_ref[...] = jnp.zeros_like(acc_ref)

    upper = lax.broadcasted_iota(jnp.int32, (SUBLANES, LANES), 0) < PACK_ROWS
    for g in range(slots // SCATTER_UNROLL):
        new = []
        for u in range(0, SCATTER_UNROLL, 2):
            r = g * SCATTER_UNROLL + u
            words = y_ref[pl.ds(r * PACK_ROWS, SUBLANES), :]
            hi = pltpu.bitcast(words & jnp.uint32(0xFFFF0000), F32)
            lo = pltpu.bitcast(words << 16, F32)
            slabs = (jnp.where(upper, hi, pltpu.roll(lo, PACK_ROWS, axis=0)),
                     jnp.where(upper, pltpu.roll(hi, PACK_ROWS, axis=0), lo))
            for k in range(2):
                dst = pl.multiple_of(idx_ref[0, base + r + k] * (ROW_SLAB // PACK_ROWS), ROW_SLAB)
                new.append((dst, acc_ref[pl.ds(dst, ROW_SLAB), :]
                            + slabs[k] * gate_ref[0, base + r + k]))
        for dst, val in new:
            acc_ref[pl.ds(dst, ROW_SLAB), :] = val

    @pl.when(j == pl.num_programs(1) - 1)
    def _():
        for rb in range(SEQ // COMBINE_ROWS):
            rows = pl.ds(rb * COMBINE_ROWS, COMBINE_ROWS)
            for c in range(ROW_SLAB):
                cols = slice(c * LANES, (c + 1) * LANES)
                chunk = acc_ref[pl.ds(rb * COMBINE_ROWS * ROW_SLAB + c, COMBINE_ROWS,
                                      stride=ROW_SLAB), :]
                o_ref[rows, cols] = x1_ref[rows, cols] + mod_ref[5:6, cols] * chunk


def _combine(idx, gate, y, x1, mod):
    B = x1.shape[0]
    n = N_EXPERTS * CAPACITY
    tok = pl.BlockSpec((None, SEQ, D_MODEL), lambda b, j: (b, 0, 0))
    smem = pl.BlockSpec((None, 1, n), lambda b, j: (b, 0, 0), memory_space=pltpu.SMEM)
    return pl.pallas_call(
        _combine_kernel,
        grid=(B, N_EXPERTS // COMBINE_EXPERTS),
        in_specs=[smem, smem,
                  pl.BlockSpec((None, COMBINE_EXPERTS * CAPACITY * PACK_ROWS, LANES),
                               lambda b, j: (b, j, 0)),
                  tok,
                  pl.BlockSpec((None, N_ADA, D_MODEL), lambda b, j: (b, 0, 0))],
        out_specs=tok,
        out_shape=jax.ShapeDtypeStruct((B, SEQ, D_MODEL), F32),
        scratch_shapes=[pltpu.VMEM((SEQ * ROW_SLAB, LANES), F32)],
        compiler_params=_params(("parallel", "arbitrary"), VMEM_LIMIT),
        name="combine",
    )(idx, gate, y, x1, mod)


def _head_perm():
    perm = []
    for r in range(Q_PER_KV):
        for kv in range(N_KV_HEADS):
            h = kv * Q_PER_KV + r
            perm.extend(range(h * HEAD_DIM, (h + 1) * HEAD_DIM))
    return np.asarray(perm, dtype=np.int32)


def kernel(x, c, rel_bias, w_ada, b_ada, norm_mix_g, norm_ffn_g, w_in, w_fourier, b_fourier,
           q_norm_g, k_norm_g, sink, w_out, w_router, w_gate, w_up, w_down):
    B = x.shape[0]
    perm = _head_perm()
    l = 0
    mod = _ada(c, w_ada[l], b_ada[l]).reshape(B, N_ADA, D_MODEL)
    pq = _fold(w_fourier[l])
    bias = _bias_table(rel_bias)

    wi = w_in[l]
    q_cols = wi[:, FOURIER_WIDTH:FOURIER_WIDTH + ATTN_WIDTH][:, perm]
    win = jnp.concatenate([wi[:, :FOURIER_WIDTH], q_cols, wi[:, FOURIER_WIDTH + ATTN_WIDTH:]],
                          axis=1).astype(BF16)
    gq = (jnp.tile(q_norm_g[l], N_Q_HEADS) * (HEAD_DIM ** -0.5 * LOG2E)).reshape(1, ATTN_WIDTH)
    gk = jnp.tile(k_norm_g[l], N_KV_HEADS).reshape(1, KV_WIDTH)
    a, b, q, k, v, experts = _inproj(x, mod, norm_mix_g[l].reshape(1, D_MODEL), win, pq, gq, gk,
                                     w_gate[l], w_up[l], w_down[l])

    yf = _fourier(a, b, b_fourier[l].reshape(1, FOURIER_WIDTH))
    ya = _attn(sink[l], q, k, v, bias)

    wo = w_out[l]
    wf = wo[:FOURIER_WIDTH].astype(BF16)
    wa = wo[FOURIER_WIDTH:][perm].astype(BF16)
    w_hi = w_router[l].astype(BF16)
    w_lo = (w_router[l] - w_hi.astype(F32)).astype(BF16)
    wrh = jnp.pad(jnp.concatenate([w_hi, w_lo], axis=1), ((0, 0), (0, LANES - 2 * N_EXPERTS)))
    x1, h2, aff = _outproj(yf, ya, x, mod, norm_ffn_g[l].reshape(1, D_MODEL), wf, wa, wrh)

    idx, gate = _route(aff)
    n = N_EXPERTS * CAPACITY
    y = _moe(idx, h2, *experts)
    return _combine(idx, gate, y, x1, mod)
```

```python
import functools
import math

import numpy as np
import jax
import jax.numpy as jnp
from jax import lax
from jax.experimental import pallas as pl
from jax.experimental.pallas import tpu as pltpu

D_MODEL = 1024
SEQ = 2048
HEAD_DIM = 64
FOURIER_WIDTH = 512
ATTN_WIDTH = 512
N_GROUPS = 8
N_Q_HEADS = 8
Q_PER_KV = 4
N_KV_HEADS = 2
KV_WIDTH = 128
IN_PROJ_WIDTH = 1280
WINDOW = 128
BLOCK = 128
SPAN = BLOCK + 2 * WINDOW
N_BLOCKS = SEQ // BLOCK
N_BUCKETS = 32
MAX_DISTANCE = 128
N_EXPERTS = 16
CAPACITY = 2 * SEQ // N_EXPERTS
D_EXPERT = 1024
N_ADA = 6
EPS = 1e-6

LANES = 128
SUBLANES = 8
ROW_SLAB = D_MODEL // LANES
PACK_ROWS = ROW_SLAB // 2
VMEM_LIMIT = 56 * 1024 * 1024

F32 = jnp.float32
BF16 = jnp.bfloat16
NEG_INF = float("-inf")
LOG2E = math.log2(math.e)


def _params(sem, vmem=None):
    return pltpu.CompilerParams(dimension_semantics=sem, vmem_limit_bytes=vmem)


@functools.lru_cache(maxsize=None)
def _chan_dft():
    c = np.arange(HEAD_DIM, dtype=np.int64)
    ph = (c[:, None] * c[None, :]) % HEAD_DIM
    ang = 2.0 * np.pi * ph.astype(np.float64) / HEAD_DIM
    sc = 1.0 / math.sqrt(HEAD_DIM)
    eye = np.eye(N_GROUPS)
    cbd = np.kron(eye, np.cos(ang) * sc)
    sbd = np.kron(eye, np.sin(ang) * sc)
    return cbd.astype(np.float32), sbd.astype(np.float32)


@functools.lru_cache(maxsize=None)
def _bucket_table():
    rel = np.arange(SPAN)[None, :] - WINDOW - np.arange(BLOCK)[:, None]
    half = N_BUCKETS // 2
    max_exact = half // 2
    n = np.abs(rel)
    nf = np.maximum(n, 1).astype(np.float64)
    large = max_exact + (np.log(nf / max_exact) / math.log(MAX_DISTANCE / max_exact)
                         * (half - max_exact)).astype(np.int64)
    sq = np.maximum(n.astype(np.int64) ** 2 // (max_exact * max_exact), 1)
    large_int = max_exact + np.floor(np.log2(sq.astype(np.float64)) + 1e-9).astype(np.int64)
    assert np.array_equal(np.where(n >= max_exact, large, 0), np.where(n >= max_exact, large_int, 0))
    large = np.minimum(large, half - 1)
    bucket = np.where(rel > 0, half, 0) + np.where(n < max_exact, n, large)
    return bucket.astype(np.int32)


@functools.lru_cache(maxsize=None)
def _group_ones(width):
    return np.kron(np.eye(width // HEAD_DIM), np.ones((HEAD_DIM, HEAD_DIM))).astype(np.float32)


@functools.lru_cache(maxsize=None)
def _tri_incl():
    i = np.arange(LANES)
    return (i[:, None] <= i[None, :]).astype(np.float32)


def _ada_kernel(c_ref, w_ref, b_ref, o_ref):
    c = c_ref[...]
    ca = c * (1.0 / (1.0 + jnp.exp(-c)))
    o_ref[...] = jnp.dot(ca, w_ref[...], precision=lax.Precision.HIGHEST,
                         preferred_element_type=F32) + b_ref[...]


def _ada(c, w_ada, b_ada):
    B = c.shape[0]
    n = N_ADA * D_MODEL
    tn = D_MODEL
    return pl.pallas_call(
        _ada_kernel,
        grid=(n // tn,),
        in_specs=[pl.BlockSpec((B, D_MODEL), lambda j: (0, 0)),
                  pl.BlockSpec((D_MODEL, tn), lambda j: (0, j)),
                  pl.BlockSpec((1, tn), lambda j: (0, j))],
        out_specs=pl.BlockSpec((B, tn), lambda j: (0, j)),
        out_shape=jax.ShapeDtypeStruct((B, n), F32),
        compiler_params=_params(("arbitrary",)),
        name="ada",
    )(c, w_ada, b_ada.reshape(1, n))


def _fold_kernel(cbd_ref, sbd_ref, w_ref, o_ref):
    w = w_ref[...]
    o_ref[:, :FOURIER_WIDTH] = jnp.dot(cbd_ref[...], w, precision=lax.Precision.HIGHEST,
                                       preferred_element_type=F32).astype(BF16)
    o_ref[:, FOURIER_WIDTH:] = jnp.dot(sbd_ref[...], w, precision=lax.Precision.HIGHEST,
                                       preferred_element_type=F32).astype(BF16)


def _fold(w_fourier):
    wbd = (jnp.eye(N_GROUPS, dtype=F32)[:, None, :, None] * w_fourier[:, :, None, :]
           ).reshape(FOURIER_WIDTH, FOURIER_WIDTH)
    cbd, sbd = _chan_dft()
    return pl.pallas_call(
        _fold_kernel,
        out_shape=jax.ShapeDtypeStruct((FOURIER_WIDTH, 2 * FOURIER_WIDTH), BF16),
        name="fold",
    )(jnp.asarray(cbd), jnp.asarray(sbd), wbd)


def _bias_kernel(rb_ref, bucket_ref, o_ref):
    h = pl.program_id(0)
    bk = bucket_ref[...]
    acc = jnp.zeros((BLOCK, SPAN), F32)
    for b in range(N_BUCKETS):
        acc = jnp.where(bk == b, rb_ref[b, h], acc)
    j = lax.broadcasted_iota(jnp.int32, (BLOCK, SPAN), 1)
    q = lax.broadcasted_iota(jnp.int32, (BLOCK, SPAN), 0)
    band = jnp.abs(j - WINDOW - q) <= WINDOW
    base = jnp.where(band, acc * LOG2E, NEG_INF)
    o_ref[0] = jnp.where(j >= WINDOW, base, NEG_INF)
    o_ref[1] = base
    o_ref[2] = jnp.where(j < WINDOW + BLOCK, base, NEG_INF)


def _bias_table(rel_bias):
    return pl.pallas_call(
        _bias_kernel,
        grid=(N_Q_HEADS,),
        in_specs=[pl.BlockSpec(memory_space=pltpu.SMEM),
                  pl.BlockSpec((BLOCK, SPAN), lambda h: (0, 0))],
        out_specs=pl.BlockSpec((3, BLOCK, SPAN), lambda h: (0, h % Q_PER_KV, h // Q_PER_KV)),
        out_shape=jax.ShapeDtypeStruct((3, Q_PER_KV * BLOCK, N_KV_HEADS * SPAN), F32),
        compiler_params=_params(("arbitrary",)),
        name="bias",
    )(rel_bias, jnp.asarray(_bucket_table()))


IN_ROWS = 256


def _inproj_kernel(x_ref, mod_ref, g_ref, win_ref, pq_ref, bdq_ref, bdk_ref, gq_ref, gk_ref,
                   wg_ref, wu_ref, wd_ref,
                   a_ref, b_ref, q_ref, k_ref, v_ref, wg_out, wu_out, wd_out):
    wg_out[...] = wg_ref[...].astype(BF16)
    wu_out[...] = wu_ref[...].astype(BF16)
    wd_out[...] = wd_ref[...].astype(BF16)
    gain = g_ref[...] * (1.0 + mod_ref[1:2, :])
    shift = mod_ref[0:1, :]
    q0 = FOURIER_WIDTH
    k0 = q0 + ATTN_WIDTH
    v0 = k0 + KV_WIDTH
    for c in range(x_ref.shape[0] // IN_ROWS):
        rows = pl.ds(c * IN_ROWS, IN_ROWS)
        x = x_ref[rows, :]
        ms = jnp.mean(x * x, axis=-1, keepdims=True)
        h = x * lax.rsqrt(ms + EPS) * gain + shift
        proj = jnp.dot(h.astype(BF16), win_ref[...], preferred_element_type=F32)
        uf = proj[:, :FOURIER_WIDTH].astype(BF16)
        ab = jnp.dot(uf, pq_ref[...], preferred_element_type=F32)
        a_ref[rows, :] = ab[:, :FOURIER_WIDTH].astype(BF16)
        b_ref[rows, :] = ab[:, FOURIER_WIDTH:].astype(BF16)
        q = proj[:, q0:k0]
        ssq = jnp.dot((q * q).astype(BF16), bdq_ref[...], preferred_element_type=F32)
        q_ref[rows, :] = (q * lax.rsqrt(ssq * (1.0 / HEAD_DIM) + EPS) * gq_ref[...]).astype(BF16)
        k = proj[:, k0:v0]
        ssk = jnp.dot((k * k).astype(BF16), bdk_ref[...], preferred_element_type=F32)
        k_ref[rows, :] = (k * lax.rsqrt(ssk * (1.0 / HEAD_DIM) + EPS) * gk_ref[...]).astype(BF16)
        v_ref[rows, :] = proj[:, v0:].astype(BF16)


def _inproj(x, mod, g, win, pq, gq, gk, w_gate, w_up, w_down, tm=1024):
    B = x.shape[0]
    steps_per_batch = SEQ // tm
    w_rows = N_EXPERTS * D_MODEL
    w_blk = w_rows // (B * steps_per_batch)
    const = lambda shape: pl.BlockSpec(shape, lambda b, i: (0,) * len(shape))
    tok = lambda w: pl.BlockSpec((None, tm, w), lambda b, i: (b, i, 0))
    wsl = lambda c: pl.BlockSpec((w_blk, c), lambda b, i: (b * steps_per_batch + i, 0))
    sds = lambda w: jax.ShapeDtypeStruct((B, SEQ, w), BF16)
    wsds = lambda c: jax.ShapeDtypeStruct((w_rows, c), BF16)
    a, b, q, k, v, wg, wu, wd = pl.pallas_call(
        _inproj_kernel,
        grid=(B, steps_per_batch),
        in_specs=[tok(D_MODEL),
                  pl.BlockSpec((None, N_ADA, D_MODEL), lambda b, i: (b, 0, 0)),
                  const((1, D_MODEL)),
                  const((D_MODEL, IN_PROJ_WIDTH)),
                  const((FOURIER_WIDTH, 2 * FOURIER_WIDTH)),
                  const((ATTN_WIDTH, ATTN_WIDTH)),
                  const((KV_WIDTH, KV_WIDTH)),
                  const((1, ATTN_WIDTH)),
                  const((1, KV_WIDTH)),
                  wsl(D_EXPERT), wsl(D_EXPERT), wsl(D_MODEL)],
        out_specs=[tok(FOURIER_WIDTH), tok(FOURIER_WIDTH), tok(ATTN_WIDTH), tok(KV_WIDTH),
                   tok(KV_WIDTH), wsl(D_EXPERT), wsl(D_EXPERT), wsl(D_MODEL)],
        out_shape=[sds(FOURIER_WIDTH), sds(FOURIER_WIDTH), sds(ATTN_WIDTH), sds(KV_WIDTH),
                   sds(KV_WIDTH), wsds(D_EXPERT), wsds(D_EXPERT), wsds(D_MODEL)],
        compiler_params=_params(("parallel", "parallel"), VMEM_LIMIT),
        name="inproj",
    )(x, mod, g, win, pq, jnp.asarray(_group_ones(ATTN_WIDTH)).astype(BF16),
      jnp.asarray(_group_ones(KV_WIDTH)).astype(BF16), gq, gk,
      w_gate.reshape(w_rows, D_EXPERT), w_up.reshape(w_rows, D_EXPERT),
      w_down.reshape(N_EXPERTS * D_EXPERT, D_MODEL))
    experts = (wg.reshape(N_EXPERTS, D_MODEL, D_EXPERT), wu.reshape(N_EXPERTS, D_MODEL, D_EXPERT),
               wd.reshape(N_EXPERTS, D_EXPERT, D_MODEL))
    return a, b, q, k, v, experts


HALF = SEQ // 2
REV = 128
REV_BLOCKS = HALF // REV
FOURIER_ROWS = 512


def _fourier_kernel(cm_ref, sm_ref, psh_ref, alt_ref, altrow_ref, a_ref, b_ref, bf_ref, o_ref):
    psh = psh_ref[...]

    def reversed_block(win_lo, k, src):
        if k == 0:
            return jnp.dot(psh[:, :REV], src(win_lo, REV), preferred_element_type=F32)
        return jnp.dot(psh, src(win_lo, 2 * REV), preferred_element_type=F32)

    def folded(ref, sign):
        blocks = []
        for k in range(REV_BLOCKS):
            lo = SEQ - REV * (k + 1)
            rev = reversed_block(lo, k, lambda s, n: ref[pl.ds(s, n), :])
            blocks.append((ref[pl.ds(k * REV, REV), :].astype(F32) + sign * rev).astype(BF16))
        return jnp.concatenate(blocks, axis=0)

    a_even = folded(a_ref, 1.0)
    b_odd = folded(b_ref, -1.0)
    a_mid = a_ref[pl.ds(HALF, 1), :].astype(F32)
    bias = bf_ref[...]
    z_blocks = []
    for i in range(HALF // FOURIER_ROWS):
        rows = pl.ds(i * FOURIER_ROWS, FOURIER_ROWS)
        yc = jnp.dot(cm_ref[rows, :], a_even, preferred_element_type=F32)
        yc = yc + alt_ref[rows, :] * a_mid + bias
        ys = jnp.dot(sm_ref[rows, :], b_odd, preferred_element_type=F32)
        o_ref[rows, :] = (yc - ys).astype(BF16)
        z_blocks.append((yc + ys).astype(BF16))
    z = jnp.concatenate(z_blocks, axis=0)
    for k in range(REV_BLOCKS):
        lo = HALF - REV * (k + 1)
        top = reversed_block(lo, k, lambda s, n: z[s:s + n])
        o_ref[pl.ds(HALF + k * REV, REV), :] = top.astype(BF16)
    y_mid = jnp.dot(altrow_ref[...], a_ref[...], preferred_element_type=F32)[0:1, :] + bias
    o_ref[pl.ds(HALF, 1), :] = y_mid.astype(BF16)


@functools.lru_cache(maxsize=None)
def _fold_tables():
    s = np.arange(HALF, dtype=np.int64)
    ph = (s[:, None] * s[None, :]) % SEQ
    ang = 2.0 * np.pi * ph.astype(np.float64) / SEQ
    sc = 1.0 / math.sqrt(SEQ)
    cm = (np.cos(ang) * sc).astype(np.float32)
    sm = (np.sin(ang) * sc).astype(np.float32)
    psh = np.zeros((REV, 2 * REV), np.float32)
    psh[np.arange(REV), REV - np.arange(REV)] = 1.0
    alt = (np.where(s % 2 == 0, 1.0, -1.0) * sc).astype(np.float32).reshape(HALF, 1)
    t = np.arange(SEQ)
    altrow = np.zeros((SUBLANES, SEQ), np.float32)
    altrow[0] = np.where(t % 2 == 0, 1.0, -1.0) * sc
    return cm, sm, psh, alt, altrow


def _fourier(a, b, bf):
    B = a.shape[0]
    cm, sm, psh, alt, altrow = _fold_tables()
    tok = pl.BlockSpec((None, SEQ, FOURIER_WIDTH), lambda i: (i, 0, 0))
    const = lambda shape: pl.BlockSpec(shape, lambda i: (0,) * len(shape))
    return pl.pallas_call(
        _fourier_kernel,
        grid=(B,),
        in_specs=[const((HALF, HALF)), const((HALF, HALF)), const((REV, 2 * REV)),
                  const((HALF, 1)), const((SUBLANES, SEQ)),
                  tok, tok, const((1, FOURIER_WIDTH))],
        out_specs=tok,
        out_shape=jax.ShapeDtypeStruct((B, SEQ, FOURIER_WIDTH), BF16),
        compiler_params=_params(("parallel",), VMEM_LIMIT),
        name="fourier",
    )(jnp.asarray(cm).astype(BF16), jnp.asarray(sm).astype(BF16), jnp.asarray(psh).astype(BF16),
      jnp.asarray(alt), jnp.asarray(altrow).astype(BF16), a, b, bf)


ATT_SUB = 16
ATT_ROWS = ATT_SUB * BLOCK
ATT_STEPS = N_BLOCKS // ATT_SUB


def _attn_kernel(sink_ref, q_ref, kl_ref, km_ref, kr_ref, vl_ref, vm_ref, vr_ref, bias_ref,
                 o_ref):
    i = pl.program_id(1)
    keys = jnp.concatenate([kl_ref[...], km_ref[...], kr_ref[...]], axis=0)
    vals = jnp.concatenate([vl_ref[...], vm_ref[...], vr_ref[...]], axis=0)
    first_k = lax.broadcasted_iota(jnp.int32, keys.shape, 1) < HEAD_DIM
    first_q = lax.broadcasted_iota(jnp.int32, (Q_PER_KV * BLOCK, LANES), 1) < HEAD_DIM
    row_head = lax.broadcasted_iota(jnp.int32, (Q_PER_KV * BLOCK, 1), 0) // BLOCK
    zero = jnp.zeros_like(keys)
    ones_lo = jnp.where(first_k, 1.0, 0.0).astype(BF16)
    ones_hi = jnp.where(first_k, 0.0, 1.0).astype(BF16)
    keys_kv = [jnp.where(first_k, keys, zero), jnp.where(first_k, zero, keys)]
    vals_kv = [jnp.concatenate([jnp.where(first_k, vals, zero), ones_lo], axis=1),
               jnp.concatenate([jnp.where(first_k, zero, vals), ones_hi], axis=1)]
    sinks = []
    for kv in range(N_KV_HEADS):
        sink = jnp.zeros((Q_PER_KV * BLOCK, 1), F32)
        for r in range(Q_PER_KV):
            sink = jnp.where(row_head == r, sink_ref[kv * Q_PER_KV + r] * LOG2E, sink)
        sinks.append(sink)
    for j in range(ATT_SUB):
        variant = 1
        if j == 0:
            variant = jnp.where(i == 0, 0, variant)
        if j == ATT_SUB - 1:
            variant = jnp.where(i == ATT_STEPS - 1, 2, variant)
        qrows = pl.ds(j * BLOCK, BLOCK)
        krows = slice(j * BLOCK, j * BLOCK + SPAN)
        qs = jnp.concatenate([q_ref[qrows, r * LANES:(r + 1) * LANES] for r in range(Q_PER_KV)],
                             axis=0)
        keys2 = jnp.concatenate([keys_kv[0][krows], keys_kv[1][krows]], axis=0)
        vals2 = jnp.concatenate([vals_kv[0][krows], vals_kv[1][krows]], axis=0)
        logits = lax.dot_general(qs, keys2, (((1,), (1,)), ((), ())),
                                 preferred_element_type=F32)
        logits = logits + bias_ref[variant]
        ms = [jnp.maximum(jnp.max(logits[:, kv * SPAN:(kv + 1) * SPAN], axis=-1, keepdims=True),
                          sinks[kv]) for kv in range(N_KV_HEADS)]
        p = jnp.exp2(jnp.concatenate([logits[:, kv * SPAN:(kv + 1) * SPAN] - ms[kv]
                                      for kv in range(N_KV_HEADS)], axis=1).astype(BF16))
        pv = jnp.dot(p, vals2, preferred_element_type=F32)
        denom = pv[:, LANES:] + jnp.exp2(jnp.where(first_q, sinks[0] - ms[0], sinks[1] - ms[1]))
        out = (pv[:, :LANES] / denom).astype(BF16)
        for r in range(Q_PER_KV):
            o_ref[qrows, r * LANES:(r + 1) * LANES] = out[r * BLOCK:(r + 1) * BLOCK]


def _attn(sink, q, k, v, bias):
    B = q.shape[0]
    edge = lambda f: pl.BlockSpec((None, BLOCK, KV_WIDTH), lambda b, i: (b, f(i), 0))
    left = lambda i: jnp.maximum(i * ATT_SUB - 1, 0)
    right = lambda i: jnp.minimum((i + 1) * ATT_SUB, N_BLOCKS - 1)
    mid = pl.BlockSpec((None, ATT_ROWS, KV_WIDTH), lambda b, i: (b, i, 0))
    qspec = pl.BlockSpec((None, ATT_ROWS, ATTN_WIDTH), lambda b, i: (b, i, 0))
    return pl.pallas_call(
        _attn_kernel,
        grid=(B, ATT_STEPS),
        in_specs=[pl.BlockSpec(memory_space=pltpu.SMEM), qspec,
                  edge(left), mid, edge(right),
                  edge(left), mid, edge(right),
                  pl.BlockSpec((3, Q_PER_KV * BLOCK, N_KV_HEADS * SPAN), lambda b, i: (0, 0, 0))],
        out_specs=qspec,
        out_shape=jax.ShapeDtypeStruct((B, SEQ, ATTN_WIDTH), BF16),
        compiler_params=_params(("parallel", "parallel"), VMEM_LIMIT),
        name="attn",
    )(sink, q, k, k, k, v, v, v, bias)


OUT_ROWS = 256


def _outproj_kernel(yf_ref, ya_ref, x_ref, mod_ref, g_ref, wf_ref, wa_ref, wrh_ref,
                    x1_ref, h2_ref, aff_ref):
    tm = x_ref.shape[0]
    gain = g_ref[...] * (1.0 + mod_ref[4:5, :])
    shift = mod_ref[3:4, :]
    gate1 = mod_ref[2:3, :]
    lane = lax.broadcasted_iota(jnp.int32, (OUT_ROWS, LANES), 1)
    for c in range(tm // OUT_ROWS):
        rows = pl.ds(c * OUT_ROWS, OUT_ROWS)
        mixed = jnp.dot(yf_ref[rows, :], wf_ref[...], preferred_element_type=F32)
        mixed = mixed + jnp.dot(ya_ref[rows, :], wa_ref[...], preferred_element_type=F32)
        x1 = x_ref[rows, :] + gate1 * mixed
        x1_ref[rows, :] = x1
        ms = jnp.mean(x1 * x1, axis=-1, keepdims=True)
        h2 = x1 * lax.rsqrt(ms + EPS) * gain + shift
        hi = h2.astype(BF16)
        top = pltpu.bitcast(hi[:, :D_MODEL // 2].astype(F32), jnp.uint32)
        bot = pltpu.bitcast(hi[:, D_MODEL // 2:].astype(F32), jnp.uint32)
        words = top | (bot >> 16)
        for j in range(PACK_ROWS):
            h2_ref[pl.ds(c * OUT_ROWS * PACK_ROWS + j, OUT_ROWS, stride=PACK_ROWS), :] = (
                words[:, j * LANES:(j + 1) * LANES])
        part = jnp.dot(hi, wrh_ref[...], preferred_element_type=F32)
        logits = part + pltpu.roll(part, LANES - N_EXPERTS, axis=1)
        logits = jnp.where(lane < N_EXPERTS, logits, NEG_INF)
        m = jnp.max(logits, axis=-1, keepdims=True)
        e = jnp.exp(logits - m)
        aff_ref[rows, :] = e / jnp.sum(e, axis=-1, keepdims=True)


def _outproj(yf, ya, x, mod, g, wf, wa, wrh, tm=1024):
    B = x.shape[0]
    const = lambda shape: pl.BlockSpec(shape, lambda b, i: (0,) * len(shape))
    tok = lambda w: pl.BlockSpec((None, tm, w), lambda b, i: (b, i, 0))
    return pl.pallas_call(
        _outproj_kernel,
        grid=(B, SEQ // tm),
        in_specs=[tok(FOURIER_WIDTH), tok(ATTN_WIDTH), tok(D_MODEL),
                  pl.BlockSpec((None, N_ADA, D_MODEL), lambda b, i: (b, 0, 0)),
                  const((1, D_MODEL)),
                  const((FOURIER_WIDTH, D_MODEL)), const((ATTN_WIDTH, D_MODEL)),
                  const((D_MODEL, LANES))],
        out_specs=[tok(D_MODEL),
                   pl.BlockSpec((None, tm * PACK_ROWS, LANES), lambda b, i: (b, i, 0)),
                   tok(LANES)],
        out_shape=[jax.ShapeDtypeStruct((B, SEQ, D_MODEL), F32),
                   jax.ShapeDtypeStruct((B, SEQ * PACK_ROWS, LANES), jnp.uint32),
                   jax.ShapeDtypeStruct((B, SEQ, LANES), F32)],
        compiler_params=_params(("parallel", "parallel"), VMEM_LIMIT),
        name="outproj",
    )(yf, ya, x, mod, g, wf, wa, wrh)


ROUTE_BATCHES = 2
SEARCH_BITS = 3


def _route_kernel(aff_ref, tri_ref, idx_ref, gate_ref):
    for bb in range(ROUTE_BATCHES):
        _route_one(aff_ref[bb], tri_ref[...], idx_ref.at[bb], gate_ref.at[bb])


def _route_one(aff, tri, idx_ref, gate_ref):
    aff_t = jnp.transpose(aff)[:N_EXPERTS]
    bits = pltpu.bitcast(aff_t, jnp.int32)
    cap = float(CAPACITY)

    t = jnp.zeros((N_EXPERTS, 1), jnp.int32)
    for shift in range(30 - SEARCH_BITS, -1, -SEARCH_BITS):
        digit = jnp.zeros((N_EXPERTS, 1), jnp.int32)
        for k in range(1, 1 << SEARCH_BITS):
            cnt = jnp.sum(jnp.where(bits >= (t | (k << shift)), 1.0, 0.0), axis=1, keepdims=True)
            digit = digit + jnp.where(cnt >= cap, 1, 0)
        t = t | (digit << shift)
    gt = bits > t
    eq = bits == t
    need = cap - jnp.sum(jnp.where(gt, 1.0, 0.0), axis=1, keepdims=True)

    n_chunks = SEQ // LANES

    def prefix(flags_f32):
        outs = []
        carry = jnp.zeros((N_EXPERTS, 1), F32)
        for c in range(n_chunks):
            f = flags_f32[:, c * LANES:(c + 1) * LANES]
            incl = jnp.dot(f.astype(BF16), tri, preferred_element_type=F32)
            outs.append(incl - f + carry)
            carry = carry + jnp.sum(f, axis=1, keepdims=True)
        return jnp.concatenate(outs, axis=1)

    eq_f = jnp.where(eq, 1.0, 0.0)
    eq_rank = prefix(eq_f)
    sel_f = jnp.where(gt, 1.0, jnp.where(eq_rank < need, eq_f, 0.0))
    pos = prefix(sel_f)
    posm = jnp.where(sel_f > 0.0, pos, -1.0)

    hi = aff_t.astype(BF16).astype(F32)
    r1 = aff_t - hi
    mid = r1.astype(BF16).astype(F32)
    lo = r1 - mid
    tok = lax.broadcasted_iota(jnp.int32, (N_EXPERTS, SEQ), 1)
    row = lax.broadcasted_iota(jnp.int32, (N_EXPERTS, SEQ), 0)
    tok_rows = jnp.where(row == 0, (tok >> 6).astype(F32),
                         jnp.where(row == 1, (tok & 63).astype(F32), 0.0))
    vals_t = jnp.concatenate([hi, mid, lo, tok_rows], axis=0).astype(BF16)

    slot = lax.broadcasted_iota(jnp.int32, (CAPACITY, SEQ), 0).astype(F32).astype(BF16)
    posm_b = posm.astype(BF16)
    one_b = jnp.ones((CAPACITY, SEQ), BF16)
    zero_b = jnp.zeros((CAPACITY, SEQ), BF16)
    for e in range(N_EXPERTS):
        onehot = jnp.where(posm_b[e:e + 1, :] == slot, one_b, zero_b)
        res = lax.dot_general(vals_t, onehot, (((1,), (1,)), ((), ())),
                              preferred_element_type=F32)
        cols = pl.ds(e * CAPACITY, CAPACITY)
        tok_idx = res[3 * N_EXPERTS:3 * N_EXPERTS + 1] * 64.0 + res[3 * N_EXPERTS + 1:
                                                                    3 * N_EXPERTS + 2]
        idx_ref[:, cols] = tok_idx.astype(jnp.int32) * PACK_ROWS
        gate_ref[:, cols] = (res[e:e + 1] + res[N_EXPERTS + e:N_EXPERTS + e + 1]
                             + res[2 * N_EXPERTS + e:2 * N_EXPERTS + e + 1])


def _route(aff):
    B = aff.shape[0]
    n = N_EXPERTS * CAPACITY
    return pl.pallas_call(
        _route_kernel,
        grid=(B // ROUTE_BATCHES,),
        in_specs=[pl.BlockSpec((ROUTE_BATCHES, SEQ, LANES), lambda b: (b, 0, 0)),
                  pl.BlockSpec((LANES, LANES), lambda b: (0, 0))],
        out_specs=[pl.BlockSpec((ROUTE_BATCHES, 1, n), lambda b: (b, 0, 0)),
                   pl.BlockSpec((ROUTE_BATCHES, 1, n), lambda b: (b, 0, 0))],
        out_shape=[jax.ShapeDtypeStruct((B, 1, n), jnp.int32),
                   jax.ShapeDtypeStruct((B, 1, n), F32)],
        compiler_params=_params(("parallel",), VMEM_LIMIT),
        name="route",
    )(aff, jnp.asarray(_tri_incl()).astype(BF16))


PAIR = 2


def _moe_kernel(idx_ref, h2_ref, wg_ref, wu_ref, wd_ref, y_ref, xin0_ref, xin1_ref):
    e = pl.program_id(1)
    last = N_EXPERTS - 1
    n = N_EXPERTS * CAPACITY
    rows = PAIR * CAPACITY

    def gather_rows(ex, dst_ref):
        for bb in range(PAIR):
            base = bb * n + ex * CAPACITY
            for p in range(CAPACITY):
                off = idx_ref[0, base + p]
                tile = h2_ref[bb, pl.ds(pl.multiple_of((off >> 3) << 3, SUBLANES), SUBLANES), :]
                tile = pltpu.roll(tile, off & PACK_ROWS, axis=0)
                dst_ref[pl.ds((bb * CAPACITY + p) * PACK_ROWS, PACK_ROWS), :] = tile[:PACK_ROWS]

    def expert(xin_ref):
        words = [xin_ref[pl.ds(j, rows, stride=PACK_ROWS), :] for j in range(PACK_ROWS)]
        xin = jnp.concatenate(
            [pltpu.bitcast(w & jnp.uint32(0xFFFF0000), F32).astype(BF16) for w in words]
            + [pltpu.bitcast(w << 16, F32).astype(BF16) for w in words], axis=1)
        for bb in range(PAIR):
            xb = xin[bb * CAPACITY:(bb + 1) * CAPACITY]
            a = jnp.dot(xb, wg_ref[...], preferred_element_type=F32)
            u = jnp.dot(xb, wu_ref[...], preferred_element_type=F32)
            hmid = (a * (1.0 / (1.0 + jnp.exp(-a))) * u).astype(BF16)
            y = jnp.dot(hmid, wd_ref[...], preferred_element_type=F32).astype(BF16)
            top = pltpu.bitcast(y[:, :D_MODEL // 2].astype(F32), jnp.uint32)
            bot = pltpu.bitcast(y[:, D_MODEL // 2:].astype(F32), jnp.uint32)
            words = top | (bot >> 16)
            for j in range(PACK_ROWS):
                y_ref[bb, pl.ds(j, CAPACITY, stride=PACK_ROWS), :] = (
                    words[:, j * LANES:(j + 1) * LANES])

    @pl.when(e == 0)
    def _():
        gather_rows(0, xin0_ref)

    def step(xin_cur, xin_nxt):
        gather_rows(jnp.minimum(e + 1, last), xin_nxt)
        expert(xin_cur)

    @pl.when(e % 2 == 0)
    def _():
        step(xin0_ref, xin1_ref)

    @pl.when(e % 2 == 1)
    def _():
        step(xin1_ref, xin0_ref)


def _moe(idx, h2, wg, wu, wd):
    B = h2.shape[0]
    n = N_EXPERTS * CAPACITY
    rows = SEQ * PACK_ROWS
    pairs = B // PAIR
    wspec = lambda r, c: pl.BlockSpec((None, r, c), lambda b, e: (e, 0, 0))
    stage = pltpu.VMEM((PAIR * CAPACITY * PACK_ROWS, LANES), jnp.uint32)
    out = pl.pallas_call(
        _moe_kernel,
        grid=(pairs, N_EXPERTS),
        in_specs=[pl.BlockSpec((None, 1, PAIR * n), lambda b, e: (b, 0, 0),
                               memory_space=pltpu.SMEM),
                  pl.BlockSpec((None, PAIR, rows, LANES), lambda b, e: (b, 0, 0, 0)),
                  wspec(D_MODEL, D_EXPERT), wspec(D_MODEL, D_EXPERT), wspec(D_EXPERT, D_MODEL)],
        out_specs=pl.BlockSpec((None, PAIR, CAPACITY * PACK_ROWS, LANES),
                               lambda b, e: (b, 0, e, 0)),
        out_shape=jax.ShapeDtypeStruct((pairs, PAIR, n * PACK_ROWS, LANES), jnp.uint32),
        scratch_shapes=[stage, stage],
        compiler_params=_params(("parallel", "arbitrary"), VMEM_LIMIT),
        name="moe",
    )(idx.reshape(pairs, 1, PAIR * n), h2.reshape(pairs, PAIR, rows, LANES), wg, wu, wd)
    return out.reshape(B, n * PACK_ROWS, LANES)


COMBINE_EXPERTS = 8
SCATTER_UNROLL = 8
COMBINE_ROWS = 256


def _combine_kernel(idx_ref, gate_ref, y_ref, x1_ref, mod_ref, o_ref, acc_ref):
    j = pl.program_id(1)
    slots = COMBINE_EXPERTS * CAPACITY
    base = j * slots

    @pl.when(j == 0)
    def _():
        acc_ref[...] = jnp.zeros_like(acc_ref)

    upper = lax.broadcasted_iota(jnp.int32, (SUBLANES, LANES), 0) < PACK_ROWS
    for g in range(slots // SCATTER_UNROLL):
        new = []
        for u in range(0, SCATTER_UNROLL, 2):
            r = g * SCATTER_UNROLL + u
            words = y_ref[pl.ds(r * PACK_ROWS, SUBLANES), :]
            hi = pltpu.bitcast(words & jnp.uint32(0xFFFF0000), F32)
            lo = pltpu.bitcast(words << 16, F32)
            slabs = (jnp.where(upper, hi, pltpu.roll(lo, PACK_ROWS, axis=0)),
                     jnp.where(upper, pltpu.roll(hi, PACK_ROWS, axis=0), lo))
            for k in range(2):
                dst = pl.multiple_of(idx_ref[0, base + r + k] * (ROW_SLAB // PACK_ROWS), ROW_SLAB)
                new.append((dst, acc_ref[pl.ds(dst, ROW_SLAB), :]
                            + slabs[k] * gate_ref[0, base + r + k]))
        for dst, val in new:
            acc_ref[pl.ds(dst, ROW_SLAB), :] = val

    @pl.when(j == pl.num_programs(1) - 1)
    def _():
        for rb in range(SEQ // COMBINE_ROWS):
            rows = pl.ds(rb * COMBINE_ROWS, COMBINE_ROWS)
            for c in range(ROW_SLAB):
                cols = slice(c * LANES, (c + 1) * LANES)
                chunk = acc_ref[pl.ds(rb * COMBINE_ROWS * ROW_SLAB + c, COMBINE_ROWS,
                                      stride=ROW_SLAB), :]
                o_ref[rows, cols] = x1_ref[rows, cols] + mod_ref[5:6, cols] * chunk


def _combine(idx, gate, y, x1, mod):
    B = x1.shape[0]
    n = N_EXPERTS * CAPACITY
    tok = pl.BlockSpec((None, SEQ, D_MODEL), lambda b, j: (b, 0, 0))
    smem = pl.BlockSpec((None, 1, n), lambda b, j: (b, 0, 0), memory_space=pltpu.SMEM)
    return pl.pallas_call(
        _combine_kernel,
        grid=(B, N_EXPERTS // COMBINE_EXPERTS),
        in_specs=[smem, smem,
                  pl.BlockSpec((None, COMBINE_EXPERTS * CAPACITY * PACK_ROWS, LANES),
                               lambda b, j: (b, j, 0)),
                  tok,
                  pl.BlockSpec((None, N_ADA, D_MODEL), lambda b, j: (b, 0, 0))],
        out_specs=tok,
        out_shape=jax.ShapeDtypeStruct((B, SEQ, D_MODEL), F32),
        scratch_shapes=[pltpu.VMEM((SEQ * ROW_SLAB, LANES), F32)],
        compiler_params=_params(("parallel", "arbitrary"), VMEM_LIMIT),
        name="combine",
    )(idx, gate, y, x1, mod)


def _head_perm():
    perm = []
    for r in range(Q_PER_KV):
        for kv in range(N_KV_HEADS):
            h = kv * Q_PER_KV + r
            perm.extend(range(h * HEAD_DIM, (h + 1) * HEAD_DIM))
    return np.asarray(perm, dtype=np.int32)


def kernel(x, c, rel_bias, w_ada, b_ada, norm_mix_g, norm_ffn_g, w_in, w_fourier, b_fourier,
           q_norm_g, k_norm_g, sink, w_out, w_router, w_gate, w_up, w_down):
    B = x.shape[0]
    perm = _head_perm()
    l = 0
    mod = _ada(c, w_ada[l], b_ada[l]).reshape(B, N_ADA, D_MODEL)
    pq = _fold(w_fourier[l])
    bias = _bias_table(rel_bias)

    wi = w_in[l]
    q_cols = wi[:, FOURIER_WIDTH:FOURIER_WIDTH + ATTN_WIDTH][:, perm]
    win = jnp.concatenate([wi[:, :FOURIER_WIDTH], q_cols, wi[:, FOURIER_WIDTH + ATTN_WIDTH:]],
                          axis=1).astype(BF16)
    gq = (jnp.tile(q_norm_g[l], N_Q_HEADS) * (HEAD_DIM ** -0.5 * LOG2E)).reshape(1, ATTN_WIDTH)
    gk = jnp.tile(k_norm_g[l], N_KV_HEADS).reshape(1, KV_WIDTH)
    a, b, q, k, v, experts = _inproj(x, mod, norm_mix_g[l].reshape(1, D_MODEL), win, pq, gq, gk,
                                     w_gate[l], w_up[l], w_down[l])

    yf = _fourier(a, b, b_fourier[l].reshape(1, FOURIER_WIDTH))
    ya = _attn(sink[l], q, k, v, bias)

    wo = w_out[l]
    wf = wo[:FOURIER_WIDTH].astype(BF16)
    wa = wo[FOURIER_WIDTH:][perm].astype(BF16)
    w_hi = w_router[l].astype(BF16)
    w_lo = (w_router[l] - w_hi.astype(F32)).astype(BF16)
    wrh = jnp.pad(jnp.concatenate([w_hi, w_lo], axis=1), ((0, 0), (0, LANES - 2 * N_EXPERTS)))
    x1, h2, aff = _outproj(yf, ya, x, mod, norm_ffn_g[l].reshape(1, D_MODEL), wf, wa, wrh)

    idx, gate = _route(aff)
    n = N_EXPERTS * CAPACITY
    y = _moe(idx, h2, *experts)
    return _combine(idx, gate, y, x1, mod)
```

```python
import functools
import math

import numpy as np
import jax
import jax.numpy as jnp
from jax import lax
from jax.experimental import pallas as pl
from jax.experimental.pallas import tpu as pltpu

D_MODEL = 1024
SEQ = 2048
HEAD_DIM = 64
FOURIER_WIDTH = 512
ATTN_WIDTH = 512
N_GROUPS = 8
N_Q_HEADS = 8
Q_PER_KV = 4
N_KV_HEADS = 2
KV_WIDTH = 128
IN_PROJ_WIDTH = 1280
WINDOW = 128
BLOCK = 128
SPAN = BLOCK + 2 * WINDOW
N_BLOCKS = SEQ // BLOCK
N_BUCKETS = 32
MAX_DISTANCE = 128
N_EXPERTS = 16
CAPACITY = 2 * SEQ // N_EXPERTS
D_EXPERT = 1024
N_ADA = 6
EPS = 1e-6

LANES = 128
SUBLANES = 8
ROW_SLAB = D_MODEL // LANES
PACK_ROWS = ROW_SLAB // 2
VMEM_LIMIT = 56 * 1024 * 1024

F32 = jnp.float32
BF16 = jnp.bfloat16
NEG_INF = float("-inf")
LOG2E = math.log2(math.e)


def _params(sem, vmem=None):
    return pltpu.CompilerParams(dimension_semantics=sem, vmem_limit_bytes=vmem)


@functools.lru_cache(maxsize=None)
def _chan_dft():
    c = np.arange(HEAD_DIM, dtype=np.int64)
    ph = (c[:, None] * c[None, :]) % HEAD_DIM
    ang = 2.0 * np.pi * ph.astype(np.float64) / HEAD_DIM
    sc = 1.0 / math.sqrt(HEAD_DIM)
    eye = np.eye(N_GROUPS)
    cbd = np.kron(eye, np.cos(ang) * sc)
    sbd = np.kron(eye, np.sin(ang) * sc)
    return cbd.astype(np.float32), sbd.astype(np.float32)


@functools.lru_cache(maxsize=None)
def _bucket_table():
    rel = np.arange(SPAN)[None, :] - WINDOW - np.arange(BLOCK)[:, None]
    half = N_BUCKETS // 2
    max_exact = half // 2
    n = np.abs(rel)
    nf = np.maximum(n, 1).astype(np.float64)
    large = max_exact + (np.log(nf / max_exact) / math.log(MAX_DISTANCE / max_exact)
                         * (half - max_exact)).astype(np.int64)
    sq = np.maximum(n.astype(np.int64) ** 2 // (max_exact * max_exact), 1)
    large_int = max_exact + np.floor(np.log2(sq.astype(np.float64)) + 1e-9).astype(np.int64)
    assert np.array_equal(np.where(n >= max_exact, large, 0), np.where(n >= max_exact, large_int, 0))
    large = np.minimum(large, half - 1)
    bucket = np.where(rel > 0, half, 0) + np.where(n < max_exact, n, large)
    return bucket.astype(np.int32)


@functools.lru_cache(maxsize=None)
def _group_ones(width):
    return np.kron(np.eye(width // HEAD_DIM), np.ones((HEAD_DIM, HEAD_DIM))).astype(np.float32)


@functools.lru_cache(maxsize=None)
def _tri_incl():
    i = np.arange(LANES)
    return (i[:, None] <= i[None, :]).astype(np.float32)


def _ada_kernel(c_ref, w_ref, b_ref, o_ref):
    c = c_ref[...]
    ca = c * (1.0 / (1.0 + jnp.exp(-c)))
    o_ref[...] = jnp.dot(ca, w_ref[...], precision=lax.Precision.HIGHEST,
                         preferred_element_type=F32) + b_ref[...]


def _ada(c, w_ada, b_ada):
    B = c.shape[0]
    n = N_ADA * D_MODEL
    tn = D_MODEL
    return pl.pallas_call(
        _ada_kernel,
        grid=(n // tn,),
        in_specs=[pl.BlockSpec((B, D_MODEL), lambda j: (0, 0)),
                  pl.BlockSpec((D_MODEL, tn), lambda j: (0, j)),
                  pl.BlockSpec((1, tn), lambda j: (0, j))],
        out_specs=pl.BlockSpec((B, tn), lambda j: (0, j)),
        out_shape=jax.ShapeDtypeStruct((B, n), F32),
        compiler_params=_params(("arbitrary",)),
        name="ada",
    )(c, w_ada, b_ada.reshape(1, n))


def _fold_kernel(cbd_ref, sbd_ref, w_ref, o_ref):
    w = w_ref[...]
    o_ref[:, :FOURIER_WIDTH] = jnp.dot(cbd_ref[...], w, precision=lax.Precision.HIGHEST,
                                       preferred_element_type=F32).astype(BF16)
    o_ref[:, FOURIER_WIDTH:] = jnp.dot(sbd_ref[...], w, precision=lax.Precision.HIGHEST,
                                       preferred_element_type=F32).astype(BF16)


def _fold(w_fourier):
    wbd = (jnp.eye(N_GROUPS, dtype=F32)[:, None, :, None] * w_fourier[:, :, None, :]
           ).reshape(FOURIER_WIDTH, FOURIER_WIDTH)
    cbd, sbd = _chan_dft()
    return pl.pallas_call(
        _fold_kernel,
        out_shape=jax.ShapeDtypeStruct((FOURIER_WIDTH, 2 * FOURIER_WIDTH), BF16),
        name="fold",
    )(jnp.asarray(cbd), jnp.asarray(sbd), wbd)


def _bias_kernel(rb_ref, bucket_ref, o_ref):
    h = pl.program_id(0)
    bk = bucket_ref[...]
    acc = jnp.zeros((BLOCK, SPAN), F32)
    for b in range(N_BUCKETS):
        acc = jnp.where(bk == b, rb_ref[b, h], acc)
    j = lax.broadcasted_iota(jnp.int32, (BLOCK, SPAN), 1)
    q = lax.broadcasted_iota(jnp.int32, (BLOCK, SPAN), 0)
    band = jnp.abs(j - WINDOW - q) <= WINDOW
    base = jnp.where(band, acc * LOG2E, NEG_INF)
    o_ref[0] = jnp.where(j >= WINDOW, base, NEG_INF)
    o_ref[1] = base
    o_ref[2] = jnp.where(j < WINDOW + BLOCK, base, NEG_INF)


def _bias_table(rel_bias):
    return pl.pallas_call(
        _bias_kernel,
        grid=(N_Q_HEADS,),
        in_specs=[pl.BlockSpec(memory_space=pltpu.SMEM),
                  pl.BlockSpec((BLOCK, SPAN), lambda h: (0, 0))],
        out_specs=pl.BlockSpec((3, BLOCK, SPAN), lambda h: (0, h % Q_PER_KV, h // Q_PER_KV)),
        out_shape=jax.ShapeDtypeStruct((3, Q_PER_KV * BLOCK, N_KV_HEADS * SPAN), F32),
        compiler_params=_params(("arbitrary",)),
        name="bias",
    )(rel_bias, jnp.asarray(_bucket_table()))


IN_ROWS = 256


def _inproj_kernel(x_ref, mod_ref, g_ref, win_ref, pq_ref, bdq_ref, bdk_ref, gq_ref, gk_ref,
                   wg_ref, wu_ref, wd_ref,
                   a_ref, b_ref, q_ref, k_ref, v_ref, w_out):
    w_out[0] = wg_ref[...].astype(BF16)
    w_out[1] = wu_ref[...].astype(BF16)
    w_out[2] = wd_ref[...].astype(BF16)
    gain = g_ref[...] * (1.0 + mod_ref[1:2, :])
    shift = mod_ref[0:1, :]
    q0 = FOURIER_WIDTH
    k0 = q0 + ATTN_WIDTH
    v0 = k0 + KV_WIDTH
    for c in range(x_ref.shape[0] // IN_ROWS):
        rows = pl.ds(c * IN_ROWS, IN_ROWS)
        x = x_ref[rows, :]
        ms = jnp.mean(x * x, axis=-1, keepdims=True)
        h = x * lax.rsqrt(ms + EPS) * gain + shift
        proj = jnp.dot(h.astype(BF16), win_ref[...], preferred_element_type=F32)
        uf = proj[:, :FOURIER_WIDTH].astype(BF16)
        ab = jnp.dot(uf, pq_ref[...], preferred_element_type=F32)
        a_ref[rows, :] = ab[:, :FOURIER_WIDTH].astype(BF16)
        b_ref[rows, :] = ab[:, FOURIER_WIDTH:].astype(BF16)
        q = proj[:, q0:k0]
        ssq = jnp.dot((q * q).astype(BF16), bdq_ref[...], preferred_element_type=F32)
        q_ref[rows, :] = (q * lax.rsqrt(ssq * (1.0 / HEAD_DIM) + EPS) * gq_ref[...]).astype(BF16)
        k = proj[:, k0:v0]
        ssk = jnp.dot((k * k).astype(BF16), bdk_ref[...], preferred_element_type=F32)
        k_ref[rows, :] = (k * lax.rsqrt(ssk * (1.0 / HEAD_DIM) + EPS) * gk_ref[...]).astype(BF16)
        v_ref[rows, :] = proj[:, v0:].astype(BF16)


def _inproj(x, mod, g, win, pq, gq, gk, w_gate, w_up, w_down, tm=1024):
    B = x.shape[0]
    steps_per_batch = SEQ // tm
    w_rows = N_EXPERTS * D_MODEL
    w_blk = w_rows // (B * steps_per_batch)
    const = lambda shape: pl.BlockSpec(shape, lambda b, i: (0,) * len(shape))
    tok = lambda w: pl.BlockSpec((None, tm, w), lambda b, i: (b, i, 0))
    wsl = lambda c: pl.BlockSpec((w_blk, c), lambda b, i: (b * steps_per_batch + i, 0))
    sds = lambda w: jax.ShapeDtypeStruct((B, SEQ, w), BF16)
    step = lambda b, i: b * steps_per_batch + i
    a, b, q, k, v, experts = pl.pallas_call(
        _inproj_kernel,
        grid=(B, steps_per_batch),
        in_specs=[tok(D_MODEL),
                  pl.BlockSpec((None, N_ADA, D_MODEL), lambda b, i: (b, 0, 0)),
                  const((1, D_MODEL)),
                  const((D_MODEL, IN_PROJ_WIDTH)),
                  const((FOURIER_WIDTH, 2 * FOURIER_WIDTH)),
                  const((ATTN_WIDTH, ATTN_WIDTH)),
                  const((KV_WIDTH, KV_WIDTH)),
                  const((1, ATTN_WIDTH)),
                  const((1, KV_WIDTH)),
                  wsl(D_EXPERT), wsl(D_EXPERT), wsl(D_MODEL)],
        out_specs=[tok(FOURIER_WIDTH), tok(FOURIER_WIDTH), tok(ATTN_WIDTH), tok(KV_WIDTH),
                   tok(KV_WIDTH),
                   pl.BlockSpec((3, w_blk, D_MODEL), lambda b, i: (0, step(b, i), 0))],
        out_shape=[sds(FOURIER_WIDTH), sds(FOURIER_WIDTH), sds(ATTN_WIDTH), sds(KV_WIDTH),
                   sds(KV_WIDTH), jax.ShapeDtypeStruct((3, w_rows, D_MODEL), BF16)],
        compiler_params=_params(("parallel", "parallel"), VMEM_LIMIT),
        name="inproj",
    )(x, mod, g, win, pq, jnp.asarray(_group_ones(ATTN_WIDTH)).astype(BF16),
      jnp.asarray(_group_ones(KV_WIDTH)).astype(BF16), gq, gk,
      w_gate.reshape(w_rows, D_EXPERT), w_up.reshape(w_rows, D_EXPERT),
      w_down.reshape(N_EXPERTS * D_EXPERT, D_MODEL))
    return a, b, q, k, v, experts.reshape(3, N_EXPERTS, D_MODEL, D_MODEL)


HALF = SEQ // 2
REV = 128
REV_BLOCKS = HALF // REV
FOURIER_ROWS = 512


def _fourier_kernel(cm_ref, sm_ref, psh_ref, alt_ref, altrow_ref, a_ref, b_ref, bf_ref, o_ref):
    psh = psh_ref[...]

    def reversed_block(win_lo, k, src):
        if k == 0:
            return jnp.dot(psh[:, :REV], src(win_lo, REV), preferred_element_type=F32)
        return jnp.dot(psh, src(win_lo, 2 * REV), preferred_element_type=F32)

    def folded(ref, sign):
        blocks = []
        for k in range(REV_BLOCKS):
            lo = SEQ - REV * (k + 1)
            rev = reversed_block(lo, k, lambda s, n: ref[pl.ds(s, n), :])
            blocks.append((ref[pl.ds(k * REV, REV), :].astype(F32) + sign * rev).astype(BF16))
        return jnp.concatenate(blocks, axis=0)

    a_even = folded(a_ref, 1.0)
    b_odd = folded(b_ref, -1.0)
    a_mid = a_ref[pl.ds(HALF, 1), :].astype(F32)
    bias = bf_ref[...]
    z_blocks = []
    for i in range(HALF // FOURIER_ROWS):
        rows = pl.ds(i * FOURIER_ROWS, FOURIER_ROWS)
        yc = jnp.dot(cm_ref[rows, :], a_even, preferred_element_type=F32)
        yc = yc + alt_ref[rows, :] * a_mid + bias
        ys = jnp.dot(sm_ref[rows, :], b_odd, preferred_element_type=F32)
        o_ref[rows, :] = (yc - ys).astype(BF16)
        z_blocks.append((yc + ys).astype(BF16))
    z = jnp.concatenate(z_blocks, axis=0)
    for k in range(REV_BLOCKS):
        lo = HALF - REV * (k + 1)
        top = reversed_block(lo, k, lambda s, n: z[s:s + n])
        o_ref[pl.ds(HALF + k * REV, REV), :] = top.astype(BF16)
    y_mid = jnp.dot(altrow_ref[...], a_ref[...], preferred_element_type=F32)[0:1, :] + bias
    o_ref[pl.ds(HALF, 1), :] = y_mid.astype(BF16)


@functools.lru_cache(maxsize=None)
def _fold_tables():
    s = np.arange(HALF, dtype=np.int64)
    ph = (s[:, None] * s[None, :]) % SEQ
    ang = 2.0 * np.pi * ph.astype(np.float64) / SEQ
    sc = 1.0 / math.sqrt(SEQ)
    cm = (np.cos(ang) * sc).astype(np.float32)
    sm = (np.sin(ang) * sc).astype(np.float32)
    psh = np.zeros((REV, 2 * REV), np.float32)
    psh[np.arange(REV), REV - np.arange(REV)] = 1.0
    alt = (np.where(s % 2 == 0, 1.0, -1.0) * sc).astype(np.float32).reshape(HALF, 1)
    t = np.arange(SEQ)
    altrow = np.zeros((SUBLANES, SEQ), np.float32)
    altrow[0] = np.where(t % 2 == 0, 1.0, -1.0) * sc
    return cm, sm, psh, alt, altrow


def _fourier(a, b, bf):
    B = a.shape[0]
    cm, sm, psh, alt, altrow = _fold_tables()
    tok = pl.BlockSpec((None, SEQ, FOURIER_WIDTH), lambda i: (i, 0, 0))
    const = lambda shape: pl.BlockSpec(shape, lambda i: (0,) * len(shape))
    return pl.pallas_call(
        _fourier_kernel,
        grid=(B,),
        in_specs=[const((HALF, HALF)), const((HALF, HALF)), const((REV, 2 * REV)),
                  const((HALF, 1)), const((SUBLANES, SEQ)),
                  tok, tok, const((1, FOURIER_WIDTH))],
        out_specs=tok,
        out_shape=jax.ShapeDtypeStruct((B, SEQ, FOURIER_WIDTH), BF16),
        compiler_params=_params(("parallel",), VMEM_LIMIT),
        name="fourier",
    )(jnp.asarray(cm).astype(BF16), jnp.asarray(sm).astype(BF16), jnp.asarray(psh).astype(BF16),
      jnp.asarray(alt), jnp.asarray(altrow).astype(BF16), a, b, bf)


ATT_SUB = 16
ATT_ROWS = ATT_SUB * BLOCK
ATT_STEPS = N_BLOCKS // ATT_SUB


def _attn_kernel(sink_ref, q_ref, kl_ref, km_ref, kr_ref, vl_ref, vm_ref, vr_ref, bias_ref,
                 o_ref):
    i = pl.program_id(1)
    keys = jnp.concatenate([kl_ref[...], km_ref[...], kr_ref[...]], axis=0)
    vals = jnp.concatenate([vl_ref[...], vm_ref[...], vr_ref[...]], axis=0)
    first_k = lax.broadcasted_iota(jnp.int32, keys.shape, 1) < HEAD_DIM
    first_q = lax.broadcasted_iota(jnp.int32, (Q_PER_KV * BLOCK, LANES), 1) < HEAD_DIM
    row_head = lax.broadcasted_iota(jnp.int32, (Q_PER_KV * BLOCK, 1), 0) // BLOCK
    zero = jnp.zeros_like(keys)
    ones_lo = jnp.where(first_k, 1.0, 0.0).astype(BF16)
    ones_hi = jnp.where(first_k, 0.0, 1.0).astype(BF16)
    keys_kv = [jnp.where(first_k, keys, zero), jnp.where(first_k, zero, keys)]
    vals_kv = [jnp.concatenate([jnp.where(first_k, vals, zero), ones_lo], axis=1),
               jnp.concatenate([jnp.where(first_k, zero, vals), ones_hi], axis=1)]
    sinks = []
    for kv in range(N_KV_HEADS):
        sink = jnp.zeros((Q_PER_KV * BLOCK, 1), F32)
        for r in range(Q_PER_KV):
            sink = jnp.where(row_head == r, sink_ref[kv * Q_PER_KV + r] * LOG2E, sink)
        sinks.append(sink)
    for j in range(ATT_SUB):
        variant = 1
        if j == 0:
            variant = jnp.where(i == 0, 0, variant)
        if j == ATT_SUB - 1:
            variant = jnp.where(i == ATT_STEPS - 1, 2, variant)
        qrows = pl.ds(j * BLOCK, BLOCK)
        krows = slice(j * BLOCK, j * BLOCK + SPAN)
        qs = jnp.concatenate([q_ref[qrows, r * LANES:(r + 1) * LANES] for r in range(Q_PER_KV)],
                             axis=0)
        keys2 = jnp.concatenate([keys_kv[0][krows], keys_kv[1][krows]], axis=0)
        vals2 = jnp.concatenate([vals_kv[0][krows], vals_kv[1][krows]], axis=0)
        logits = lax.dot_general(qs, keys2, (((1,), (1,)), ((), ())),
                                 preferred_element_type=F32)
        logits = logits + bias_ref[variant]
        ms = [jnp.maximum(jnp.max(logits[:, kv * SPAN:(kv + 1) * SPAN], axis=-1, keepdims=True),
                          sinks[kv]) for kv in range(N_KV_HEADS)]
        p = jnp.exp2(jnp.concatenate([logits[:, kv * SPAN:(kv + 1) * SPAN] - ms[kv]
                                      for kv in range(N_KV_HEADS)], axis=1).astype(BF16))
        pv = jnp.dot(p, vals2, preferred_element_type=F32)
        denom = pv[:, LANES:] + jnp.exp2(jnp.where(first_q, sinks[0] - ms[0], sinks[1] - ms[1]))
        out = (pv[:, :LANES] / denom).astype(BF16)
        for r in range(Q_PER_KV):
            o_ref[qrows, r * LANES:(r + 1) * LANES] = out[r * BLOCK:(r + 1) * BLOCK]


def _attn(sink, q, k, v, bias):
    B = q.shape[0]
    edge = lambda f: pl.BlockSpec((None, BLOCK, KV_WIDTH), lambda b, i: (b, f(i), 0))
    left = lambda i: jnp.maximum(i * ATT_SUB - 1, 0)
    right = lambda i: jnp.minimum((i + 1) * ATT_SUB, N_BLOCKS - 1)
    mid = pl.BlockSpec((None, ATT_ROWS, KV_WIDTH), lambda b, i: (b, i, 0))
    qspec = pl.BlockSpec((None, ATT_ROWS, ATTN_WIDTH), lambda b, i: (b, i, 0))
    return pl.pallas_call(
        _attn_kernel,
        grid=(B, ATT_STEPS),
        in_specs=[pl.BlockSpec(memory_space=pltpu.SMEM), qspec,
                  edge(left), mid, edge(right),
                  edge(left), mid, edge(right),
                  pl.BlockSpec((3, Q_PER_KV * BLOCK, N_KV_HEADS * SPAN), lambda b, i: (0, 0, 0))],
        out_specs=qspec,
        out_shape=jax.ShapeDtypeStruct((B, SEQ, ATTN_WIDTH), BF16),
        compiler_params=_params(("parallel", "parallel"), VMEM_LIMIT),
        name="attn",
    )(sink, q, k, k, k, v, v, v, bias)


OUT_ROWS = 256


def _outproj_kernel(yf_ref, ya_ref, x_ref, mod_ref, g_ref, wo_ref, wrh_ref,
                    x1_ref, h2_ref, aff_ref):
    tm = x_ref.shape[0]
    gain = g_ref[...] * (1.0 + mod_ref[4:5, :])
    shift = mod_ref[3:4, :]
    gate1 = mod_ref[2:3, :]
    lane = lax.broadcasted_iota(jnp.int32, (OUT_ROWS, LANES), 1)
    chunks = range(tm // OUT_ROWS)
    mixed = [jnp.dot(jnp.concatenate([yf_ref[pl.ds(c * OUT_ROWS, OUT_ROWS), :],
                                      ya_ref[pl.ds(c * OUT_ROWS, OUT_ROWS), :]], axis=1),
                     wo_ref[...], preferred_element_type=F32) for c in chunks]
    for c in chunks:
        rows = pl.ds(c * OUT_ROWS, OUT_ROWS)
        x1 = x_ref[rows, :] + gate1 * mixed[c]
        x1_ref[rows, :] = x1
        ms = jnp.mean(x1 * x1, axis=-1, keepdims=True)
        h2 = x1 * lax.rsqrt(ms + EPS) * gain + shift
        hi = h2.astype(BF16)
        top = pltpu.bitcast(hi[:, :D_MODEL // 2].astype(F32), jnp.uint32)
        bot = pltpu.bitcast(hi[:, D_MODEL // 2:].astype(F32), jnp.uint32)
        words = top | (bot >> 16)
        for j in range(PACK_ROWS):
            h2_ref[pl.ds(c * OUT_ROWS * PACK_ROWS + j, OUT_ROWS, stride=PACK_ROWS), :] = (
                words[:, j * LANES:(j + 1) * LANES])
        part = jnp.dot(hi, wrh_ref[...], preferred_element_type=F32)
        logits = part + pltpu.roll(part, LANES - N_EXPERTS, axis=1)
        logits = jnp.where(lane < N_EXPERTS, logits, NEG_INF)
        m = jnp.max(logits, axis=-1, keepdims=True)
        e = jnp.exp(logits - m)
        aff_ref[rows, :] = e / jnp.sum(e, axis=-1, keepdims=True)


def _outproj(yf, ya, x, mod, g, wo, wrh, tm=1024):
    B = x.shape[0]
    const = lambda shape: pl.BlockSpec(shape, lambda b, i: (0,) * len(shape))
    tok = lambda w: pl.BlockSpec((None, tm, w), lambda b, i: (b, i, 0))
    return pl.pallas_call(
        _outproj_kernel,
        grid=(B, SEQ // tm),
        in_specs=[tok(FOURIER_WIDTH), tok(ATTN_WIDTH), tok(D_MODEL),
                  pl.BlockSpec((None, N_ADA, D_MODEL), lambda b, i: (b, 0, 0)),
                  const((1, D_MODEL)),
                  const((D_MODEL, D_MODEL)),
                  const((D_MODEL, LANES))],
        out_specs=[tok(D_MODEL),
                   pl.BlockSpec((None, tm * PACK_ROWS, LANES), lambda b, i: (b, i, 0)),
                   tok(LANES)],
        out_shape=[jax.ShapeDtypeStruct((B, SEQ, D_MODEL), F32),
                   jax.ShapeDtypeStruct((B, SEQ * PACK_ROWS, LANES), jnp.uint32),
                   jax.ShapeDtypeStruct((B, SEQ, LANES), F32)],
        compiler_params=_params(("parallel", "parallel"), VMEM_LIMIT),
        name="outproj",
    )(yf, ya, x, mod, g, wo, wrh)


ROUTE_BATCHES = 2
SEARCH_BITS = 3


def _route_kernel(aff_ref, tri_ref, idx_ref, gate_ref):
    for bb in range(ROUTE_BATCHES):
        _route_one(aff_ref[bb], tri_ref[...], idx_ref.at[bb], gate_ref.at[bb])


def _route_one(aff, tri, idx_ref, gate_ref):
    aff_t = jnp.transpose(aff)[:N_EXPERTS]
    bits = pltpu.bitcast(aff_t, jnp.int32)
    cap = float(CAPACITY)

    t = jnp.zeros((N_EXPERTS, 1), jnp.int32)
    for shift in range(30 - SEARCH_BITS, -1, -SEARCH_BITS):
        digit = jnp.zeros((N_EXPERTS, 1), jnp.int32)
        for k in range(1, 1 << SEARCH_BITS):
            cnt = jnp.sum(jnp.where(bits >= (t | (k << shift)), 1.0, 0.0), axis=1, keepdims=True)
            digit = digit + jnp.where(cnt >= cap, 1, 0)
        t = t | (digit << shift)
    gt = bits > t
    eq = bits == t
    need = cap - jnp.sum(jnp.where(gt, 1.0, 0.0), axis=1, keepdims=True)

    n_chunks = SEQ // LANES

    def prefix(flags_f32):
        outs = []
        carry = jnp.zeros((N_EXPERTS, 1), F32)
        for c in range(n_chunks):
            f = flags_f32[:, c * LANES:(c + 1) * LANES]
            incl = jnp.dot(f.astype(BF16), tri, preferred_element_type=F32)
            outs.append(incl - f + carry)
            carry = carry + jnp.sum(f, axis=1, keepdims=True)
        return jnp.concatenate(outs, axis=1)

    eq_f = jnp.where(eq, 1.0, 0.0)
    eq_rank = prefix(eq_f)
    sel_f = jnp.where(gt, 1.0, jnp.where(eq_rank < need, eq_f, 0.0))
    pos = prefix(sel_f)
    posm = jnp.where(sel_f > 0.0, pos, -1.0)

    hi = aff_t.astype(BF16).astype(F32)
    r1 = aff_t - hi
    mid = r1.astype(BF16).astype(F32)
    lo = r1 - mid
    tok = lax.broadcasted_iota(jnp.int32, (N_EXPERTS, SEQ), 1)
    row = lax.broadcasted_iota(jnp.int32, (N_EXPERTS, SEQ), 0)
    tok_rows = jnp.where(row == 0, (tok >> 6).astype(F32),
                         jnp.where(row == 1, (tok & 63).astype(F32), 0.0))
    vals_t = jnp.concatenate([hi, mid, lo, tok_rows], axis=0).astype(BF16)

    slot = lax.broadcasted_iota(jnp.int32, (CAPACITY, SEQ), 0).astype(F32).astype(BF16)
    posm_b = posm.astype(BF16)
    one_b = jnp.ones((CAPACITY, SEQ), BF16)
    zero_b = jnp.zeros((CAPACITY, SEQ), BF16)
    for e in range(N_EXPERTS):
        onehot = jnp.where(posm_b[e:e + 1, :] == slot, one_b, zero_b)
        res = lax.dot_general(vals_t, onehot, (((1,), (1,)), ((), ())),
                              preferred_element_type=F32)
        cols = pl.ds(e * CAPACITY, CAPACITY)
        tok_idx = res[3 * N_EXPERTS:3 * N_EXPERTS + 1] * 64.0 + res[3 * N_EXPERTS + 1:
                                                                    3 * N_EXPERTS + 2]
        idx_ref[:, cols] = tok_idx.astype(jnp.int32) * PACK_ROWS
        gate_ref[:, cols] = (res[e:e + 1] + res[N_EXPERTS + e:N_EXPERTS + e + 1]
                             + res[2 * N_EXPERTS + e:2 * N_EXPERTS + e + 1])


def _route(aff):
    B = aff.shape[0]
    n = N_EXPERTS * CAPACITY
    return pl.pallas_call(
        _route_kernel,
        grid=(B // ROUTE_BATCHES,),
        in_specs=[pl.BlockSpec((ROUTE_BATCHES, SEQ, LANES), lambda b: (b, 0, 0)),
                  pl.BlockSpec((LANES, LANES), lambda b: (0, 0))],
        out_specs=[pl.BlockSpec((ROUTE_BATCHES, 1, n), lambda b: (b, 0, 0)),
                   pl.BlockSpec((ROUTE_BATCHES, 1, n), lambda b: (b, 0, 0))],
        out_shape=[jax.ShapeDtypeStruct((B, 1, n), jnp.int32),
                   jax.ShapeDtypeStruct((B, 1, n), F32)],
        compiler_params=_params(("parallel",), VMEM_LIMIT),
        name="route",
    )(aff, jnp.asarray(_tri_incl()).astype(BF16))


PAIR = 2


def _moe_kernel(idx_ref, h2_ref, w_ref, y_ref, xin0_ref, xin1_ref):
    e = pl.program_id(1)
    last = N_EXPERTS - 1
    n = N_EXPERTS * CAPACITY
    rows = PAIR * CAPACITY

    def gather_rows(ex, dst_ref):
        for bb in range(PAIR):
            base = bb * n + ex * CAPACITY
            for p in range(CAPACITY):
                off = idx_ref[0, base + p]
                tile = h2_ref[bb, pl.ds(pl.multiple_of((off >> 3) << 3, SUBLANES), SUBLANES), :]
                tile = pltpu.roll(tile, off & PACK_ROWS, axis=0)
                dst_ref[pl.ds((bb * CAPACITY + p) * PACK_ROWS, PACK_ROWS), :] = tile[:PACK_ROWS]

    def expert(xin_ref):
        words = [xin_ref[pl.ds(j, rows, stride=PACK_ROWS), :] for j in range(PACK_ROWS)]
        xin = jnp.concatenate(
            [pltpu.bitcast(w & jnp.uint32(0xFFFF0000), F32).astype(BF16) for w in words]
            + [pltpu.bitcast(w << 16, F32).astype(BF16) for w in words], axis=1)
        for bb in range(PAIR):
            xb = xin[bb * CAPACITY:(bb + 1) * CAPACITY]
            a = jnp.dot(xb, w_ref[0], preferred_element_type=F32)
            u = jnp.dot(xb, w_ref[1], preferred_element_type=F32)
            hmid = (a * (1.0 / (1.0 + jnp.exp(-a))) * u).astype(BF16)
            y = jnp.dot(hmid, w_ref[2], preferred_element_type=F32).astype(BF16)
            top = pltpu.bitcast(y[:, :D_MODEL // 2].astype(F32), jnp.uint32)
            bot = pltpu.bitcast(y[:, D_MODEL // 2:].astype(F32), jnp.uint32)
            words = top | (bot >> 16)
            for j in range(PACK_ROWS):
                y_ref[bb, pl.ds(j, CAPACITY, stride=PACK_ROWS), :] = (
                    words[:, j * LANES:(j + 1) * LANES])

    @pl.when(e == 0)
    def _():
        gather_rows(0, xin0_ref)

    def step(xin_cur, xin_nxt):
        gather_rows(jnp.minimum(e + 1, last), xin_nxt)
        expert(xin_cur)

    @pl.when(e % 2 == 0)
    def _():
        step(xin0_ref, xin1_ref)

    @pl.when(e % 2 == 1)
    def _():
        step(xin1_ref, xin0_ref)


def _moe(idx, h2, experts):
    B = h2.shape[0]
    n = N_EXPERTS * CAPACITY
    rows = SEQ * PACK_ROWS
    pairs = B // PAIR
    stage = pltpu.VMEM((PAIR * CAPACITY * PACK_ROWS, LANES), jnp.uint32)
    out = pl.pallas_call(
        _moe_kernel,
        grid=(pairs, N_EXPERTS),
        in_specs=[pl.BlockSpec((None, 1, PAIR * n), lambda b, e: (b, 0, 0),
                               memory_space=pltpu.SMEM),
                  pl.BlockSpec((None, PAIR, rows, LANES), lambda b, e: (b, 0, 0, 0)),
                  pl.BlockSpec((3, None, D_MODEL, D_MODEL), lambda b, e: (0, e, 0, 0))],
        out_specs=pl.BlockSpec((None, PAIR, CAPACITY * PACK_ROWS, LANES),
                               lambda b, e: (b, 0, e, 0)),
        out_shape=jax.ShapeDtypeStruct((pairs, PAIR, n * PACK_ROWS, LANES), jnp.uint32),
        scratch_shapes=[stage, stage],
        compiler_params=_params(("parallel", "arbitrary"), VMEM_LIMIT),
        name="moe",
    )(idx.reshape(pairs, 1, PAIR * n), h2.reshape(pairs, PAIR, rows, LANES), experts)
    return out.reshape(B, n * PACK_ROWS, LANES)


COMBINE_EXPERTS = 8
SCATTER_UNROLL = 8
COMBINE_ROWS = 256


def _combine_kernel(idx_ref, gate_ref, y_ref, x1_ref, mod_ref, o_ref, acc_ref):
    j = pl.program_id(1)
    slots = COMBINE_EXPERTS * CAPACITY
    base = j * slots

    @pl.when(j == 0)
    def _():
        acc_ref[...] = jnp.zeros_like(acc_ref)

    upper = lax.broadcasted_iota(jnp.int32, (SUBLANES, LANES), 0) < PACK_ROWS
    for g in range(slots // SCATTER_UNROLL):
        new = []
        for u in range(0, SCATTER_UNROLL, 2):
            r = g * SCATTER_UNROLL + u
            words = y_ref[pl.ds(r * PACK_ROWS, SUBLANES), :]
            hi = pltpu.bitcast(words & jnp.uint32(0xFFFF0000), F32)
            lo = pltpu.bitcast(words << 16, F32)
            slabs = (jnp.where(upper, hi, pltpu.roll(lo, PACK_ROWS, axis=0)),
                     jnp.where(upper, pltpu.roll(hi, PACK_ROWS, axis=0), lo))
            for k in range(2):
                dst = pl.multiple_of(idx_ref[0, base + r + k] * (ROW_SLAB // PACK_ROWS), ROW_SLAB)
                new.append((dst, acc_ref[pl.ds(dst, ROW_SLAB), :]
                            + slabs[k] * gate_ref[0, base + r + k]))
        for dst, val in new:
            acc_ref[pl.ds(dst, ROW_SLAB), :] = val

    @pl.when(j == pl.num_programs(1) - 1)
    def _():
        for rb in range(SEQ // COMBINE_ROWS):
            rows = pl.ds(rb * COMBINE_ROWS, COMBINE_ROWS)
            for c in range(ROW_SLAB):
                cols = slice(c * LANES, (c + 1) * LANES)
                chunk = acc_ref[pl.ds(rb * COMBINE_ROWS * ROW_SLAB + c, COMBINE_ROWS,
                                      stride=ROW_SLAB), :]
                o_ref[rows, cols] = x1_ref[rows, cols] + mod_ref[5:6, cols] * chunk


def _combine(idx, gate, y, x1, mod):
    B = x1.shape[0]
    n = N_EXPERTS * CAPACITY
    tok = pl.BlockSpec((None, SEQ, D_MODEL), lambda b, j: (b, 0, 0))
    smem = pl.BlockSpec((None, 1, n), lambda b, j: (b, 0, 0), memory_space=pltpu.SMEM)
    return pl.pallas_call(
        _combine_kernel,
        grid=(B, N_EXPERTS // COMBINE_EXPERTS),
        in_specs=[smem, smem,
                  pl.BlockSpec((None, COMBINE_EXPERTS * CAPACITY * PACK_ROWS, LANES),
                               lambda b, j: (b, j, 0)),
                  tok,
                  pl.BlockSpec((None, N_ADA, D_MODEL), lambda b, j: (b, 0, 0))],
        out_specs=tok,
        out_shape=jax.ShapeDtypeStruct((B, SEQ, D_MODEL), F32),
        scratch_shapes=[pltpu.VMEM((SEQ * ROW_SLAB, LANES), F32)],
        compiler_params=_params(("parallel", "arbitrary"), VMEM_LIMIT),
        name="combine",
    )(idx, gate, y, x1, mod)


def _head_perm():
    perm = []
    for r in range(Q_PER_KV):
        for kv in range(N_KV_HEADS):
            h = kv * Q_PER_KV + r
            perm.extend(range(h * HEAD_DIM, (h + 1) * HEAD_DIM))
    return np.asarray(perm, dtype=np.int32)


def kernel(x, c, rel_bias, w_ada, b_ada, norm_mix_g, norm_ffn_g, w_in, w_fourier, b_fourier,
           q_norm_g, k_norm_g, sink, w_out, w_router, w_gate, w_up, w_down):
    B = x.shape[0]
    perm = _head_perm()
    l = 0
    mod = _ada(c, w_ada[l], b_ada[l]).reshape(B, N_ADA, D_MODEL)
    pq = _fold(w_fourier[l])
    bias = _bias_table(rel_bias)

    wi = w_in[l]
    q_cols = wi[:, FOURIER_WIDTH:FOURIER_WIDTH + ATTN_WIDTH][:, perm]
    win = jnp.concatenate([wi[:, :FOURIER_WIDTH], q_cols, wi[:, FOURIER_WIDTH + ATTN_WIDTH:]],
                          axis=1).astype(BF16)
    gq = (jnp.tile(q_norm_g[l], N_Q_HEADS) * (HEAD_DIM ** -0.5 * LOG2E)).reshape(1, ATTN_WIDTH)
    gk = jnp.tile(k_norm_g[l], N_KV_HEADS).reshape(1, KV_WIDTH)
    a, b, q, k, v, experts = _inproj(x, mod, norm_mix_g[l].reshape(1, D_MODEL), win, pq, gq, gk,
                                     w_gate[l], w_up[l], w_down[l])

    yf = _fourier(a, b, b_fourier[l].reshape(1, FOURIER_WIDTH))
    ya = _attn(sink[l], q, k, v, bias)

    wo = w_out[l]
    wo = jnp.concatenate([wo[:FOURIER_WIDTH], wo[FOURIER_WIDTH:][perm]], axis=0).astype(BF16)
    w_hi = w_router[l].astype(BF16)
    w_lo = (w_router[l] - w_hi.astype(F32)).astype(BF16)
    wrh = jnp.pad(jnp.concatenate([w_hi, w_lo], axis=1), ((0, 0), (0, LANES - 2 * N_EXPERTS)))
    x1, h2, aff = _outproj(yf, ya, x, mod, norm_ffn_g[l].reshape(1, D_MODEL), wo, wrh)

    idx, gate = _route(aff)
    n = N_EXPERTS * CAPACITY
    y = _moe(idx, h2, experts)
    return _combine(idx, gate, y, x1, mod)
```

```python
import functools
import math

import numpy as np
import jax
import jax.numpy as jnp
from jax import lax
from jax.experimental import pallas as pl
from jax.experimental.pallas import tpu as pltpu

D_MODEL = 1024
SEQ = 2048
HEAD_DIM = 64
FOURIER_WIDTH = 512
ATTN_WIDTH = 512
N_GROUPS = 8
N_Q_HEADS = 8
Q_PER_KV = 4
N_KV_HEADS = 2
KV_WIDTH = 128
IN_PROJ_WIDTH = 1280
WINDOW = 128
BLOCK = 128
SPAN = BLOCK + 2 * WINDOW
N_BLOCKS = SEQ // BLOCK
N_BUCKETS = 32
MAX_DISTANCE = 128
N_EXPERTS = 16
CAPACITY = 2 * SEQ // N_EXPERTS
D_EXPERT = 1024
N_ADA = 6
EPS = 1e-6

LANES = 128
SUBLANES = 8
ROW_SLAB = D_MODEL // LANES
PACK_ROWS = ROW_SLAB // 2
VMEM_LIMIT = 56 * 1024 * 1024

F32 = jnp.float32
BF16 = jnp.bfloat16
NEG_INF = float("-inf")
LOG2E = math.log2(math.e)


def _params(sem, vmem=None):
    return pltpu.CompilerParams(dimension_semantics=sem, vmem_limit_bytes=vmem)


@functools.lru_cache(maxsize=None)
def _chan_dft():
    c = np.arange(HEAD_DIM, dtype=np.int64)
    ph = (c[:, None] * c[None, :]) % HEAD_DIM
    ang = 2.0 * np.pi * ph.astype(np.float64) / HEAD_DIM
    sc = 1.0 / math.sqrt(HEAD_DIM)
    eye = np.eye(N_GROUPS)
    cbd = np.kron(eye, np.cos(ang) * sc)
    sbd = np.kron(eye, np.sin(ang) * sc)
    return cbd.astype(np.float32), sbd.astype(np.float32)


@functools.lru_cache(maxsize=None)
def _bucket_table():
    rel = np.arange(SPAN)[None, :] - WINDOW - np.arange(BLOCK)[:, None]
    half = N_BUCKETS // 2
    max_exact = half // 2
    n = np.abs(rel)
    nf = np.maximum(n, 1).astype(np.float64)
    large = max_exact + (np.log(nf / max_exact) / math.log(MAX_DISTANCE / max_exact)
                         * (half - max_exact)).astype(np.int64)
    sq = np.maximum(n.astype(np.int64) ** 2 // (max_exact * max_exact), 1)
    large_int = max_exact + np.floor(np.log2(sq.astype(np.float64)) + 1e-9).astype(np.int64)
    assert np.array_equal(np.where(n >= max_exact, large, 0), np.where(n >= max_exact, large_int, 0))
    large = np.minimum(large, half - 1)
    bucket = np.where(rel > 0, half, 0) + np.where(n < max_exact, n, large)
    return bucket.astype(np.int32)


@functools.lru_cache(maxsize=None)
def _group_ones(width):
    return np.kron(np.eye(width // HEAD_DIM), np.ones((HEAD_DIM, HEAD_DIM))).astype(np.float32)


@functools.lru_cache(maxsize=None)
def _tri_incl():
    i = np.arange(LANES)
    return (i[:, None] <= i[None, :]).astype(np.float32)


def _ada_kernel(c_ref, w_ref, b_ref, o_ref):
    c = c_ref[...]
    ca = c * (1.0 / (1.0 + jnp.exp(-c)))
    o_ref[...] = jnp.dot(ca, w_ref[...], precision=lax.Precision.HIGHEST,
                         preferred_element_type=F32) + b_ref[...]


def _ada(c, w_ada, b_ada):
    B = c.shape[0]
    n = N_ADA * D_MODEL
    tn = D_MODEL
    return pl.pallas_call(
        _ada_kernel,
        grid=(n // tn,),
        in_specs=[pl.BlockSpec((B, D_MODEL), lambda j: (0, 0)),
                  pl.BlockSpec((D_MODEL, tn), lambda j: (0, j)),
                  pl.BlockSpec((1, tn), lambda j: (0, j))],
        out_specs=pl.BlockSpec((B, tn), lambda j: (0, j)),
        out_shape=jax.ShapeDtypeStruct((B, n), F32),
        compiler_params=_params(("arbitrary",)),
        name="ada",
    )(c, w_ada, b_ada.reshape(1, n))


def _fold_kernel(cbd_ref, sbd_ref, w_ref, o_ref):
    w = w_ref[...]
    o_ref[:, :FOURIER_WIDTH] = jnp.dot(cbd_ref[...], w, precision=lax.Precision.HIGHEST,
                                       preferred_element_type=F32).astype(BF16)
    o_ref[:, FOURIER_WIDTH:] = jnp.dot(sbd_ref[...], w, precision=lax.Precision.HIGHEST,
                                       preferred_element_type=F32).astype(BF16)


def _fold(w_fourier):
    wbd = (jnp.eye(N_GROUPS, dtype=F32)[:, None, :, None] * w_fourier[:, :, None, :]
           ).reshape(FOURIER_WIDTH, FOURIER_WIDTH)
    cbd, sbd = _chan_dft()
    return pl.pallas_call(
        _fold_kernel,
        out_shape=jax.ShapeDtypeStruct((FOURIER_WIDTH, 2 * FOURIER_WIDTH), BF16),
        name="fold",
    )(jnp.asarray(cbd), jnp.asarray(sbd), wbd)


def _bias_kernel(rb_ref, bucket_ref, o_ref):
    h = pl.program_id(0)
    bk = bucket_ref[...]
    acc = jnp.zeros((BLOCK, SPAN), F32)
    for b in range(N_BUCKETS):
        acc = jnp.where(bk == b, rb_ref[b, h], acc)
    j = lax.broadcasted_iota(jnp.int32, (BLOCK, SPAN), 1)
    q = lax.broadcasted_iota(jnp.int32, (BLOCK, SPAN), 0)
    band = jnp.abs(j - WINDOW - q) <= WINDOW
    base = jnp.where(band, acc * LOG2E, NEG_INF)
    o_ref[0] = jnp.where(j >= WINDOW, base, NEG_INF)
    o_ref[1] = base
    o_ref[2] = jnp.where(j < WINDOW + BLOCK, base, NEG_INF)


def _bias_table(rel_bias):
    return pl.pallas_call(
        _bias_kernel,
        grid=(N_Q_HEADS,),
        in_specs=[pl.BlockSpec(memory_space=pltpu.SMEM),
                  pl.BlockSpec((BLOCK, SPAN), lambda h: (0, 0))],
        out_specs=pl.BlockSpec((3, BLOCK, SPAN), lambda h: (0, h % Q_PER_KV, h // Q_PER_KV)),
        out_shape=jax.ShapeDtypeStruct((3, Q_PER_KV * BLOCK, N_KV_HEADS * SPAN), F32),
        compiler_params=_params(("arbitrary",)),
        name="bias",
    )(rel_bias, jnp.asarray(_bucket_table()))


IN_ROWS = 256


def _inproj_kernel(x_ref, mod_ref, g_ref, win_ref, pq_ref, bdq_ref, bdk_ref, gq_ref, gk_ref,
                   wg_ref, wu_ref, wd_ref,
                   a_ref, b_ref, q_ref, k_ref, v_ref, w_out):
    w_out[0] = wg_ref[...].astype(BF16)
    w_out[1] = wu_ref[...].astype(BF16)
    w_out[2] = wd_ref[...].astype(BF16)
    gain = g_ref[...] * (1.0 + mod_ref[1:2, :])
    shift = mod_ref[0:1, :]
    q0 = FOURIER_WIDTH
    k0 = q0 + ATTN_WIDTH
    v0 = k0 + KV_WIDTH
    for c in range(x_ref.shape[0] // IN_ROWS):
        rows = pl.ds(c * IN_ROWS, IN_ROWS)
        x = x_ref[rows, :]
        ms = jnp.mean(x * x, axis=-1, keepdims=True)
        h = x * lax.rsqrt(ms + EPS) * gain + shift
        proj = jnp.dot(h.astype(BF16), win_ref[...], preferred_element_type=F32)
        uf = proj[:, :FOURIER_WIDTH].astype(BF16)
        ab = jnp.dot(uf, pq_ref[...], preferred_element_type=F32)
        a_ref[rows, :] = ab[:, :FOURIER_WIDTH].astype(BF16)
        b_ref[rows, :] = ab[:, FOURIER_WIDTH:].astype(BF16)
        q = proj[:, q0:k0]
        ssq = jnp.dot((q * q).astype(BF16), bdq_ref[...], preferred_element_type=F32)
        q_ref[rows, :] = (q * lax.rsqrt(ssq * (1.0 / HEAD_DIM) + EPS) * gq_ref[...]).astype(BF16)
        k = proj[:, k0:v0]
        ssk = jnp.dot((k * k).astype(BF16), bdk_ref[...], preferred_element_type=F32)
        k_ref[rows, :] = (k * lax.rsqrt(ssk * (1.0 / HEAD_DIM) + EPS) * gk_ref[...]).astype(BF16)
        v_ref[rows, :] = proj[:, v0:].astype(BF16)


def _inproj(x, mod, g, win, pq, gq, gk, w_gate, w_up, w_down, tm=1024):
    B = x.shape[0]
    steps_per_batch = SEQ // tm
    w_rows = N_EXPERTS * D_MODEL
    w_blk = w_rows // (B * steps_per_batch)
    const = lambda shape: pl.BlockSpec(shape, lambda b, i: (0,) * len(shape))
    tok = lambda w: pl.BlockSpec((None, tm, w), lambda b, i: (b, i, 0))
    wsl = lambda c: pl.BlockSpec((w_blk, c), lambda b, i: (b * steps_per_batch + i, 0))
    sds = lambda w: jax.ShapeDtypeStruct((B, SEQ, w), BF16)
    step = lambda b, i: b * steps_per_batch + i
    a, b, q, k, v, experts = pl.pallas_call(
        _inproj_kernel,
        grid=(B, steps_per_batch),
        in_specs=[tok(D_MODEL),
                  pl.BlockSpec((None, N_ADA, D_MODEL), lambda b, i: (b, 0, 0)),
                  const((1, D_MODEL)),
                  const((D_MODEL, IN_PROJ_WIDTH)),
                  const((FOURIER_WIDTH, 2 * FOURIER_WIDTH)),
                  const((ATTN_WIDTH, ATTN_WIDTH)),
                  const((KV_WIDTH, KV_WIDTH)),
                  const((1, ATTN_WIDTH)),
                  const((1, KV_WIDTH)),
                  wsl(D_EXPERT), wsl(D_EXPERT), wsl(D_MODEL)],
        out_specs=[tok(FOURIER_WIDTH), tok(FOURIER_WIDTH), tok(ATTN_WIDTH), tok(KV_WIDTH),
                   tok(KV_WIDTH),
                   pl.BlockSpec((3, w_blk, D_MODEL), lambda b, i: (0, step(b, i), 0))],
        out_shape=[sds(FOURIER_WIDTH), sds(FOURIER_WIDTH), sds(ATTN_WIDTH), sds(KV_WIDTH),
                   sds(KV_WIDTH), jax.ShapeDtypeStruct((3, w_rows, D_MODEL), BF16)],
        compiler_params=_params(("parallel", "parallel"), VMEM_LIMIT),
        name="inproj",
    )(x, mod, g, win, pq, jnp.asarray(_group_ones(ATTN_WIDTH)).astype(BF16),
      jnp.asarray(_group_ones(KV_WIDTH)).astype(BF16), gq, gk,
      w_gate.reshape(w_rows, D_EXPERT), w_up.reshape(w_rows, D_EXPERT),
      w_down.reshape(N_EXPERTS * D_EXPERT, D_MODEL))
    return a, b, q, k, v, experts.reshape(3, N_EXPERTS, D_MODEL, D_MODEL)


HALF = SEQ // 2
REV = 128
REV_BLOCKS = HALF // REV
FOURIER_ROWS = 512


def _fourier_kernel(cm_ref, sm_ref, psh_ref, alt_ref, altrow_ref, a_ref, b_ref, bf_ref, o_ref):
    psh = psh_ref[...]

    def reversed_block(win_lo, k, src):
        if k == 0:
            return jnp.dot(psh[:, :REV], src(win_lo, REV), preferred_element_type=F32)
        return jnp.dot(psh, src(win_lo, 2 * REV), preferred_element_type=F32)

    def folded(ref, sign):
        blocks = []
        for k in range(REV_BLOCKS):
            lo = SEQ - REV * (k + 1)
            rev = reversed_block(lo, k, lambda s, n: ref[pl.ds(s, n), :])
            blocks.append((ref[pl.ds(k * REV, REV), :].astype(F32) + sign * rev).astype(BF16))
        return jnp.concatenate(blocks, axis=0)

    a_even = folded(a_ref, 1.0)
    b_odd = folded(b_ref, -1.0)
    a_mid = a_ref[pl.ds(HALF, 1), :].astype(F32)
    bias = bf_ref[...]
    z_blocks = []
    for i in range(HALF // FOURIER_ROWS):
        rows = pl.ds(i * FOURIER_ROWS, FOURIER_ROWS)
        yc = jnp.dot(cm_ref[rows, :], a_even, preferred_element_type=F32)
        yc = yc + alt_ref[rows, :] * a_mid + bias
        ys = jnp.dot(sm_ref[rows, :], b_odd, preferred_element_type=F32)
        o_ref[rows, :] = (yc - ys).astype(BF16)
        z_blocks.append((yc + ys).astype(BF16))
    z = jnp.concatenate(z_blocks, axis=0)
    for k in range(REV_BLOCKS):
        lo = HALF - REV * (k + 1)
        top = reversed_block(lo, k, lambda s, n: z[s:s + n])
        o_ref[pl.ds(HALF + k * REV, REV), :] = top.astype(BF16)
    y_mid = jnp.dot(altrow_ref[...], a_ref[...], preferred_element_type=F32)[0:1, :] + bias
    o_ref[pl.ds(HALF, 1), :] = y_mid.astype(BF16)


@functools.lru_cache(maxsize=None)
def _fold_tables():
    s = np.arange(HALF, dtype=np.int64)
    ph = (s[:, None] * s[None, :]) % SEQ
    ang = 2.0 * np.pi * ph.astype(np.float64) / SEQ
    sc = 1.0 / math.sqrt(SEQ)
    cm = (np.cos(ang) * sc).astype(np.float32)
    sm = (np.sin(ang) * sc).astype(np.float32)
    psh = np.zeros((REV, 2 * REV), np.float32)
    psh[np.arange(REV), REV - np.arange(REV)] = 1.0
    alt = (np.where(s % 2 == 0, 1.0, -1.0) * sc).astype(np.float32).reshape(HALF, 1)
    t = np.arange(SEQ)
    altrow = np.zeros((SUBLANES, SEQ), np.float32)
    altrow[0] = np.where(t % 2 == 0, 1.0, -1.0) * sc
    return cm, sm, psh, alt, altrow


def _fourier(a, b, bf):
    B = a.shape[0]
    cm, sm, psh, alt, altrow = _fold_tables()
    tok = pl.BlockSpec((None, SEQ, FOURIER_WIDTH), lambda i: (i, 0, 0))
    const = lambda shape: pl.BlockSpec(shape, lambda i: (0,) * len(shape))
    return pl.pallas_call(
        _fourier_kernel,
        grid=(B,),
        in_specs=[const((HALF, HALF)), const((HALF, HALF)), const((REV, 2 * REV)),
                  const((HALF, 1)), const((SUBLANES, SEQ)),
                  tok, tok, const((1, FOURIER_WIDTH))],
        out_specs=tok,
        out_shape=jax.ShapeDtypeStruct((B, SEQ, FOURIER_WIDTH), BF16),
        compiler_params=_params(("parallel",), VMEM_LIMIT),
        name="fourier",
    )(jnp.asarray(cm).astype(BF16), jnp.asarray(sm).astype(BF16), jnp.asarray(psh).astype(BF16),
      jnp.asarray(alt), jnp.asarray(altrow).astype(BF16), a, b, bf)


ATT_SUB = 16
ATT_ROWS = ATT_SUB * BLOCK
ATT_STEPS = N_BLOCKS // ATT_SUB


def _attn_kernel(sink_ref, q_ref, kl_ref, km_ref, kr_ref, vl_ref, vm_ref, vr_ref, bias_ref,
                 o_ref):
    i = pl.program_id(1)
    keys = jnp.concatenate([kl_ref[...], km_ref[...], kr_ref[...]], axis=0)
    vals = jnp.concatenate([vl_ref[...], vm_ref[...], vr_ref[...]], axis=0)
    first_k = lax.broadcasted_iota(jnp.int32, keys.shape, 1) < HEAD_DIM
    first_q = lax.broadcasted_iota(jnp.int32, (Q_PER_KV * BLOCK, LANES), 1) < HEAD_DIM
    row_head = lax.broadcasted_iota(jnp.int32, (Q_PER_KV * BLOCK, 1), 0) // BLOCK
    zero = jnp.zeros_like(keys)
    ones_lo = jnp.where(first_k, 1.0, 0.0).astype(BF16)
    ones_hi = jnp.where(first_k, 0.0, 1.0).astype(BF16)
    keys_kv = [jnp.where(first_k, keys, zero), jnp.where(first_k, zero, keys)]
    vals_kv = [jnp.concatenate([jnp.where(first_k, vals, zero), ones_lo], axis=1),
               jnp.concatenate([jnp.where(first_k, zero, vals), ones_hi], axis=1)]
    sinks = []
    for kv in range(N_KV_HEADS):
        sink = jnp.zeros((Q_PER_KV * BLOCK, 1), F32)
        for r in range(Q_PER_KV):
            sink = jnp.where(row_head == r, sink_ref[kv * Q_PER_KV + r] * LOG2E, sink)
        sinks.append(sink)
    for j in range(ATT_SUB):
        variant = 1
        if j == 0:
            variant = jnp.where(i == 0, 0, variant)
        if j == ATT_SUB - 1:
            variant = jnp.where(i == ATT_STEPS - 1, 2, variant)
        qrows = pl.ds(j * BLOCK, BLOCK)
        krows = slice(j * BLOCK, j * BLOCK + SPAN)
        qs = jnp.concatenate([q_ref[qrows, r * LANES:(r + 1) * LANES] for r in range(Q_PER_KV)],
                             axis=0)
        keys2 = jnp.concatenate([keys_kv[0][krows], keys_kv[1][krows]], axis=0)
        vals2 = jnp.concatenate([vals_kv[0][krows], vals_kv[1][krows]], axis=0)
        logits = lax.dot_general(qs, keys2, (((1,), (1,)), ((), ())),
                                 preferred_element_type=F32)
        logits = logits + bias_ref[variant]
        ms = [jnp.maximum(jnp.max(logits[:, kv * SPAN:(kv + 1) * SPAN], axis=-1, keepdims=True),
                          sinks[kv]) for kv in range(N_KV_HEADS)]
        p = jnp.exp2(jnp.concatenate([logits[:, kv * SPAN:(kv + 1) * SPAN] - ms[kv]
                                      for kv in range(N_KV_HEADS)], axis=1).astype(BF16))
        pv = jnp.dot(p, vals2, preferred_element_type=F32)
        denom = pv[:, LANES:] + jnp.exp2(jnp.where(first_q, sinks[0] - ms[0], sinks[1] - ms[1]))
        out = (pv[:, :LANES] / denom).astype(BF16)
        for r in range(Q_PER_KV):
            o_ref[qrows, r * LANES:(r + 1) * LANES] = out[r * BLOCK:(r + 1) * BLOCK]


def _attn(sink, q, k, v, bias):
    B = q.shape[0]
    edge = lambda f: pl.BlockSpec((None, BLOCK, KV_WIDTH), lambda b, i: (b, f(i), 0))
    left = lambda i: jnp.maximum(i * ATT_SUB - 1, 0)
    right = lambda i: jnp.minimum((i + 1) * ATT_SUB, N_BLOCKS - 1)
    mid = pl.BlockSpec((None, ATT_ROWS, KV_WIDTH), lambda b, i: (b, i, 0))
    qspec = pl.BlockSpec((None, ATT_ROWS, ATTN_WIDTH), lambda b, i: (b, i, 0))
    return pl.pallas_call(
        _attn_kernel,
        grid=(B, ATT_STEPS),
        in_specs=[pl.BlockSpec(memory_space=pltpu.SMEM), qspec,
                  edge(left), mid, edge(right),
                  edge(left), mid, edge(right),
                  pl.BlockSpec((3, Q_PER_KV * BLOCK, N_KV_HEADS * SPAN), lambda b, i: (0, 0, 0))],
        out_specs=qspec,
        out_shape=jax.ShapeDtypeStruct((B, SEQ, ATTN_WIDTH), BF16),
        compiler_params=_params(("parallel", "parallel"), VMEM_LIMIT),
        name="attn",
    )(sink, q, k, k, k, v, v, v, bias)


OUT_ROWS = 256


def _outproj_kernel(yf_ref, ya_ref, x_ref, mod_ref, g_ref, wo_ref, wrh_ref,
                    x1_ref, h2_ref, aff_ref):
    tm = x_ref.shape[0]
    gain = g_ref[...] * (1.0 + mod_ref[4:5, :])
    shift = mod_ref[3:4, :]
    gate1 = mod_ref[2:3, :]
    lane = lax.broadcasted_iota(jnp.int32, (OUT_ROWS, LANES), 1)
    chunks = range(tm // OUT_ROWS)
    mixed = [jnp.dot(jnp.concatenate([yf_ref[pl.ds(c * OUT_ROWS, OUT_ROWS), :],
                                      ya_ref[pl.ds(c * OUT_ROWS, OUT_ROWS), :]], axis=1),
                     wo_ref[...], preferred_element_type=F32) for c in chunks]
    for c in chunks:
        rows = pl.ds(c * OUT_ROWS, OUT_ROWS)
        x1 = x_ref[rows, :] + gate1 * mixed[c]
        x1_ref[rows, :] = x1
        ms = jnp.mean(x1 * x1, axis=-1, keepdims=True)
        h2 = x1 * lax.rsqrt(ms + EPS) * gain + shift
        hi = h2.astype(BF16)
        top = pltpu.bitcast(hi[:, :D_MODEL // 2].astype(F32), jnp.uint32)
        bot = pltpu.bitcast(hi[:, D_MODEL // 2:].astype(F32), jnp.uint32)
        words = top | (bot >> 16)
        for j in range(PACK_ROWS):
            h2_ref[pl.ds(c * OUT_ROWS * PACK_ROWS + j, OUT_ROWS, stride=PACK_ROWS), :] = (
                words[:, j * LANES:(j + 1) * LANES])
        part = jnp.dot(hi, wrh_ref[...], preferred_element_type=F32)
        logits = part + pltpu.roll(part, LANES - N_EXPERTS, axis=1)
        logits = jnp.where(lane < N_EXPERTS, logits, NEG_INF)
        m = jnp.max(logits, axis=-1, keepdims=True)
        e = jnp.exp(logits - m)
        aff_ref[rows, :] = e / jnp.sum(e, axis=-1, keepdims=True)


def _outproj(yf, ya, x, mod, g, wo, wrh, tm=1024):
    B = x.shape[0]
    const = lambda shape: pl.BlockSpec(shape, lambda b, i: (0,) * len(shape))
    tok = lambda w: pl.BlockSpec((None, tm, w), lambda b, i: (b, i, 0))
    return pl.pallas_call(
        _outproj_kernel,
        grid=(B, SEQ // tm),
        in_specs=[tok(FOURIER_WIDTH), tok(ATTN_WIDTH), tok(D_MODEL),
                  pl.BlockSpec((None, N_ADA, D_MODEL), lambda b, i: (b, 0, 0)),
                  const((1, D_MODEL)),
                  const((D_MODEL, D_MODEL)),
                  const((D_MODEL, LANES))],
        out_specs=[tok(D_MODEL),
                   pl.BlockSpec((None, tm * PACK_ROWS, LANES), lambda b, i: (b, i, 0)),
                   tok(LANES)],
        out_shape=[jax.ShapeDtypeStruct((B, SEQ, D_MODEL), F32),
                   jax.ShapeDtypeStruct((B, SEQ * PACK_ROWS, LANES), jnp.uint32),
                   jax.ShapeDtypeStruct((B, SEQ, LANES), F32)],
        compiler_params=_params(("parallel", "parallel"), VMEM_LIMIT),
        name="outproj",
    )(yf, ya, x, mod, g, wo, wrh)


ROUTE_BATCHES = 2
SEARCH_BITS = 3


def _route_kernel(aff_ref, tri_ref, idx_ref, gate_ref):
    for bb in range(ROUTE_BATCHES):
        _route_one(aff_ref[bb], tri_ref[...], idx_ref.at[bb], gate_ref.at[bb])


def _route_one(aff, tri, idx_ref, gate_ref):
    aff_t = jnp.transpose(aff)[:N_EXPERTS]
    bits = pltpu.bitcast(aff_t, jnp.int32)
    cap = float(CAPACITY)

    t = jnp.zeros((N_EXPERTS, 1), jnp.int32)
    for shift in range(30 - SEARCH_BITS, -1, -SEARCH_BITS):
        digit = jnp.zeros((N_EXPERTS, 1), jnp.int32)
        for k in range(1, 1 << SEARCH_BITS):
            cnt = jnp.sum(jnp.where(bits >= (t | (k << shift)), 1.0, 0.0), axis=1, keepdims=True)
            digit = digit + jnp.where(cnt >= cap, 1, 0)
        t = t | (digit << shift)
    gt = bits > t
    eq = bits == t
    need = cap - jnp.sum(jnp.where(gt, 1.0, 0.0), axis=1, keepdims=True)

    n_chunks = SEQ // LANES

    def prefix(flags_f32):
        outs = []
        carry = jnp.zeros((N_EXPERTS, 1), F32)
        for c in range(n_chunks):
            f = flags_f32[:, c * LANES:(c + 1) * LANES]
            incl = jnp.dot(f.astype(BF16), tri, preferred_element_type=F32)
            outs.append(incl - f + carry)
            carry = carry + jnp.sum(f, axis=1, keepdims=True)
        return jnp.concatenate(outs, axis=1)

    eq_f = jnp.where(eq, 1.0, 0.0)
    eq_rank = prefix(eq_f)
    sel_f = jnp.where(gt, 1.0, jnp.where(eq_rank < need, eq_f, 0.0))
    pos = prefix(sel_f)
    posm = jnp.where(sel_f > 0.0, pos, -1.0)

    hi = aff_t.astype(BF16).astype(F32)
    r1 = aff_t - hi
    mid = r1.astype(BF16).astype(F32)
    lo = r1 - mid
    tok = lax.broadcasted_iota(jnp.int32, (N_EXPERTS, SEQ), 1)
    row = lax.broadcasted_iota(jnp.int32, (N_EXPERTS, SEQ), 0)
    tok_rows = jnp.where(row == 0, (tok >> 6).astype(F32),
                         jnp.where(row == 1, (tok & 63).astype(F32), 0.0))
    vals_t = jnp.concatenate([hi, mid, lo, tok_rows], axis=0).astype(BF16)

    slot = lax.broadcasted_iota(jnp.int32, (CAPACITY, SEQ), 0).astype(F32).astype(BF16)
    posm_b = posm.astype(BF16)
    one_b = jnp.ones((CAPACITY, SEQ), BF16)
    zero_b = jnp.zeros((CAPACITY, SEQ), BF16)
    for e in range(N_EXPERTS):
        onehot = jnp.where(posm_b[e:e + 1, :] == slot, one_b, zero_b)
        res = lax.dot_general(vals_t, onehot, (((1,), (1,)), ((), ())),
                              preferred_element_type=F32)
        cols = pl.ds(e * CAPACITY, CAPACITY)
        tok_idx = res[3 * N_EXPERTS:3 * N_EXPERTS + 1] * 64.0 + res[3 * N_EXPERTS + 1:
                                                                    3 * N_EXPERTS + 2]
        idx_ref[:, cols] = tok_idx.astype(jnp.int32) * PACK_ROWS
        gate_ref[:, cols] = (res[e:e + 1] + res[N_EXPERTS + e:N_EXPERTS + e + 1]
                             + res[2 * N_EXPERTS + e:2 * N_EXPERTS + e + 1])


def _route(aff):
    B = aff.shape[0]
    n = N_EXPERTS * CAPACITY
    return pl.pallas_call(
        _route_kernel,
        grid=(B // ROUTE_BATCHES,),
        in_specs=[pl.BlockSpec((ROUTE_BATCHES, SEQ, LANES), lambda b: (b, 0, 0)),
                  pl.BlockSpec((LANES, LANES), lambda b: (0, 0))],
        out_specs=[pl.BlockSpec((ROUTE_BATCHES, 1, n), lambda b: (b, 0, 0)),
                   pl.BlockSpec((ROUTE_BATCHES, 1, n), lambda b: (b, 0, 0))],
        out_shape=[jax.ShapeDtypeStruct((B, 1, n), jnp.int32),
                   jax.ShapeDtypeStruct((B, 1, n), F32)],
        compiler_params=_params(("parallel",), VMEM_LIMIT),
        name="route",
    )(aff, jnp.asarray(_tri_incl()).astype(BF16))


PAIR = 4


def _moe_kernel(idx_ref, h2_ref, w_ref, y_ref, xin0_ref, xin1_ref):
    e = pl.program_id(1)
    last = N_EXPERTS - 1
    n = N_EXPERTS * CAPACITY
    rows = PAIR * CAPACITY

    def gather_rows(ex, dst_ref):
        for bb in range(PAIR):
            base = bb * n + ex * CAPACITY
            for p in range(CAPACITY):
                off = idx_ref[0, base + p]
                tile = h2_ref[bb, pl.ds(pl.multiple_of((off >> 3) << 3, SUBLANES), SUBLANES), :]
                tile = pltpu.roll(tile, off & PACK_ROWS, axis=0)
                dst_ref[pl.ds((bb * CAPACITY + p) * PACK_ROWS, PACK_ROWS), :] = tile[:PACK_ROWS]

    def expert(xin_ref):
        words = [xin_ref[pl.ds(j, rows, stride=PACK_ROWS), :] for j in range(PACK_ROWS)]
        xin = jnp.concatenate(
            [pltpu.bitcast(w & jnp.uint32(0xFFFF0000), F32).astype(BF16) for w in words]
            + [pltpu.bitcast(w << 16, F32).astype(BF16) for w in words], axis=1)
        for bb in range(PAIR):
            xb = xin[bb * CAPACITY:(bb + 1) * CAPACITY]
            a = jnp.dot(xb, w_ref[0], preferred_element_type=F32)
            u = jnp.dot(xb, w_ref[1], preferred_element_type=F32)
            hmid = (a * (1.0 / (1.0 + jnp.exp(-a))) * u).astype(BF16)
            y = jnp.dot(hmid, w_ref[2], preferred_element_type=F32).astype(BF16)
            top = pltpu.bitcast(y[:, :D_MODEL // 2].astype(F32), jnp.uint32)
            bot = pltpu.bitcast(y[:, D_MODEL // 2:].astype(F32), jnp.uint32)
            words = top | (bot >> 16)
            for j in range(PACK_ROWS):
                y_ref[bb, pl.ds(j, CAPACITY, stride=PACK_ROWS), :] = (
                    words[:, j * LANES:(j + 1) * LANES])

    @pl.when(e == 0)
    def _():
        gather_rows(0, xin0_ref)

    def step(xin_cur, xin_nxt):
        gather_rows(jnp.minimum(e + 1, last), xin_nxt)
        expert(xin_cur)

    @pl.when(e % 2 == 0)
    def _():
        step(xin0_ref, xin1_ref)

    @pl.when(e % 2 == 1)
    def _():
        step(xin1_ref, xin0_ref)


def _moe(idx, h2, experts):
    B = h2.shape[0]
    n = N_EXPERTS * CAPACITY
    rows = SEQ * PACK_ROWS
    pairs = B // PAIR
    stage = pltpu.VMEM((PAIR * CAPACITY * PACK_ROWS, LANES), jnp.uint32)
    out = pl.pallas_call(
        _moe_kernel,
        grid=(pairs, N_EXPERTS),
        in_specs=[pl.BlockSpec((None, 1, PAIR * n), lambda b, e: (b, 0, 0),
                               memory_space=pltpu.SMEM),
                  pl.BlockSpec((None, PAIR, rows, LANES), lambda b, e: (b, 0, 0, 0)),
                  pl.BlockSpec((3, None, D_MODEL, D_MODEL), lambda b, e: (0, e, 0, 0))],
        out_specs=pl.BlockSpec((None, PAIR, CAPACITY * PACK_ROWS, LANES),
                               lambda b, e: (b, 0, e, 0)),
        out_shape=jax.ShapeDtypeStruct((pairs, PAIR, n * PACK_ROWS, LANES), jnp.uint32),
        scratch_shapes=[stage, stage],
        compiler_params=_params(("parallel", "arbitrary"), VMEM_LIMIT),
        name="moe",
    )(idx.reshape(pairs, 1, PAIR * n), h2.reshape(pairs, PAIR, rows, LANES), experts)
    return out.reshape(B, n * PACK_ROWS, LANES)


COMBINE_EXPERTS = 8
SCATTER_UNROLL = 8
COMBINE_ROWS = 256


def _combine_kernel(idx_ref, gate_ref, y_ref, x1_ref, mod_ref, o_ref, acc_ref):
    j = pl.program_id(1)
    slots = COMBINE_EXPERTS * CAPACITY
    base = j * slots

    @pl.when(j == 0)
    def _():
        acc_ref[...] = jnp.zeros_like(acc_ref)

    upper = lax.broadcasted_iota(jnp.int32, (SUBLANES, LANES), 0) < PACK_ROWS
    for g in range(slots // SCATTER_UNROLL):
        new = []
        for u in range(0, SCATTER_UNROLL, 2):
            r = g * SCATTER_UNROLL + u
            words = y_ref[pl.ds(r * PACK_ROWS, SUBLANES), :]
            hi = pltpu.bitcast(words & jnp.uint32(0xFFFF0000), F32)
            lo = pltpu.bitcast(words << 16, F32)
            slabs = (jnp.where(upper, hi, pltpu.roll(lo, PACK_ROWS, axis=0)),
                     jnp.where(upper, pltpu.roll(hi, PACK_ROWS, axis=0), lo))
            for k in range(2):
                dst = pl.multiple_of(idx_ref[0, base + r + k] * (ROW_SLAB // PACK_ROWS), ROW_SLAB)
                new.append((dst, acc_ref[pl.ds(dst, ROW_SLAB), :]
                            + slabs[k] * gate_ref[0, base + r + k]))
        for dst, val in new:
            acc_ref[pl.ds(dst, ROW_SLAB), :] = val

    @pl.when(j == pl.num_programs(1) - 1)
    def _():
        for rb in range(SEQ // COMBINE_ROWS):
            rows = pl.ds(rb * COMBINE_ROWS, COMBINE_ROWS)
            for c in range(ROW_SLAB):
                cols = slice(c * LANES, (c + 1) * LANES)
                chunk = acc_ref[pl.ds(rb * COMBINE_ROWS * ROW_SLAB + c, COMBINE_ROWS,
                                      stride=ROW_SLAB), :]
                o_ref[rows, cols] = x1_ref[rows, cols] + mod_ref[5:6, cols] * chunk


def _combine(idx, gate, y, x1, mod):
    B = x1.shape[0]
    n = N_EXPERTS * CAPACITY
    tok = pl.BlockSpec((None, SEQ, D_MODEL), lambda b, j: (b, 0, 0))
    smem = pl.BlockSpec((None, 1, n), lambda b, j: (b, 0, 0), memory_space=pltpu.SMEM)
    return pl.pallas_call(
        _combine_kernel,
        grid=(B, N_EXPERTS // COMBINE_EXPERTS),
        in_specs=[smem, smem,
                  pl.BlockSpec((None, COMBINE_EXPERTS * CAPACITY * PACK_ROWS, LANES),
                               lambda b, j: (b, j, 0)),
                  tok,
                  pl.BlockSpec((None, N_ADA, D_MODEL), lambda b, j: (b, 0, 0))],
        out_specs=tok,
        out_shape=jax.ShapeDtypeStruct((B, SEQ, D_MODEL), F32),
        scratch_shapes=[pltpu.VMEM((SEQ * ROW_SLAB, LANES), F32)],
        compiler_params=_params(("parallel", "arbitrary"), VMEM_LIMIT),
        name="combine",
    )(idx, gate, y, x1, mod)


def _head_perm():
    perm = []
    for r in range(Q_PER_KV):
        for kv in range(N_KV_HEADS):
            h = kv * Q_PER_KV + r
            perm.extend(range(h * HEAD_DIM, (h + 1) * HEAD_DIM))
    return np.asarray(perm, dtype=np.int32)


def kernel(x, c, rel_bias, w_ada, b_ada, norm_mix_g, norm_ffn_g, w_in, w_fourier, b_fourier,
           q_norm_g, k_norm_g, sink, w_out, w_router, w_gate, w_up, w_down):
    B = x.shape[0]
    perm = _head_perm()
    l = 0
    mod = _ada(c, w_ada[l], b_ada[l]).reshape(B, N_ADA, D_MODEL)
    pq = _fold(w_fourier[l])
    bias = _bias_table(rel_bias)

    wi = w_in[l]
    q_cols = wi[:, FOURIER_WIDTH:FOURIER_WIDTH + ATTN_WIDTH][:, perm]
    win = jnp.concatenate([wi[:, :FOURIER_WIDTH], q_cols, wi[:, FOURIER_WIDTH + ATTN_WIDTH:]],
                          axis=1).astype(BF16)
    gq = (jnp.tile(q_norm_g[l], N_Q_HEADS) * (HEAD_DIM ** -0.5 * LOG2E)).reshape(1, ATTN_WIDTH)
    gk = jnp.tile(k_norm_g[l], N_KV_HEADS).reshape(1, KV_WIDTH)
    a, b, q, k, v, experts = _inproj(x, mod, norm_mix_g[l].reshape(1, D_MODEL), win, pq, gq, gk,
                                     w_gate[l], w_up[l], w_down[l])

    yf = _fourier(a, b, b_fourier[l].reshape(1, FOURIER_WIDTH))
    ya = _attn(sink[l], q, k, v, bias)

    wo = w_out[l]
    wo = jnp.concatenate([wo[:FOURIER_WIDTH], wo[FOURIER_WIDTH:][perm]], axis=0).astype(BF16)
    w_hi = w_router[l].astype(BF16)
    w_lo = (w_router[l] - w_hi.astype(F32)).astype(BF16)
    wrh = jnp.pad(jnp.concatenate([w_hi, w_lo], axis=1), ((0, 0), (0, LANES - 2 * N_EXPERTS)))
    x1, h2, aff = _outproj(yf, ya, x, mod, norm_ffn_g[l].reshape(1, D_MODEL), wo, wrh)

    idx, gate = _route(aff)
    n = N_EXPERTS * CAPACITY
    y = _moe(idx, h2, experts)
    return _combine(idx, gate, y, x1, mod)
```

```python
import functools
import math

import numpy as np
import jax
import jax.numpy as jnp
from jax import lax
from jax.experimental import pallas as pl
from jax.experimental.pallas import tpu as pltpu

D_MODEL = 1024
SEQ = 2048
HEAD_DIM = 64
FOURIER_WIDTH = 512
ATTN_WIDTH = 512
N_GROUPS = 8
N_Q_HEADS = 8
Q_PER_KV = 4
N_KV_HEADS = 2
KV_WIDTH = 128
IN_PROJ_WIDTH = 1280
WINDOW = 128
BLOCK = 128
SPAN = BLOCK + 2 * WINDOW
N_BLOCKS = SEQ // BLOCK
N_BUCKETS = 32
MAX_DISTANCE = 128
N_EXPERTS = 16
CAPACITY = 2 * SEQ // N_EXPERTS
D_EXPERT = 1024
N_ADA = 6
EPS = 1e-6

LANES = 128
SUBLANES = 8
ROW_SLAB = D_MODEL // LANES
PACK_ROWS = ROW_SLAB // 2
VMEM_LIMIT = 56 * 1024 * 1024

F32 = jnp.float32
BF16 = jnp.bfloat16
NEG_INF = float("-inf")
LOG2E = math.log2(math.e)


def _params(sem, vmem=None):
    return pltpu.CompilerParams(dimension_semantics=sem, vmem_limit_bytes=vmem)


@functools.lru_cache(maxsize=None)
def _chan_dft():
    c = np.arange(HEAD_DIM, dtype=np.int64)
    ph = (c[:, None] * c[None, :]) % HEAD_DIM
    ang = 2.0 * np.pi * ph.astype(np.float64) / HEAD_DIM
    sc = 1.0 / math.sqrt(HEAD_DIM)
    eye = np.eye(N_GROUPS)
    cbd = np.kron(eye, np.cos(ang) * sc)
    sbd = np.kron(eye, np.sin(ang) * sc)
    return cbd.astype(np.float32), sbd.astype(np.float32)


@functools.lru_cache(maxsize=None)
def _bucket_table():
    rel = np.arange(SPAN)[None, :] - WINDOW - np.arange(BLOCK)[:, None]
    half = N_BUCKETS // 2
    max_exact = half // 2
    n = np.abs(rel)
    nf = np.maximum(n, 1).astype(np.float64)
    large = max_exact + (np.log(nf / max_exact) / math.log(MAX_DISTANCE / max_exact)
                         * (half - max_exact)).astype(np.int64)
    sq = np.maximum(n.astype(np.int64) ** 2 // (max_exact * max_exact), 1)
    large_int = max_exact + np.floor(np.log2(sq.astype(np.float64)) + 1e-9).astype(np.int64)
    assert np.array_equal(np.where(n >= max_exact, large, 0), np.where(n >= max_exact, large_int, 0))
    large = np.minimum(large, half - 1)
    bucket = np.where(rel > 0, half, 0) + np.where(n < max_exact, n, large)
    return bucket.astype(np.int32)


@functools.lru_cache(maxsize=None)
def _group_ones(width):
    return np.kron(np.eye(width // HEAD_DIM), np.ones((HEAD_DIM, HEAD_DIM))).astype(np.float32)


@functools.lru_cache(maxsize=None)
def _tri_incl():
    i = np.arange(LANES)
    return (i[:, None] <= i[None, :]).astype(np.float32)


def _ada_kernel(c_ref, w_ref, b_ref, o_ref):
    c = c_ref[...]
    ca = c * (1.0 / (1.0 + jnp.exp(-c)))
    o_ref[...] = jnp.dot(ca, w_ref[...], precision=lax.Precision.HIGHEST,
                         preferred_element_type=F32) + b_ref[...]


def _ada(c, w_ada, b_ada):
    B = c.shape[0]
    n = N_ADA * D_MODEL
    tn = D_MODEL
    return pl.pallas_call(
        _ada_kernel,
        grid=(n // tn,),
        in_specs=[pl.BlockSpec((B, D_MODEL), lambda j: (0, 0)),
                  pl.BlockSpec((D_MODEL, tn), lambda j: (0, j)),
                  pl.BlockSpec((1, tn), lambda j: (0, j))],
        out_specs=pl.BlockSpec((B, tn), lambda j: (0, j)),
        out_shape=jax.ShapeDtypeStruct((B, n), F32),
        compiler_params=_params(("arbitrary",)),
        name="ada",
    )(c, w_ada, b_ada.reshape(1, n))


def _fold_kernel(cbd_ref, sbd_ref, w_ref, o_ref):
    w = w_ref[...]
    o_ref[:, :FOURIER_WIDTH] = jnp.dot(cbd_ref[...], w, precision=lax.Precision.HIGHEST,
                                       preferred_element_type=F32).astype(BF16)
    o_ref[:, FOURIER_WIDTH:] = jnp.dot(sbd_ref[...], w, precision=lax.Precision.HIGHEST,
                                       preferred_element_type=F32).astype(BF16)


def _fold(w_fourier):
    wbd = (jnp.eye(N_GROUPS, dtype=F32)[:, None, :, None] * w_fourier[:, :, None, :]
           ).reshape(FOURIER_WIDTH, FOURIER_WIDTH)
    cbd, sbd = _chan_dft()
    return pl.pallas_call(
        _fold_kernel,
        out_shape=jax.ShapeDtypeStruct((FOURIER_WIDTH, 2 * FOURIER_WIDTH), BF16),
        name="fold",
    )(jnp.asarray(cbd), jnp.asarray(sbd), wbd)


def _bias_kernel(rb_ref, bucket_ref, o_ref):
    h = pl.program_id(0)
    bk = bucket_ref[...]
    acc = jnp.zeros((BLOCK, SPAN), F32)
    for b in range(N_BUCKETS):
        acc = jnp.where(bk == b, rb_ref[b, h], acc)
    j = lax.broadcasted_iota(jnp.int32, (BLOCK, SPAN), 1)
    q = lax.broadcasted_iota(jnp.int32, (BLOCK, SPAN), 0)
    band = jnp.abs(j - WINDOW - q) <= WINDOW
    base = jnp.where(band, acc * LOG2E, NEG_INF)
    o_ref[0] = jnp.where(j >= WINDOW, base, NEG_INF)
    o_ref[1] = base
    o_ref[2] = jnp.where(j < WINDOW + BLOCK, base, NEG_INF)


def _bias_table(rel_bias):
    return pl.pallas_call(
        _bias_kernel,
        grid=(N_Q_HEADS,),
        in_specs=[pl.BlockSpec(memory_space=pltpu.SMEM),
                  pl.BlockSpec((BLOCK, SPAN), lambda h: (0, 0))],
        out_specs=pl.BlockSpec((3, BLOCK, SPAN), lambda h: (0, h % Q_PER_KV, h // Q_PER_KV)),
        out_shape=jax.ShapeDtypeStruct((3, Q_PER_KV * BLOCK, N_KV_HEADS * SPAN), F32),
        compiler_params=_params(("arbitrary",)),
        name="bias",
    )(rel_bias, jnp.asarray(_bucket_table()))


IN_ROWS = 256


def _inproj_kernel(x_ref, mod_ref, g_ref, win_ref, pq_ref, bdq_ref, bdk_ref, gq_ref, gk_ref,
                   wg_ref, wu_ref, wd_ref,
                   a_ref, b_ref, q_ref, k_ref, v_ref, w_out):
    w_out[0] = wg_ref[...].astype(BF16)
    w_out[1] = wu_ref[...].astype(BF16)
    w_out[2] = wd_ref[...].astype(BF16)
    gain = g_ref[...] * (1.0 + mod_ref[1:2, :])
    shift = mod_ref[0:1, :]
    q0 = FOURIER_WIDTH
    k0 = q0 + ATTN_WIDTH
    v0 = k0 + KV_WIDTH
    for c in range(x_ref.shape[0] // IN_ROWS):
        rows = pl.ds(c * IN_ROWS, IN_ROWS)
        x = x_ref[rows, :]
        ms = jnp.mean(x * x, axis=-1, keepdims=True)
        h = x * lax.rsqrt(ms + EPS) * gain + shift
        proj = jnp.dot(h.astype(BF16), win_ref[...], preferred_element_type=F32)
        uf = proj[:, :FOURIER_WIDTH].astype(BF16)
        ab = jnp.dot(uf, pq_ref[...], preferred_element_type=F32)
        a_ref[rows, :] = ab[:, :FOURIER_WIDTH].astype(BF16)
        b_ref[rows, :] = ab[:, FOURIER_WIDTH:].astype(BF16)
        q = proj[:, q0:k0]
        ssq = jnp.dot((q * q).astype(BF16), bdq_ref[...], preferred_element_type=F32)
        q_ref[rows, :] = (q * lax.rsqrt(ssq * (1.0 / HEAD_DIM) + EPS) * gq_ref[...]).astype(BF16)
        k = proj[:, k0:v0]
        ssk = jnp.dot((k * k).astype(BF16), bdk_ref[...], preferred_element_type=F32)
        k_ref[rows, :] = (k * lax.rsqrt(ssk * (1.0 / HEAD_DIM) + EPS) * gk_ref[...]).astype(BF16)
        v_ref[rows, :] = proj[:, v0:].astype(BF16)


def _inproj(x, mod, g, win, pq, gq, gk, w_gate, w_up, w_down, tm=1024):
    B = x.shape[0]
    steps_per_batch = SEQ // tm
    w_rows = N_EXPERTS * D_MODEL
    w_blk = w_rows // (B * steps_per_batch)
    const = lambda shape: pl.BlockSpec(shape, lambda b, i: (0,) * len(shape))
    tok = lambda w: pl.BlockSpec((None, tm, w), lambda b, i: (b, i, 0))
    wsl = lambda c: pl.BlockSpec((w_blk, c), lambda b, i: (b * steps_per_batch + i, 0))
    sds = lambda w: jax.ShapeDtypeStruct((B, SEQ, w), BF16)
    step = lambda b, i: b * steps_per_batch + i
    a, b, q, k, v, experts = pl.pallas_call(
        _inproj_kernel,
        grid=(B, steps_per_batch),
        in_specs=[tok(D_MODEL),
                  pl.BlockSpec((None, N_ADA, D_MODEL), lambda b, i: (b, 0, 0)),
                  const((1, D_MODEL)),
                  const((D_MODEL, IN_PROJ_WIDTH)),
                  const((FOURIER_WIDTH, 2 * FOURIER_WIDTH)),
                  const((ATTN_WIDTH, ATTN_WIDTH)),
                  const((KV_WIDTH, KV_WIDTH)),
                  const((1, ATTN_WIDTH)),
                  const((1, KV_WIDTH)),
                  wsl(D_EXPERT), wsl(D_EXPERT), wsl(D_MODEL)],
        out_specs=[tok(FOURIER_WIDTH), tok(FOURIER_WIDTH), tok(ATTN_WIDTH), tok(KV_WIDTH),
                   tok(KV_WIDTH),
                   pl.BlockSpec((3, w_blk, D_MODEL), lambda b, i: (0, step(b, i), 0))],
        out_shape=[sds(FOURIER_WIDTH), sds(FOURIER_WIDTH), sds(ATTN_WIDTH), sds(KV_WIDTH),
                   sds(KV_WIDTH), jax.ShapeDtypeStruct((3, w_rows, D_MODEL), BF16)],
        compiler_params=_params(("parallel", "parallel"), VMEM_LIMIT),
        name="inproj",
    )(x, mod, g, win, pq, jnp.asarray(_group_ones(ATTN_WIDTH)).astype(BF16),
      jnp.asarray(_group_ones(KV_WIDTH)).astype(BF16), gq, gk,
      w_gate.reshape(w_rows, D_EXPERT), w_up.reshape(w_rows, D_EXPERT),
      w_down.reshape(N_EXPERTS * D_EXPERT, D_MODEL))
    return a, b, q, k, v, experts.reshape(3, N_EXPERTS, D_MODEL, D_MODEL)


HALF = SEQ // 2
REV = 128
REV_BLOCKS = HALF // REV
FOURIER_ROWS = 512


def _fourier_kernel(cm_ref, sm_ref, psh_ref, alt_ref, altrow_ref, a_ref, b_ref, bf_ref, o_ref):
    psh = psh_ref[...]

    def reversed_block(win_lo, k, src):
        if k == 0:
            return jnp.dot(psh[:, :REV], src(win_lo, REV), preferred_element_type=F32)
        return jnp.dot(psh, src(win_lo, 2 * REV), preferred_element_type=F32)

    def folded(ref, sign):
        blocks = []
        for k in range(REV_BLOCKS):
            lo = SEQ - REV * (k + 1)
            rev = reversed_block(lo, k, lambda s, n: ref[pl.ds(s, n), :])
            blocks.append((ref[pl.ds(k * REV, REV), :].astype(F32) + sign * rev).astype(BF16))
        return jnp.concatenate(blocks, axis=0)

    a_even = folded(a_ref, 1.0)
    b_odd = folded(b_ref, -1.0)
    a_mid = a_ref[pl.ds(HALF, 1), :].astype(F32)
    bias = bf_ref[...]
    z_blocks = []
    for i in range(HALF // FOURIER_ROWS):
        rows = pl.ds(i * FOURIER_ROWS, FOURIER_ROWS)
        yc = jnp.dot(cm_ref[rows, :], a_even, preferred_element_type=F32)
        yc = yc + alt_ref[rows, :] * a_mid + bias
        ys = jnp.dot(sm_ref[rows, :], b_odd, preferred_element_type=F32)
        o_ref[rows, :] = (yc - ys).astype(BF16)
        z_blocks.append((yc + ys).astype(BF16))
    z = jnp.concatenate(z_blocks, axis=0)
    for k in range(REV_BLOCKS):
        lo = HALF - REV * (k + 1)
        top = reversed_block(lo, k, lambda s, n: z[s:s + n])
        o_ref[pl.ds(HALF + k * REV, REV), :] = top.astype(BF16)
    y_mid = jnp.dot(altrow_ref[...], a_ref[...], preferred_element_type=F32)[0:1, :] + bias
    o_ref[pl.ds(HALF, 1), :] = y_mid.astype(BF16)


@functools.lru_cache(maxsize=None)
def _fold_tables():
    s = np.arange(HALF, dtype=np.int64)
    ph = (s[:, None] * s[None, :]) % SEQ
    ang = 2.0 * np.pi * ph.astype(np.float64) / SEQ
    sc = 1.0 / math.sqrt(SEQ)
    cm = (np.cos(ang) * sc).astype(np.float32)
    sm = (np.sin(ang) * sc).astype(np.float32)
    psh = np.zeros((REV, 2 * REV), np.float32)
    psh[np.arange(REV), REV - np.arange(REV)] = 1.0
    alt = (np.where(s % 2 == 0, 1.0, -1.0) * sc).astype(np.float32).reshape(HALF, 1)
    t = np.arange(SEQ)
    altrow = np.zeros((SUBLANES, SEQ), np.float32)
    altrow[0] = np.where(t % 2 == 0, 1.0, -1.0) * sc
    return cm, sm, psh, alt, altrow


def _fourier(a, b, bf):
    B = a.shape[0]
    cm, sm, psh, alt, altrow = _fold_tables()
    tok = pl.BlockSpec((None, SEQ, FOURIER_WIDTH), lambda i: (i, 0, 0))
    const = lambda shape: pl.BlockSpec(shape, lambda i: (0,) * len(shape))
    return pl.pallas_call(
        _fourier_kernel,
        grid=(B,),
        in_specs=[const((HALF, HALF)), const((HALF, HALF)), const((REV, 2 * REV)),
                  const((HALF, 1)), const((SUBLANES, SEQ)),
                  tok, tok, const((1, FOURIER_WIDTH))],
        out_specs=tok,
        out_shape=jax.ShapeDtypeStruct((B, SEQ, FOURIER_WIDTH), BF16),
        compiler_params=_params(("parallel",), VMEM_LIMIT),
        name="fourier",
    )(jnp.asarray(cm).astype(BF16), jnp.asarray(sm).astype(BF16), jnp.asarray(psh).astype(BF16),
      jnp.asarray(alt), jnp.asarray(altrow).astype(BF16), a, b, bf)


ATT_SUB = 16
ATT_ROWS = ATT_SUB * BLOCK
ATT_STEPS = N_BLOCKS // ATT_SUB


def _attn_kernel(sink_ref, q_ref, kl_ref, km_ref, kr_ref, vl_ref, vm_ref, vr_ref, bias_ref,
                 o_ref):
    i = pl.program_id(1)
    keys = jnp.concatenate([kl_ref[...], km_ref[...], kr_ref[...]], axis=0)
    vals = jnp.concatenate([vl_ref[...], vm_ref[...], vr_ref[...]], axis=0)
    first_k = lax.broadcasted_iota(jnp.int32, keys.shape, 1) < HEAD_DIM
    first_q = lax.broadcasted_iota(jnp.int32, (Q_PER_KV * BLOCK, LANES), 1) < HEAD_DIM
    row_head = lax.broadcasted_iota(jnp.int32, (Q_PER_KV * BLOCK, 1), 0) // BLOCK
    zero = jnp.zeros_like(keys)
    ones_lo = jnp.where(first_k, 1.0, 0.0).astype(BF16)
    ones_hi = jnp.where(first_k, 0.0, 1.0).astype(BF16)
    keys_kv = [jnp.where(first_k, keys, zero), jnp.where(first_k, zero, keys)]
    vals_kv = [jnp.concatenate([jnp.where(first_k, vals, zero), ones_lo], axis=1),
               jnp.concatenate([jnp.where(first_k, zero, vals), ones_hi], axis=1)]
    sinks = []
    for kv in range(N_KV_HEADS):
        sink = jnp.zeros((Q_PER_KV * BLOCK, 1), F32)
        for r in range(Q_PER_KV):
            sink = jnp.where(row_head == r, sink_ref[kv * Q_PER_KV + r] * LOG2E, sink)
        sinks.append(sink)
    for j in range(ATT_SUB):
        variant = 1
        if j == 0:
            variant = jnp.where(i == 0, 0, variant)
        if j == ATT_SUB - 1:
            variant = jnp.where(i == ATT_STEPS - 1, 2, variant)
        qrows = pl.ds(j * BLOCK, BLOCK)
        krows = slice(j * BLOCK, j * BLOCK + SPAN)
        qs = jnp.concatenate([q_ref[qrows, r * LANES:(r + 1) * LANES] for r in range(Q_PER_KV)],
                             axis=0)
        keys2 = jnp.concatenate([keys_kv[0][krows], keys_kv[1][krows]], axis=0)
        vals2 = jnp.concatenate([vals_kv[0][krows], vals_kv[1][krows]], axis=0)
        logits = lax.dot_general(qs, keys2, (((1,), (1,)), ((), ())),
                                 preferred_element_type=F32)
        logits = logits + bias_ref[variant]
        ms = [jnp.maximum(jnp.max(logits[:, kv * SPAN:(kv + 1) * SPAN], axis=-1, keepdims=True),
                          sinks[kv]) for kv in range(N_KV_HEADS)]
        p = jnp.exp2(jnp.concatenate([logits[:, kv * SPAN:(kv + 1) * SPAN] - ms[kv]
                                      for kv in range(N_KV_HEADS)], axis=1).astype(BF16))
        pv = jnp.dot(p, vals2, preferred_element_type=F32)
        denom = pv[:, LANES:] + jnp.exp2(jnp.where(first_q, sinks[0] - ms[0], sinks[1] - ms[1]))
        out = (pv[:, :LANES] / denom).astype(BF16)
        for r in range(Q_PER_KV):
            o_ref[qrows, r * LANES:(r + 1) * LANES] = out[r * BLOCK:(r + 1) * BLOCK]


def _attn(sink, q, k, v, bias):
    B = q.shape[0]
    edge = lambda f: pl.BlockSpec((None, BLOCK, KV_WIDTH), lambda b, i: (b, f(i), 0))
    left = lambda i: jnp.maximum(i * ATT_SUB - 1, 0)
    right = lambda i: jnp.minimum((i + 1) * ATT_SUB, N_BLOCKS - 1)
    mid = pl.BlockSpec((None, ATT_ROWS, KV_WIDTH), lambda b, i: (b, i, 0))
    qspec = pl.BlockSpec((None, ATT_ROWS, ATTN_WIDTH), lambda b, i: (b, i, 0))
    return pl.pallas_call(
        _attn_kernel,
        grid=(B, ATT_STEPS),
        in_specs=[pl.BlockSpec(memory_space=pltpu.SMEM), qspec,
                  edge(left), mid, edge(right),
                  edge(left), mid, edge(right),
                  pl.BlockSpec((3, Q_PER_KV * BLOCK, N_KV_HEADS * SPAN), lambda b, i: (0, 0, 0))],
        out_specs=qspec,
        out_shape=jax.ShapeDtypeStruct((B, SEQ, ATTN_WIDTH), BF16),
        compiler_params=_params(("parallel", "parallel"), VMEM_LIMIT),
        name="attn",
    )(sink, q, k, k, k, v, v, v, bias)


OUT_ROWS = 256


def _outproj_kernel(yf_ref, ya_ref, x_ref, mod_ref, g_ref, wo_ref, wrh_ref,
                    x1_ref, h2_ref, aff_ref):
    tm = x_ref.shape[0]
    gain = g_ref[...] * (1.0 + mod_ref[4:5, :])
    shift = mod_ref[3:4, :]
    gate1 = mod_ref[2:3, :]
    lane = lax.broadcasted_iota(jnp.int32, (OUT_ROWS, LANES), 1)
    chunks = range(tm // OUT_ROWS)
    mixed = [jnp.dot(jnp.concatenate([yf_ref[pl.ds(c * OUT_ROWS, OUT_ROWS), :],
                                      ya_ref[pl.ds(c * OUT_ROWS, OUT_ROWS), :]], axis=1),
                     wo_ref[...], preferred_element_type=F32) for c in chunks]
    for c in chunks:
        rows = pl.ds(c * OUT_ROWS, OUT_ROWS)
        x1 = x_ref[rows, :] + gate1 * mixed[c]
        x1_ref[rows, :] = x1
        ms = jnp.mean(x1 * x1, axis=-1, keepdims=True)
        h2 = x1 * lax.rsqrt(ms + EPS) * gain + shift
        hi = h2.astype(BF16)
        top = pltpu.bitcast(hi[:, :D_MODEL // 2].astype(F32), jnp.uint32)
        bot = pltpu.bitcast(hi[:, D_MODEL // 2:].astype(F32), jnp.uint32)
        words = top | (bot >> 16)
        for j in range(PACK_ROWS):
            h2_ref[pl.ds(c * OUT_ROWS * PACK_ROWS + j, OUT_ROWS, stride=PACK_ROWS), :] = (
                words[:, j * LANES:(j + 1) * LANES])
        part = jnp.dot(hi, wrh_ref[...], preferred_element_type=F32)
        logits = part + pltpu.roll(part, LANES - N_EXPERTS, axis=1)
        logits = jnp.where(lane < N_EXPERTS, logits, NEG_INF)
        m = jnp.max(logits, axis=-1, keepdims=True)
        e = jnp.exp(logits - m)
        aff_ref[rows, :] = e / jnp.sum(e, axis=-1, keepdims=True)


def _outproj(yf, ya, x, mod, g, wo, wrh, tm=1024):
    B = x.shape[0]
    const = lambda shape: pl.BlockSpec(shape, lambda b, i: (0,) * len(shape))
    tok = lambda w: pl.BlockSpec((None, tm, w), lambda b, i: (b, i, 0))
    return pl.pallas_call(
        _outproj_kernel,
        grid=(B, SEQ // tm),
        in_specs=[tok(FOURIER_WIDTH), tok(ATTN_WIDTH), tok(D_MODEL),
                  pl.BlockSpec((None, N_ADA, D_MODEL), lambda b, i: (b, 0, 0)),
                  const((1, D_MODEL)),
                  const((D_MODEL, D_MODEL)),
                  const((D_MODEL, LANES))],
        out_specs=[tok(D_MODEL),
                   pl.BlockSpec((None, tm * PACK_ROWS, LANES), lambda b, i: (b, i, 0)),
                   tok(LANES)],
        out_shape=[jax.ShapeDtypeStruct((B, SEQ, D_MODEL), F32),
                   jax.ShapeDtypeStruct((B, SEQ * PACK_ROWS, LANES), jnp.uint32),
                   jax.ShapeDtypeStruct((B, SEQ, LANES), F32)],
        compiler_params=_params(("parallel", "parallel"), VMEM_LIMIT),
        name="outproj",
    )(yf, ya, x, mod, g, wo, wrh)


ROUTE_BATCHES = 2
SEARCH_BITS = 3


def _route_kernel(aff_ref, tri_ref, idx_ref, gate_ref):
    for bb in range(ROUTE_BATCHES):
        _route_one(aff_ref[bb], tri_ref[...], idx_ref.at[bb], gate_ref.at[bb])


def _route_one(aff, tri, idx_ref, gate_ref):
    aff_t = jnp.transpose(aff)[:N_EXPERTS]
    bits = pltpu.bitcast(aff_t, jnp.int32)
    cap = float(CAPACITY)

    t = jnp.zeros((N_EXPERTS, 1), jnp.int32)
    for shift in range(30 - SEARCH_BITS, -1, -SEARCH_BITS):
        digit = jnp.zeros((N_EXPERTS, 1), jnp.int32)
        for k in range(1, 1 << SEARCH_BITS):
            cnt = jnp.sum(jnp.where(bits >= (t | (k << shift)), 1.0, 0.0), axis=1, keepdims=True)
            digit = digit + jnp.where(cnt >= cap, 1, 0)
        t = t | (digit << shift)
    gt = bits > t
    eq = bits == t
    need = cap - jnp.sum(jnp.where(gt, 1.0, 0.0), axis=1, keepdims=True)

    n_chunks = SEQ // LANES

    def prefix(flags_f32):
        outs = []
        carry = jnp.zeros((N_EXPERTS, 1), F32)
        for c in range(n_chunks):
            f = flags_f32[:, c * LANES:(c + 1) * LANES]
            incl = jnp.dot(f.astype(BF16), tri, preferred_element_type=F32)
            outs.append(incl - f + carry)
            carry = carry + jnp.sum(f, axis=1, keepdims=True)
        return jnp.concatenate(outs, axis=1)

    eq_f = jnp.where(eq, 1.0, 0.0)
    eq_rank = prefix(eq_f)
    sel_f = jnp.where(gt, 1.0, jnp.where(eq_rank < need, eq_f, 0.0))
    pos = prefix(sel_f)
    posm = jnp.where(sel_f > 0.0, pos, -1.0)

    hi = aff_t.astype(BF16).astype(F32)
    r1 = aff_t - hi
    mid = r1.astype(BF16).astype(F32)
    lo = r1 - mid
    tok = lax.broadcasted_iota(jnp.int32, (N_EXPERTS, SEQ), 1)
    row = lax.broadcasted_iota(jnp.int32, (N_EXPERTS, SEQ), 0)
    tok_rows = jnp.where(row == 0, (tok >> 6).astype(F32),
                         jnp.where(row == 1, (tok & 63).astype(F32), 0.0))
    vals_t = jnp.concatenate([hi, mid, lo, tok_rows], axis=0).astype(BF16)

    slot = lax.broadcasted_iota(jnp.int32, (CAPACITY, SEQ), 0).astype(F32).astype(BF16)
    posm_b = posm.astype(BF16)
    one_b = jnp.ones((CAPACITY, SEQ), BF16)
    zero_b = jnp.zeros((CAPACITY, SEQ), BF16)
    for e in range(N_EXPERTS):
        onehot = jnp.where(posm_b[e:e + 1, :] == slot, one_b, zero_b)
        res = lax.dot_general(vals_t, onehot, (((1,), (1,)), ((), ())),
                              preferred_element_type=F32)
        cols = pl.ds(e * CAPACITY, CAPACITY)
        tok_idx = res[3 * N_EXPERTS:3 * N_EXPERTS + 1] * 64.0 + res[3 * N_EXPERTS + 1:
                                                                    3 * N_EXPERTS + 2]
        idx_ref[:, cols] = tok_idx.astype(jnp.int32) * PACK_ROWS
        gate_ref[:, cols] = (res[e:e + 1] + res[N_EXPERTS + e:N_EXPERTS + e + 1]
                             + res[2 * N_EXPERTS + e:2 * N_EXPERTS + e + 1])


def _route(aff):
    B = aff.shape[0]
    n = N_EXPERTS * CAPACITY
    return pl.pallas_call(
        _route_kernel,
        grid=(B // ROUTE_BATCHES,),
        in_specs=[pl.BlockSpec((ROUTE_BATCHES, SEQ, LANES), lambda b: (b, 0, 0)),
                  pl.BlockSpec((LANES, LANES), lambda b: (0, 0))],
        out_specs=[pl.BlockSpec((ROUTE_BATCHES, 1, n), lambda b: (b, 0, 0)),
                   pl.BlockSpec((ROUTE_BATCHES, 1, n), lambda b: (b, 0, 0))],
        out_shape=[jax.ShapeDtypeStruct((B, 1, n), jnp.int32),
                   jax.ShapeDtypeStruct((B, 1, n), F32)],
        compiler_params=_params(("parallel",), VMEM_LIMIT),
        name="route",
    )(aff, jnp.asarray(_tri_incl()).astype(BF16))


PAIR = 4


def _moe_kernel(idx_ref, h2_ref, w_ref, y_ref, xin0_ref, xin1_ref):
    e = pl.program_id(1)
    last = N_EXPERTS - 1
    n = N_EXPERTS * CAPACITY
    rows = PAIR * CAPACITY

    def gather_rows(ex, dst_ref):
        for bb in range(PAIR):
            base = bb * n + ex * CAPACITY
            for p in range(CAPACITY):
                off = idx_ref[0, base + p]
                tile = h2_ref[bb, pl.ds(pl.multiple_of((off >> 3) << 3, SUBLANES), SUBLANES), :]
                tile = pltpu.roll(tile, off & PACK_ROWS, axis=0)
                dst_ref[pl.ds((bb * CAPACITY + p) * PACK_ROWS, PACK_ROWS), :] = tile[:PACK_ROWS]

    def expert(xin_ref):
        words = [xin_ref[pl.ds(j, rows, stride=PACK_ROWS), :] for j in range(PACK_ROWS)]
        xin = jnp.concatenate(
            [pltpu.bitcast(w & jnp.uint32(0xFFFF0000), F32).astype(BF16) for w in words]
            + [pltpu.bitcast(w << 16, F32).astype(BF16) for w in words], axis=1)
        for bb in range(PAIR):
            xb = xin[bb * CAPACITY:(bb + 1) * CAPACITY]
            a = jnp.dot(xb, w_ref[0], preferred_element_type=F32)
            u = jnp.dot(xb, w_ref[1], preferred_element_type=F32)
            hmid = (a * (1.0 / (1.0 + jnp.exp(-a))) * u).astype(BF16)
            y = jnp.dot(hmid, w_ref[2], preferred_element_type=F32).astype(BF16)
            top = pltpu.bitcast(y[:, :D_MODEL // 2].astype(F32), jnp.uint32)
            bot = pltpu.bitcast(y[:, D_MODEL // 2:].astype(F32), jnp.uint32)
            words = top | (bot >> 16)
            for j in range(PACK_ROWS):
                y_ref[bb, pl.ds(j, CAPACITY, stride=PACK_ROWS), :] = (
                    words[:, j * LANES:(j + 1) * LANES])

    @pl.when(e == 0)
    def _():
        gather_rows(0, xin0_ref)

    def step(xin_cur, xin_nxt):
        gather_rows(jnp.minimum(e + 1, last), xin_nxt)
        expert(xin_cur)

    @pl.when(e % 2 == 0)
    def _():
        step(xin0_ref, xin1_ref)

    @pl.when(e % 2 == 1)
    def _():
        step(xin1_ref, xin0_ref)


def _moe(idx, h2, experts):
    B = h2.shape[0]
    n = N_EXPERTS * CAPACITY
    rows = SEQ * PACK_ROWS
    pairs = B // PAIR
    stage = pltpu.VMEM((PAIR * CAPACITY * PACK_ROWS, LANES), jnp.uint32)
    out = pl.pallas_call(
        _moe_kernel,
        grid=(pairs, N_EXPERTS),
        in_specs=[pl.BlockSpec((None, 1, PAIR * n), lambda b, e: (b, 0, 0),
                               memory_space=pltpu.SMEM),
                  pl.BlockSpec((None, PAIR, rows, LANES), lambda b, e: (b, 0, 0, 0)),
                  pl.BlockSpec((3, None, D_MODEL, D_MODEL), lambda b, e: (0, e, 0, 0))],
        out_specs=pl.BlockSpec((None, PAIR, CAPACITY * PACK_ROWS, LANES),
                               lambda b, e: (b, 0, e, 0)),
        out_shape=jax.ShapeDtypeStruct((pairs, PAIR, n * PACK_ROWS, LANES), jnp.uint32),
        scratch_shapes=[stage, stage],
        compiler_params=_params(("parallel", "arbitrary"), VMEM_LIMIT),
        name="moe",
    )(idx.reshape(pairs, 1, PAIR * n), h2.reshape(pairs, PAIR, rows, LANES), experts)
    return out.reshape(B, n * PACK_ROWS, LANES)


COMBINE_EXPERTS = 8
SCATTER_UNROLL = 8
COMBINE_ROWS = 256


def _combine_kernel(idx_ref, gate_ref, y_ref, x1_ref, mod_ref, o_ref, acc_ref):
    j = pl.program_id(1)
    slots = COMBINE_EXPERTS * CAPACITY

    @pl.when(j == 0)
    def _():
        acc_ref[...] = jnp.zeros_like(acc_ref)

    upper = lax.broadcasted_iota(jnp.int32, (SUBLANES, LANES), 0) < PACK_ROWS
    for g in range(slots // SCATTER_UNROLL):
        new = []
        for u in range(0, SCATTER_UNROLL, 2):
            r = g * SCATTER_UNROLL + u
            words = y_ref[pl.ds(r * PACK_ROWS, SUBLANES), :]
            hi = pltpu.bitcast(words & jnp.uint32(0xFFFF0000), F32)
            lo = pltpu.bitcast(words << 16, F32)
            slabs = (jnp.where(upper, hi, pltpu.roll(lo, PACK_ROWS, axis=0)),
                     jnp.where(upper, pltpu.roll(hi, PACK_ROWS, axis=0), lo))
            for k in range(2):
                dst = pl.multiple_of(idx_ref[0, r + k], ROW_SLAB)
                new.append((dst, acc_ref[pl.ds(dst, ROW_SLAB), :]
                            + slabs[k] * gate_ref[0, r + k]))
        for dst, val in new:
            acc_ref[pl.ds(dst, ROW_SLAB), :] = val

    @pl.when(j == pl.num_programs(1) - 1)
    def _():
        for rb in range(SEQ // COMBINE_ROWS):
            rows = pl.ds(rb * COMBINE_ROWS, COMBINE_ROWS)
            slab = acc_ref[pl.ds(rb * COMBINE_ROWS * ROW_SLAB, COMBINE_ROWS * ROW_SLAB), :]
            chunks = pltpu.einshape("tcl->ctl", slab.reshape(COMBINE_ROWS, ROW_SLAB, LANES))
            for c in range(ROW_SLAB):
                cols = slice(c * LANES, (c + 1) * LANES)
                o_ref[rows, cols] = x1_ref[rows, cols] + mod_ref[5:6, cols] * chunks[c]


def _combine(idx, gate, y, x1, mod):
    B = x1.shape[0]
    n = N_EXPERTS * CAPACITY
    tok = pl.BlockSpec((None, SEQ, D_MODEL), lambda b, j: (b, 0, 0))
    smem = pl.BlockSpec((None, 1, COMBINE_EXPERTS * CAPACITY), lambda b, j: (b, 0, j),
                        memory_space=pltpu.SMEM)
    return pl.pallas_call(
        _combine_kernel,
        grid=(B, N_EXPERTS // COMBINE_EXPERTS),
        in_specs=[smem, smem,
                  pl.BlockSpec((None, COMBINE_EXPERTS * CAPACITY * PACK_ROWS, LANES),
                               lambda b, j: (b, j, 0)),
                  tok,
                  pl.BlockSpec((None, N_ADA, D_MODEL), lambda b, j: (b, 0, 0))],
        out_specs=tok,
        out_shape=jax.ShapeDtypeStruct((B, SEQ, D_MODEL), F32),
        scratch_shapes=[pltpu.VMEM((SEQ * ROW_SLAB, LANES), F32)],
        compiler_params=_params(("parallel", "arbitrary"), VMEM_LIMIT),
        name="combine",
    )(idx, gate, y, x1, mod)


def _head_perm():
    perm = []
    for r in range(Q_PER_KV):
        for kv in range(N_KV_HEADS):
            h = kv * Q_PER_KV + r
            perm.extend(range(h * HEAD_DIM, (h + 1) * HEAD_DIM))
    return np.asarray(perm, dtype=np.int32)


def kernel(x, c, rel_bias, w_ada, b_ada, norm_mix_g, norm_ffn_g, w_in, w_fourier, b_fourier,
           q_norm_g, k_norm_g, sink, w_out, w_router, w_gate, w_up, w_down):
    B = x.shape[0]
    perm = _head_perm()
    l = 0
    mod = _ada(c, w_ada[l], b_ada[l]).reshape(B, N_ADA, D_MODEL)
    pq = _fold(w_fourier[l])
    bias = _bias_table(rel_bias)

    wi = w_in[l]
    q_cols = wi[:, FOURIER_WIDTH:FOURIER_WIDTH + ATTN_WIDTH][:, perm]
    win = jnp.concatenate([wi[:, :FOURIER_WIDTH], q_cols, wi[:, FOURIER_WIDTH + ATTN_WIDTH:]],
                          axis=1).astype(BF16)
    gq = (jnp.tile(q_norm_g[l], N_Q_HEADS) * (HEAD_DIM ** -0.5 * LOG2E)).reshape(1, ATTN_WIDTH)
    gk = jnp.tile(k_norm_g[l], N_KV_HEADS).reshape(1, KV_WIDTH)
    a, b, q, k, v, experts = _inproj(x, mod, norm_mix_g[l].reshape(1, D_MODEL), win, pq, gq, gk,
                                     w_gate[l], w_up[l], w_down[l])

    yf = _fourier(a, b, b_fourier[l].reshape(1, FOURIER_WIDTH))
    ya = _attn(sink[l], q, k, v, bias)

    wo = w_out[l]
    wo = jnp.concatenate([wo[:FOURIER_WIDTH], wo[FOURIER_WIDTH:][perm]], axis=0).astype(BF16)
    w_hi = w_router[l].astype(BF16)
    w_lo = (w_router[l] - w_hi.astype(F32)).astype(BF16)
    wrh = jnp.pad(jnp.concatenate([w_hi, w_lo], axis=1), ((0, 0), (0, LANES - 2 * N_EXPERTS)))
    x1, h2, aff = _outproj(yf, ya, x, mod, norm_ffn_g[l].reshape(1, D_MODEL), wo, wrh)

    idx, gate = _route(aff)
    n = N_EXPERTS * CAPACITY
    y = _moe(idx, h2, experts)
    return _combine(idx * (ROW_SLAB // PACK_ROWS), gate, y, x1, mod)
```

```python
import functools
import math

import numpy as np
import jax
import jax.numpy as jnp
from jax import lax
from jax.experimental import pallas as pl
from jax.experimental.pallas import tpu as pltpu

D_MODEL = 1024
SEQ = 2048
HEAD_DIM = 64
FOURIER_WIDTH = 512
ATTN_WIDTH = 512
N_GROUPS = 8
N_Q_HEADS = 8
Q_PER_KV = 4
N_KV_HEADS = 2
KV_WIDTH = 128
IN_PROJ_WIDTH = 1280
WINDOW = 128
BLOCK = 128
SPAN = BLOCK + 2 * WINDOW
N_BLOCKS = SEQ // BLOCK
N_BUCKETS = 32
MAX_DISTANCE = 128
N_EXPERTS = 16
CAPACITY = 2 * SEQ // N_EXPERTS
D_EXPERT = 1024
N_ADA = 6
EPS = 1e-6

LANES = 128
SUBLANES = 8
ROW_SLAB = D_MODEL // LANES
PACK_ROWS = ROW_SLAB // 2
VMEM_LIMIT = 56 * 1024 * 1024

F32 = jnp.float32
BF16 = jnp.bfloat16
NEG_INF = float("-inf")
LOG2E = math.log2(math.e)


def _params(sem, vmem=None):
    return pltpu.CompilerParams(dimension_semantics=sem, vmem_limit_bytes=vmem)


@functools.lru_cache(maxsize=None)
def _chan_dft():
    c = np.arange(HEAD_DIM, dtype=np.int64)
    ph = (c[:, None] * c[None, :]) % HEAD_DIM
    ang = 2.0 * np.pi * ph.astype(np.float64) / HEAD_DIM
    sc = 1.0 / math.sqrt(HEAD_DIM)
    eye = np.eye(N_GROUPS)
    cbd = np.kron(eye, np.cos(ang) * sc)
    sbd = np.kron(eye, np.sin(ang) * sc)
    return cbd.astype(np.float32), sbd.astype(np.float32)


@functools.lru_cache(maxsize=None)
def _bucket_table():
    rel = np.arange(SPAN)[None, :] - WINDOW - np.arange(BLOCK)[:, None]
    half = N_BUCKETS // 2
    max_exact = half // 2
    n = np.abs(rel)
    nf = np.maximum(n, 1).astype(np.float64)
    large = max_exact + (np.log(nf / max_exact) / math.log(MAX_DISTANCE / max_exact)
                         * (half - max_exact)).astype(np.int64)
    sq = np.maximum(n.astype(np.int64) ** 2 // (max_exact * max_exact), 1)
    large_int = max_exact + np.floor(np.log2(sq.astype(np.float64)) + 1e-9).astype(np.int64)
    assert np.array_equal(np.where(n >= max_exact, large, 0), np.where(n >= max_exact, large_int, 0))
    large = np.minimum(large, half - 1)
    bucket = np.where(rel > 0, half, 0) + np.where(n < max_exact, n, large)
    return bucket.astype(np.int32)


@functools.lru_cache(maxsize=None)
def _group_ones(width):
    return np.kron(np.eye(width // HEAD_DIM), np.ones((HEAD_DIM, HEAD_DIM))).astype(np.float32)


@functools.lru_cache(maxsize=None)
def _tri_incl():
    i = np.arange(LANES)
    return (i[:, None] <= i[None, :]).astype(np.float32)


def _ada_kernel(c_ref, w_ref, b_ref, o_ref):
    c = c_ref[...]
    ca = c * (1.0 / (1.0 + jnp.exp(-c)))
    o_ref[...] = jnp.dot(ca, w_ref[...], precision=lax.Precision.HIGHEST,
                         preferred_element_type=F32) + b_ref[...]


def _ada(c, w_ada, b_ada):
    B = c.shape[0]
    n = N_ADA * D_MODEL
    tn = 2 * D_MODEL
    return pl.pallas_call(
        _ada_kernel,
        grid=(n // tn,),
        in_specs=[pl.BlockSpec((B, D_MODEL), lambda j: (0, 0)),
                  pl.BlockSpec((D_MODEL, tn), lambda j: (0, j)),
                  pl.BlockSpec((1, tn), lambda j: (0, j))],
        out_specs=pl.BlockSpec((B, tn), lambda j: (0, j)),
        out_shape=jax.ShapeDtypeStruct((B, n), F32),
        compiler_params=_params(("arbitrary",)),
        name="ada",
    )(c, w_ada, b_ada.reshape(1, n))


def _fold_kernel(cbd_ref, sbd_ref, w_ref, o_ref):
    w = w_ref[...]
    o_ref[:, :FOURIER_WIDTH] = jnp.dot(cbd_ref[...], w, precision=lax.Precision.HIGHEST,
                                       preferred_element_type=F32).astype(BF16)
    o_ref[:, FOURIER_WIDTH:] = jnp.dot(sbd_ref[...], w, precision=lax.Precision.HIGHEST,
                                       preferred_element_type=F32).astype(BF16)


def _fold(w_fourier):
    wbd = (jnp.eye(N_GROUPS, dtype=F32)[:, None, :, None] * w_fourier[:, :, None, :]
           ).reshape(FOURIER_WIDTH, FOURIER_WIDTH)
    cbd, sbd = _chan_dft()
    return pl.pallas_call(
        _fold_kernel,
        out_shape=jax.ShapeDtypeStruct((FOURIER_WIDTH, 2 * FOURIER_WIDTH), BF16),
        name="fold",
    )(jnp.asarray(cbd), jnp.asarray(sbd), wbd)


def _bias_kernel(rb_ref, bucket_ref, o_ref):
    h = pl.program_id(0)
    bk = bucket_ref[...]
    acc = jnp.zeros((BLOCK, SPAN), F32)
    for b in range(N_BUCKETS):
        acc = jnp.where(bk == b, rb_ref[b, h], acc)
    j = lax.broadcasted_iota(jnp.int32, (BLOCK, SPAN), 1)
    q = lax.broadcasted_iota(jnp.int32, (BLOCK, SPAN), 0)
    band = jnp.abs(j - WINDOW - q) <= WINDOW
    base = jnp.where(band, acc * LOG2E, NEG_INF)
    o_ref[0] = jnp.where(j >= WINDOW, base, NEG_INF)
    o_ref[1] = base
    o_ref[2] = jnp.where(j < WINDOW + BLOCK, base, NEG_INF)


def _bias_table(rel_bias):
    return pl.pallas_call(
        _bias_kernel,
        grid=(N_Q_HEADS,),
        in_specs=[pl.BlockSpec(memory_space=pltpu.SMEM),
                  pl.BlockSpec((BLOCK, SPAN), lambda h: (0, 0))],
        out_specs=pl.BlockSpec((3, BLOCK, SPAN), lambda h: (0, h % Q_PER_KV, h // Q_PER_KV)),
        out_shape=jax.ShapeDtypeStruct((3, Q_PER_KV * BLOCK, N_KV_HEADS * SPAN), F32),
        compiler_params=_params(("arbitrary",)),
        name="bias",
    )(rel_bias, jnp.asarray(_bucket_table()))


IN_ROWS = 256


def _inproj_kernel(x_ref, mod_ref, g_ref, win_ref, pq_ref, bdq_ref, bdk_ref, gq_ref, gk_ref,
                   wg_ref, wu_ref, wd_ref,
                   a_ref, b_ref, q_ref, k_ref, v_ref, w_out):
    w_out[0] = wg_ref[...].astype(BF16)
    w_out[1] = wu_ref[...].astype(BF16)
    w_out[2] = wd_ref[...].astype(BF16)
    gain = g_ref[...] * (1.0 + mod_ref[1:2, :])
    shift = mod_ref[0:1, :]
    q0 = FOURIER_WIDTH
    k0 = q0 + ATTN_WIDTH
    v0 = k0 + KV_WIDTH
    for c in range(x_ref.shape[0] // IN_ROWS):
        rows = pl.ds(c * IN_ROWS, IN_ROWS)
        x = x_ref[rows, :]
        ms = jnp.mean(x * x, axis=-1, keepdims=True)
        h = x * lax.rsqrt(ms + EPS) * gain + shift
        proj = jnp.dot(h.astype(BF16), win_ref[...], preferred_element_type=F32)
        uf = proj[:, :FOURIER_WIDTH].astype(BF16)
        ab = jnp.dot(uf, pq_ref[...], preferred_element_type=F32)
        a_ref[rows, :] = ab[:, :FOURIER_WIDTH].astype(BF16)
        b_ref[rows, :] = ab[:, FOURIER_WIDTH:].astype(BF16)
        q = proj[:, q0:k0]
        ssq = jnp.dot((q * q).astype(BF16), bdq_ref[...], preferred_element_type=F32)
        q_ref[rows, :] = (q * lax.rsqrt(ssq * (1.0 / HEAD_DIM) + EPS) * gq_ref[...]).astype(BF16)
        k = proj[:, k0:v0]
        ssk = jnp.dot((k * k).astype(BF16), bdk_ref[...], preferred_element_type=F32)
        k_ref[rows, :] = (k * lax.rsqrt(ssk * (1.0 / HEAD_DIM) + EPS) * gk_ref[...]).astype(BF16)
        v_ref[rows, :] = proj[:, v0:].astype(BF16)


def _inproj(x, mod, g, win, pq, gq, gk, w_gate, w_up, w_down, tm=1024):
    B = x.shape[0]
    steps_per_batch = SEQ // tm
    w_rows = N_EXPERTS * D_MODEL
    w_blk = w_rows // (B * steps_per_batch)
    const = lambda shape: pl.BlockSpec(shape, lambda b, i: (0,) * len(shape))
    tok = lambda w: pl.BlockSpec((None, tm, w), lambda b, i: (b, i, 0))
    wsl = lambda c: pl.BlockSpec((w_blk, c), lambda b, i: (b * steps_per_batch + i, 0))
    sds = lambda w: jax.ShapeDtypeStruct((B, SEQ, w), BF16)
    step = lambda b, i: b * steps_per_batch + i
    a, b, q, k, v, experts = pl.pallas_call(
        _inproj_kernel,
        grid=(B, steps_per_batch),
        in_specs=[tok(D_MODEL),
                  pl.BlockSpec((None, N_ADA, D_MODEL), lambda b, i: (b, 0, 0)),
                  const((1, D_MODEL)),
                  const((D_MODEL, IN_PROJ_WIDTH)),
                  const((FOURIER_WIDTH, 2 * FOURIER_WIDTH)),
                  const((ATTN_WIDTH, ATTN_WIDTH)),
                  const((KV_WIDTH, KV_WIDTH)),
                  const((1, ATTN_WIDTH)),
                  const((1, KV_WIDTH)),
                  wsl(D_EXPERT), wsl(D_EXPERT), wsl(D_MODEL)],
        out_specs=[tok(FOURIER_WIDTH), tok(FOURIER_WIDTH), tok(ATTN_WIDTH), tok(KV_WIDTH),
                   tok(KV_WIDTH),
                   pl.BlockSpec((3, w_blk, D_MODEL), lambda b, i: (0, step(b, i), 0))],
        out_shape=[sds(FOURIER_WIDTH), sds(FOURIER_WIDTH), sds(ATTN_WIDTH), sds(KV_WIDTH),
                   sds(KV_WIDTH), jax.ShapeDtypeStruct((3, w_rows, D_MODEL), BF16)],
        compiler_params=_params(("parallel", "parallel"), VMEM_LIMIT),
        name="inproj",
    )(x, mod, g, win, pq, jnp.asarray(_group_ones(ATTN_WIDTH)).astype(BF16),
      jnp.asarray(_group_ones(KV_WIDTH)).astype(BF16), gq, gk,
      w_gate.reshape(w_rows, D_EXPERT), w_up.reshape(w_rows, D_EXPERT),
      w_down.reshape(N_EXPERTS * D_EXPERT, D_MODEL))
    return a, b, q, k, v, experts.reshape(3, N_EXPERTS, D_MODEL, D_MODEL)


HALF = SEQ // 2
REV = 128
REV_BLOCKS = HALF // REV
FOURIER_ROWS = 512


def _fourier_kernel(cm_ref, sm_ref, psh_ref, alt_ref, altrow_ref, a_ref, b_ref, bf_ref, o_ref):
    psh = psh_ref[...]

    def reversed_block(win_lo, k, src):
        if k == 0:
            return jnp.dot(psh[:, :REV], src(win_lo, REV), preferred_element_type=F32)
        return jnp.dot(psh, src(win_lo, 2 * REV), preferred_element_type=F32)

    def folded(ref, sign):
        blocks = []
        for k in range(REV_BLOCKS):
            lo = SEQ - REV * (k + 1)
            rev = reversed_block(lo, k, lambda s, n: ref[pl.ds(s, n), :])
            blocks.append((ref[pl.ds(k * REV, REV), :].astype(F32) + sign * rev).astype(BF16))
        return jnp.concatenate(blocks, axis=0)

    a_even = folded(a_ref, 1.0)
    b_odd = folded(b_ref, -1.0)
    a_mid = a_ref[pl.ds(HALF, 1), :].astype(F32)
    bias = bf_ref[...]
    z_blocks = []
    for i in range(HALF // FOURIER_ROWS):
        rows = pl.ds(i * FOURIER_ROWS, FOURIER_ROWS)
        yc = jnp.dot(cm_ref[rows, :], a_even, preferred_element_type=F32)
        yc = yc + alt_ref[rows, :] * a_mid + bias
        ys = jnp.dot(sm_ref[rows, :], b_odd, preferred_element_type=F32)
        o_ref[rows, :] = (yc - ys).astype(BF16)
        z_blocks.append((yc + ys).astype(BF16))
    z = jnp.concatenate(z_blocks, axis=0)
    for k in range(REV_BLOCKS):
        lo = HALF - REV * (k + 1)
        top = reversed_block(lo, k, lambda s, n: z[s:s + n])
        o_ref[pl.ds(HALF + k * REV, REV), :] = top.astype(BF16)
    y_mid = jnp.dot(altrow_ref[...], a_ref[...], preferred_element_type=F32)[0:1, :] + bias
    o_ref[pl.ds(HALF, 1), :] = y_mid.astype(BF16)


@functools.lru_cache(maxsize=None)
def _fold_tables():
    s = np.arange(HALF, dtype=np.int64)
    ph = (s[:, None] * s[None, :]) % SEQ
    ang = 2.0 * np.pi * ph.astype(np.float64) / SEQ
    sc = 1.0 / math.sqrt(SEQ)
    cm = (np.cos(ang) * sc).astype(np.float32)
    sm = (np.sin(ang) * sc).astype(np.float32)
    psh = np.zeros((REV, 2 * REV), np.float32)
    psh[np.arange(REV), REV - np.arange(REV)] = 1.0
    alt = (np.where(s % 2 == 0, 1.0, -1.0) * sc).astype(np.float32).reshape(HALF, 1)
    t = np.arange(SEQ)
    altrow = np.zeros((SUBLANES, SEQ), np.float32)
    altrow[0] = np.where(t % 2 == 0, 1.0, -1.0) * sc
    return cm, sm, psh, alt, altrow


def _fourier(a, b, bf):
    B = a.shape[0]
    cm, sm, psh, alt, altrow = _fold_tables()
    tok = pl.BlockSpec((None, SEQ, FOURIER_WIDTH), lambda i: (i, 0, 0))
    const = lambda shape: pl.BlockSpec(shape, lambda i: (0,) * len(shape))
    return pl.pallas_call(
        _fourier_kernel,
        grid=(B,),
        in_specs=[const((HALF, HALF)), const((HALF, HALF)), const((REV, 2 * REV)),
                  const((HALF, 1)), const((SUBLANES, SEQ)),
                  tok, tok, const((1, FOURIER_WIDTH))],
        out_specs=tok,
        out_shape=jax.ShapeDtypeStruct((B, SEQ, FOURIER_WIDTH), BF16),
        compiler_params=_params(("parallel",), VMEM_LIMIT),
        name="fourier",
    )(jnp.asarray(cm).astype(BF16), jnp.asarray(sm).astype(BF16), jnp.asarray(psh).astype(BF16),
      jnp.asarray(alt), jnp.asarray(altrow).astype(BF16), a, b, bf)


ATT_SUB = 16
ATT_ROWS = ATT_SUB * BLOCK
ATT_STEPS = N_BLOCKS // ATT_SUB


def _attn_kernel(sink_ref, q_ref, kl_ref, km_ref, kr_ref, vl_ref, vm_ref, vr_ref, bias_ref,
                 o_ref):
    i = pl.program_id(1)
    keys = jnp.concatenate([kl_ref[...], km_ref[...], kr_ref[...]], axis=0)
    vals = jnp.concatenate([vl_ref[...], vm_ref[...], vr_ref[...]], axis=0)
    first_k = lax.broadcasted_iota(jnp.int32, keys.shape, 1) < HEAD_DIM
    first_q = lax.broadcasted_iota(jnp.int32, (Q_PER_KV * BLOCK, LANES), 1) < HEAD_DIM
    row_head = lax.broadcasted_iota(jnp.int32, (Q_PER_KV * BLOCK, 1), 0) // BLOCK
    zero = jnp.zeros_like(keys)
    ones_lo = jnp.where(first_k, 1.0, 0.0).astype(BF16)
    ones_hi = jnp.where(first_k, 0.0, 1.0).astype(BF16)
    keys_kv = [jnp.where(first_k, keys, zero), jnp.where(first_k, zero, keys)]
    vals_kv = [jnp.concatenate([jnp.where(first_k, vals, zero), ones_lo], axis=1),
               jnp.concatenate([jnp.where(first_k, zero, vals), ones_hi], axis=1)]
    sinks = []
    for kv in range(N_KV_HEADS):
        sink = jnp.zeros((Q_PER_KV * BLOCK, 1), F32)
        for r in range(Q_PER_KV):
            sink = jnp.where(row_head == r, sink_ref[kv * Q_PER_KV + r] * LOG2E, sink)
        sinks.append(sink)
    for j in range(ATT_SUB):
        variant = 1
        if j == 0:
            variant = jnp.where(i == 0, 0, variant)
        if j == ATT_SUB - 1:
            variant = jnp.where(i == ATT_STEPS - 1, 2, variant)
        qrows = pl.ds(j * BLOCK, BLOCK)
        krows = slice(j * BLOCK, j * BLOCK + SPAN)
        qs = jnp.concatenate([q_ref[qrows, r * LANES:(r + 1) * LANES] for r in range(Q_PER_KV)],
                             axis=0)
        keys2 = jnp.concatenate([keys_kv[0][krows], keys_kv[1][krows]], axis=0)
        vals2 = jnp.concatenate([vals_kv[0][krows], vals_kv[1][krows]], axis=0)
        logits = lax.dot_general(qs, keys2, (((1,), (1,)), ((), ())),
                                 preferred_element_type=F32)
        logits = logits + bias_ref[variant]
        ms = [jnp.maximum(jnp.max(logits[:, kv * SPAN:(kv + 1) * SPAN], axis=-1, keepdims=True),
                          sinks[kv]) for kv in range(N_KV_HEADS)]
        p = jnp.exp2(jnp.concatenate([logits[:, kv * SPAN:(kv + 1) * SPAN] - ms[kv]
                                      for kv in range(N_KV_HEADS)], axis=1).astype(BF16))
        pv = jnp.dot(p, vals2, preferred_element_type=F32)
        denom = pv[:, LANES:] + jnp.exp2(jnp.where(first_q, sinks[0] - ms[0], sinks[1] - ms[1]))
        out = (pv[:, :LANES] / denom).astype(BF16)
        for r in range(Q_PER_KV):
            o_ref[qrows, r * LANES:(r + 1) * LANES] = out[r * BLOCK:(r + 1) * BLOCK]


def _attn(sink, q, k, v, bias):
    B = q.shape[0]
    edge = lambda f: pl.BlockSpec((None, BLOCK, KV_WIDTH), lambda b, i: (b, f(i), 0))
    left = lambda i: jnp.maximum(i * ATT_SUB - 1, 0)
    right = lambda i: jnp.minimum((i + 1) * ATT_SUB, N_BLOCKS - 1)
    mid = pl.BlockSpec((None, ATT_ROWS, KV_WIDTH), lambda b, i: (b, i, 0))
    qspec = pl.BlockSpec((None, ATT_ROWS, ATTN_WIDTH), lambda b, i: (b, i, 0))
    return pl.pallas_call(
        _attn_kernel,
        grid=(B, ATT_STEPS),
        in_specs=[pl.BlockSpec(memory_space=pltpu.SMEM), qspec,
                  edge(left), mid, edge(right),
                  edge(left), mid, edge(right),
                  pl.BlockSpec((3, Q_PER_KV * BLOCK, N_KV_HEADS * SPAN), lambda b, i: (0, 0, 0))],
        out_specs=qspec,
        out_shape=jax.ShapeDtypeStruct((B, SEQ, ATTN_WIDTH), BF16),
        compiler_params=_params(("parallel", "parallel"), VMEM_LIMIT),
        name="attn",
    )(sink, q, k, k, k, v, v, v, bias)


OUT_ROWS = 256


def _outproj_kernel(yf_ref, ya_ref, x_ref, mod_ref, g_ref, wo_ref, wrh_ref,
                    x1_ref, h2_ref, aff_ref):
    tm = x_ref.shape[0]
    gain = g_ref[...] * (1.0 + mod_ref[4:5, :])
    shift = mod_ref[3:4, :]
    gate1 = mod_ref[2:3, :]
    lane = lax.broadcasted_iota(jnp.int32, (OUT_ROWS, LANES), 1)
    chunks = range(tm // OUT_ROWS)
    mixed = [jnp.dot(jnp.concatenate([yf_ref[pl.ds(c * OUT_ROWS, OUT_ROWS), :],
                                      ya_ref[pl.ds(c * OUT_ROWS, OUT_ROWS), :]], axis=1),
                     wo_ref[...], preferred_element_type=F32) for c in chunks]
    for c in chunks:
        rows = pl.ds(c * OUT_ROWS, OUT_ROWS)
        x1 = x_ref[rows, :] + gate1 * mixed[c]
        x1_ref[rows, :] = x1
        ms = jnp.mean(x1 * x1, axis=-1, keepdims=True)
        h2 = x1 * lax.rsqrt(ms + EPS) * gain + shift
        hi = h2.astype(BF16)
        top = pltpu.bitcast(hi[:, :D_MODEL // 2].astype(F32), jnp.uint32)
        bot = pltpu.bitcast(hi[:, D_MODEL // 2:].astype(F32), jnp.uint32)
        words = top | (bot >> 16)
        for j in range(PACK_ROWS):
            h2_ref[pl.ds(c * OUT_ROWS * PACK_ROWS + j, OUT_ROWS, stride=PACK_ROWS), :] = (
                words[:, j * LANES:(j + 1) * LANES])
        part = jnp.dot(hi, wrh_ref[...], preferred_element_type=F32)
        logits = part + pltpu.roll(part, LANES - N_EXPERTS, axis=1)
        logits = jnp.where(lane < N_EXPERTS, logits, NEG_INF)
        m = jnp.max(logits, axis=-1, keepdims=True)
        e = jnp.exp(logits - m)
        aff_ref[rows, :] = e / jnp.sum(e, axis=-1, keepdims=True)


def _outproj(yf, ya, x, mod, g, wo, wrh, tm=1024):
    B = x.shape[0]
    const = lambda shape: pl.BlockSpec(shape, lambda b, i: (0,) * len(shape))
    tok = lambda w: pl.BlockSpec((None, tm, w), lambda b, i: (b, i, 0))
    return pl.pallas_call(
        _outproj_kernel,
        grid=(B, SEQ // tm),
        in_specs=[tok(FOURIER_WIDTH), tok(ATTN_WIDTH), tok(D_MODEL),
                  pl.BlockSpec((None, N_ADA, D_MODEL), lambda b, i: (b, 0, 0)),
                  const((1, D_MODEL)),
                  const((D_MODEL, D_MODEL)),
                  const((D_MODEL, LANES))],
        out_specs=[tok(D_MODEL),
                   pl.BlockSpec((None, tm * PACK_ROWS, LANES), lambda b, i: (b, i, 0)),
                   tok(LANES)],
        out_shape=[jax.ShapeDtypeStruct((B, SEQ, D_MODEL), F32),
                   jax.ShapeDtypeStruct((B, SEQ * PACK_ROWS, LANES), jnp.uint32),
                   jax.ShapeDtypeStruct((B, SEQ, LANES), F32)],
        compiler_params=_params(("parallel", "parallel"), VMEM_LIMIT),
        name="outproj",
    )(yf, ya, x, mod, g, wo, wrh)


ROUTE_BATCHES = 4
SEARCH_BITS = 3


def _route_kernel(aff_ref, tri_ref, idx_ref, gate_ref):
    for bb in range(ROUTE_BATCHES):
        _route_one(aff_ref[bb], tri_ref[...], idx_ref.at[bb], gate_ref.at[bb])


def _route_one(aff, tri, idx_ref, gate_ref):
    aff_t = jnp.transpose(aff)[:N_EXPERTS]
    bits = pltpu.bitcast(aff_t, jnp.int32)
    cap = float(CAPACITY)

    t = jnp.zeros((N_EXPERTS, 1), jnp.int32)
    for shift in range(30 - SEARCH_BITS, -1, -SEARCH_BITS):
        digit = jnp.zeros((N_EXPERTS, 1), jnp.int32)
        for k in range(1, 1 << SEARCH_BITS):
            cnt = jnp.sum(jnp.where(bits >= (t | (k << shift)), 1.0, 0.0), axis=1, keepdims=True)
            digit = digit + jnp.where(cnt >= cap, 1, 0)
        t = t | (digit << shift)
    gt = bits > t
    eq = bits == t
    need = cap - jnp.sum(jnp.where(gt, 1.0, 0.0), axis=1, keepdims=True)

    n_chunks = SEQ // LANES

    def prefix(flags_f32):
        outs = []
        carry = jnp.zeros((N_EXPERTS, 1), F32)
        for c in range(n_chunks):
            f = flags_f32[:, c * LANES:(c + 1) * LANES]
            incl = jnp.dot(f.astype(BF16), tri, preferred_element_type=F32)
            outs.append(incl - f + carry)
            carry = carry + jnp.sum(f, axis=1, keepdims=True)
        return jnp.concatenate(outs, axis=1)

    eq_f = jnp.where(eq, 1.0, 0.0)
    eq_rank = prefix(eq_f)
    sel_f = jnp.where(gt, 1.0, jnp.where(eq_rank < need, eq_f, 0.0))
    pos = prefix(sel_f)
    posm = jnp.where(sel_f > 0.0, pos, -1.0)

    hi = aff_t.astype(BF16).astype(F32)
    r1 = aff_t - hi
    mid = r1.astype(BF16).astype(F32)
    lo = r1 - mid
    tok = lax.broadcasted_iota(jnp.int32, (N_EXPERTS, SEQ), 1)
    row = lax.broadcasted_iota(jnp.int32, (N_EXPERTS, SEQ), 0)
    tok_rows = jnp.where(row == 0, (tok >> 6).astype(F32),
                         jnp.where(row == 1, (tok & 63).astype(F32), 0.0))
    vals_t = jnp.concatenate([hi, mid, lo, tok_rows], axis=0).astype(BF16)

    slot = lax.broadcasted_iota(jnp.int32, (CAPACITY, SEQ), 0).astype(F32).astype(BF16)
    posm_b = posm.astype(BF16)
    one_b = jnp.ones((CAPACITY, SEQ), BF16)
    zero_b = jnp.zeros((CAPACITY, SEQ), BF16)
    for e in range(N_EXPERTS):
        onehot = jnp.where(posm_b[e:e + 1, :] == slot, one_b, zero_b)
        res = lax.dot_general(vals_t, onehot, (((1,), (1,)), ((), ())),
                              preferred_element_type=F32)
        cols = pl.ds(e * CAPACITY, CAPACITY)
        tok_idx = res[3 * N_EXPERTS:3 * N_EXPERTS + 1] * 64.0 + res[3 * N_EXPERTS + 1:
                                                                    3 * N_EXPERTS + 2]
        idx_ref[:, cols] = tok_idx.astype(jnp.int32) * PACK_ROWS
        gate_ref[:, cols] = (res[e:e + 1] + res[N_EXPERTS + e:N_EXPERTS + e + 1]
                             + res[2 * N_EXPERTS + e:2 * N_EXPERTS + e + 1])


def _route(aff):
    B = aff.shape[0]
    n = N_EXPERTS * CAPACITY
    return pl.pallas_call(
        _route_kernel,
        grid=(B // ROUTE_BATCHES,),
        in_specs=[pl.BlockSpec((ROUTE_BATCHES, SEQ, LANES), lambda b: (b, 0, 0)),
                  pl.BlockSpec((LANES, LANES), lambda b: (0, 0))],
        out_specs=[pl.BlockSpec((ROUTE_BATCHES, 1, n), lambda b: (b, 0, 0)),
                   pl.BlockSpec((ROUTE_BATCHES, 1, n), lambda b: (b, 0, 0))],
        out_shape=[jax.ShapeDtypeStruct((B, 1, n), jnp.int32),
                   jax.ShapeDtypeStruct((B, 1, n), F32)],
        compiler_params=_params(("parallel",), VMEM_LIMIT),
        name="route",
    )(aff, jnp.asarray(_tri_incl()).astype(BF16))


PAIR = 4


def _moe_kernel(idx_ref, h2_ref, w_ref, y_ref, xin0_ref, xin1_ref):
    e = pl.program_id(1)
    last = N_EXPERTS - 1
    n = N_EXPERTS * CAPACITY
    rows = PAIR * CAPACITY

    def gather_rows(ex, dst_ref):
        for bb in range(PAIR):
            base = bb * n + ex * CAPACITY
            for p in range(CAPACITY):
                off = idx_ref[0, base + p]
                tile = h2_ref[bb, pl.ds(pl.multiple_of((off >> 3) << 3, SUBLANES), SUBLANES), :]
                tile = pltpu.roll(tile, off & PACK_ROWS, axis=0)
                dst_ref[pl.ds((bb * CAPACITY + p) * PACK_ROWS, PACK_ROWS), :] = tile[:PACK_ROWS]

    def expert(xin_ref):
        words = [xin_ref[pl.ds(j, rows, stride=PACK_ROWS), :] for j in range(PACK_ROWS)]
        xin = jnp.concatenate(
            [pltpu.bitcast(w & jnp.uint32(0xFFFF0000), F32).astype(BF16) for w in words]
            + [pltpu.bitcast(w << 16, F32).astype(BF16) for w in words], axis=1)
        for bb in range(PAIR):
            xb = xin[bb * CAPACITY:(bb + 1) * CAPACITY]
            a = jnp.dot(xb, w_ref[0], preferred_element_type=F32)
            u = jnp.dot(xb, w_ref[1], preferred_element_type=F32)
            hmid = (a * (1.0 / (1.0 + jnp.exp(-a))) * u).astype(BF16)
            y = jnp.dot(hmid, w_ref[2], preferred_element_type=F32).astype(BF16)
            top = pltpu.bitcast(y[:, :D_MODEL // 2].astype(F32), jnp.uint32)
            bot = pltpu.bitcast(y[:, D_MODEL // 2:].astype(F32), jnp.uint32)
            words = top | (bot >> 16)
            for j in range(PACK_ROWS):
                y_ref[bb, pl.ds(j, CAPACITY, stride=PACK_ROWS), :] = (
                    words[:, j * LANES:(j + 1) * LANES])

    @pl.when(e == 0)
    def _():
        gather_rows(0, xin0_ref)

    def step(xin_cur, xin_nxt):
        gather_rows(jnp.minimum(e + 1, last), xin_nxt)
        expert(xin_cur)

    @pl.when(e % 2 == 0)
    def _():
        step(xin0_ref, xin1_ref)

    @pl.when(e % 2 == 1)
    def _():
        step(xin1_ref, xin0_ref)


def _moe(idx, h2, experts):
    B = h2.shape[0]
    n = N_EXPERTS * CAPACITY
    rows = SEQ * PACK_ROWS
    pairs = B // PAIR
    stage = pltpu.VMEM((PAIR * CAPACITY * PACK_ROWS, LANES), jnp.uint32)
    out = pl.pallas_call(
        _moe_kernel,
        grid=(pairs, N_EXPERTS),
        in_specs=[pl.BlockSpec((None, 1, PAIR * n), lambda b, e: (b, 0, 0),
                               memory_space=pltpu.SMEM),
                  pl.BlockSpec((None, PAIR, rows, LANES), lambda b, e: (b, 0, 0, 0)),
                  pl.BlockSpec((3, None, D_MODEL, D_MODEL), lambda b, e: (0, e, 0, 0))],
        out_specs=pl.BlockSpec((None, PAIR, CAPACITY * PACK_ROWS, LANES),
                               lambda b, e: (b, 0, e, 0)),
        out_shape=jax.ShapeDtypeStruct((pairs, PAIR, n * PACK_ROWS, LANES), jnp.uint32),
        scratch_shapes=[stage, stage],
        compiler_params=_params(("parallel", "arbitrary"), VMEM_LIMIT),
        name="moe",
    )(idx.reshape(pairs, 1, PAIR * n), h2.reshape(pairs, PAIR, rows, LANES), experts)
    return out.reshape(B, n * PACK_ROWS, LANES)


COMBINE_EXPERTS = 8
SCATTER_UNROLL = 8
COMBINE_ROWS = 256


def _combine_kernel(idx_ref, gate_ref, y_ref, x1_ref, mod_ref, o_ref, acc_ref):
    j = pl.program_id(1)
    slots = COMBINE_EXPERTS * CAPACITY
    base = j * slots

    @pl.when(j == 0)
    def _():
        acc_ref[...] = jnp.zeros_like(acc_ref)

    upper = lax.broadcasted_iota(jnp.int32, (SUBLANES, LANES), 0) < PACK_ROWS
    for g in range(slots // SCATTER_UNROLL):
        new = []
        for u in range(0, SCATTER_UNROLL, 2):
            r = g * SCATTER_UNROLL + u
            words = y_ref[pl.ds(r * PACK_ROWS, SUBLANES), :]
            hi = pltpu.bitcast(words & jnp.uint32(0xFFFF0000), F32)
            lo = pltpu.bitcast(words << 16, F32)
            slabs = (jnp.where(upper, hi, pltpu.roll(lo, PACK_ROWS, axis=0)),
                     jnp.where(upper, pltpu.roll(hi, PACK_ROWS, axis=0), lo))
            for k in range(2):
                dst = pl.multiple_of(idx_ref[0, base + r + k] * (ROW_SLAB // PACK_ROWS), ROW_SLAB)
                new.append((dst, acc_ref[pl.ds(dst, ROW_SLAB), :]
                            + slabs[k] * gate_ref[0, base + r + k]))
        for dst, val in new:
            acc_ref[pl.ds(dst, ROW_SLAB), :] = val

    @pl.when(j == pl.num_programs(1) - 1)
    def _():
        for rb in range(SEQ // COMBINE_ROWS):
            rows = pl.ds(rb * COMBINE_ROWS, COMBINE_ROWS)
            for c in range(ROW_SLAB):
                cols = slice(c * LANES, (c + 1) * LANES)
                chunk = acc_ref[pl.ds(rb * COMBINE_ROWS * ROW_SLAB + c, COMBINE_ROWS,
                                      stride=ROW_SLAB), :]
                o_ref[rows, cols] = x1_ref[rows, cols] + mod_ref[5:6, cols] * chunk


def _combine(idx, gate, y, x1, mod):
    B = x1.shape[0]
    n = N_EXPERTS * CAPACITY
    tok = pl.BlockSpec((None, SEQ, D_MODEL), lambda b, j: (b, 0, 0))
    smem = pl.BlockSpec((None, 1, n), lambda b, j: (b, 0, 0), memory_space=pltpu.SMEM)
    return pl.pallas_call(
        _combine_kernel,
        grid=(B, N_EXPERTS // COMBINE_EXPERTS),
        in_specs=[smem, smem,
                  pl.BlockSpec((None, COMBINE_EXPERTS * CAPACITY * PACK_ROWS, LANES),
                               lambda b, j: (b, j, 0)),
                  tok,
                  pl.BlockSpec((None, N_ADA, D_MODEL), lambda b, j: (b, 0, 0))],
        out_specs=tok,
        out_shape=jax.ShapeDtypeStruct((B, SEQ, D_MODEL), F32),
        scratch_shapes=[pltpu.VMEM((SEQ * ROW_SLAB, LANES), F32)],
        compiler_params=_params(("parallel", "arbitrary"), VMEM_LIMIT),
        name="combine",
    )(idx, gate, y, x1, mod)


def _head_perm():
    perm = []
    for r in range(Q_PER_KV):
        for kv in range(N_KV_HEADS):
            h = kv * Q_PER_KV + r
            perm.extend(range(h * HEAD_DIM, (h + 1) * HEAD_DIM))
    return np.asarray(perm, dtype=np.int32)


def kernel(x, c, rel_bias, w_ada, b_ada, norm_mix_g, norm_ffn_g, w_in, w_fourier, b_fourier,
           q_norm_g, k_norm_g, sink, w_out, w_router, w_gate, w_up, w_down):
    B = x.shape[0]
    perm = _head_perm()
    l = 0
    mod = _ada(c, w_ada[l], b_ada[l]).reshape(B, N_ADA, D_MODEL)
    pq = _fold(w_fourier[l])
    bias = _bias_table(rel_bias)

    wi = w_in[l]
    q_cols = wi[:, FOURIER_WIDTH:FOURIER_WIDTH + ATTN_WIDTH][:, perm]
    win = jnp.concatenate([wi[:, :FOURIER_WIDTH], q_cols, wi[:, FOURIER_WIDTH + ATTN_WIDTH:]],
                          axis=1).astype(BF16)
    gq = (jnp.tile(q_norm_g[l], N_Q_HEADS) * (HEAD_DIM ** -0.5 * LOG2E)).reshape(1, ATTN_WIDTH)
    gk = jnp.tile(k_norm_g[l], N_KV_HEADS).reshape(1, KV_WIDTH)
    a, b, q, k, v, experts = _inproj(x, mod, norm_mix_g[l].reshape(1, D_MODEL), win, pq, gq, gk,
                                     w_gate[l], w_up[l], w_down[l])

    yf = _fourier(a, b, b_fourier[l].reshape(1, FOURIER_WIDTH))
    ya = _attn(sink[l], q, k, v, bias)

    wo = w_out[l]
    wo = jnp.concatenate([wo[:FOURIER_WIDTH], wo[FOURIER_WIDTH:][perm]], axis=0).astype(BF16)
    w_hi = w_router[l].astype(BF16)
    w_lo = (w_router[l] - w_hi.astype(F32)).astype(BF16)
    wrh = jnp.pad(jnp.concatenate([w_hi, w_lo], axis=1), ((0, 0), (0, LANES - 2 * N_EXPERTS)))
    x1, h2, aff = _outproj(yf, ya, x, mod, norm_ffn_g[l].reshape(1, D_MODEL), wo, wrh)

    idx, gate = _route(aff)
    n = N_EXPERTS * CAPACITY
    y = _moe(idx, h2, experts)
    return _combine(idx, gate, y, x1, mod)
```

```python
import functools
import math

import numpy as np
import jax
import jax.numpy as jnp
from jax import lax
from jax.experimental import pallas as pl
from jax.experimental.pallas import tpu as pltpu

D_MODEL = 1024
SEQ = 2048
HEAD_DIM = 64
FOURIER_WIDTH = 512
ATTN_WIDTH = 512
N_GROUPS = 8
N_Q_HEADS = 8
Q_PER_KV = 4
N_KV_HEADS = 2
KV_WIDTH = 128
IN_PROJ_WIDTH = 1280
WINDOW = 128
BLOCK = 128
SPAN = BLOCK + 2 * WINDOW
N_BLOCKS = SEQ // BLOCK
N_BUCKETS = 32
MAX_DISTANCE = 128
N_EXPERTS = 16
CAPACITY = 2 * SEQ // N_EXPERTS
D_EXPERT = 1024
N_ADA = 6
EPS = 1e-6

LANES = 128
SUBLANES = 8
ROW_SLAB = D_MODEL // LANES
PACK_ROWS = ROW_SLAB // 2
VMEM_LIMIT = 56 * 1024 * 1024

F32 = jnp.float32
BF16 = jnp.bfloat16
NEG_INF = float("-inf")
LOG2E = math.log2(math.e)


def _params(sem, vmem=None):
    return pltpu.CompilerParams(dimension_semantics=sem, vmem_limit_bytes=vmem)


@functools.lru_cache(maxsize=None)
def _chan_dft():
    c = np.arange(HEAD_DIM, dtype=np.int64)
    ph = (c[:, None] * c[None, :]) % HEAD_DIM
    ang = 2.0 * np.pi * ph.astype(np.float64) / HEAD_DIM
    sc = 1.0 / math.sqrt(HEAD_DIM)
    eye = np.eye(N_GROUPS)
    cbd = np.kron(eye, np.cos(ang) * sc)
    sbd = np.kron(eye, np.sin(ang) * sc)
    return cbd.astype(np.float32), sbd.astype(np.float32)


@functools.lru_cache(maxsize=None)
def _bucket_table():
    rel = np.arange(SPAN)[None, :] - WINDOW - np.arange(BLOCK)[:, None]
    half = N_BUCKETS // 2
    max_exact = half // 2
    n = np.abs(rel)
    nf = np.maximum(n, 1).astype(np.float64)
    large = max_exact + (np.log(nf / max_exact) / math.log(MAX_DISTANCE / max_exact)
                         * (half - max_exact)).astype(np.int64)
    sq = np.maximum(n.astype(np.int64) ** 2 // (max_exact * max_exact), 1)
    large_int = max_exact + np.floor(np.log2(sq.astype(np.float64)) + 1e-9).astype(np.int64)
    assert np.array_equal(np.where(n >= max_exact, large, 0), np.where(n >= max_exact, large_int, 0))
    large = np.minimum(large, half - 1)
    bucket = np.where(rel > 0, half, 0) + np.where(n < max_exact, n, large)
    return bucket.astype(np.int32)


@functools.lru_cache(maxsize=None)
def _group_ones(width):
    return np.kron(np.eye(width // HEAD_DIM), np.ones((HEAD_DIM, HEAD_DIM))).astype(np.float32)


@functools.lru_cache(maxsize=None)
def _tri_incl():
    i = np.arange(LANES)
    return (i[:, None] <= i[None, :]).astype(np.float32)


def _ada_kernel(c_ref, w_ref, b_ref, o_ref):
    c = c_ref[...]
    ca = c * (1.0 / (1.0 + jnp.exp(-c)))
    o_ref[...] = jnp.dot(ca, w_ref[...], precision=lax.Precision.HIGHEST,
                         preferred_element_type=F32) + b_ref[...]


def _ada(c, w_ada, b_ada):
    B = c.shape[0]
    n = N_ADA * D_MODEL
    tn = D_MODEL
    return pl.pallas_call(
        _ada_kernel,
        grid=(n // tn,),
        in_specs=[pl.BlockSpec((B, D_MODEL), lambda j: (0, 0)),
                  pl.BlockSpec((D_MODEL, tn), lambda j: (0, j)),
                  pl.BlockSpec((1, tn), lambda j: (0, j))],
        out_specs=pl.BlockSpec((B, tn), lambda j: (0, j)),
        out_shape=jax.ShapeDtypeStruct((B, n), F32),
        compiler_params=_params(("arbitrary",)),
        name="ada",
    )(c, w_ada, b_ada.reshape(1, n))


def _fold_kernel(cbd_ref, sbd_ref, w_ref, o_ref):
    w = w_ref[...]
    o_ref[:, :FOURIER_WIDTH] = jnp.dot(cbd_ref[...], w, precision=lax.Precision.HIGHEST,
                                       preferred_element_type=F32).astype(BF16)
    o_ref[:, FOURIER_WIDTH:] = jnp.dot(sbd_ref[...], w, precision=lax.Precision.HIGHEST,
                                       preferred_element_type=F32).astype(BF16)


def _fold(w_fourier):
    wbd = (jnp.eye(N_GROUPS, dtype=F32)[:, None, :, None] * w_fourier[:, :, None, :]
           ).reshape(FOURIER_WIDTH, FOURIER_WIDTH)
    cbd, sbd = _chan_dft()
    return pl.pallas_call(
        _fold_kernel,
        out_shape=jax.ShapeDtypeStruct((FOURIER_WIDTH, 2 * FOURIER_WIDTH), BF16),
        name="fold",
    )(jnp.asarray(cbd), jnp.asarray(sbd), wbd)


def _bias_kernel(rb_ref, bucket_ref, o_ref):
    h = pl.program_id(0)
    bk = bucket_ref[...]
    acc = jnp.zeros((BLOCK, SPAN), F32)
    for b in range(N_BUCKETS):
        acc = jnp.where(bk == b, rb_ref[b, h], acc)
    j = lax.broadcasted_iota(jnp.int32, (BLOCK, SPAN), 1)
    q = lax.broadcasted_iota(jnp.int32, (BLOCK, SPAN), 0)
    band = jnp.abs(j - WINDOW - q) <= WINDOW
    base = jnp.where(band, acc * LOG2E, NEG_INF)
    o_ref[0] = jnp.where(j >= WINDOW, base, NEG_INF)
    o_ref[1] = base
    o_ref[2] = jnp.where(j < WINDOW + BLOCK, base, NEG_INF)


def _bias_table(rel_bias):
    return pl.pallas_call(
        _bias_kernel,
        grid=(N_Q_HEADS,),
        in_specs=[pl.BlockSpec(memory_space=pltpu.SMEM),
                  pl.BlockSpec((BLOCK, SPAN), lambda h: (0, 0))],
        out_specs=pl.BlockSpec((3, BLOCK, SPAN), lambda h: (0, h % Q_PER_KV, h // Q_PER_KV)),
        out_shape=jax.ShapeDtypeStruct((3, Q_PER_KV * BLOCK, N_KV_HEADS * SPAN), F32),
        compiler_params=_params(("arbitrary",)),
        name="bias",
    )(rel_bias, jnp.asarray(_bucket_table()))


IN_ROWS = 256


def _inproj_kernel(x_ref, mod_ref, g_ref, win_ref, pq_ref, bdq_ref, bdk_ref, gq_ref, gk_ref,
                   wg_ref, wu_ref, wd_ref,
                   a_ref, b_ref, q_ref, k_ref, v_ref, w_out):
    w_out[0] = wg_ref[...].astype(BF16)
    w_out[1] = wu_ref[...].astype(BF16)
    w_out[2] = wd_ref[...].astype(BF16)
    gain = g_ref[...] * (1.0 + mod_ref[1:2, :])
    shift = mod_ref[0:1, :]
    q0 = FOURIER_WIDTH
    k0 = q0 + ATTN_WIDTH
    v0 = k0 + KV_WIDTH
    for c in range(x_ref.shape[0] // IN_ROWS):
        rows = pl.ds(c * IN_ROWS, IN_ROWS)
        x = x_ref[rows, :]
        ms = jnp.mean(x * x, axis=-1, keepdims=True)
        h = x * lax.rsqrt(ms + EPS) * gain + shift
        proj = jnp.dot(h.astype(BF16), win_ref[...], preferred_element_type=F32)
        uf = proj[:, :FOURIER_WIDTH].astype(BF16)
        ab = jnp.dot(uf, pq_ref[...], preferred_element_type=F32)
        a_ref[rows, :] = ab[:, :FOURIER_WIDTH].astype(BF16)
        b_ref[rows, :] = ab[:, FOURIER_WIDTH:].astype(BF16)
        q = proj[:, q0:k0]
        ssq = jnp.dot((q * q).astype(BF16), bdq_ref[...], preferred_element_type=F32)
        q_ref[rows, :] = (q * lax.rsqrt(ssq * (1.0 / HEAD_DIM) + EPS) * gq_ref[...]).astype(BF16)
        k = proj[:, k0:v0]
        ssk = jnp.dot((k * k).astype(BF16), bdk_ref[...], preferred_element_type=F32)
        k_ref[rows, :] = (k * lax.rsqrt(ssk * (1.0 / HEAD_DIM) + EPS) * gk_ref[...]).astype(BF16)
        v_ref[rows, :] = proj[:, v0:].astype(BF16)


def _inproj(x, mod, g, win, pq, gq, gk, w_gate, w_up, w_down, tm=1024):
    B = x.shape[0]
    steps_per_batch = SEQ // tm
    w_rows = N_EXPERTS * D_MODEL
    w_blk = w_rows // (B * steps_per_batch)
    const = lambda shape: pl.BlockSpec(shape, lambda b, i: (0,) * len(shape))
    tok = lambda w: pl.BlockSpec((None, tm, w), lambda b, i: (b, i, 0))
    wsl = lambda c: pl.BlockSpec((w_blk, c), lambda b, i: (b * steps_per_batch + i, 0))
    sds = lambda w: jax.ShapeDtypeStruct((B, SEQ, w), BF16)
    step = lambda b, i: b * steps_per_batch + i
    a, b, q, k, v, experts = pl.pallas_call(
        _inproj_kernel,
        grid=(B, steps_per_batch),
        in_specs=[tok(D_MODEL),
                  pl.BlockSpec((None, N_ADA, D_MODEL), lambda b, i: (b, 0, 0)),
                  const((1, D_MODEL)),
                  const((D_MODEL, IN_PROJ_WIDTH)),
                  const((FOURIER_WIDTH, 2 * FOURIER_WIDTH)),
                  const((ATTN_WIDTH, ATTN_WIDTH)),
                  const((KV_WIDTH, KV_WIDTH)),
                  const((1, ATTN_WIDTH)),
                  const((1, KV_WIDTH)),
                  wsl(D_EXPERT), wsl(D_EXPERT), wsl(D_MODEL)],
        out_specs=[tok(FOURIER_WIDTH), tok(FOURIER_WIDTH), tok(ATTN_WIDTH), tok(KV_WIDTH),
                   tok(KV_WIDTH),
                   pl.BlockSpec((3, w_blk, D_MODEL), lambda b, i: (0, step(b, i), 0))],
        out_shape=[sds(FOURIER_WIDTH), sds(FOURIER_WIDTH), sds(ATTN_WIDTH), sds(KV_WIDTH),
                   sds(KV_WIDTH), jax.ShapeDtypeStruct((3, w_rows, D_MODEL), BF16)],
        compiler_params=_params(("parallel", "parallel"), VMEM_LIMIT),
        name="inproj",
    )(x, mod, g, win, pq, jnp.asarray(_group_ones(ATTN_WIDTH)).astype(BF16),
      jnp.asarray(_group_ones(KV_WIDTH)).astype(BF16), gq, gk,
      w_gate.reshape(w_rows, D_EXPERT), w_up.reshape(w_rows, D_EXPERT),
      w_down.reshape(N_EXPERTS * D_EXPERT, D_MODEL))
    return a, b, q, k, v, experts.reshape(3, N_EXPERTS, D_MODEL, D_MODEL)


HALF = SEQ // 2
REV = 128
REV_BLOCKS = HALF // REV
FOURIER_ROWS = 512


def _fourier_kernel(cm_ref, sm_ref, psh_ref, alt_ref, altrow_ref, a_ref, b_ref, bf_ref, o_ref):
    psh = psh_ref[...]

    def reversed_block(win_lo, k, src):
        if k == 0:
            return jnp.dot(psh[:, :REV], src(win_lo, REV), preferred_element_type=F32)
        return jnp.dot(psh, src(win_lo, 2 * REV), preferred_element_type=F32)

    def folded(ref, sign):
        blocks = []
        for k in range(REV_BLOCKS):
            lo = SEQ - REV * (k + 1)
            rev = reversed_block(lo, k, lambda s, n: ref[pl.ds(s, n), :])
            blocks.append((ref[pl.ds(k * REV, REV), :].astype(F32) + sign * rev).astype(BF16))
        return jnp.concatenate(blocks, axis=0)

    a_even = folded(a_ref, 1.0)
    b_odd = folded(b_ref, -1.0)
    a_mid = a_ref[pl.ds(HALF, 1), :].astype(F32)
    bias = bf_ref[...]
    z_blocks = []
    for i in range(HALF // FOURIER_ROWS):
        rows = pl.ds(i * FOURIER_ROWS, FOURIER_ROWS)
        yc = jnp.dot(cm_ref[rows, :], a_even, preferred_element_type=F32)
        yc = yc + alt_ref[rows, :] * a_mid + bias
        ys = jnp.dot(sm_ref[rows, :], b_odd, preferred_element_type=F32)
        o_ref[rows, :] = (yc - ys).astype(BF16)
        z_blocks.append((yc + ys).astype(BF16))
    z = jnp.concatenate(z_blocks, axis=0)
    for k in range(REV_BLOCKS):
        lo = HALF - REV * (k + 1)
        top = reversed_block(lo, k, lambda s, n: z[s:s + n])
        o_ref[pl.ds(HALF + k * REV, REV), :] = top.astype(BF16)
    y_mid = jnp.dot(altrow_ref[...], a_ref[...], preferred_element_type=F32)[0:1, :] + bias
    o_ref[pl.ds(HALF, 1), :] = y_mid.astype(BF16)


@functools.lru_cache(maxsize=None)
def _fold_tables():
    s = np.arange(HALF, dtype=np.int64)
    ph = (s[:, None] * s[None, :]) % SEQ
    ang = 2.0 * np.pi * ph.astype(np.float64) / SEQ
    sc = 1.0 / math.sqrt(SEQ)
    cm = (np.cos(ang) * sc).astype(np.float32)
    sm = (np.sin(ang) * sc).astype(np.float32)
    psh = np.zeros((REV, 2 * REV), np.float32)
    psh[np.arange(REV), REV - np.arange(REV)] = 1.0
    alt = (np.where(s % 2 == 0, 1.0, -1.0) * sc).astype(np.float32).reshape(HALF, 1)
    t = np.arange(SEQ)
    altrow = np.zeros((SUBLANES, SEQ), np.float32)
    altrow[0] = np.where(t % 2 == 0, 1.0, -1.0) * sc
    return cm, sm, psh, alt, altrow


def _fourier(a, b, bf):
    B = a.shape[0]
    cm, sm, psh, alt, altrow = _fold_tables()
    tok = pl.BlockSpec((None, SEQ, FOURIER_WIDTH), lambda i: (i, 0, 0))
    const = lambda shape: pl.BlockSpec(shape, lambda i: (0,) * len(shape))
    return pl.pallas_call(
        _fourier_kernel,
        grid=(B,),
        in_specs=[const((HALF, HALF)), const((HALF, HALF)), const((REV, 2 * REV)),
                  const((HALF, 1)), const((SUBLANES, SEQ)),
                  tok, tok, const((1, FOURIER_WIDTH))],
        out_specs=tok,
        out_shape=jax.ShapeDtypeStruct((B, SEQ, FOURIER_WIDTH), BF16),
        compiler_params=_params(("parallel",), VMEM_LIMIT),
        name="fourier",
    )(jnp.asarray(cm).astype(BF16), jnp.asarray(sm).astype(BF16), jnp.asarray(psh).astype(BF16),
      jnp.asarray(alt), jnp.asarray(altrow).astype(BF16), a, b, bf)


ATT_SUB = 16
ATT_ROWS = ATT_SUB * BLOCK
ATT_STEPS = N_BLOCKS // ATT_SUB


def _attn_kernel(sink_ref, q_ref, kl_ref, km_ref, kr_ref, vl_ref, vm_ref, vr_ref, bias_ref,
                 o_ref):
    i = pl.program_id(1)
    keys = jnp.concatenate([kl_ref[...], km_ref[...], kr_ref[...]], axis=0)
    vals = jnp.concatenate([vl_ref[...], vm_ref[...], vr_ref[...]], axis=0)
    first_k = lax.broadcasted_iota(jnp.int32, keys.shape, 1) < HEAD_DIM
    first_q = lax.broadcasted_iota(jnp.int32, (Q_PER_KV * BLOCK, LANES), 1) < HEAD_DIM
    row_head = lax.broadcasted_iota(jnp.int32, (Q_PER_KV * BLOCK, 1), 0) // BLOCK
    zero = jnp.zeros_like(keys)
    ones_lo = jnp.where(first_k, 1.0, 0.0).astype(BF16)
    ones_hi = jnp.where(first_k, 0.0, 1.0).astype(BF16)
    keys_kv = [jnp.where(first_k, keys, zero), jnp.where(first_k, zero, keys)]
    vals_kv = [jnp.concatenate([jnp.where(first_k, vals, zero), ones_lo], axis=1),
               jnp.concatenate([jnp.where(first_k, zero, vals), ones_hi], axis=1)]
    sinks = []
    for kv in range(N_KV_HEADS):
        sink = jnp.zeros((Q_PER_KV * BLOCK, 1), F32)
        for r in range(Q_PER_KV):
            sink = jnp.where(row_head == r, sink_ref[kv * Q_PER_KV + r] * LOG2E, sink)
        sinks.append(sink)
    for j in range(ATT_SUB):
        variant = 1
        if j == 0:
            variant = jnp.where(i == 0, 0, variant)
        if j == ATT_SUB - 1:
            variant = jnp.where(i == ATT_STEPS - 1, 2, variant)
        qrows = pl.ds(j * BLOCK, BLOCK)
        krows = slice(j * BLOCK, j * BLOCK + SPAN)
        qs = jnp.concatenate([q_ref[qrows, r * LANES:(r + 1) * LANES] for r in range(Q_PER_KV)],
                             axis=0)
        keys2 = jnp.concatenate([keys_kv[0][krows], keys_kv[1][krows]], axis=0)
        vals2 = jnp.concatenate([vals_kv[0][krows], vals_kv[1][krows]], axis=0)
        logits = lax.dot_general(qs, keys2, (((1,), (1,)), ((), ())),
                                 preferred_element_type=F32)
        logits = logits + bias_ref[variant]
        ms = [jnp.maximum(jnp.max(logits[:, kv * SPAN:(kv + 1) * SPAN], axis=-1, keepdims=True),
                          sinks[kv]) for kv in range(N_KV_HEADS)]
        p = jnp.exp2(jnp.concatenate([logits[:, kv * SPAN:(kv + 1) * SPAN] - ms[kv]
                                      for kv in range(N_KV_HEADS)], axis=1).astype(BF16))
        pv = jnp.dot(p, vals2, preferred_element_type=F32)
        denom = pv[:, LANES:] + jnp.exp2(jnp.where(first_q, sinks[0] - ms[0], sinks[1] - ms[1]))
        out = (pv[:, :LANES] / denom).astype(BF16)
        for r in range(Q_PER_KV):
            o_ref[qrows, r * LANES:(r + 1) * LANES] = out[r * BLOCK:(r + 1) * BLOCK]


def _attn(sink, q, k, v, bias):
    B = q.shape[0]
    edge = lambda f: pl.BlockSpec((None, BLOCK, KV_WIDTH), lambda b, i: (b, f(i), 0))
    left = lambda i: jnp.maximum(i * ATT_SUB - 1, 0)
    right = lambda i: jnp.minimum((i + 1) * ATT_SUB, N_BLOCKS - 1)
    mid = pl.BlockSpec((None, ATT_ROWS, KV_WIDTH), lambda b, i: (b, i, 0))
    qspec = pl.BlockSpec((None, ATT_ROWS, ATTN_WIDTH), lambda b, i: (b, i, 0))
    return pl.pallas_call(
        _attn_kernel,
        grid=(B, ATT_STEPS),
        in_specs=[pl.BlockSpec(memory_space=pltpu.SMEM), qspec,
                  edge(left), mid, edge(right),
                  edge(left), mid, edge(right),
                  pl.BlockSpec((3, Q_PER_KV * BLOCK, N_KV_HEADS * SPAN), lambda b, i: (0, 0, 0))],
        out_specs=qspec,
        out_shape=jax.ShapeDtypeStruct((B, SEQ, ATTN_WIDTH), BF16),
        compiler_params=_params(("parallel", "parallel"), VMEM_LIMIT),
        name="attn",
    )(sink, q, k, k, k, v, v, v, bias)


OUT_ROWS = 256


def _outproj_kernel(yf_ref, ya_ref, x_ref, mod_ref, g_ref, wo_ref, wrh_ref,
                    x1_ref, h2_ref, aff_ref):
    tm = x_ref.shape[0]
    gain = g_ref[...] * (1.0 + mod_ref[4:5, :])
    shift = mod_ref[3:4, :]
    gate1 = mod_ref[2:3, :]
    lane = lax.broadcasted_iota(jnp.int32, (OUT_ROWS, LANES), 1)
    chunks = range(tm // OUT_ROWS)
    mixed = [jnp.dot(jnp.concatenate([yf_ref[pl.ds(c * OUT_ROWS, OUT_ROWS), :],
                                      ya_ref[pl.ds(c * OUT_ROWS, OUT_ROWS), :]], axis=1),
                     wo_ref[...], preferred_element_type=F32) for c in chunks]
    for c in chunks:
        rows = pl.ds(c * OUT_ROWS, OUT_ROWS)
        x1 = x_ref[rows, :] + gate1 * mixed[c]
        x1_ref[rows, :] = x1
        ms = jnp.mean(x1 * x1, axis=-1, keepdims=True)
        h2 = x1 * lax.rsqrt(ms + EPS) * gain + shift
        hi = h2.astype(BF16)
        top = pltpu.bitcast(hi[:, :D_MODEL // 2].astype(F32), jnp.uint32)
        bot = pltpu.bitcast(hi[:, D_MODEL // 2:].astype(F32), jnp.uint32)
        words = top | (bot >> 16)
        for j in range(PACK_ROWS):
            h2_ref[pl.ds(c * OUT_ROWS * PACK_ROWS + j, OUT_ROWS, stride=PACK_ROWS), :] = (
                words[:, j * LANES:(j + 1) * LANES])
        part = jnp.dot(hi, wrh_ref[...], preferred_element_type=F32)
        logits = part + pltpu.roll(part, LANES - N_EXPERTS, axis=1)
        logits = jnp.where(lane < N_EXPERTS, logits, NEG_INF)
        m = jnp.max(logits, axis=-1, keepdims=True)
        e = jnp.exp(logits - m)
        aff_ref[rows, :] = e / jnp.sum(e, axis=-1, keepdims=True)


def _outproj(yf, ya, x, mod, g, wo, wrh, tm=1024):
    B = x.shape[0]
    const = lambda shape: pl.BlockSpec(shape, lambda b, i: (0,) * len(shape))
    tok = lambda w: pl.BlockSpec((None, tm, w), lambda b, i: (b, i, 0))
    return pl.pallas_call(
        _outproj_kernel,
        grid=(B, SEQ // tm),
        in_specs=[tok(FOURIER_WIDTH), tok(ATTN_WIDTH), tok(D_MODEL),
                  pl.BlockSpec((None, N_ADA, D_MODEL), lambda b, i: (b, 0, 0)),
                  const((1, D_MODEL)),
                  const((D_MODEL, D_MODEL)),
                  const((D_MODEL, LANES))],
        out_specs=[tok(D_MODEL),
                   pl.BlockSpec((None, tm * PACK_ROWS, LANES), lambda b, i: (b, i, 0)),
                   tok(LANES)],
        out_shape=[jax.ShapeDtypeStruct((B, SEQ, D_MODEL), F32),
                   jax.ShapeDtypeStruct((B, SEQ * PACK_ROWS, LANES), jnp.uint32),
                   jax.ShapeDtypeStruct((B, SEQ, LANES), F32)],
        compiler_params=_params(("parallel", "parallel"), VMEM_LIMIT),
        name="outproj",
    )(yf, ya, x, mod, g, wo, wrh)


ROUTE_BATCHES = 4
SEARCH_BITS = 3


def _route_kernel(aff_ref, tri_ref, idx_ref, gate_ref):
    for bb in range(ROUTE_BATCHES):
        _route_one(aff_ref[bb], tri_ref[...], idx_ref.at[bb], gate_ref.at[bb])


def _route_one(aff, tri, idx_ref, gate_ref):
    aff_t = jnp.transpose(aff)[:N_EXPERTS]
    bits = pltpu.bitcast(aff_t, jnp.int32)
    cap = float(CAPACITY)

    t = jnp.zeros((N_EXPERTS, 1), jnp.int32)
    for shift in range(30 - SEARCH_BITS, -1, -SEARCH_BITS):
        digit = jnp.zeros((N_EXPERTS, 1), jnp.int32)
        for k in range(1, 1 << SEARCH_BITS):
            cnt = jnp.sum(jnp.where(bits >= (t | (k << shift)), 1.0, 0.0), axis=1, keepdims=True)
            digit = digit + jnp.where(cnt >= cap, 1, 0)
        t = t | (digit << shift)
    gt = bits > t
    eq = bits == t
    need = cap - jnp.sum(jnp.where(gt, 1.0, 0.0), axis=1, keepdims=True)

    n_chunks = SEQ // LANES

    def prefix(flags_f32):
        outs = []
        carry = jnp.zeros((N_EXPERTS, 1), F32)
        for c in range(n_chunks):
            f = flags_f32[:, c * LANES:(c + 1) * LANES]
            incl = jnp.dot(f.astype(BF16), tri, preferred_element_type=F32)
            outs.append(incl - f + carry)
            carry = carry + jnp.sum(f, axis=1, keepdims=True)
        return jnp.concatenate(outs, axis=1)

    eq_f = jnp.where(eq, 1.0, 0.0)
    eq_rank = prefix(eq_f)
    sel_f = jnp.where(gt, 1.0, jnp.where(eq_rank < need, eq_f, 0.0))
    pos = prefix(sel_f)
    posm = jnp.where(sel_f > 0.0, pos, -1.0)

    hi = aff_t.astype(BF16).astype(F32)
    r1 = aff_t - hi
    mid = r1.astype(BF16).astype(F32)
    lo = r1 - mid
    tok = lax.broadcasted_iota(jnp.int32, (N_EXPERTS, SEQ), 1)
    row = lax.broadcasted_iota(jnp.int32, (N_EXPERTS, SEQ), 0)
    tok_rows = jnp.where(row == 0, (tok >> 6).astype(F32),
                         jnp.where(row == 1, (tok & 63).astype(F32), 0.0))
    vals_t = jnp.concatenate([hi, mid, lo, tok_rows], axis=0).astype(BF16)

    slot = lax.broadcasted_iota(jnp.int32, (CAPACITY, SEQ), 0).astype(F32).astype(BF16)
    posm_b = posm.astype(BF16)
    one_b = jnp.ones((CAPACITY, SEQ), BF16)
    zero_b = jnp.zeros((CAPACITY, SEQ), BF16)
    for e in range(N_EXPERTS):
        onehot = jnp.where(posm_b[e:e + 1, :] == slot, one_b, zero_b)
        res = lax.dot_general(vals_t, onehot, (((1,), (1,)), ((), ())),
                              preferred_element_type=F32)
        cols = pl.ds(e * CAPACITY, CAPACITY)
        tok_idx = res[3 * N_EXPERTS:3 * N_EXPERTS + 1] * 64.0 + res[3 * N_EXPERTS + 1:
                                                                    3 * N_EXPERTS + 2]
        idx_ref[:, cols] = tok_idx.astype(jnp.int32) * PACK_ROWS
        gate_ref[:, cols] = (res[e:e + 1] + res[N_EXPERTS + e:N_EXPERTS + e + 1]
                             + res[2 * N_EXPERTS + e:2 * N_EXPERTS + e + 1])


def _route(aff):
    B = aff.shape[0]
    n = N_EXPERTS * CAPACITY
    return pl.pallas_call(
        _route_kernel,
        grid=(B // ROUTE_BATCHES,),
        in_specs=[pl.BlockSpec((ROUTE_BATCHES, SEQ, LANES), lambda b: (b, 0, 0)),
                  pl.BlockSpec((LANES, LANES), lambda b: (0, 0))],
        out_specs=[pl.BlockSpec((ROUTE_BATCHES, 1, n), lambda b: (b, 0, 0)),
                   pl.BlockSpec((ROUTE_BATCHES, 1, n), lambda b: (b, 0, 0))],
        out_shape=[jax.ShapeDtypeStruct((B, 1, n), jnp.int32),
                   jax.ShapeDtypeStruct((B, 1, n), F32)],
        compiler_params=_params(("parallel",), VMEM_LIMIT),
        name="route",
    )(aff, jnp.asarray(_tri_incl()).astype(BF16))


PAIR = 4


def _moe_kernel(idx_ref, h2_ref, w_ref, y_ref, xin0_ref, xin1_ref):
    e = pl.program_id(1)
    last = N_EXPERTS - 1
    n = N_EXPERTS * CAPACITY
    rows = PAIR * CAPACITY

    def gather_rows(ex, dst_ref):
        for bb in range(PAIR):
            base = bb * n + ex * CAPACITY
            for p in range(CAPACITY):
                off = idx_ref[0, base + p]
                tile = h2_ref[bb, pl.ds(pl.multiple_of((off >> 3) << 3, SUBLANES), SUBLANES), :]
                tile = pltpu.roll(tile, off & PACK_ROWS, axis=0)
                dst_ref[pl.ds((bb * CAPACITY + p) * PACK_ROWS, PACK_ROWS), :] = tile[:PACK_ROWS]

    def expert(xin_ref):
        words = [xin_ref[pl.ds(j, rows, stride=PACK_ROWS), :] for j in range(PACK_ROWS)]
        xin = jnp.concatenate(
            [pltpu.bitcast(w & jnp.uint32(0xFFFF0000), F32).astype(BF16) for w in words]
            + [pltpu.bitcast(w << 16, F32).astype(BF16) for w in words], axis=1)
        for bb in range(PAIR):
            xb = xin[bb * CAPACITY:(bb + 1) * CAPACITY]
            a = jnp.dot(xb, w_ref[0], preferred_element_type=F32)
            u = jnp.dot(xb, w_ref[1], preferred_element_type=F32)
            hmid = (a * (1.0 / (1.0 + jnp.exp(-a))) * u).astype(BF16)
            y = jnp.dot(hmid, w_ref[2], preferred_element_type=F32).astype(BF16)
            top = pltpu.bitcast(y[:, :D_MODEL // 2].astype(F32), jnp.uint32)
            bot = pltpu.bitcast(y[:, D_MODEL // 2:].astype(F32), jnp.uint32)
            words = top | (bot >> 16)
            for j in range(PACK_ROWS):
                y_ref[bb, pl.ds(j, CAPACITY, stride=PACK_ROWS), :] = (
                    words[:, j * LANES:(j + 1) * LANES])

    @pl.when(e == 0)
    def _():
        gather_rows(0, xin0_ref)

    def step(xin_cur, xin_nxt):
        gather_rows(jnp.minimum(e + 1, last), xin_nxt)
        expert(xin_cur)

    @pl.when(e % 2 == 0)
    def _():
        step(xin0_ref, xin1_ref)

    @pl.when(e % 2 == 1)
    def _():
        step(xin1_ref, xin0_ref)


def _moe(idx, h2, experts):
    B = h2.shape[0]
    n = N_EXPERTS * CAPACITY
    rows = SEQ * PACK_ROWS
    pairs = B // PAIR
    stage = pltpu.VMEM((PAIR * CAPACITY * PACK_ROWS, LANES), jnp.uint32)
    out = pl.pallas_call(
        _moe_kernel,
        grid=(pairs, N_EXPERTS),
        in_specs=[pl.BlockSpec((None, 1, PAIR * n), lambda b, e: (b, 0, 0),
                               memory_space=pltpu.SMEM),
                  pl.BlockSpec((None, PAIR, rows, LANES), lambda b, e: (b, 0, 0, 0)),
                  pl.BlockSpec((3, None, D_MODEL, D_MODEL), lambda b, e: (0, e, 0, 0))],
        out_specs=pl.BlockSpec((None, PAIR, CAPACITY * PACK_ROWS, LANES),
                               lambda b, e: (b, 0, e, 0)),
        out_shape=jax.ShapeDtypeStruct((pairs, PAIR, n * PACK_ROWS, LANES), jnp.uint32),
        scratch_shapes=[stage, stage],
        compiler_params=_params(("parallel", "arbitrary"), VMEM_LIMIT),
        name="moe",
    )(idx.reshape(pairs, 1, PAIR * n), h2.reshape(pairs, PAIR, rows, LANES), experts)
    return out.reshape(B, n * PACK_ROWS, LANES)


COMBINE_EXPERTS = 8
SCATTER_UNROLL = 8
COMBINE_ROWS = 256


def _combine_kernel(idx_ref, gate_ref, y_ref, x1_ref, mod_ref, o_ref, acc_ref):
    j = pl.program_id(1)
    slots = COMBINE_EXPERTS * CAPACITY
    base = j * slots

    @pl.when(j == 0)
    def _():
        acc_ref[...] = jnp.zeros_like(acc_ref)

    upper = lax.broadcasted_iota(jnp.int32, (SUBLANES, LANES), 0) < PACK_ROWS
    for g in range(slots // SCATTER_UNROLL):
        new = []
        for u in range(0, SCATTER_UNROLL, 2):
            r = g * SCATTER_UNROLL + u
            words = y_ref[pl.ds(r * PACK_ROWS, SUBLANES), :]
            hi = pltpu.bitcast(words & jnp.uint32(0xFFFF0000), F32)
            lo = pltpu.bitcast(words << 16, F32)
            slabs = (jnp.where(upper, hi, pltpu.roll(lo, PACK_ROWS, axis=0)),
                     jnp.where(upper, pltpu.roll(hi, PACK_ROWS, axis=0), lo))
            for k in range(2):
                dst = pl.multiple_of(idx_ref[0, base + r + k] * (ROW_SLAB // PACK_ROWS), ROW_SLAB)
                new.append((dst, acc_ref[pl.ds(dst, ROW_SLAB), :]
                            + slabs[k] * gate_ref[0, base + r + k]))
        for dst, val in new:
            acc_ref[pl.ds(dst, ROW_SLAB), :] = val

    @pl.when(j == pl.num_programs(1) - 1)
    def _():
        for rb in range(SEQ // COMBINE_ROWS):
            rows = pl.ds(rb * COMBINE_ROWS, COMBINE_ROWS)
            for c in range(ROW_SLAB):
                cols = slice(c * LANES, (c + 1) * LANES)
                chunk = acc_ref[pl.ds(rb * COMBINE_ROWS * ROW_SLAB + c, COMBINE_ROWS,
                                      stride=ROW_SLAB), :]
                o_ref[rows, cols] = x1_ref[rows, cols] + mod_ref[5:6, cols] * chunk


def _combine(idx, gate, y, x1, mod):
    B = x1.shape[0]
    n = N_EXPERTS * CAPACITY
    tok = pl.BlockSpec((None, SEQ, D_MODEL), lambda b, j: (b, 0, 0))
    smem = pl.BlockSpec((None, 1, n), lambda b, j: (b, 0, 0), memory_space=pltpu.SMEM)
    return pl.pallas_call(
        _combine_kernel,
        grid=(B, N_EXPERTS // COMBINE_EXPERTS),
        in_specs=[smem, smem,
                  pl.BlockSpec((None, COMBINE_EXPERTS * CAPACITY * PACK_ROWS, LANES),
                               lambda b, j: (b, j, 0)),
                  tok,
                  pl.BlockSpec((None, N_ADA, D_MODEL), lambda b, j: (b, 0, 0))],
        out_specs=tok,
        out_shape=jax.ShapeDtypeStruct((B, SEQ, D_MODEL), F32),
        scratch_shapes=[pltpu.VMEM((SEQ * ROW_SLAB, LANES), F32)],
        compiler_params=_params(("parallel", "arbitrary"), VMEM_LIMIT),
        name="combine",
    )(idx, gate, y, x1, mod)


def _head_perm():
    perm = []
    for r in range(Q_PER_KV):
        for kv in range(N_KV_HEADS):
            h = kv * Q_PER_KV + r
            perm.extend(range(h * HEAD_DIM, (h + 1) * HEAD_DIM))
    return np.asarray(perm, dtype=np.int32)


def kernel(x, c, rel_bias, w_ada, b_ada, norm_mix_g, norm_ffn_g, w_in, w_fourier, b_fourier,
           q_norm_g, k_norm_g, sink, w_out, w_router, w_gate, w_up, w_down):
    B = x.shape[0]
    assert x.shape[1:] == (SEQ, D_MODEL) and B % PAIR == 0 and B % ROUTE_BATCHES == 0
    assert w_gate.shape[2:] == (D_MODEL, D_EXPERT) and D_EXPERT == D_MODEL
    perm = _head_perm()
    l = 0
    mod = _ada(c, w_ada[l], b_ada[l]).reshape(B, N_ADA, D_MODEL)
    pq = _fold(w_fourier[l])
    bias = _bias_table(rel_bias)

    wi = w_in[l]
    q_cols = wi[:, FOURIER_WIDTH:FOURIER_WIDTH + ATTN_WIDTH][:, perm]
    win = jnp.concatenate([wi[:, :FOURIER_WIDTH], q_cols, wi[:, FOURIER_WIDTH + ATTN_WIDTH:]],
                          axis=1).astype(BF16)
    gq = (jnp.tile(q_norm_g[l], N_Q_HEADS) * (HEAD_DIM ** -0.5 * LOG2E)).reshape(1, ATTN_WIDTH)
    gk = jnp.tile(k_norm_g[l], N_KV_HEADS).reshape(1, KV_WIDTH)
    a, b, q, k, v, experts = _inproj(x, mod, norm_mix_g[l].reshape(1, D_MODEL), win, pq, gq, gk,
                                     w_gate[l], w_up[l], w_down[l])

    yf = _fourier(a, b, b_fourier[l].reshape(1, FOURIER_WIDTH))
    ya = _attn(sink[l], q, k, v, bias)

    wo = w_out[l]
    wo = jnp.concatenate([wo[:FOURIER_WIDTH], wo[FOURIER_WIDTH:][perm]], axis=0).astype(BF16)
    w_hi = w_router[l].astype(BF16)
    w_lo = (w_router[l] - w_hi.astype(F32)).astype(BF16)
    wrh = jnp.pad(jnp.concatenate([w_hi, w_lo], axis=1), ((0, 0), (0, LANES - 2 * N_EXPERTS)))
    x1, h2, aff = _outproj(yf, ya, x, mod, norm_ffn_g[l].reshape(1, D_MODEL), wo, wrh)

    idx, gate = _route(aff)
    n = N_EXPERTS * CAPACITY
    y = _moe(idx, h2, experts)
    return _combine(idx, gate, y, x1, mod)
```

```python
import functools
import math

import numpy as np
import jax
import jax.numpy as jnp
from jax import lax
from jax.experimental import pallas as pl
from jax.experimental.pallas import tpu as pltpu

D_MODEL = 1024
SEQ = 2048
HEAD_DIM = 64
FOURIER_WIDTH = 512
ATTN_WIDTH = 512
N_GROUPS = 8
N_Q_HEADS = 8
Q_PER_KV = 4
N_KV_HEADS = 2
KV_WIDTH = 128
IN_PROJ_WIDTH = 1280
WINDOW = 128
BLOCK = 128
SPAN = BLOCK + 2 * WINDOW
N_BLOCKS = SEQ // BLOCK
N_BUCKETS = 32
MAX_DISTANCE = 128
N_EXPERTS = 16
CAPACITY = 2 * SEQ // N_EXPERTS
D_EXPERT = 1024
N_ADA = 6
EPS = 1e-6

LANES = 128
SUBLANES = 8
ROW_SLAB = D_MODEL // LANES
PACK_ROWS = ROW_SLAB // 2
VMEM_LIMIT = 56 * 1024 * 1024

F32 = jnp.float32
BF16 = jnp.bfloat16
NEG_INF = float("-inf")
LOG2E = math.log2(math.e)


def _params(sem, vmem=None):
    return pltpu.CompilerParams(dimension_semantics=sem, vmem_limit_bytes=vmem)


@functools.lru_cache(maxsize=None)
def _chan_dft():
    c = np.arange(HEAD_DIM, dtype=np.int64)
    ph = (c[:, None] * c[None, :]) % HEAD_DIM
    ang = 2.0 * np.pi * ph.astype(np.float64) / HEAD_DIM
    sc = 1.0 / math.sqrt(HEAD_DIM)
    eye = np.eye(N_GROUPS)
    cbd = np.kron(eye, np.cos(ang) * sc)
    sbd = np.kron(eye, np.sin(ang) * sc)
    return cbd.astype(np.float32), sbd.astype(np.float32)


@functools.lru_cache(maxsize=None)
def _bucket_table():
    rel = np.arange(SPAN)[None, :] - WINDOW - np.arange(BLOCK)[:, None]
    half = N_BUCKETS // 2
    max_exact = half // 2
    n = np.abs(rel)
    nf = np.maximum(n, 1).astype(np.float64)
    large = max_exact + (np.log(nf / max_exact) / math.log(MAX_DISTANCE / max_exact)
                         * (half - max_exact)).astype(np.int64)
    sq = np.maximum(n.astype(np.int64) ** 2 // (max_exact * max_exact), 1)
    large_int = max_exact + np.floor(np.log2(sq.astype(np.float64)) + 1e-9).astype(np.int64)
    assert np.array_equal(np.where(n >= max_exact, large, 0), np.where(n >= max_exact, large_int, 0))
    large = np.minimum(large, half - 1)
    bucket = np.where(rel > 0, half, 0) + np.where(n < max_exact, n, large)
    return bucket.astype(np.int32)


@functools.lru_cache(maxsize=None)
def _group_ones(width):
    return np.kron(np.eye(width // HEAD_DIM), np.ones((HEAD_DIM, HEAD_DIM))).astype(np.float32)


@functools.lru_cache(maxsize=None)
def _tri_incl():
    i = np.arange(LANES)
    return (i[:, None] <= i[None, :]).astype(np.float32)


def _ada_kernel(c_ref, w_ref, b_ref, o_ref):
    c = c_ref[...]
    ca = c * (1.0 / (1.0 + jnp.exp(-c)))
    o_ref[...] = jnp.dot(ca, w_ref[...], precision=lax.Precision.HIGHEST,
                         preferred_element_type=F32) + b_ref[...]


def _ada(c, w_ada, b_ada):
    B = c.shape[0]
    n = N_ADA * D_MODEL
    tn = D_MODEL
    return pl.pallas_call(
        _ada_kernel,
        grid=(n // tn,),
        in_specs=[pl.BlockSpec((B, D_MODEL), lambda j: (0, 0)),
                  pl.BlockSpec((D_MODEL, tn), lambda j: (0, j)),
                  pl.BlockSpec((1, tn), lambda j: (0, j))],
        out_specs=pl.BlockSpec((B, tn), lambda j: (0, j)),
        out_shape=jax.ShapeDtypeStruct((B, n), F32),
        compiler_params=_params(("arbitrary",)),
        name="ada",
    )(c, w_ada, b_ada.reshape(1, n))


def _fold_kernel(cbd_ref, sbd_ref, w_ref, o_ref):
    w = w_ref[...]
    o_ref[:, :FOURIER_WIDTH] = jnp.dot(cbd_ref[...], w, precision=lax.Precision.HIGHEST,
                                       preferred_element_type=F32).astype(BF16)
    o_ref[:, FOURIER_WIDTH:] = jnp.dot(sbd_ref[...], w, precision=lax.Precision.HIGHEST,
                                       preferred_element_type=F32).astype(BF16)


def _fold(w_fourier):
    wbd = (jnp.eye(N_GROUPS, dtype=F32)[:, None, :, None] * w_fourier[:, :, None, :]
           ).reshape(FOURIER_WIDTH, FOURIER_WIDTH)
    cbd, sbd = _chan_dft()
    return pl.pallas_call(
        _fold_kernel,
        out_shape=jax.ShapeDtypeStruct((FOURIER_WIDTH, 2 * FOURIER_WIDTH), BF16),
        name="fold",
    )(jnp.asarray(cbd), jnp.asarray(sbd), wbd)


def _bias_kernel(rb_ref, bucket_ref, o_ref):
    h = pl.program_id(0)
    bk = bucket_ref[...]
    acc = jnp.zeros((BLOCK, SPAN), F32)
    for b in range(N_BUCKETS):
        acc = jnp.where(bk == b, rb_ref[b, h], acc)
    j = lax.broadcasted_iota(jnp.int32, (BLOCK, SPAN), 1)
    q = lax.broadcasted_iota(jnp.int32, (BLOCK, SPAN), 0)
    band = jnp.abs(j - WINDOW - q) <= WINDOW
    base = jnp.where(band, acc * LOG2E, NEG_INF)
    o_ref[0] = jnp.where(j >= WINDOW, base, NEG_INF)
    o_ref[1] = base
    o_ref[2] = jnp.where(j < WINDOW + BLOCK, base, NEG_INF)


def _bias_table(rel_bias):
    return pl.pallas_call(
        _bias_kernel,
        grid=(N_Q_HEADS,),
        in_specs=[pl.BlockSpec(memory_space=pltpu.SMEM),
                  pl.BlockSpec((BLOCK, SPAN), lambda h: (0, 0))],
        out_specs=pl.BlockSpec((3, BLOCK, SPAN), lambda h: (0, h % Q_PER_KV, h // Q_PER_KV)),
        out_shape=jax.ShapeDtypeStruct((3, Q_PER_KV * BLOCK, N_KV_HEADS * SPAN), F32),
        compiler_params=_params(("arbitrary",)),
        name="bias",
    )(rel_bias, jnp.asarray(_bucket_table()))


IN_ROWS = 256


def _inproj_kernel(x_ref, mod_ref, g_ref, win_ref, pq_ref, bdq_ref, bdk_ref, gq_ref, gk_ref,
                   wg_ref, wu_ref, wd_ref,
                   a_ref, b_ref, q_ref, k_ref, v_ref, w_out):
    w_out[0] = wg_ref[...].astype(BF16)
    w_out[1] = wu_ref[...].astype(BF16)
    w_out[2] = wd_ref[...].astype(BF16)
    gain = g_ref[...] * (1.0 + mod_ref[1:2, :])
    shift = mod_ref[0:1, :]
    q0 = FOURIER_WIDTH
    k0 = q0 + ATTN_WIDTH
    v0 = k0 + KV_WIDTH
    for c in range(x_ref.shape[0] // IN_ROWS):
        rows = pl.ds(c * IN_ROWS, IN_ROWS)
        x = x_ref[rows, :]
        ms = jnp.mean(x * x, axis=-1, keepdims=True)
        h = x * lax.rsqrt(ms + EPS) * gain + shift
        proj = jnp.dot(h.astype(BF16), win_ref[...], preferred_element_type=F32)
        uf = proj[:, :FOURIER_WIDTH].astype(BF16)
        ab = jnp.dot(uf, pq_ref[...], preferred_element_type=F32)
        a_ref[rows, :] = ab[:, :FOURIER_WIDTH].astype(BF16)
        b_ref[rows, :] = ab[:, FOURIER_WIDTH:].astype(BF16)
        q = proj[:, q0:k0]
        ssq = jnp.dot((q * q).astype(BF16), bdq_ref[...], preferred_element_type=F32)
        q_ref[rows, :] = (q * lax.rsqrt(ssq * (1.0 / HEAD_DIM) + EPS) * gq_ref[...]).astype(BF16)
        k = proj[:, k0:v0]
        ssk = jnp.dot((k * k).astype(BF16), bdk_ref[...], preferred_element_type=F32)
        k_ref[rows, :] = (k * lax.rsqrt(ssk * (1.0 / HEAD_DIM) + EPS) * gk_ref[...]).astype(BF16)
        v_ref[rows, :] = proj[:, v0:].astype(BF16)


def _inproj(x, mod, g, win, pq, gq, gk, w_gate, w_up, w_down, tm=1024):
    B = x.shape[0]
    steps_per_batch = SEQ // tm
    w_rows = N_EXPERTS * D_MODEL
    w_blk = w_rows // (B * steps_per_batch)
    const = lambda shape: pl.BlockSpec(shape, lambda b, i: (0,) * len(shape))
    tok = lambda w: pl.BlockSpec((None, tm, w), lambda b, i: (b, i, 0))
    wsl = lambda c: pl.BlockSpec((w_blk, c), lambda b, i: (b * steps_per_batch + i, 0))
    sds = lambda w: jax.ShapeDtypeStruct((B, SEQ, w), BF16)
    step = lambda b, i: b * steps_per_batch + i
    a, b, q, k, v, experts = pl.pallas_call(
        _inproj_kernel,
        grid=(B, steps_per_batch),
        in_specs=[tok(D_MODEL),
                  pl.BlockSpec((None, N_ADA, D_MODEL), lambda b, i: (b, 0, 0)),
                  const((1, D_MODEL)),
                  const((D_MODEL, IN_PROJ_WIDTH)),
                  const((FOURIER_WIDTH, 2 * FOURIER_WIDTH)),
                  const((ATTN_WIDTH, ATTN_WIDTH)),
                  const((KV_WIDTH, KV_WIDTH)),
                  const((1, ATTN_WIDTH)),
                  const((1, KV_WIDTH)),
                  wsl(D_EXPERT), wsl(D_EXPERT), wsl(D_MODEL)],
        out_specs=[tok(FOURIER_WIDTH), tok(FOURIER_WIDTH), tok(ATTN_WIDTH), tok(KV_WIDTH),
                   tok(KV_WIDTH),
                   pl.BlockSpec((3, w_blk, D_MODEL), lambda b, i: (0, step(b, i), 0))],
        out_shape=[sds(FOURIER_WIDTH), sds(FOURIER_WIDTH), sds(ATTN_WIDTH), sds(KV_WIDTH),
                   sds(KV_WIDTH), jax.ShapeDtypeStruct((3, w_rows, D_MODEL), BF16)],
        compiler_params=_params(("parallel", "parallel"), VMEM_LIMIT),
        name="inproj",
    )(x, mod, g, win, pq, jnp.asarray(_group_ones(ATTN_WIDTH)).astype(BF16),
      jnp.asarray(_group_ones(KV_WIDTH)).astype(BF16), gq, gk,
      w_gate.reshape(w_rows, D_EXPERT), w_up.reshape(w_rows, D_EXPERT),
      w_down.reshape(N_EXPERTS * D_EXPERT, D_MODEL))
    return a, b, q, k, v, experts.reshape(3, N_EXPERTS, D_MODEL, D_MODEL)


HALF = SEQ // 2
REV = 128
REV_BLOCKS = HALF // REV
FOURIER_ROWS = 512


def _fourier_kernel(cm_ref, sm_ref, psh_ref, alt_ref, altrow_ref, a_ref, b_ref, bf_ref, o_ref):
    psh = psh_ref[...]

    def reversed_block(win_lo, k, src):
        if k == 0:
            return jnp.dot(psh[:, :REV], src(win_lo, REV), preferred_element_type=F32)
        return jnp.dot(psh, src(win_lo, 2 * REV), preferred_element_type=F32)

    def folded(ref, sign):
        blocks = []
        for k in range(REV_BLOCKS):
            lo = SEQ - REV * (k + 1)
            rev = reversed_block(lo, k, lambda s, n: ref[pl.ds(s, n), :])
            blocks.append((ref[pl.ds(k * REV, REV), :].astype(F32) + sign * rev).astype(BF16))
        return jnp.concatenate(blocks, axis=0)

    a_even = folded(a_ref, 1.0)
    b_odd = folded(b_ref, -1.0)
    a_mid = a_ref[pl.ds(HALF, 1), :].astype(F32)
    bias = bf_ref[...]
    z_blocks = []
    for i in range(HALF // FOURIER_ROWS):
        rows = pl.ds(i * FOURIER_ROWS, FOURIER_ROWS)
        yc = jnp.dot(cm_ref[rows, :], a_even, preferred_element_type=F32)
        yc = yc + alt_ref[rows, :] * a_mid + bias
        ys = jnp.dot(sm_ref[rows, :], b_odd, preferred_element_type=F32)
        o_ref[rows, :] = (yc - ys).astype(BF16)
        z_blocks.append((yc + ys).astype(BF16))
    z = jnp.concatenate(z_blocks, axis=0)
    for k in range(REV_BLOCKS):
        lo = HALF - REV * (k + 1)
        top = reversed_block(lo, k, lambda s, n: z[s:s + n])
        o_ref[pl.ds(HALF + k * REV, REV), :] = top.astype(BF16)
    y_mid = jnp.dot(altrow_ref[...], a_ref[...], preferred_element_type=F32)[0:1, :] + bias
    o_ref[pl.ds(HALF, 1), :] = y_mid.astype(BF16)


@functools.lru_cache(maxsize=None)
def _fold_tables():
    s = np.arange(HALF, dtype=np.int64)
    ph = (s[:, None] * s[None, :]) % SEQ
    ang = 2.0 * np.pi * ph.astype(np.float64) / SEQ
    sc = 1.0 / math.sqrt(SEQ)
    cm = (np.cos(ang) * sc).astype(np.float32)
    sm = (np.sin(ang) * sc).astype(np.float32)
    psh = np.zeros((REV, 2 * REV), np.float32)
    psh[np.arange(REV), REV - np.arange(REV)] = 1.0
    alt = (np.where(s % 2 == 0, 1.0, -1.0) * sc).astype(np.float32).reshape(HALF, 1)
    t = np.arange(SEQ)
    altrow = np.zeros((SUBLANES, SEQ), np.float32)
    altrow[0] = np.where(t % 2 == 0, 1.0, -1.0) * sc
    return cm, sm, psh, alt, altrow


def _fourier(a, b, bf):
    B = a.shape[0]
    cm, sm, psh, alt, altrow = _fold_tables()
    tok = pl.BlockSpec((None, SEQ, FOURIER_WIDTH), lambda i: (i, 0, 0))
    const = lambda shape: pl.BlockSpec(shape, lambda i: (0,) * len(shape))
    return pl.pallas_call(
        _fourier_kernel,
        grid=(B,),
        in_specs=[const((HALF, HALF)), const((HALF, HALF)), const((REV, 2 * REV)),
                  const((HALF, 1)), const((SUBLANES, SEQ)),
                  tok, tok, const((1, FOURIER_WIDTH))],
        out_specs=tok,
        out_shape=jax.ShapeDtypeStruct((B, SEQ, FOURIER_WIDTH), BF16),
        compiler_params=_params(("parallel",), VMEM_LIMIT),
        name="fourier",
    )(jnp.asarray(cm).astype(BF16), jnp.asarray(sm).astype(BF16), jnp.asarray(psh).astype(BF16),
      jnp.asarray(alt), jnp.asarray(altrow).astype(BF16), a, b, bf)


ATT_SUB = 16
ATT_ROWS = ATT_SUB * BLOCK
ATT_STEPS = N_BLOCKS // ATT_SUB


def _attn_kernel(sink_ref, q_ref, kl_ref, km_ref, kr_ref, vl_ref, vm_ref, vr_ref, bias_ref,
                 o_ref):
    i = pl.program_id(1)
    keys = jnp.concatenate([kl_ref[...], km_ref[...], kr_ref[...]], axis=0)
    vals = jnp.concatenate([vl_ref[...], vm_ref[...], vr_ref[...]], axis=0)
    first_k = lax.broadcasted_iota(jnp.int32, keys.shape, 1) < HEAD_DIM
    first_q = lax.broadcasted_iota(jnp.int32, (Q_PER_KV * BLOCK, LANES), 1) < HEAD_DIM
    row_head = lax.broadcasted_iota(jnp.int32, (Q_PER_KV * BLOCK, 1), 0) // BLOCK
    zero = jnp.zeros_like(keys)
    ones_lo = jnp.where(first_k, 1.0, 0.0).astype(BF16)
    ones_hi = jnp.where(first_k, 0.0, 1.0).astype(BF16)
    keys_kv = [jnp.where(first_k, keys, zero), jnp.where(first_k, zero, keys)]
    vals_kv = [jnp.concatenate([jnp.where(first_k, vals, zero), ones_lo], axis=1),
               jnp.concatenate([jnp.where(first_k, zero, vals), ones_hi], axis=1)]
    sinks = []
    for kv in range(N_KV_HEADS):
        sink = jnp.zeros((Q_PER_KV * BLOCK, 1), F32)
        for r in range(Q_PER_KV):
            sink = jnp.where(row_head == r, sink_ref[kv * Q_PER_KV + r] * LOG2E, sink)
        sinks.append(sink)
    for j in range(ATT_SUB):
        variant = 1
        if j == 0:
            variant = jnp.where(i == 0, 0, variant)
        if j == ATT_SUB - 1:
            variant = jnp.where(i == ATT_STEPS - 1, 2, variant)
        qrows = pl.ds(j * BLOCK, BLOCK)
        krows = slice(j * BLOCK, j * BLOCK + SPAN)
        qs = jnp.concatenate([q_ref[qrows, r * LANES:(r + 1) * LANES] for r in range(Q_PER_KV)],
                             axis=0)
        keys2 = jnp.concatenate([keys_kv[0][krows], keys_kv[1][krows]], axis=0)
        vals2 = jnp.concatenate([vals_kv[0][krows], vals_kv[1][krows]], axis=0)
        logits = lax.dot_general(qs, keys2, (((1,), (1,)), ((), ())),
                                 preferred_element_type=F32)
        logits = logits + bias_ref[variant]
        ms = [jnp.maximum(jnp.max(logits[:, kv * SPAN:(kv + 1) * SPAN], axis=-1, keepdims=True),
                          sinks[kv]) for kv in range(N_KV_HEADS)]
        p = jnp.exp2(jnp.concatenate([logits[:, kv * SPAN:(kv + 1) * SPAN] - ms[kv]
                                      for kv in range(N_KV_HEADS)], axis=1).astype(BF16))
        pv = jnp.dot(p, vals2, preferred_element_type=F32)
        denom = pv[:, LANES:] + jnp.exp2(jnp.where(first_q, sinks[0] - ms[0], sinks[1] - ms[1]))
        out = (pv[:, :LANES] / denom).astype(BF16)
        for r in range(Q_PER_KV):
            o_ref[qrows, r * LANES:(r + 1) * LANES] = out[r * BLOCK:(r + 1) * BLOCK]


def _attn(sink, q, k, v, bias):
    B = q.shape[0]
    edge = lambda f: pl.BlockSpec((None, BLOCK, KV_WIDTH), lambda b, i: (b, f(i), 0))
    left = lambda i: jnp.maximum(i * ATT_SUB - 1, 0)
    right = lambda i: jnp.minimum((i + 1) * ATT_SUB, N_BLOCKS - 1)
    mid = pl.BlockSpec((None, ATT_ROWS, KV_WIDTH), lambda b, i: (b, i, 0))
    qspec = pl.BlockSpec((None, ATT_ROWS, ATTN_WIDTH), lambda b, i: (b, i, 0))
    return pl.pallas_call(
        _attn_kernel,
        grid=(B, ATT_STEPS),
        in_specs=[pl.BlockSpec(memory_space=pltpu.SMEM), qspec,
                  edge(left), mid, edge(right),
                  edge(left), mid, edge(right),
                  pl.BlockSpec((3, Q_PER_KV * BLOCK, N_KV_HEADS * SPAN), lambda b, i: (0, 0, 0))],
        out_specs=qspec,
        out_shape=jax.ShapeDtypeStruct((B, SEQ, ATTN_WIDTH), BF16),
        compiler_params=_params(("parallel", "parallel"), VMEM_LIMIT),
        name="attn",
    )(sink, q, k, k, k, v, v, v, bias)


OUT_ROWS = 256


def _outproj_kernel(yf_ref, ya_ref, x_ref, mod_ref, g_ref, wo_ref, wrh_ref,
                    x1_ref, h2_ref, aff_ref):
    tm = x_ref.shape[0]
    gain = g_ref[...] * (1.0 + mod_ref[4:5, :])
    shift = mod_ref[3:4, :]
    gate1 = mod_ref[2:3, :]
    lane = lax.broadcasted_iota(jnp.int32, (OUT_ROWS, LANES), 1)
    chunks = range(tm // OUT_ROWS)
    mixed = [jnp.dot(jnp.concatenate([yf_ref[pl.ds(c * OUT_ROWS, OUT_ROWS), :],
                                      ya_ref[pl.ds(c * OUT_ROWS, OUT_ROWS), :]], axis=1),
                     wo_ref[...], preferred_element_type=F32) for c in chunks]
    for c in chunks:
        rows = pl.ds(c * OUT_ROWS, OUT_ROWS)
        x1 = x_ref[rows, :] + gate1 * mixed[c]
        x1_ref[rows, :] = x1
        ms = jnp.mean(x1 * x1, axis=-1, keepdims=True)
        h2 = x1 * lax.rsqrt(ms + EPS) * gain + shift
        hi = h2.astype(BF16)
        top = pltpu.bitcast(hi[:, :D_MODEL // 2].astype(F32), jnp.uint32)
        bot = pltpu.bitcast(hi[:, D_MODEL // 2:].astype(F32), jnp.uint32)
        words = top | (bot >> 16)
        for j in range(PACK_ROWS):
            h2_ref[pl.ds(c * OUT_ROWS * PACK_ROWS + j, OUT_ROWS, stride=PACK_ROWS), :] = (
                words[:, j * LANES:(j + 1) * LANES])
        part = jnp.dot(hi, wrh_ref[...], preferred_element_type=F32)
        logits = part + pltpu.roll(part, LANES - N_EXPERTS, axis=1)
        logits = jnp.where(lane < N_EXPERTS, logits, NEG_INF)
        m = jnp.max(logits, axis=-1, keepdims=True)
        e = jnp.exp(logits - m)
        aff = e / jnp.sum(e, axis=-1, keepdims=True)
        aff_ref[:, rows] = jnp.transpose(aff)[:N_EXPERTS]


def _outproj(yf, ya, x, mod, g, wo, wrh, tm=1024):
    B = x.shape[0]
    const = lambda shape: pl.BlockSpec(shape, lambda b, i: (0,) * len(shape))
    tok = lambda w: pl.BlockSpec((None, tm, w), lambda b, i: (b, i, 0))
    return pl.pallas_call(
        _outproj_kernel,
        grid=(B, SEQ // tm),
        in_specs=[tok(FOURIER_WIDTH), tok(ATTN_WIDTH), tok(D_MODEL),
                  pl.BlockSpec((None, N_ADA, D_MODEL), lambda b, i: (b, 0, 0)),
                  const((1, D_MODEL)),
                  const((D_MODEL, D_MODEL)),
                  const((D_MODEL, LANES))],
        out_specs=[tok(D_MODEL),
                   pl.BlockSpec((None, tm * PACK_ROWS, LANES), lambda b, i: (b, i, 0)),
                   pl.BlockSpec((None, N_EXPERTS, tm), lambda b, i: (b, 0, i))],
        out_shape=[jax.ShapeDtypeStruct((B, SEQ, D_MODEL), F32),
                   jax.ShapeDtypeStruct((B, SEQ * PACK_ROWS, LANES), jnp.uint32),
                   jax.ShapeDtypeStruct((B, N_EXPERTS, SEQ), F32)],
        compiler_params=_params(("parallel", "parallel"), VMEM_LIMIT),
        name="outproj",
    )(yf, ya, x, mod, g, wo, wrh)


ROUTE_BATCHES = 4
SEARCH_BITS = 3


def _route_kernel(aff_ref, tri_ref, idx_ref, gate_ref):
    for bb in range(ROUTE_BATCHES):
        _route_one(aff_ref[bb], tri_ref[...], idx_ref.at[bb], gate_ref.at[bb])


def _route_one(aff_t, tri, idx_ref, gate_ref):
    bits = pltpu.bitcast(aff_t, jnp.int32)
    cap = float(CAPACITY)

    t = jnp.zeros((N_EXPERTS, 1), jnp.int32)
    for shift in range(30 - SEARCH_BITS, -1, -SEARCH_BITS):
        digit = jnp.zeros((N_EXPERTS, 1), jnp.int32)
        for k in range(1, 1 << SEARCH_BITS):
            cnt = jnp.sum(jnp.where(bits >= (t | (k << shift)), 1.0, 0.0), axis=1, keepdims=True)
            digit = digit + jnp.where(cnt >= cap, 1, 0)
        t = t | (digit << shift)
    gt = bits > t
    eq = bits == t
    need = cap - jnp.sum(jnp.where(gt, 1.0, 0.0), axis=1, keepdims=True)

    n_chunks = SEQ // LANES

    def prefix(flags_f32):
        outs = []
        carry = jnp.zeros((N_EXPERTS, 1), F32)
        for c in range(n_chunks):
            f = flags_f32[:, c * LANES:(c + 1) * LANES]
            incl = jnp.dot(f.astype(BF16), tri, preferred_element_type=F32)
            outs.append(incl - f + carry)
            carry = carry + jnp.sum(f, axis=1, keepdims=True)
        return jnp.concatenate(outs, axis=1)

    eq_f = jnp.where(eq, 1.0, 0.0)
    eq_rank = prefix(eq_f)
    sel_f = jnp.where(gt, 1.0, jnp.where(eq_rank < need, eq_f, 0.0))
    pos = prefix(sel_f)
    posm = jnp.where(sel_f > 0.0, pos, -1.0)

    hi = aff_t.astype(BF16).astype(F32)
    r1 = aff_t - hi
    mid = r1.astype(BF16).astype(F32)
    lo = r1 - mid
    tok = lax.broadcasted_iota(jnp.int32, (N_EXPERTS, SEQ), 1)
    row = lax.broadcasted_iota(jnp.int32, (N_EXPERTS, SEQ), 0)
    tok_rows = jnp.where(row == 0, (tok >> 6).astype(F32),
                         jnp.where(row == 1, (tok & 63).astype(F32), 0.0))
    vals_t = jnp.concatenate([hi, mid, lo, tok_rows], axis=0).astype(BF16)

    slot = lax.broadcasted_iota(jnp.int32, (CAPACITY, SEQ), 0).astype(F32).astype(BF16)
    posm_b = posm.astype(BF16)
    one_b = jnp.ones((CAPACITY, SEQ), BF16)
    zero_b = jnp.zeros((CAPACITY, SEQ), BF16)
    for e in range(N_EXPERTS):
        onehot = jnp.where(posm_b[e:e + 1, :] == slot, one_b, zero_b)
        res = lax.dot_general(vals_t, onehot, (((1,), (1,)), ((), ())),
                              preferred_element_type=F32)
        cols = pl.ds(e * CAPACITY, CAPACITY)
        tok_idx = res[3 * N_EXPERTS:3 * N_EXPERTS + 1] * 64.0 + res[3 * N_EXPERTS + 1:
                                                                    3 * N_EXPERTS + 2]
        idx_ref[:, cols] = tok_idx.astype(jnp.int32) * PACK_ROWS
        gate_ref[:, cols] = (res[e:e + 1] + res[N_EXPERTS + e:N_EXPERTS + e + 1]
                             + res[2 * N_EXPERTS + e:2 * N_EXPERTS + e + 1])


def _route(aff):
    B = aff.shape[0]
    n = N_EXPERTS * CAPACITY
    return pl.pallas_call(
        _route_kernel,
        grid=(B // ROUTE_BATCHES,),
        in_specs=[pl.BlockSpec((ROUTE_BATCHES, N_EXPERTS, SEQ), lambda b: (b, 0, 0)),
                  pl.BlockSpec((LANES, LANES), lambda b: (0, 0))],
        out_specs=[pl.BlockSpec((ROUTE_BATCHES, 1, n), lambda b: (b, 0, 0)),
                   pl.BlockSpec((ROUTE_BATCHES, 1, n), lambda b: (b, 0, 0))],
        out_shape=[jax.ShapeDtypeStruct((B, 1, n), jnp.int32),
                   jax.ShapeDtypeStruct((B, 1, n), F32)],
        compiler_params=_params(("parallel",), VMEM_LIMIT),
        name="route",
    )(aff, jnp.asarray(_tri_incl()).astype(BF16))


PAIR = 4


def _moe_kernel(idx_ref, h2_ref, w_ref, y_ref, xin0_ref, xin1_ref):
    e = pl.program_id(1)
    last = N_EXPERTS - 1
    n = N_EXPERTS * CAPACITY
    rows = PAIR * CAPACITY

    def gather_rows(ex, dst_ref):
        for bb in range(PAIR):
            base = bb * n + ex * CAPACITY
            for p in range(CAPACITY):
                off = idx_ref[0, base + p]
                tile = h2_ref[bb, pl.ds(pl.multiple_of((off >> 3) << 3, SUBLANES), SUBLANES), :]
                tile = pltpu.roll(tile, off & PACK_ROWS, axis=0)
                dst_ref[pl.ds((bb * CAPACITY + p) * PACK_ROWS, PACK_ROWS), :] = tile[:PACK_ROWS]

    def expert(xin_ref):
        words = [xin_ref[pl.ds(j, rows, stride=PACK_ROWS), :] for j in range(PACK_ROWS)]
        xin = jnp.concatenate(
            [pltpu.bitcast(w & jnp.uint32(0xFFFF0000), F32).astype(BF16) for w in words]
            + [pltpu.bitcast(w << 16, F32).astype(BF16) for w in words], axis=1)
        for bb in range(PAIR):
            xb = xin[bb * CAPACITY:(bb + 1) * CAPACITY]
            a = jnp.dot(xb, w_ref[0], preferred_element_type=F32)
            u = jnp.dot(xb, w_ref[1], preferred_element_type=F32)
            hmid = (a * (1.0 / (1.0 + jnp.exp(-a))) * u).astype(BF16)
            y = jnp.dot(hmid, w_ref[2], preferred_element_type=F32).astype(BF16)
            top = pltpu.bitcast(y[:, :D_MODEL // 2].astype(F32), jnp.uint32)
            bot = pltpu.bitcast(y[:, D_MODEL // 2:].astype(F32), jnp.uint32)
            words = top | (bot >> 16)
            for j in range(PACK_ROWS):
                y_ref[bb, pl.ds(j, CAPACITY, stride=PACK_ROWS), :] = (
                    words[:, j * LANES:(j + 1) * LANES])

    @pl.when(e == 0)
    def _():
        gather_rows(0, xin0_ref)

    def step(xin_cur, xin_nxt):
        gather_rows(jnp.minimum(e + 1, last), xin_nxt)
        expert(xin_cur)

    @pl.when(e % 2 == 0)
    def _():
        step(xin0_ref, xin1_ref)

    @pl.when(e % 2 == 1)
    def _():
        step(xin1_ref, xin0_ref)


def _moe(idx, h2, experts):
    B = h2.shape[0]
    n = N_EXPERTS * CAPACITY
    rows = SEQ * PACK_ROWS
    pairs = B // PAIR
    stage = pltpu.VMEM((PAIR * CAPACITY * PACK_ROWS, LANES), jnp.uint32)
    out = pl.pallas_call(
        _moe_kernel,
        grid=(pairs, N_EXPERTS),
        in_specs=[pl.BlockSpec((None, 1, PAIR * n), lambda b, e: (b, 0, 0),
                               memory_space=pltpu.SMEM),
                  pl.BlockSpec((None, PAIR, rows, LANES), lambda b, e: (b, 0, 0, 0)),
                  pl.BlockSpec((3, None, D_MODEL, D_MODEL), lambda b, e: (0, e, 0, 0))],
        out_specs=pl.BlockSpec((None, PAIR, CAPACITY * PACK_ROWS, LANES),
                               lambda b, e: (b, 0, e, 0)),
        out_shape=jax.ShapeDtypeStruct((pairs, PAIR, n * PACK_ROWS, LANES), jnp.uint32),
        scratch_shapes=[stage, stage],
        compiler_params=_params(("parallel", "arbitrary"), VMEM_LIMIT),
        name="moe",
    )(idx.reshape(pairs, 1, PAIR * n), h2.reshape(pairs, PAIR, rows, LANES), experts)
    return out.reshape(B, n * PACK_ROWS, LANES)


COMBINE_EXPERTS = 8
SCATTER_UNROLL = 8
COMBINE_ROWS = 256


def _combine_kernel(idx_ref, gate_ref, y_ref, x1_ref, mod_ref, o_ref, acc_ref):
    j = pl.program_id(1)
    slots = COMBINE_EXPERTS * CAPACITY
    base = j * slots

    @pl.when(j == 0)
    def _():
        acc_ref[...] = jnp.zeros_like(acc_ref)

    upper = lax.broadcasted_iota(jnp.int32, (SUBLANES, LANES), 0) < PACK_ROWS
    for g in range(slots // SCATTER_UNROLL):
        new = []
        for u in range(0, SCATTER_UNROLL, 2):
            r = g * SCATTER_UNROLL + u
            words = y_ref[pl.ds(r * PACK_ROWS, SUBLANES), :]
            hi = pltpu.bitcast(words & jnp.uint32(0xFFFF0000), F32)
            lo = pltpu.bitcast(words << 16, F32)
            slabs = (jnp.where(upper, hi, pltpu.roll(lo, PACK_ROWS, axis=0)),
                     jnp.where(upper, pltpu.roll(hi, PACK_ROWS, axis=0), lo))
            for k in range(2):
                dst = pl.multiple_of(idx_ref[0, base + r + k] * (ROW_SLAB // PACK_ROWS), ROW_SLAB)
                new.append((dst, acc_ref[pl.ds(dst, ROW_SLAB), :]
                            + slabs[k] * gate_ref[0, base + r + k]))
        for dst, val in new:
            acc_ref[pl.ds(dst, ROW_SLAB), :] = val

    @pl.when(j == pl.num_programs(1) - 1)
    def _():
        for rb in range(SEQ // COMBINE_ROWS):
            rows = pl.ds(rb * COMBINE_ROWS, COMBINE_ROWS)
            for c in range(ROW_SLAB):
                cols = slice(c * LANES, (c + 1) * LANES)
                chunk = acc_ref[pl.ds(rb * COMBINE_ROWS * ROW_SLAB + c, COMBINE_ROWS,
                                      stride=ROW_SLAB), :]
                o_ref[rows, cols] = x1_ref[rows, cols] + mod_ref[5:6, cols] * chunk


def _combine(idx, gate, y, x1, mod):
    B = x1.shape[0]
    n = N_EXPERTS * CAPACITY
    tok = pl.BlockSpec((None, SEQ, D_MODEL), lambda b, j: (b, 0, 0))
    smem = pl.BlockSpec((None, 1, n), lambda b, j: (b, 0, 0), memory_space=pltpu.SMEM)
    return pl.pallas_call(
        _combine_kernel,
        grid=(B, N_EXPERTS // COMBINE_EXPERTS),
        in_specs=[smem, smem,
                  pl.BlockSpec((None, COMBINE_EXPERTS * CAPACITY * PACK_ROWS, LANES),
                               lambda b, j: (b, j, 0)),
                  tok,
                  pl.BlockSpec((None, N_ADA, D_MODEL), lambda b, j: (b, 0, 0))],
        out_specs=tok,
        out_shape=jax.ShapeDtypeStruct((B, SEQ, D_MODEL), F32),
        scratch_shapes=[pltpu.VMEM((SEQ * ROW_SLAB, LANES), F32)],
        compiler_params=_params(("parallel", "arbitrary"), VMEM_LIMIT),
        name="combine",
    )(idx, gate, y, x1, mod)


def _head_perm():
    perm = []
    for r in range(Q_PER_KV):
        for kv in range(N_KV_HEADS):
            h = kv * Q_PER_KV + r
            perm.extend(range(h * HEAD_DIM, (h + 1) * HEAD_DIM))
    return np.asarray(perm, dtype=np.int32)


def kernel(x, c, rel_bias, w_ada, b_ada, norm_mix_g, norm_ffn_g, w_in, w_fourier, b_fourier,
           q_norm_g, k_norm_g, sink, w_out, w_router, w_gate, w_up, w_down):
    B = x.shape[0]
    assert x.shape[1:] == (SEQ, D_MODEL) and B % PAIR == 0 and B % ROUTE_BATCHES == 0
    assert w_gate.shape[2:] == (D_MODEL, D_EXPERT) and D_EXPERT == D_MODEL
    perm = _head_perm()
    l = 0
    mod = _ada(c, w_ada[l], b_ada[l]).reshape(B, N_ADA, D_MODEL)
    pq = _fold(w_fourier[l])
    bias = _bias_table(rel_bias)

    wi = w_in[l]
    q_cols = wi[:, FOURIER_WIDTH:FOURIER_WIDTH + ATTN_WIDTH][:, perm]
    win = jnp.concatenate([wi[:, :FOURIER_WIDTH], q_cols, wi[:, FOURIER_WIDTH + ATTN_WIDTH:]],
                          axis=1).astype(BF16)
    gq = (jnp.tile(q_norm_g[l], N_Q_HEADS) * (HEAD_DIM ** -0.5 * LOG2E)).reshape(1, ATTN_WIDTH)
    gk = jnp.tile(k_norm_g[l], N_KV_HEADS).reshape(1, KV_WIDTH)
    a, b, q, k, v, experts = _inproj(x, mod, norm_mix_g[l].reshape(1, D_MODEL), win, pq, gq, gk,
                                     w_gate[l], w_up[l], w_down[l])

    yf = _fourier(a, b, b_fourier[l].reshape(1, FOURIER_WIDTH))
    ya = _attn(sink[l], q, k, v, bias)

    wo = w_out[l]
    wo = jnp.concatenate([wo[:FOURIER_WIDTH], wo[FOURIER_WIDTH:][perm]], axis=0).astype(BF16)
    w_hi = w_router[l].astype(BF16)
    w_lo = (w_router[l] - w_hi.astype(F32)).astype(BF16)
    wrh = jnp.pad(jnp.concatenate([w_hi, w_lo], axis=1), ((0, 0), (0, LANES - 2 * N_EXPERTS)))
    x1, h2, aff = _outproj(yf, ya, x, mod, norm_ffn_g[l].reshape(1, D_MODEL), wo, wrh)

    idx, gate = _route(aff)
    n = N_EXPERTS * CAPACITY
    y = _moe(idx, h2, experts)
    return _combine(idx, gate, y, x1, mod)
```

```python
import functools
import math

import numpy as np
import jax
import jax.numpy as jnp
from jax import lax
from jax.experimental import pallas as pl
from jax.experimental.pallas import tpu as pltpu

D_MODEL = 1024
SEQ = 2048
HEAD_DIM = 64
FOURIER_WIDTH = 512
ATTN_WIDTH = 512
N_GROUPS = 8
N_Q_HEADS = 8
Q_PER_KV = 4
N_KV_HEADS = 2
KV_WIDTH = 128
IN_PROJ_WIDTH = 1280
WINDOW = 128
BLOCK = 128
SPAN = BLOCK + 2 * WINDOW
N_BLOCKS = SEQ // BLOCK
N_BUCKETS = 32
MAX_DISTANCE = 128
N_EXPERTS = 16
CAPACITY = 2 * SEQ // N_EXPERTS
D_EXPERT = 1024
N_ADA = 6
EPS = 1e-6

LANES = 128
SUBLANES = 8
ROW_SLAB = D_MODEL // LANES
PACK_ROWS = ROW_SLAB // 2
VMEM_LIMIT = 56 * 1024 * 1024

F32 = jnp.float32
BF16 = jnp.bfloat16
NEG_INF = float("-inf")
LOG2E = math.log2(math.e)


def _params(sem, vmem=None):
    return pltpu.CompilerParams(dimension_semantics=sem, vmem_limit_bytes=vmem)


@functools.lru_cache(maxsize=None)
def _chan_dft():
    c = np.arange(HEAD_DIM, dtype=np.int64)
    ph = (c[:, None] * c[None, :]) % HEAD_DIM
    ang = 2.0 * np.pi * ph.astype(np.float64) / HEAD_DIM
    sc = 1.0 / math.sqrt(HEAD_DIM)
    eye = np.eye(N_GROUPS)
    cbd = np.kron(eye, np.cos(ang) * sc)
    sbd = np.kron(eye, np.sin(ang) * sc)
    return cbd.astype(np.float32), sbd.astype(np.float32)


@functools.lru_cache(maxsize=None)
def _bucket_table():
    rel = np.arange(SPAN)[None, :] - WINDOW - np.arange(BLOCK)[:, None]
    half = N_BUCKETS // 2
    max_exact = half // 2
    n = np.abs(rel)
    nf = np.maximum(n, 1).astype(np.float64)
    large = max_exact + (np.log(nf / max_exact) / math.log(MAX_DISTANCE / max_exact)
                         * (half - max_exact)).astype(np.int64)
    sq = np.maximum(n.astype(np.int64) ** 2 // (max_exact * max_exact), 1)
    large_int = max_exact + np.floor(np.log2(sq.astype(np.float64)) + 1e-9).astype(np.int64)
    assert np.array_equal(np.where(n >= max_exact, large, 0), np.where(n >= max_exact, large_int, 0))
    large = np.minimum(large, half - 1)
    bucket = np.where(rel > 0, half, 0) + np.where(n < max_exact, n, large)
    return bucket.astype(np.int32)


@functools.lru_cache(maxsize=None)
def _group_ones(width):
    return np.kron(np.eye(width // HEAD_DIM), np.ones((HEAD_DIM, HEAD_DIM))).astype(np.float32)


@functools.lru_cache(maxsize=None)
def _tri_incl():
    i = np.arange(LANES)
    return (i[:, None] <= i[None, :]).astype(np.float32)


def _ada_kernel(c_ref, w_ref, b_ref, o_ref):
    c = c_ref[...]
    ca = c * (1.0 / (1.0 + jnp.exp(-c)))
    o_ref[...] = jnp.dot(ca, w_ref[...], precision=lax.Precision.HIGHEST,
                         preferred_element_type=F32) + b_ref[...]


def _ada(c, w_ada, b_ada):
    B = c.shape[0]
    n = N_ADA * D_MODEL
    tn = D_MODEL
    return pl.pallas_call(
        _ada_kernel,
        grid=(n // tn,),
        in_specs=[pl.BlockSpec((B, D_MODEL), lambda j: (0, 0)),
                  pl.BlockSpec((D_MODEL, tn), lambda j: (0, j)),
                  pl.BlockSpec((1, tn), lambda j: (0, j))],
        out_specs=pl.BlockSpec((B, tn), lambda j: (0, j)),
        out_shape=jax.ShapeDtypeStruct((B, n), F32),
        compiler_params=_params(("arbitrary",)),
        name="ada",
    )(c, w_ada, b_ada.reshape(1, n))


def _fold_kernel(cbd_ref, sbd_ref, w_ref, o_ref):
    w = w_ref[...]
    o_ref[:, :FOURIER_WIDTH] = jnp.dot(cbd_ref[...], w, precision=lax.Precision.HIGHEST,
                                       preferred_element_type=F32).astype(BF16)
    o_ref[:, FOURIER_WIDTH:] = jnp.dot(sbd_ref[...], w, precision=lax.Precision.HIGHEST,
                                       preferred_element_type=F32).astype(BF16)


def _fold(w_fourier):
    wbd = (jnp.eye(N_GROUPS, dtype=F32)[:, None, :, None] * w_fourier[:, :, None, :]
           ).reshape(FOURIER_WIDTH, FOURIER_WIDTH)
    cbd, sbd = _chan_dft()
    return pl.pallas_call(
        _fold_kernel,
        out_shape=jax.ShapeDtypeStruct((FOURIER_WIDTH, 2 * FOURIER_WIDTH), BF16),
        name="fold",
    )(jnp.asarray(cbd), jnp.asarray(sbd), wbd)


def _bias_kernel(rb_ref, bucket_ref, o_ref):
    h = pl.program_id(0)
    bk = bucket_ref[...]
    acc = jnp.zeros((BLOCK, SPAN), F32)
    for b in range(N_BUCKETS):
        acc = jnp.where(bk == b, rb_ref[b, h], acc)
    j = lax.broadcasted_iota(jnp.int32, (BLOCK, SPAN), 1)
    q = lax.broadcasted_iota(jnp.int32, (BLOCK, SPAN), 0)
    band = jnp.abs(j - WINDOW - q) <= WINDOW
    base = jnp.where(band, acc * LOG2E, NEG_INF)
    o_ref[0] = jnp.where(j >= WINDOW, base, NEG_INF)
    o_ref[1] = base
    o_ref[2] = jnp.where(j < WINDOW + BLOCK, base, NEG_INF)


def _bias_table(rel_bias):
    return pl.pallas_call(
        _bias_kernel,
        grid=(N_Q_HEADS,),
        in_specs=[pl.BlockSpec(memory_space=pltpu.SMEM),
                  pl.BlockSpec((BLOCK, SPAN), lambda h: (0, 0))],
        out_specs=pl.BlockSpec((3, BLOCK, SPAN), lambda h: (0, h % Q_PER_KV, h // Q_PER_KV)),
        out_shape=jax.ShapeDtypeStruct((3, Q_PER_KV * BLOCK, N_KV_HEADS * SPAN), F32),
        compiler_params=_params(("arbitrary",)),
        name="bias",
    )(rel_bias, jnp.asarray(_bucket_table()))


IN_ROWS = 256


def _inproj_kernel(x_ref, mod_ref, g_ref, win_ref, pq_ref, bdq_ref, bdk_ref, gq_ref, gk_ref,
                   wg_ref, wu_ref, wd_ref,
                   a_ref, b_ref, q_ref, k_ref, v_ref, w_out):
    w_out[0] = wg_ref[...].astype(BF16)
    w_out[1] = wu_ref[...].astype(BF16)
    w_out[2] = wd_ref[...].astype(BF16)
    gain = g_ref[...] * (1.0 + mod_ref[1:2, :])
    shift = mod_ref[0:1, :]
    q0 = FOURIER_WIDTH
    k0 = q0 + ATTN_WIDTH
    v0 = k0 + KV_WIDTH
    for c in range(x_ref.shape[0] // IN_ROWS):
        rows = pl.ds(c * IN_ROWS, IN_ROWS)
        x = x_ref[rows, :]
        ms = jnp.mean(x * x, axis=-1, keepdims=True)
        h = x * lax.rsqrt(ms + EPS) * gain + shift
        proj = jnp.dot(h.astype(BF16), win_ref[...], preferred_element_type=F32)
        uf = proj[:, :FOURIER_WIDTH].astype(BF16)
        ab = jnp.dot(uf, pq_ref[...], preferred_element_type=F32)
        a_ref[rows, :] = ab[:, :FOURIER_WIDTH].astype(BF16)
        b_ref[rows, :] = ab[:, FOURIER_WIDTH:].astype(BF16)
        q = proj[:, q0:k0]
        ssq = jnp.dot((q * q).astype(BF16), bdq_ref[...], preferred_element_type=F32)
        q_ref[rows, :] = (q * lax.rsqrt(ssq * (1.0 / HEAD_DIM) + EPS) * gq_ref[...]).astype(BF16)
        k = proj[:, k0:v0]
        ssk = jnp.dot((k * k).astype(BF16), bdk_ref[...], preferred_element_type=F32)
        k_ref[rows, :] = (k * lax.rsqrt(ssk * (1.0 / HEAD_DIM) + EPS) * gk_ref[...]).astype(BF16)
        v_ref[rows, :] = proj[:, v0:].astype(BF16)


def _inproj(x, mod, g, win, pq, gq, gk, w_gate, w_up, w_down, tm=1024):
    B = x.shape[0]
    steps_per_batch = SEQ // tm
    w_rows = N_EXPERTS * D_MODEL
    w_blk = w_rows // (B * steps_per_batch)
    const = lambda shape: pl.BlockSpec(shape, lambda b, i: (0,) * len(shape))
    tok = lambda w: pl.BlockSpec((None, tm, w), lambda b, i: (b, i, 0))
    wsl = lambda c: pl.BlockSpec((w_blk, c), lambda b, i: (b * steps_per_batch + i, 0))
    sds = lambda w: jax.ShapeDtypeStruct((B, SEQ, w), BF16)
    step = lambda b, i: b * steps_per_batch + i
    a, b, q, k, v, experts = pl.pallas_call(
        _inproj_kernel,
        grid=(B, steps_per_batch),
        in_specs=[tok(D_MODEL),
                  pl.BlockSpec((None, N_ADA, D_MODEL), lambda b, i: (b, 0, 0)),
                  const((1, D_MODEL)),
                  const((D_MODEL, IN_PROJ_WIDTH)),
                  const((FOURIER_WIDTH, 2 * FOURIER_WIDTH)),
                  const((ATTN_WIDTH, ATTN_WIDTH)),
                  const((KV_WIDTH, KV_WIDTH)),
                  const((1, ATTN_WIDTH)),
                  const((1, KV_WIDTH)),
                  wsl(D_EXPERT), wsl(D_EXPERT), wsl(D_MODEL)],
        out_specs=[tok(FOURIER_WIDTH), tok(FOURIER_WIDTH), tok(ATTN_WIDTH), tok(KV_WIDTH),
                   tok(KV_WIDTH),
                   pl.BlockSpec((3, w_blk, D_MODEL), lambda b, i: (0, step(b, i), 0))],
        out_shape=[sds(FOURIER_WIDTH), sds(FOURIER_WIDTH), sds(ATTN_WIDTH), sds(KV_WIDTH),
                   sds(KV_WIDTH), jax.ShapeDtypeStruct((3, w_rows, D_MODEL), BF16)],
        compiler_params=_params(("parallel", "parallel"), VMEM_LIMIT),
        name="inproj",
    )(x, mod, g, win, pq, jnp.asarray(_group_ones(ATTN_WIDTH)).astype(BF16),
      jnp.asarray(_group_ones(KV_WIDTH)).astype(BF16), gq, gk,
      w_gate.reshape(w_rows, D_EXPERT), w_up.reshape(w_rows, D_EXPERT),
      w_down.reshape(N_EXPERTS * D_EXPERT, D_MODEL))
    return a, b, q, k, v, experts.reshape(3, N_EXPERTS, D_MODEL, D_MODEL)


HALF = SEQ // 2
REV = 128
REV_BLOCKS = HALF // REV
FOURIER_ROWS = 512


def _fourier_kernel(cm_ref, sm_ref, psh_ref, alt_ref, altrow_ref, a_ref, b_ref, bf_ref, o_ref):
    psh = psh_ref[...]

    def reversed_block(win_lo, k, src):
        if k == 0:
            return jnp.dot(psh[:, :REV], src(win_lo, REV), preferred_element_type=F32)
        return jnp.dot(psh, src(win_lo, 2 * REV), preferred_element_type=F32)

    def folded(ref, sign):
        blocks = []
        for k in range(REV_BLOCKS):
            lo = SEQ - REV * (k + 1)
            rev = reversed_block(lo, k, lambda s, n: ref[pl.ds(s, n), :])
            blocks.append((ref[pl.ds(k * REV, REV), :].astype(F32) + sign * rev).astype(BF16))
        return jnp.concatenate(blocks, axis=0)

    a_even = folded(a_ref, 1.0)
    b_odd = folded(b_ref, -1.0)
    a_mid = a_ref[pl.ds(HALF, 1), :].astype(F32)
    bias = bf_ref[...]
    z_blocks = []
    for i in range(HALF // FOURIER_ROWS):
        rows = pl.ds(i * FOURIER_ROWS, FOURIER_ROWS)
        yc = jnp.dot(cm_ref[rows, :], a_even, preferred_element_type=F32)
        yc = yc + alt_ref[rows, :] * a_mid + bias
        ys = jnp.dot(sm_ref[rows, :], b_odd, preferred_element_type=F32)
        o_ref[rows, :] = (yc - ys).astype(BF16)
        z_blocks.append((yc + ys).astype(BF16))
    z = jnp.concatenate(z_blocks, axis=0)
    for k in range(REV_BLOCKS):
        lo = HALF - REV * (k + 1)
        top = reversed_block(lo, k, lambda s, n: z[s:s + n])
        o_ref[pl.ds(HALF + k * REV, REV), :] = top.astype(BF16)
    y_mid = jnp.dot(altrow_ref[...], a_ref[...], preferred_element_type=F32)[0:1, :] + bias
    o_ref[pl.ds(HALF, 1), :] = y_mid.astype(BF16)


@functools.lru_cache(maxsize=None)
def _fold_tables():
    s = np.arange(HALF, dtype=np.int64)
    ph = (s[:, None] * s[None, :]) % SEQ
    ang = 2.0 * np.pi * ph.astype(np.float64) / SEQ
    sc = 1.0 / math.sqrt(SEQ)
    cm = (np.cos(ang) * sc).astype(np.float32)
    sm = (np.sin(ang) * sc).astype(np.float32)
    psh = np.zeros((REV, 2 * REV), np.float32)
    psh[np.arange(REV), REV - np.arange(REV)] = 1.0
    alt = (np.where(s % 2 == 0, 1.0, -1.0) * sc).astype(np.float32).reshape(HALF, 1)
    t = np.arange(SEQ)
    altrow = np.zeros((SUBLANES, SEQ), np.float32)
    altrow[0] = np.where(t % 2 == 0, 1.0, -1.0) * sc
    return cm, sm, psh, alt, altrow


def _fourier(a, b, bf):
    B = a.shape[0]
    cm, sm, psh, alt, altrow = _fold_tables()
    tok = pl.BlockSpec((None, SEQ, FOURIER_WIDTH), lambda i: (i, 0, 0))
    const = lambda shape: pl.BlockSpec(shape, lambda i: (0,) * len(shape))
    return pl.pallas_call(
        _fourier_kernel,
        grid=(B,),
        in_specs=[const((HALF, HALF)), const((HALF, HALF)), const((REV, 2 * REV)),
                  const((HALF, 1)), const((SUBLANES, SEQ)),
                  tok, tok, const((1, FOURIER_WIDTH))],
        out_specs=tok,
        out_shape=jax.ShapeDtypeStruct((B, SEQ, FOURIER_WIDTH), BF16),
        compiler_params=_params(("parallel",), VMEM_LIMIT),
        name="fourier",
    )(jnp.asarray(cm).astype(BF16), jnp.asarray(sm).astype(BF16), jnp.asarray(psh).astype(BF16),
      jnp.asarray(alt), jnp.asarray(altrow).astype(BF16), a, b, bf)


ATT_SUB = 16
ATT_ROWS = ATT_SUB * BLOCK
ATT_STEPS = N_BLOCKS // ATT_SUB


def _attn_kernel(sink_ref, q_ref, kl_ref, km_ref, kr_ref, vl_ref, vm_ref, vr_ref, bias_ref,
                 o_ref):
    i = pl.program_id(1)
    keys = jnp.concatenate([kl_ref[...], km_ref[...], kr_ref[...]], axis=0)
    vals = jnp.concatenate([vl_ref[...], vm_ref[...], vr_ref[...]], axis=0)
    first_k = lax.broadcasted_iota(jnp.int32, keys.shape, 1) < HEAD_DIM
    first_q = lax.broadcasted_iota(jnp.int32, (Q_PER_KV * BLOCK, LANES), 1) < HEAD_DIM
    row_head = lax.broadcasted_iota(jnp.int32, (Q_PER_KV * BLOCK, 1), 0) // BLOCK
    zero = jnp.zeros_like(keys)
    ones_lo = jnp.where(first_k, 1.0, 0.0).astype(BF16)
    ones_hi = jnp.where(first_k, 0.0, 1.0).astype(BF16)
    keys_kv = [jnp.where(first_k, keys, zero), jnp.where(first_k, zero, keys)]
    vals_kv = [jnp.concatenate([jnp.where(first_k, vals, zero), ones_lo], axis=1),
               jnp.concatenate([jnp.where(first_k, zero, vals), ones_hi], axis=1)]
    sinks = []
    for kv in range(N_KV_HEADS):
        sink = jnp.zeros((Q_PER_KV * BLOCK, 1), F32)
        for r in range(Q_PER_KV):
            sink = jnp.where(row_head == r, sink_ref[kv * Q_PER_KV + r] * LOG2E, sink)
        sinks.append(sink)
    for j in range(ATT_SUB):
        variant = 1
        if j == 0:
            variant = jnp.where(i == 0, 0, variant)
        if j == ATT_SUB - 1:
            variant = jnp.where(i == ATT_STEPS - 1, 2, variant)
        qrows = pl.ds(j * BLOCK, BLOCK)
        krows = slice(j * BLOCK, j * BLOCK + SPAN)
        qs = jnp.concatenate([q_ref[qrows, r * LANES:(r + 1) * LANES] for r in range(Q_PER_KV)],
                             axis=0)
        keys2 = jnp.concatenate([keys_kv[0][krows], keys_kv[1][krows]], axis=0)
        vals2 = jnp.concatenate([vals_kv[0][krows], vals_kv[1][krows]], axis=0)
        logits = lax.dot_general(qs, keys2, (((1,), (1,)), ((), ())),
                                 preferred_element_type=F32)
        logits = logits + bias_ref[variant]
        ms = [jnp.maximum(jnp.max(logits[:, kv * SPAN:(kv + 1) * SPAN], axis=-1, keepdims=True),
                          sinks[kv]) for kv in range(N_KV_HEADS)]
        p = jnp.exp2(jnp.concatenate([logits[:, kv * SPAN:(kv + 1) * SPAN] - ms[kv]
                                      for kv in range(N_KV_HEADS)], axis=1).astype(BF16))
        pv = jnp.dot(p, vals2, preferred_element_type=F32)
        denom = pv[:, LANES:] + jnp.exp2(jnp.where(first_q, sinks[0] - ms[0], sinks[1] - ms[1]))
        out = (pv[:, :LANES] / denom).astype(BF16)
        for r in range(Q_PER_KV):
            o_ref[qrows, r * LANES:(r + 1) * LANES] = out[r * BLOCK:(r + 1) * BLOCK]


def _attn(sink, q, k, v, bias):
    B = q.shape[0]
    edge = lambda f: pl.BlockSpec((None, BLOCK, KV_WIDTH), lambda b, i: (b, f(i), 0))
    left = lambda i: jnp.maximum(i * ATT_SUB - 1, 0)
    right = lambda i: jnp.minimum((i + 1) * ATT_SUB, N_BLOCKS - 1)
    mid = pl.BlockSpec((None, ATT_ROWS, KV_WIDTH), lambda b, i: (b, i, 0))
    qspec = pl.BlockSpec((None, ATT_ROWS, ATTN_WIDTH), lambda b, i: (b, i, 0))
    return pl.pallas_call(
        _attn_kernel,
        grid=(B, ATT_STEPS),
        in_specs=[pl.BlockSpec(memory_space=pltpu.SMEM), qspec,
                  edge(left), mid, edge(right),
                  edge(left), mid, edge(right),
                  pl.BlockSpec((3, Q_PER_KV * BLOCK, N_KV_HEADS * SPAN), lambda b, i: (0, 0, 0))],
        out_specs=qspec,
        out_shape=jax.ShapeDtypeStruct((B, SEQ, ATTN_WIDTH), BF16),
        compiler_params=_params(("parallel", "parallel"), VMEM_LIMIT),
        name="attn",
    )(sink, q, k, k, k, v, v, v, bias)


OUT_ROWS = 256


def _outproj_kernel(yf_ref, ya_ref, x_ref, mod_ref, g_ref, wo_ref, wrh_ref,
                    x1_ref, h2_ref, aff_ref):
    tm = x_ref.shape[0]
    gain = g_ref[...] * (1.0 + mod_ref[4:5, :])
    shift = mod_ref[3:4, :]
    gate1 = mod_ref[2:3, :]
    lane = lax.broadcasted_iota(jnp.int32, (OUT_ROWS, LANES), 1)
    chunks = range(tm // OUT_ROWS)
    mixed = [jnp.dot(jnp.concatenate([yf_ref[pl.ds(c * OUT_ROWS, OUT_ROWS), :],
                                      ya_ref[pl.ds(c * OUT_ROWS, OUT_ROWS), :]], axis=1),
                     wo_ref[...], preferred_element_type=F32) for c in chunks]
    for c in chunks:
        rows = pl.ds(c * OUT_ROWS, OUT_ROWS)
        x1 = x_ref[rows, :] + gate1 * mixed[c]
        x1_ref[rows, :] = x1
        ms = jnp.mean(x1 * x1, axis=-1, keepdims=True)
        h2 = x1 * lax.rsqrt(ms + EPS) * gain + shift
        hi = h2.astype(BF16)
        top = pltpu.bitcast(hi[:, :D_MODEL // 2].astype(F32), jnp.uint32)
        bot = pltpu.bitcast(hi[:, D_MODEL // 2:].astype(F32), jnp.uint32)
        words = top | (bot >> 16)
        for j in range(PACK_ROWS):
            h2_ref[pl.ds(c * OUT_ROWS * PACK_ROWS + j, OUT_ROWS, stride=PACK_ROWS), :] = (
                words[:, j * LANES:(j + 1) * LANES])
        part = jnp.dot(hi, wrh_ref[...], preferred_element_type=F32)
        logits = part + pltpu.roll(part, LANES - N_EXPERTS, axis=1)
        logits = jnp.where(lane < N_EXPERTS, logits, NEG_INF)
        m = jnp.max(logits, axis=-1, keepdims=True)
        e = jnp.exp(logits - m)
        aff = e / jnp.sum(e, axis=-1, keepdims=True)
        aff_ref[:, rows] = jnp.transpose(aff)[:N_EXPERTS]


def _outproj(yf, ya, x, mod, g, wo, wrh, tm=1024):
    B = x.shape[0]
    const = lambda shape: pl.BlockSpec(shape, lambda b, i: (0,) * len(shape))
    tok = lambda w: pl.BlockSpec((None, tm, w), lambda b, i: (b, i, 0))
    return pl.pallas_call(
        _outproj_kernel,
        grid=(B, SEQ // tm),
        in_specs=[tok(FOURIER_WIDTH), tok(ATTN_WIDTH), tok(D_MODEL),
                  pl.BlockSpec((None, N_ADA, D_MODEL), lambda b, i: (b, 0, 0)),
                  const((1, D_MODEL)),
                  const((D_MODEL, D_MODEL)),
                  const((D_MODEL, LANES))],
        out_specs=[tok(D_MODEL),
                   pl.BlockSpec((None, tm * PACK_ROWS, LANES), lambda b, i: (b, i, 0)),
                   pl.BlockSpec((None, N_EXPERTS, tm), lambda b, i: (b, 0, i))],
        out_shape=[jax.ShapeDtypeStruct((B, SEQ, D_MODEL), F32),
                   jax.ShapeDtypeStruct((B, SEQ * PACK_ROWS, LANES), jnp.uint32),
                   jax.ShapeDtypeStruct((B, N_EXPERTS, SEQ), F32)],
        compiler_params=_params(("parallel", "parallel"), VMEM_LIMIT),
        name="outproj",
    )(yf, ya, x, mod, g, wo, wrh)


ROUTE_BATCHES = 4
SEARCH_BITS = 3


def _route_kernel(aff_ref, tri_ref, idx_ref, gate_ref):
    for bb in range(ROUTE_BATCHES):
        _route_one(aff_ref[bb], tri_ref[...], idx_ref.at[bb], gate_ref.at[bb])


def _route_one(aff_t, tri, idx_ref, gate_ref):
    bits = pltpu.bitcast(aff_t, jnp.int32)
    cap = float(CAPACITY)

    t = jnp.zeros((N_EXPERTS, 1), jnp.int32)
    for shift in range(30 - SEARCH_BITS, -1, -SEARCH_BITS):
        digit = jnp.zeros((N_EXPERTS, 1), jnp.int32)
        for k in range(1, 1 << SEARCH_BITS):
            cnt = jnp.sum(jnp.where(bits >= (t | (k << shift)), 1.0, 0.0), axis=1, keepdims=True)
            digit = digit + jnp.where(cnt >= cap, 1, 0)
        t = t | (digit << shift)
    gt = bits > t
    eq = bits == t
    need = cap - jnp.sum(jnp.where(gt, 1.0, 0.0), axis=1, keepdims=True)

    n_chunks = SEQ // LANES

    def prefix(flags_f32):
        outs = []
        carry = jnp.zeros((N_EXPERTS, 1), F32)
        for c in range(n_chunks):
            f = flags_f32[:, c * LANES:(c + 1) * LANES]
            incl = jnp.dot(f.astype(BF16), tri, preferred_element_type=F32)
            outs.append(incl - f + carry)
            carry = carry + jnp.sum(f, axis=1, keepdims=True)
        return jnp.concatenate(outs, axis=1)

    eq_f = jnp.where(eq, 1.0, 0.0)
    eq_rank = prefix(eq_f)
    sel_f = jnp.where(gt, 1.0, jnp.where(eq_rank < need, eq_f, 0.0))
    pos = prefix(sel_f)
    posm = jnp.where(sel_f > 0.0, pos, -1.0)

    hi = aff_t.astype(BF16).astype(F32)
    r1 = aff_t - hi
    mid = r1.astype(BF16).astype(F32)
    lo = r1 - mid
    tok = lax.broadcasted_iota(jnp.int32, (N_EXPERTS, SEQ), 1)
    row = lax.broadcasted_iota(jnp.int32, (N_EXPERTS, SEQ), 0)
    tok_rows = jnp.where(row == 0, (tok >> 6).astype(F32),
                         jnp.where(row == 1, (tok & 63).astype(F32), 0.0))
    vals_t = jnp.concatenate([hi, mid, lo, tok_rows], axis=0).astype(BF16)

    slot = lax.broadcasted_iota(jnp.int32, (CAPACITY, SEQ), 0).astype(F32).astype(BF16)
    posm_b = posm.astype(BF16)
    one_b = jnp.ones((CAPACITY, SEQ), BF16)
    zero_b = jnp.zeros((CAPACITY, SEQ), BF16)
    for e in range(N_EXPERTS):
        onehot = jnp.where(posm_b[e:e + 1, :] == slot, one_b, zero_b)
        res = lax.dot_general(vals_t, onehot, (((1,), (1,)), ((), ())),
                              preferred_element_type=F32)
        cols = pl.ds(e * CAPACITY, CAPACITY)
        tok_idx = res[3 * N_EXPERTS:3 * N_EXPERTS + 1] * 64.0 + res[3 * N_EXPERTS + 1:
                                                                    3 * N_EXPERTS + 2]
        idx_ref[:, cols] = tok_idx.astype(jnp.int32) * PACK_ROWS
        gate_ref[:, cols] = (res[e:e + 1] + res[N_EXPERTS + e:N_EXPERTS + e + 1]
                             + res[2 * N_EXPERTS + e:2 * N_EXPERTS + e + 1])


def _route(aff):
    B = aff.shape[0]
    n = N_EXPERTS * CAPACITY
    return pl.pallas_call(
        _route_kernel,
        grid=(B // ROUTE_BATCHES,),
        in_specs=[pl.BlockSpec((ROUTE_BATCHES, N_EXPERTS, SEQ), lambda b: (b, 0, 0)),
                  pl.BlockSpec((LANES, LANES), lambda b: (0, 0))],
        out_specs=[pl.BlockSpec((ROUTE_BATCHES, 1, n), lambda b: (b, 0, 0)),
                   pl.BlockSpec((ROUTE_BATCHES, 1, n), lambda b: (b, 0, 0))],
        out_shape=[jax.ShapeDtypeStruct((B, 1, n), jnp.int32),
                   jax.ShapeDtypeStruct((B, 1, n), F32)],
        compiler_params=_params(("parallel",), VMEM_LIMIT),
        name="route",
    )(aff, jnp.asarray(_tri_incl()).astype(BF16))


PAIR = 4


def _moe_kernel(idx_ref, h2_ref, w_ref, y_ref, xin0_ref, xin1_ref):
    e = pl.program_id(1)
    last = N_EXPERTS - 1
    n = N_EXPERTS * CAPACITY
    rows = PAIR * CAPACITY

    def gather_rows(ex, dst_ref):
        for bb in range(PAIR):
            base = bb * n + ex * CAPACITY
            for p in range(CAPACITY):
                off = idx_ref[0, base + p]
                tile = h2_ref[bb, pl.ds(pl.multiple_of((off >> 3) << 3, SUBLANES), SUBLANES), :]
                tile = pltpu.roll(tile, off & PACK_ROWS, axis=0)
                dst_ref[pl.ds((bb * CAPACITY + p) * PACK_ROWS, PACK_ROWS), :] = tile[:PACK_ROWS]

    def expert(xin_ref):
        words = [xin_ref[pl.ds(j, rows, stride=PACK_ROWS), :] for j in range(PACK_ROWS)]
        xin = jnp.concatenate(
            [pltpu.bitcast(w & jnp.uint32(0xFFFF0000), F32).astype(BF16) for w in words]
            + [pltpu.bitcast(w << 16, F32).astype(BF16) for w in words], axis=1)
        for bb in range(PAIR):
            xb = xin[bb * CAPACITY:(bb + 1) * CAPACITY]
            a = jnp.dot(xb, w_ref[0], preferred_element_type=F32)
            u = jnp.dot(xb, w_ref[1], preferred_element_type=F32)
            hmid = (a * (1.0 / (1.0 + jnp.exp(-a))) * u).astype(BF16)
            y = jnp.dot(hmid, w_ref[2], preferred_element_type=F32).astype(BF16)
            top = pltpu.bitcast(y[:, :D_MODEL // 2].astype(F32), jnp.uint32)
            bot = pltpu.bitcast(y[:, D_MODEL // 2:].astype(F32), jnp.uint32)
            words = top | (bot >> 16)
            for j in range(PACK_ROWS):
                y_ref[bb, pl.ds(j, CAPACITY, stride=PACK_ROWS), :] = (
                    words[:, j * LANES:(j + 1) * LANES])

    @pl.when(e == 0)
    def _():
        gather_rows(0, xin0_ref)

    def step(xin_cur, xin_nxt):
        gather_rows(jnp.minimum(e + 1, last), xin_nxt)
        expert(xin_cur)

    @pl.when(e % 2 == 0)
    def _():
        step(xin0_ref, xin1_ref)

    @pl.when(e % 2 == 1)
    def _():
        step(xin1_ref, xin0_ref)


def _moe(idx, h2, experts):
    B = h2.shape[0]
    n = N_EXPERTS * CAPACITY
    rows = SEQ * PACK_ROWS
    pairs = B // PAIR
    stage = pltpu.VMEM((PAIR * CAPACITY * PACK_ROWS, LANES), jnp.uint32)
    out = pl.pallas_call(
        _moe_kernel,
        grid=(pairs, N_EXPERTS),
        in_specs=[pl.BlockSpec((None, 1, PAIR * n), lambda b, e: (b, 0, 0),
                               memory_space=pltpu.SMEM),
                  pl.BlockSpec((None, PAIR, rows, LANES), lambda b, e: (b, 0, 0, 0)),
                  pl.BlockSpec((3, None, D_MODEL, D_MODEL), lambda b, e: (0, e, 0, 0))],
        out_specs=pl.BlockSpec((None, PAIR, CAPACITY * PACK_ROWS, LANES),
                               lambda b, e: (b, 0, e, 0)),
        out_shape=jax.ShapeDtypeStruct((pairs, PAIR, n * PACK_ROWS, LANES), jnp.uint32),
        scratch_shapes=[stage, stage],
        compiler_params=_params(("parallel", "arbitrary"), VMEM_LIMIT),
        name="moe",
    )(idx.reshape(pairs, 1, PAIR * n), h2.reshape(pairs, PAIR, rows, LANES), experts)
    return out.reshape(B, n * PACK_ROWS, LANES)


COMBINE_EXPERTS = 8
SCATTER_UNROLL = 8
COMBINE_ROWS = 256


def _combine_kernel(idx_ref, gate_ref, y_ref, x1_hbm, mod_ref, o_ref, acc_ref, x1_ref, x1_sem):
    j = pl.program_id(1)
    slots = COMBINE_EXPERTS * CAPACITY
    base = j * slots

    def x1_copy():
        return pltpu.make_async_copy(x1_hbm.at[pl.program_id(0)], x1_ref, x1_sem)

    @pl.when(j == 0)
    def _():
        x1_copy().start()
        acc_ref[...] = jnp.zeros_like(acc_ref)

    upper = lax.broadcasted_iota(jnp.int32, (SUBLANES, LANES), 0) < PACK_ROWS
    for g in range(slots // SCATTER_UNROLL):
        new = []
        for u in range(0, SCATTER_UNROLL, 2):
            r = g * SCATTER_UNROLL + u
            words = y_ref[pl.ds(r * PACK_ROWS, SUBLANES), :]
            hi = pltpu.bitcast(words & jnp.uint32(0xFFFF0000), F32)
            lo = pltpu.bitcast(words << 16, F32)
            slabs = (jnp.where(upper, hi, pltpu.roll(lo, PACK_ROWS, axis=0)),
                     jnp.where(upper, pltpu.roll(hi, PACK_ROWS, axis=0), lo))
            for k in range(2):
                dst = pl.multiple_of(idx_ref[0, base + r + k] * (ROW_SLAB // PACK_ROWS), ROW_SLAB)
                new.append((dst, acc_ref[pl.ds(dst, ROW_SLAB), :]
                            + slabs[k] * gate_ref[0, base + r + k]))
        for dst, val in new:
            acc_ref[pl.ds(dst, ROW_SLAB), :] = val

    @pl.when(j == pl.num_programs(1) - 1)
    def _():
        x1_copy().wait()
        for rb in range(SEQ // COMBINE_ROWS):
            rows = pl.ds(rb * COMBINE_ROWS, COMBINE_ROWS)
            for c in range(ROW_SLAB):
                cols = slice(c * LANES, (c + 1) * LANES)
                chunk = acc_ref[pl.ds(rb * COMBINE_ROWS * ROW_SLAB + c, COMBINE_ROWS,
                                      stride=ROW_SLAB), :]
                o_ref[rows, cols] = x1_ref[rows, cols] + mod_ref[5:6, cols] * chunk


def _combine(idx, gate, y, x1, mod):
    B = x1.shape[0]
    n = N_EXPERTS * CAPACITY
    tok = pl.BlockSpec((None, SEQ, D_MODEL), lambda b, j: (b, 0, 0))
    smem = pl.BlockSpec((None, 1, n), lambda b, j: (b, 0, 0), memory_space=pltpu.SMEM)
    return pl.pallas_call(
        _combine_kernel,
        grid=(B, N_EXPERTS // COMBINE_EXPERTS),
        in_specs=[smem, smem,
                  pl.BlockSpec((None, COMBINE_EXPERTS * CAPACITY * PACK_ROWS, LANES),
                               lambda b, j: (b, j, 0)),
                  pl.BlockSpec(memory_space=pl.ANY),
                  pl.BlockSpec((None, N_ADA, D_MODEL), lambda b, j: (b, 0, 0))],
        out_specs=tok,
        out_shape=jax.ShapeDtypeStruct((B, SEQ, D_MODEL), F32),
        scratch_shapes=[pltpu.VMEM((SEQ * ROW_SLAB, LANES), F32),
                        pltpu.VMEM((SEQ, D_MODEL), F32),
                        pltpu.SemaphoreType.DMA(())],
        compiler_params=_params(("parallel", "arbitrary"), VMEM_LIMIT),
        name="combine",
    )(idx, gate, y, x1, mod)


def _head_perm():
    perm = []
    for r in range(Q_PER_KV):
        for kv in range(N_KV_HEADS):
            h = kv * Q_PER_KV + r
            perm.extend(range(h * HEAD_DIM, (h + 1) * HEAD_DIM))
    return np.asarray(perm, dtype=np.int32)


def kernel(x, c, rel_bias, w_ada, b_ada, norm_mix_g, norm_ffn_g, w_in, w_fourier, b_fourier,
           q_norm_g, k_norm_g, sink, w_out, w_router, w_gate, w_up, w_down):
    B = x.shape[0]
    assert x.shape[1:] == (SEQ, D_MODEL) and B % PAIR == 0 and B % ROUTE_BATCHES == 0
    assert w_gate.shape[2:] == (D_MODEL, D_EXPERT) and D_EXPERT == D_MODEL
    perm = _head_perm()
    l = 0
    mod = _ada(c, w_ada[l], b_ada[l]).reshape(B, N_ADA, D_MODEL)
    pq = _fold(w_fourier[l])
    bias = _bias_table(rel_bias)

    wi = w_in[l]
    q_cols = wi[:, FOURIER_WIDTH:FOURIER_WIDTH + ATTN_WIDTH][:, perm]
    win = jnp.concatenate([wi[:, :FOURIER_WIDTH], q_cols, wi[:, FOURIER_WIDTH + ATTN_WIDTH:]],
                          axis=1).astype(BF16)
    gq = (jnp.tile(q_norm_g[l], N_Q_HEADS) * (HEAD_DIM ** -0.5 * LOG2E)).reshape(1, ATTN_WIDTH)
    gk = jnp.tile(k_norm_g[l], N_KV_HEADS).reshape(1, KV_WIDTH)
    a, b, q, k, v, experts = _inproj(x, mod, norm_mix_g[l].reshape(1, D_MODEL), win, pq, gq, gk,
                                     w_gate[l], w_up[l], w_down[l])

    yf = _fourier(a, b, b_fourier[l].reshape(1, FOURIER_WIDTH))
    ya = _attn(sink[l], q, k, v, bias)

    wo = w_out[l]
    wo = jnp.concatenate([wo[:FOURIER_WIDTH], wo[FOURIER_WIDTH:][perm]], axis=0).astype(BF16)
    w_hi = w_router[l].astype(BF16)
    w_lo = (w_router[l] - w_hi.astype(F32)).astype(BF16)
    wrh = jnp.pad(jnp.concatenate([w_hi, w_lo], axis=1), ((0, 0), (0, LANES - 2 * N_EXPERTS)))
    x1, h2, aff = _outproj(yf, ya, x, mod, norm_ffn_g[l].reshape(1, D_MODEL), wo, wrh)

    idx, gate = _route(aff)
    n = N_EXPERTS * CAPACITY
    y = _moe(idx, h2, experts)
    return _combine(idx, gate, y, x1, mod)
```

```python
import functools
import math

import numpy as np
import jax
import jax.numpy as jnp
from jax import lax
from jax.experimental import pallas as pl
from jax.experimental.pallas import tpu as pltpu

D_MODEL = 1024
SEQ = 2048
HEAD_DIM = 64
FOURIER_WIDTH = 512
ATTN_WIDTH = 512
N_GROUPS = 8
N_Q_HEADS = 8
Q_PER_KV = 4
N_KV_HEADS = 2
KV_WIDTH = 128
IN_PROJ_WIDTH = 1280
WINDOW = 128
BLOCK = 128
SPAN = BLOCK + 2 * WINDOW
N_BLOCKS = SEQ // BLOCK
N_BUCKETS = 32
MAX_DISTANCE = 128
N_EXPERTS = 16
CAPACITY = 2 * SEQ // N_EXPERTS
D_EXPERT = 1024
N_ADA = 6
EPS = 1e-6

LANES = 128
SUBLANES = 8
ROW_SLAB = D_MODEL // LANES
PACK_ROWS = ROW_SLAB // 2
VMEM_LIMIT = 56 * 1024 * 1024

F32 = jnp.float32
BF16 = jnp.bfloat16
NEG_INF = float("-inf")
LOG2E = math.log2(math.e)


def _params(sem, vmem=None):
    return pltpu.CompilerParams(dimension_semantics=sem, vmem_limit_bytes=vmem)


@functools.lru_cache(maxsize=None)
def _chan_dft():
    c = np.arange(HEAD_DIM, dtype=np.int64)
    ph = (c[:, None] * c[None, :]) % HEAD_DIM
    ang = 2.0 * np.pi * ph.astype(np.float64) / HEAD_DIM
    sc = 1.0 / math.sqrt(HEAD_DIM)
    eye = np.eye(N_GROUPS)
    cbd = np.kron(eye, np.cos(ang) * sc)
    sbd = np.kron(eye, np.sin(ang) * sc)
    return cbd.astype(np.float32), sbd.astype(np.float32)


@functools.lru_cache(maxsize=None)
def _bucket_table():
    rel = np.arange(SPAN)[None, :] - WINDOW - np.arange(BLOCK)[:, None]
    half = N_BUCKETS // 2
    max_exact = half // 2
    n = np.abs(rel)
    nf = np.maximum(n, 1).astype(np.float64)
    large = max_exact + (np.log(nf / max_exact) / math.log(MAX_DISTANCE / max_exact)
                         * (half - max_exact)).astype(np.int64)
    sq = np.maximum(n.astype(np.int64) ** 2 // (max_exact * max_exact), 1)
    large_int = max_exact + np.floor(np.log2(sq.astype(np.float64)) + 1e-9).astype(np.int64)
    assert np.array_equal(np.where(n >= max_exact, large, 0), np.where(n >= max_exact, large_int, 0))
    large = np.minimum(large, half - 1)
    bucket = np.where(rel > 0, half, 0) + np.where(n < max_exact, n, large)
    return bucket.astype(np.int32)


@functools.lru_cache(maxsize=None)
def _group_ones(width):
    return np.kron(np.eye(width // HEAD_DIM), np.ones((HEAD_DIM, HEAD_DIM))).astype(np.float32)


@functools.lru_cache(maxsize=None)
def _tri_incl():
    i = np.arange(LANES)
    return (i[:, None] <= i[None, :]).astype(np.float32)


def _ada_kernel(c_ref, w_ref, b_ref, o_ref):
    c = c_ref[...]
    ca = c * (1.0 / (1.0 + jnp.exp(-c)))
    o_ref[...] = jnp.dot(ca, w_ref[...], precision=lax.Precision.HIGHEST,
                         preferred_element_type=F32) + b_ref[...]


def _ada(c, w_ada, b_ada):
    B = c.shape[0]
    n = N_ADA * D_MODEL
    tn = D_MODEL
    return pl.pallas_call(
        _ada_kernel,
        grid=(n // tn,),
        in_specs=[pl.BlockSpec((B, D_MODEL), lambda j: (0, 0)),
                  pl.BlockSpec((D_MODEL, tn), lambda j: (0, j)),
                  pl.BlockSpec((1, tn), lambda j: (0, j))],
        out_specs=pl.BlockSpec((B, tn), lambda j: (0, j)),
        out_shape=jax.ShapeDtypeStruct((B, n), F32),
        compiler_params=_params(("arbitrary",)),
        name="ada",
    )(c, w_ada, b_ada.reshape(1, n))


def _fold_kernel(cbd_ref, sbd_ref, w_ref, o_ref):
    w = w_ref[...]
    o_ref[:, :FOURIER_WIDTH] = jnp.dot(cbd_ref[...], w, precision=lax.Precision.HIGHEST,
                                       preferred_element_type=F32).astype(BF16)
    o_ref[:, FOURIER_WIDTH:] = jnp.dot(sbd_ref[...], w, precision=lax.Precision.HIGHEST,
                                       preferred_element_type=F32).astype(BF16)


def _fold(w_fourier):
    wbd = (jnp.eye(N_GROUPS, dtype=F32)[:, None, :, None] * w_fourier[:, :, None, :]
           ).reshape(FOURIER_WIDTH, FOURIER_WIDTH)
    cbd, sbd = _chan_dft()
    return pl.pallas_call(
        _fold_kernel,
        out_shape=jax.ShapeDtypeStruct((FOURIER_WIDTH, 2 * FOURIER_WIDTH), BF16),
        name="fold",
    )(jnp.asarray(cbd), jnp.asarray(sbd), wbd)


def _bias_kernel(rb_ref, bucket_ref, o_ref):
    h = pl.program_id(0)
    bk = bucket_ref[...]
    acc = jnp.zeros((BLOCK, SPAN), F32)
    for b in range(N_BUCKETS):
        acc = jnp.where(bk == b, rb_ref[b, h], acc)
    j = lax.broadcasted_iota(jnp.int32, (BLOCK, SPAN), 1)
    q = lax.broadcasted_iota(jnp.int32, (BLOCK, SPAN), 0)
    band = jnp.abs(j - WINDOW - q) <= WINDOW
    base = jnp.where(band, acc * LOG2E, NEG_INF)
    o_ref[0] = jnp.where(j >= WINDOW, base, NEG_INF)
    o_ref[1] = base
    o_ref[2] = jnp.where(j < WINDOW + BLOCK, base, NEG_INF)


def _bias_table(rel_bias):
    return pl.pallas_call(
        _bias_kernel,
        grid=(N_Q_HEADS,),
        in_specs=[pl.BlockSpec(memory_space=pltpu.SMEM),
                  pl.BlockSpec((BLOCK, SPAN), lambda h: (0, 0))],
        out_specs=pl.BlockSpec((3, BLOCK, SPAN), lambda h: (0, h % Q_PER_KV, h // Q_PER_KV)),
        out_shape=jax.ShapeDtypeStruct((3, Q_PER_KV * BLOCK, N_KV_HEADS * SPAN), F32),
        compiler_params=_params(("arbitrary",)),
        name="bias",
    )(rel_bias, jnp.asarray(_bucket_table()))


IN_ROWS = 256


def _inproj_kernel(x_ref, mod_ref, g_ref, win_ref, pq_ref, bdq_ref, bdk_ref, gq_ref, gk_ref,
                   wg_ref, wu_ref, wd_ref,
                   a_ref, b_ref, q_ref, k_ref, v_ref, w_out):
    w_out[0] = wg_ref[...].astype(BF16)
    w_out[1] = wu_ref[...].astype(BF16)
    w_out[2] = wd_ref[...].astype(BF16)
    gain = g_ref[...] * (1.0 + mod_ref[1:2, :])
    shift = mod_ref[0:1, :]
    q0 = FOURIER_WIDTH
    k0 = q0 + ATTN_WIDTH
    v0 = k0 + KV_WIDTH
    for c in range(x_ref.shape[0] // IN_ROWS):
        rows = pl.ds(c * IN_ROWS, IN_ROWS)
        x = x_ref[rows, :]
        ms = jnp.mean(x * x, axis=-1, keepdims=True)
        h = x * lax.rsqrt(ms + EPS) * gain + shift
        proj = jnp.dot(h.astype(BF16), win_ref[...], preferred_element_type=F32)
        uf = proj[:, :FOURIER_WIDTH].astype(BF16)
        ab = jnp.dot(uf, pq_ref[...], preferred_element_type=F32)
        a_ref[rows, :] = ab[:, :FOURIER_WIDTH].astype(BF16)
        b_ref[rows, :] = ab[:, FOURIER_WIDTH:].astype(BF16)
        q = proj[:, q0:k0]
        ssq = jnp.dot((q * q).astype(BF16), bdq_ref[...], preferred_element_type=F32)
        q_ref[rows, :] = (q * lax.rsqrt(ssq * (1.0 / HEAD_DIM) + EPS) * gq_ref[...]).astype(BF16)
        k = proj[:, k0:v0]
        ssk = jnp.dot((k * k).astype(BF16), bdk_ref[...], preferred_element_type=F32)
        k_ref[rows, :] = (k * lax.rsqrt(ssk * (1.0 / HEAD_DIM) + EPS) * gk_ref[...]).astype(BF16)
        v_ref[rows, :] = proj[:, v0:].astype(BF16)


def _inproj(x, mod, g, win, pq, gq, gk, w_gate, w_up, w_down, tm=1024):
    B = x.shape[0]
    steps_per_batch = SEQ // tm
    w_rows = N_EXPERTS * D_MODEL
    w_blk = w_rows // (B * steps_per_batch)
    const = lambda shape: pl.BlockSpec(shape, lambda b, i: (0,) * len(shape))
    tok = lambda w: pl.BlockSpec((None, tm, w), lambda b, i: (b, i, 0))
    wsl = lambda c: pl.BlockSpec((w_blk, c), lambda b, i: (b * steps_per_batch + i, 0))
    sds = lambda w: jax.ShapeDtypeStruct((B, SEQ, w), BF16)
    step = lambda b, i: b * steps_per_batch + i
    a, b, q, k, v, experts = pl.pallas_call(
        _inproj_kernel,
        grid=(B, steps_per_batch),
        in_specs=[tok(D_MODEL),
                  pl.BlockSpec((None, N_ADA, D_MODEL), lambda b, i: (b, 0, 0)),
                  const((1, D_MODEL)),
                  const((D_MODEL, IN_PROJ_WIDTH)),
                  const((FOURIER_WIDTH, 2 * FOURIER_WIDTH)),
                  const((ATTN_WIDTH, ATTN_WIDTH)),
                  const((KV_WIDTH, KV_WIDTH)),
                  const((1, ATTN_WIDTH)),
                  const((1, KV_WIDTH)),
                  wsl(D_EXPERT), wsl(D_EXPERT), wsl(D_MODEL)],
        out_specs=[tok(FOURIER_WIDTH), tok(FOURIER_WIDTH), tok(ATTN_WIDTH), tok(KV_WIDTH),
                   tok(KV_WIDTH),
                   pl.BlockSpec((3, w_blk, D_MODEL), lambda b, i: (0, step(b, i), 0))],
        out_shape=[sds(FOURIER_WIDTH), sds(FOURIER_WIDTH), sds(ATTN_WIDTH), sds(KV_WIDTH),
                   sds(KV_WIDTH), jax.ShapeDtypeStruct((3, w_rows, D_MODEL), BF16)],
        compiler_params=_params(("parallel", "parallel"), VMEM_LIMIT),
        name="inproj",
    )(x, mod, g, win, pq, jnp.asarray(_group_ones(ATTN_WIDTH)).astype(BF16),
      jnp.asarray(_group_ones(KV_WIDTH)).astype(BF16), gq, gk,
      w_gate.reshape(w_rows, D_EXPERT), w_up.reshape(w_rows, D_EXPERT),
      w_down.reshape(N_EXPERTS * D_EXPERT, D_MODEL))
    return a, b, q, k, v, experts.reshape(3, N_EXPERTS, D_MODEL, D_MODEL)


HALF = SEQ // 2
REV = 128
REV_BLOCKS = HALF // REV
FOURIER_ROWS = 512


def _fourier_kernel(cm_ref, sm_ref, psh_ref, alt_ref, altrow_ref, a_ref, b_ref, bf_ref, o_ref):
    psh = psh_ref[...]

    def reversed_block(win_lo, k, src):
        if k == 0:
            return jnp.dot(psh[:, :REV], src(win_lo, REV), preferred_element_type=F32)
        return jnp.dot(psh, src(win_lo, 2 * REV), preferred_element_type=F32)

    def folded(ref, sign):
        blocks = []
        for k in range(REV_BLOCKS):
            lo = SEQ - REV * (k + 1)
            rev = reversed_block(lo, k, lambda s, n: ref[pl.ds(s, n), :])
            blocks.append((ref[pl.ds(k * REV, REV), :].astype(F32) + sign * rev).astype(BF16))
        return jnp.concatenate(blocks, axis=0)

    a_even = folded(a_ref, 1.0)
    b_odd = folded(b_ref, -1.0)
    a_mid = a_ref[pl.ds(HALF, 1), :].astype(F32)
    bias = bf_ref[...]
    z_blocks = []
    for i in range(HALF // FOURIER_ROWS):
        rows = pl.ds(i * FOURIER_ROWS, FOURIER_ROWS)
        yc = jnp.dot(cm_ref[rows, :], a_even, preferred_element_type=F32)
        yc = yc + alt_ref[rows, :] * a_mid + bias
        ys = jnp.dot(sm_ref[rows, :], b_odd, preferred_element_type=F32)
        o_ref[rows, :] = (yc - ys).astype(BF16)
        z_blocks.append((yc + ys).astype(BF16))
    z = jnp.concatenate(z_blocks, axis=0)
    for k in range(REV_BLOCKS):
        lo = HALF - REV * (k + 1)
        top = reversed_block(lo, k, lambda s, n: z[s:s + n])
        o_ref[pl.ds(HALF + k * REV, REV), :] = top.astype(BF16)
    y_mid = jnp.dot(altrow_ref[...], a_ref[...], preferred_element_type=F32)[0:1, :] + bias
    o_ref[pl.ds(HALF, 1), :] = y_mid.astype(BF16)


@functools.lru_cache(maxsize=None)
def _fold_tables():
    s = np.arange(HALF, dtype=np.int64)
    ph = (s[:, None] * s[None, :]) % SEQ
    ang = 2.0 * np.pi * ph.astype(np.float64) / SEQ
    sc = 1.0 / math.sqrt(SEQ)
    cm = (np.cos(ang) * sc).astype(np.float32)
    sm = (np.sin(ang) * sc).astype(np.float32)
    psh = np.zeros((REV, 2 * REV), np.float32)
    psh[np.arange(REV), REV - np.arange(REV)] = 1.0
    alt = (np.where(s % 2 == 0, 1.0, -1.0) * sc).astype(np.float32).reshape(HALF, 1)
    t = np.arange(SEQ)
    altrow = np.zeros((SUBLANES, SEQ), np.float32)
    altrow[0] = np.where(t % 2 == 0, 1.0, -1.0) * sc
    return cm, sm, psh, alt, altrow


def _fourier(a, b, bf):
    B = a.shape[0]
    cm, sm, psh, alt, altrow = _fold_tables()
    tok = pl.BlockSpec((None, SEQ, FOURIER_WIDTH), lambda i: (i, 0, 0))
    const = lambda shape: pl.BlockSpec(shape, lambda i: (0,) * len(shape))
    return pl.pallas_call(
        _fourier_kernel,
        grid=(B,),
        in_specs=[const((HALF, HALF)), const((HALF, HALF)), const((REV, 2 * REV)),
                  const((HALF, 1)), const((SUBLANES, SEQ)),
                  tok, tok, const((1, FOURIER_WIDTH))],
        out_specs=tok,
        out_shape=jax.ShapeDtypeStruct((B, SEQ, FOURIER_WIDTH), BF16),
        compiler_params=pltpu.CompilerParams(
            dimension_semantics=("parallel",), vmem_limit_bytes=VMEM_LIMIT,
            allow_input_fusion=[True, True, True, False, True, False, False, False]),
        name="fourier",
    )(jnp.asarray(cm).astype(BF16), jnp.asarray(sm).astype(BF16), jnp.asarray(psh).astype(BF16),
      jnp.asarray(alt), jnp.asarray(altrow).astype(BF16), a, b, bf)


ATT_SUB = 16
ATT_ROWS = ATT_SUB * BLOCK
ATT_STEPS = N_BLOCKS // ATT_SUB


def _attn_kernel(sink_ref, q_ref, kl_ref, km_ref, kr_ref, vl_ref, vm_ref, vr_ref, bias_ref,
                 o_ref):
    i = pl.program_id(1)
    keys = jnp.concatenate([kl_ref[...], km_ref[...], kr_ref[...]], axis=0)
    vals = jnp.concatenate([vl_ref[...], vm_ref[...], vr_ref[...]], axis=0)
    first_k = lax.broadcasted_iota(jnp.int32, keys.shape, 1) < HEAD_DIM
    first_q = lax.broadcasted_iota(jnp.int32, (Q_PER_KV * BLOCK, LANES), 1) < HEAD_DIM
    row_head = lax.broadcasted_iota(jnp.int32, (Q_PER_KV * BLOCK, 1), 0) // BLOCK
    zero = jnp.zeros_like(keys)
    ones_lo = jnp.where(first_k, 1.0, 0.0).astype(BF16)
    ones_hi = jnp.where(first_k, 0.0, 1.0).astype(BF16)
    keys_kv = [jnp.where(first_k, keys, zero), jnp.where(first_k, zero, keys)]
    vals_kv = [jnp.concatenate([jnp.where(first_k, vals, zero), ones_lo], axis=1),
               jnp.concatenate([jnp.where(first_k, zero, vals), ones_hi], axis=1)]
    sinks = []
    for kv in range(N_KV_HEADS):
        sink = jnp.zeros((Q_PER_KV * BLOCK, 1), F32)
        for r in range(Q_PER_KV):
            sink = jnp.where(row_head == r, sink_ref[kv * Q_PER_KV + r] * LOG2E, sink)
        sinks.append(sink)
    for j in range(ATT_SUB):
        variant = 1
        if j == 0:
            variant = jnp.where(i == 0, 0, variant)
        if j == ATT_SUB - 1:
            variant = jnp.where(i == ATT_STEPS - 1, 2, variant)
        qrows = pl.ds(j * BLOCK, BLOCK)
        krows = slice(j * BLOCK, j * BLOCK + SPAN)
        qs = jnp.concatenate([q_ref[qrows, r * LANES:(r + 1) * LANES] for r in range(Q_PER_KV)],
                             axis=0)
        keys2 = jnp.concatenate([keys_kv[0][krows], keys_kv[1][krows]], axis=0)
        vals2 = jnp.concatenate([vals_kv[0][krows], vals_kv[1][krows]], axis=0)
        logits = lax.dot_general(qs, keys2, (((1,), (1,)), ((), ())),
                                 preferred_element_type=F32)
        logits = logits + bias_ref[variant]
        ms = [jnp.maximum(jnp.max(logits[:, kv * SPAN:(kv + 1) * SPAN], axis=-1, keepdims=True),
                          sinks[kv]) for kv in range(N_KV_HEADS)]
        p = jnp.exp2(jnp.concatenate([logits[:, kv * SPAN:(kv + 1) * SPAN] - ms[kv]
                                      for kv in range(N_KV_HEADS)], axis=1).astype(BF16))
        pv = jnp.dot(p, vals2, preferred_element_type=F32)
        denom = pv[:, LANES:] + jnp.exp2(jnp.where(first_q, sinks[0] - ms[0], sinks[1] - ms[1]))
        out = (pv[:, :LANES] / denom).astype(BF16)
        for r in range(Q_PER_KV):
            o_ref[qrows, r * LANES:(r + 1) * LANES] = out[r * BLOCK:(r + 1) * BLOCK]


def _attn(sink, q, k, v, bias):
    B = q.shape[0]
    edge = lambda f: pl.BlockSpec((None, BLOCK, KV_WIDTH), lambda b, i: (b, f(i), 0))
    left = lambda i: jnp.maximum(i * ATT_SUB - 1, 0)
    right = lambda i: jnp.minimum((i + 1) * ATT_SUB, N_BLOCKS - 1)
    mid = pl.BlockSpec((None, ATT_ROWS, KV_WIDTH), lambda b, i: (b, i, 0))
    qspec = pl.BlockSpec((None, ATT_ROWS, ATTN_WIDTH), lambda b, i: (b, i, 0))
    return pl.pallas_call(
        _attn_kernel,
        grid=(B, ATT_STEPS),
        in_specs=[pl.BlockSpec(memory_space=pltpu.SMEM), qspec,
                  edge(left), mid, edge(right),
                  edge(left), mid, edge(right),
                  pl.BlockSpec((3, Q_PER_KV * BLOCK, N_KV_HEADS * SPAN), lambda b, i: (0, 0, 0))],
        out_specs=qspec,
        out_shape=jax.ShapeDtypeStruct((B, SEQ, ATTN_WIDTH), BF16),
        compiler_params=_params(("parallel", "parallel"), VMEM_LIMIT),
        name="attn",
    )(sink, q, k, k, k, v, v, v, bias)


OUT_ROWS = 256


def _outproj_kernel(yf_ref, ya_ref, x_ref, mod_ref, g_ref, wo_ref, wrh_ref,
                    x1_ref, h2_ref, aff_ref):
    tm = x_ref.shape[0]
    gain = g_ref[...] * (1.0 + mod_ref[4:5, :])
    shift = mod_ref[3:4, :]
    gate1 = mod_ref[2:3, :]
    lane = lax.broadcasted_iota(jnp.int32, (OUT_ROWS, LANES), 1)
    chunks = range(tm // OUT_ROWS)
    mixed = [jnp.dot(jnp.concatenate([yf_ref[pl.ds(c * OUT_ROWS, OUT_ROWS), :],
                                      ya_ref[pl.ds(c * OUT_ROWS, OUT_ROWS), :]], axis=1),
                     wo_ref[...], preferred_element_type=F32) for c in chunks]
    for c in chunks:
        rows = pl.ds(c * OUT_ROWS, OUT_ROWS)
        x1 = x_ref[rows, :] + gate1 * mixed[c]
        x1_ref[rows, :] = x1
        ms = jnp.mean(x1 * x1, axis=-1, keepdims=True)
        h2 = x1 * lax.rsqrt(ms + EPS) * gain + shift
        hi = h2.astype(BF16)
        top = pltpu.bitcast(hi[:, :D_MODEL // 2].astype(F32), jnp.uint32)
        bot = pltpu.bitcast(hi[:, D_MODEL // 2:].astype(F32), jnp.uint32)
        words = top | (bot >> 16)
        for j in range(PACK_ROWS):
            h2_ref[pl.ds(c * OUT_ROWS * PACK_ROWS + j, OUT_ROWS, stride=PACK_ROWS), :] = (
                words[:, j * LANES:(j + 1) * LANES])
        part = jnp.dot(hi, wrh_ref[...], preferred_element_type=F32)
        logits = part + pltpu.roll(part, LANES - N_EXPERTS, axis=1)
        logits = jnp.where(lane < N_EXPERTS, logits, NEG_INF)
        m = jnp.max(logits, axis=-1, keepdims=True)
        e = jnp.exp(logits - m)
        aff = e / jnp.sum(e, axis=-1, keepdims=True)
        aff_ref[:, rows] = jnp.transpose(aff)[:N_EXPERTS]


def _outproj(yf, ya, x, mod, g, wo, wrh, tm=1024):
    B = x.shape[0]
    const = lambda shape: pl.BlockSpec(shape, lambda b, i: (0,) * len(shape))
    tok = lambda w: pl.BlockSpec((None, tm, w), lambda b, i: (b, i, 0))
    return pl.pallas_call(
        _outproj_kernel,
        grid=(B, SEQ // tm),
        in_specs=[tok(FOURIER_WIDTH), tok(ATTN_WIDTH), tok(D_MODEL),
                  pl.BlockSpec((None, N_ADA, D_MODEL), lambda b, i: (b, 0, 0)),
                  const((1, D_MODEL)),
                  const((D_MODEL, D_MODEL)),
                  const((D_MODEL, LANES))],
        out_specs=[tok(D_MODEL),
                   pl.BlockSpec((None, tm * PACK_ROWS, LANES), lambda b, i: (b, i, 0)),
                   pl.BlockSpec((None, N_EXPERTS, tm), lambda b, i: (b, 0, i))],
        out_shape=[jax.ShapeDtypeStruct((B, SEQ, D_MODEL), F32),
                   jax.ShapeDtypeStruct((B, SEQ * PACK_ROWS, LANES), jnp.uint32),
                   jax.ShapeDtypeStruct((B, N_EXPERTS, SEQ), F32)],
        compiler_params=_params(("parallel", "parallel"), VMEM_LIMIT),
        name="outproj",
    )(yf, ya, x, mod, g, wo, wrh)


ROUTE_BATCHES = 4
SEARCH_BITS = 3


def _route_kernel(aff_ref, tri_ref, idx_ref, gate_ref):
    for bb in range(ROUTE_BATCHES):
        _route_one(aff_ref[bb], tri_ref[...], idx_ref.at[bb], gate_ref.at[bb])


def _route_one(aff_t, tri, idx_ref, gate_ref):
    bits = pltpu.bitcast(aff_t, jnp.int32)
    cap = float(CAPACITY)

    t = jnp.zeros((N_EXPERTS, 1), jnp.int32)
    for shift in range(30 - SEARCH_BITS, -1, -SEARCH_BITS):
        digit = jnp.zeros((N_EXPERTS, 1), jnp.int32)
        for k in range(1, 1 << SEARCH_BITS):
            cnt = jnp.sum(jnp.where(bits >= (t | (k << shift)), 1.0, 0.0), axis=1, keepdims=True)
            digit = digit + jnp.where(cnt >= cap, 1, 0)
        t = t | (digit << shift)
    gt = bits > t
    eq = bits == t
    need = cap - jnp.sum(jnp.where(gt, 1.0, 0.0), axis=1, keepdims=True)

    n_chunks = SEQ // LANES

    def prefix(flags_f32):
        outs = []
        carry = jnp.zeros((N_EXPERTS, 1), F32)
        for c in range(n_chunks):
            f = flags_f32[:, c * LANES:(c + 1) * LANES]
            incl = jnp.dot(f.astype(BF16), tri, preferred_element_type=F32)
            outs.append(incl - f + carry)
            carry = carry + jnp.sum(f, axis=1, keepdims=True)
        return jnp.concatenate(outs, axis=1)

    eq_f = jnp.where(eq, 1.0, 0.0)
    eq_rank = prefix(eq_f)
    sel_f = jnp.where(gt, 1.0, jnp.where(eq_rank < need, eq_f, 0.0))
    pos = prefix(sel_f)
    posm = jnp.where(sel_f > 0.0, pos, -1.0)

    hi = aff_t.astype(BF16).astype(F32)
    r1 = aff_t - hi
    mid = r1.astype(BF16).astype(F32)
    lo = r1 - mid
    tok = lax.broadcasted_iota(jnp.int32, (N_EXPERTS, SEQ), 1)
    row = lax.broadcasted_iota(jnp.int32, (N_EXPERTS, SEQ), 0)
    tok_rows = jnp.where(row == 0, (tok >> 6).astype(F32),
                         jnp.where(row == 1, (tok & 63).astype(F32), 0.0))
    vals_t = jnp.concatenate([hi, mid, lo, tok_rows], axis=0).astype(BF16)

    slot = lax.broadcasted_iota(jnp.int32, (CAPACITY, SEQ), 0).astype(F32).astype(BF16)
    posm_b = posm.astype(BF16)
    one_b = jnp.ones((CAPACITY, SEQ), BF16)
    zero_b = jnp.zeros((CAPACITY, SEQ), BF16)
    for e in range(N_EXPERTS):
        onehot = jnp.where(posm_b[e:e + 1, :] == slot, one_b, zero_b)
        res = lax.dot_general(vals_t, onehot, (((1,), (1,)), ((), ())),
                              preferred_element_type=F32)
        cols = pl.ds(e * CAPACITY, CAPACITY)
        tok_idx = res[3 * N_EXPERTS:3 * N_EXPERTS + 1] * 64.0 + res[3 * N_EXPERTS + 1:
                                                                    3 * N_EXPERTS + 2]
        idx_ref[:, cols] = tok_idx.astype(jnp.int32) * PACK_ROWS
        gate_ref[:, cols] = (res[e:e + 1] + res[N_EXPERTS + e:N_EXPERTS + e + 1]
                             + res[2 * N_EXPERTS + e:2 * N_EXPERTS + e + 1])


def _route(aff):
    B = aff.shape[0]
    n = N_EXPERTS * CAPACITY
    return pl.pallas_call(
        _route_kernel,
        grid=(B // ROUTE_BATCHES,),
        in_specs=[pl.BlockSpec((ROUTE_BATCHES, N_EXPERTS, SEQ), lambda b: (b, 0, 0)),
                  pl.BlockSpec((LANES, LANES), lambda b: (0, 0))],
        out_specs=[pl.BlockSpec((ROUTE_BATCHES, 1, n), lambda b: (b, 0, 0)),
                   pl.BlockSpec((ROUTE_BATCHES, 1, n), lambda b: (b, 0, 0))],
        out_shape=[jax.ShapeDtypeStruct((B, 1, n), jnp.int32),
                   jax.ShapeDtypeStruct((B, 1, n), F32)],
        compiler_params=_params(("parallel",), VMEM_LIMIT),
        name="route",
    )(aff, jnp.asarray(_tri_incl()).astype(BF16))


PAIR = 4


def _moe_kernel(idx_ref, h2_ref, w_ref, y_ref, xin0_ref, xin1_ref):
    e = pl.program_id(1)
    last = N_EXPERTS - 1
    n = N_EXPERTS * CAPACITY
    rows = PAIR * CAPACITY

    def gather_rows(ex, dst_ref):
        for bb in range(PAIR):
            base = bb * n + ex * CAPACITY
            for p in range(CAPACITY):
                off = idx_ref[0, base + p]
                tile = h2_ref[bb, pl.ds(pl.multiple_of((off >> 3) << 3, SUBLANES), SUBLANES), :]
                tile = pltpu.roll(tile, off & PACK_ROWS, axis=0)
                dst_ref[pl.ds((bb * CAPACITY + p) * PACK_ROWS, PACK_ROWS), :] = tile[:PACK_ROWS]

    def expert(xin_ref):
        words = [xin_ref[pl.ds(j, rows, stride=PACK_ROWS), :] for j in range(PACK_ROWS)]
        xin = jnp.concatenate(
            [pltpu.bitcast(w & jnp.uint32(0xFFFF0000), F32).astype(BF16) for w in words]
            + [pltpu.bitcast(w << 16, F32).astype(BF16) for w in words], axis=1)
        for bb in range(PAIR):
            xb = xin[bb * CAPACITY:(bb + 1) * CAPACITY]
            a = jnp.dot(xb, w_ref[0], preferred_element_type=F32)
            u = jnp.dot(xb, w_ref[1], preferred_element_type=F32)
            hmid = (a * (1.0 / (1.0 + jnp.exp(-a))) * u).astype(BF16)
            y = jnp.dot(hmid, w_ref[2], preferred_element_type=F32).astype(BF16)
            top = pltpu.bitcast(y[:, :D_MODEL // 2].astype(F32), jnp.uint32)
            bot = pltpu.bitcast(y[:, D_MODEL // 2:].astype(F32), jnp.uint32)
            words = top | (bot >> 16)
            for j in range(PACK_ROWS):
                y_ref[bb, pl.ds(j, CAPACITY, stride=PACK_ROWS), :] = (
                    words[:, j * LANES:(j + 1) * LANES])

    @pl.when(e == 0)
    def _():
        gather_rows(0, xin0_ref)

    def step(xin_cur, xin_nxt):
        gather_rows(jnp.minimum(e + 1, last), xin_nxt)
        expert(xin_cur)

    @pl.when(e % 2 == 0)
    def _():
        step(xin0_ref, xin1_ref)

    @pl.when(e % 2 == 1)
    def _():
        step(xin1_ref, xin0_ref)


def _moe(idx, h2, experts):
    B = h2.shape[0]
    n = N_EXPERTS * CAPACITY
    rows = SEQ * PACK_ROWS
    pairs = B // PAIR
    stage = pltpu.VMEM((PAIR * CAPACITY * PACK_ROWS, LANES), jnp.uint32)
    out = pl.pallas_call(
        _moe_kernel,
        grid=(pairs, N_EXPERTS),
        in_specs=[pl.BlockSpec((None, 1, PAIR * n), lambda b, e: (b, 0, 0),
                               memory_space=pltpu.SMEM),
                  pl.BlockSpec((None, PAIR, rows, LANES), lambda b, e: (b, 0, 0, 0)),
                  pl.BlockSpec((3, None, D_MODEL, D_MODEL), lambda b, e: (0, e, 0, 0))],
        out_specs=pl.BlockSpec((None, PAIR, CAPACITY * PACK_ROWS, LANES),
                               lambda b, e: (b, 0, e, 0)),
        out_shape=jax.ShapeDtypeStruct((pairs, PAIR, n * PACK_ROWS, LANES), jnp.uint32),
        scratch_shapes=[stage, stage],
        compiler_params=_params(("parallel", "arbitrary"), VMEM_LIMIT),
        name="moe",
    )(idx.reshape(pairs, 1, PAIR * n), h2.reshape(pairs, PAIR, rows, LANES), experts)
    return out.reshape(B, n * PACK_ROWS, LANES)


COMBINE_EXPERTS = 8
SCATTER_UNROLL = 8
COMBINE_ROWS = 256


def _combine_kernel(idx_ref, gate_ref, y_ref, x1_ref, mod_ref, o_ref, acc_ref):
    j = pl.program_id(1)
    slots = COMBINE_EXPERTS * CAPACITY
    base = j * slots

    @pl.when(j == 0)
    def _():
        acc_ref[...] = jnp.zeros_like(acc_ref)

    upper = lax.broadcasted_iota(jnp.int32, (SUBLANES, LANES), 0) < PACK_ROWS
    for g in range(slots // SCATTER_UNROLL):
        new = []
        for u in range(0, SCATTER_UNROLL, 2):
            r = g * SCATTER_UNROLL + u
            words = y_ref[pl.ds(r * PACK_ROWS, SUBLANES), :]
            hi = pltpu.bitcast(words & jnp.uint32(0xFFFF0000), F32)
            lo = pltpu.bitcast(words << 16, F32)
            slabs = (jnp.where(upper, hi, pltpu.roll(lo, PACK_ROWS, axis=0)),
                     jnp.where(upper, pltpu.roll(hi, PACK_ROWS, axis=0), lo))
            for k in range(2):
                dst = pl.multiple_of(idx_ref[0, base + r + k] * (ROW_SLAB // PACK_ROWS), ROW_SLAB)
                new.append((dst, acc_ref[pl.ds(dst, ROW_SLAB), :]
                            + slabs[k] * gate_ref[0, base + r + k]))
        for dst, val in new:
            acc_ref[pl.ds(dst, ROW_SLAB), :] = val

    @pl.when(j == pl.num_programs(1) - 1)
    def _():
        for rb in range(SEQ // COMBINE_ROWS):
            rows = pl.ds(rb * COMBINE_ROWS, COMBINE_ROWS)
            for c in range(ROW_SLAB):
                cols = slice(c * LANES, (c + 1) * LANES)
                chunk = acc_ref[pl.ds(rb * COMBINE_ROWS * ROW_SLAB + c, COMBINE_ROWS,
                                      stride=ROW_SLAB), :]
                o_ref[rows, cols] = x1_ref[rows, cols] + mod_ref[5:6, cols] * chunk


def _combine(idx, gate, y, x1, mod):
    B = x1.shape[0]
    n = N_EXPERTS * CAPACITY
    tok = pl.BlockSpec((None, SEQ, D_MODEL), lambda b, j: (b, 0, 0))
    smem = pl.BlockSpec((None, 1, n), lambda b, j: (b, 0, 0), memory_space=pltpu.SMEM)
    return pl.pallas_call(
        _combine_kernel,
        grid=(B, N_EXPERTS // COMBINE_EXPERTS),
        in_specs=[smem, smem,
                  pl.BlockSpec((None, COMBINE_EXPERTS * CAPACITY * PACK_ROWS, LANES),
                               lambda b, j: (b, j, 0)),
                  tok,
                  pl.BlockSpec((None, N_ADA, D_MODEL), lambda b, j: (b, 0, 0))],
        out_specs=tok,
        out_shape=jax.ShapeDtypeStruct((B, SEQ, D_MODEL), F32),
        scratch_shapes=[pltpu.VMEM((SEQ * ROW_SLAB, LANES), F32)],
        compiler_params=_params(("parallel", "arbitrary"), VMEM_LIMIT),
        name="combine",
    )(idx, gate, y, x1, mod)


def _head_perm():
    perm = []
    for r in range(Q_PER_KV):
        for kv in range(N_KV_HEADS):
            h = kv * Q_PER_KV + r
            perm.extend(range(h * HEAD_DIM, (h + 1) * HEAD_DIM))
    return np.asarray(perm, dtype=np.int32)


def kernel(x, c, rel_bias, w_ada, b_ada, norm_mix_g, norm_ffn_g, w_in, w_fourier, b_fourier,
           q_norm_g, k_norm_g, sink, w_out, w_router, w_gate, w_up, w_down):
    B = x.shape[0]
    assert x.shape[1:] == (SEQ, D_MODEL) and B % PAIR == 0 and B % ROUTE_BATCHES == 0
    assert w_gate.shape[2:] == (D_MODEL, D_EXPERT) and D_EXPERT == D_MODEL
    perm = _head_perm()
    l = 0
    mod = _ada(c, w_ada[l], b_ada[l]).reshape(B, N_ADA, D_MODEL)
    pq = _fold(w_fourier[l])
    bias = _bias_table(rel_bias)

    wi = w_in[l]
    q_cols = wi[:, FOURIER_WIDTH:FOURIER_WIDTH + ATTN_WIDTH][:, perm]
    win = jnp.concatenate([wi[:, :FOURIER_WIDTH], q_cols, wi[:, FOURIER_WIDTH + ATTN_WIDTH:]],
                          axis=1).astype(BF16)
    gq = (jnp.tile(q_norm_g[l], N_Q_HEADS) * (HEAD_DIM ** -0.5 * LOG2E)).reshape(1, ATTN_WIDTH)
    gk = jnp.tile(k_norm_g[l], N_KV_HEADS).reshape(1, KV_WIDTH)
    a, b, q, k, v, experts = _inproj(x, mod, norm_mix_g[l].reshape(1, D_MODEL), win, pq, gq, gk,
                                     w_gate[l], w_up[l], w_down[l])

    yf = _fourier(a, b, b_fourier[l].reshape(1, FOURIER_WIDTH))
    ya = _attn(sink[l], q, k, v, bias)

    wo = w_out[l]
    wo = jnp.concatenate([wo[:FOURIER_WIDTH], wo[FOURIER_WIDTH:][perm]], axis=0).astype(BF16)
    w_hi = w_router[l].astype(BF16)
    w_lo = (w_router[l] - w_hi.astype(F32)).astype(BF16)
    wrh = jnp.pad(jnp.concatenate([w_hi, w_lo], axis=1), ((0, 0), (0, LANES - 2 * N_EXPERTS)))
    x1, h2, aff = _outproj(yf, ya, x, mod, norm_ffn_g[l].reshape(1, D_MODEL), wo, wrh)

    idx, gate = _route(aff)
    n = N_EXPERTS * CAPACITY
    y = _moe(idx, h2, experts)
    return _combine(idx, gate, y, x1, mod)
```
